```python
import math
import jax, jax.numpy as jnp
from jax import lax
import numpy as np

D_MODEL = 1024
BATCH = 8
SEQ = 4096
DEPTH = 4

GRID_W = 64
CTX_LEN = 256
N_MIXERS = 3
HEAD_DIM = 64
ROPE_BASE = 10000.0
ROPE_PAIRS_PER_AXIS = HEAD_DIM // 4
Q_BLOCK = 128
NORM_EPS = 1e-6
NEG_INF = -1e30
DIFF_HEADS = D_MODEL // (2 * HEAD_DIM)
NA_HEADS = D_MODEL // HEAD_DIM
NA_KH_MAX = 8
NA_KW = 16
SWA_HEADS = D_MODEL // HEAD_DIM
SWA_KV_HEADS = SWA_HEADS // 4
SWA_WINDOW = 128
D_FF = 2816
CONV_W = 3
N_LAYERS_A = (DEPTH + N_MIXERS - 1) // N_MIXERS
N_LAYERS_B = (DEPTH + N_MIXERS - 2) // N_MIXERS
N_LAYERS_C = (DEPTH + N_MIXERS - 3) // N_MIXERS

kernel_name = "hybrid_diff_na_swa_convffn_prefix_dit"


def rmsnorm(t, g):
    tf = t.astype(jnp.float32)
    tf = tf * lax.rsqrt(jnp.mean(tf * tf, axis=-1, keepdims=True) + NORM_EPS)
    return tf.astype(t.dtype) * g


def rope_tables(n_tokens, dtype):
    t = jnp.arange(n_tokens, dtype=jnp.int32)
    row = (t // GRID_W).astype(jnp.float32)
    col = (t % GRID_W).astype(jnp.float32)
    inv = ROPE_BASE ** (-jnp.arange(ROPE_PAIRS_PER_AXIS, dtype=jnp.float32) / ROPE_PAIRS_PER_AXIS)
    ar = row[:, None] * inv[None, :]
    ac = col[:, None] * inv[None, :]
    return tuple(a.astype(dtype) for a in (jnp.cos(ar), jnp.sin(ar), jnp.cos(ac), jnp.sin(ac)))


def _rotate(t, cos, sin):
    t1, t2 = jnp.split(t, 2, axis=-1)
    return jnp.concatenate([t1 * cos - t2 * sin, t2 * cos + t1 * sin], axis=-1)


def rope_2d(t, tables):
    cr, sr, cc, sc = (a[None, :, None, :] for a in tables)
    half = HEAD_DIM // 2
    return jnp.concatenate([_rotate(t[..., :half], cr, sr), _rotate(t[..., half:], cc, sc)], axis=-1)


def _to_blocks(t, blk):
    b, s = t.shape[:2]
    return jnp.moveaxis(t.reshape(b, s // blk, blk, *t.shape[2:]), 1, 0)


def _from_blocks(t):
    t = jnp.moveaxis(t, 0, 1)
    return t.reshape(t.shape[0], t.shape[1] * t.shape[2], *t.shape[3:])


def _diff_attend(q, k, v, lam, subln, lam_init):
    s = jnp.einsum('bqhjd,bkhjd->bhjqk', q, k).astype(jnp.float32) * (HEAD_DIM ** -0.5)
    p = jax.nn.softmax(s, axis=-1)
    a = (p[:, :, 0] - lam * p[:, :, 1]).astype(v.dtype)
    o = jnp.einsum('bhqk,bkhe->bqhe', a, v)
    o = rmsnorm(o, subln) * (1.0 - lam_init)
    return o.reshape(o.shape[0], o.shape[1], -1)


def diff_attention(h, hc, wqkv, wo, lam_vecs, subln, layer_idx, tables, need_ctx):
    B, S, _ = h.shape
    H, dh = DIFF_HEADS, HEAD_DIM
    lam_init = 0.8 - 0.6 * math.exp(-0.3 * layer_idx)
    lv = lam_vecs.astype(jnp.float32)
    lam = jnp.exp(jnp.sum(lv[0] * lv[1])) - jnp.exp(jnp.sum(lv[2] * lv[3])) + lam_init

    def proj(t):
        n = t.shape[1]
        q, k, v = jnp.split(t @ wqkv, 3, axis=-1)
        return q.reshape(B, n, 2 * H, dh), k.reshape(B, n, 2 * H, dh), v.reshape(B, n, H, 2 * dh)

    q, k, v = proj(h)
    q, k = rope_2d(q, tables), rope_2d(k, tables)
    qc, kc, vc = proj(hc)
    q = q.reshape(B, S, H, 2, dh)
    k = k.reshape(B, S, H, 2, dh)
    n_ctx = hc.shape[1]
    qc = qc.reshape(B, n_ctx, H, 2, dh)
    kc = kc.reshape(B, n_ctx, H, 2, dh)
    k_all = jnp.concatenate([k, kc], axis=1)
    v_all = jnp.concatenate([v, vc], axis=1)
    y = _from_blocks(lax.map(lambda qb: _diff_attend(qb, k_all, v_all, lam, subln, lam_init),
                             _to_blocks(q, Q_BLOCK))) @ wo
    yc = (_diff_attend(qc, kc, vc, lam, subln, lam_init) @ wo) if need_ctx else None
    return y, yc


def _mha(q, k, v, mask=None, bias=None):
    s = jnp.einsum('bqhd,bkhd->bhqk', q, k).astype(jnp.float32) * (q.shape[-1] ** -0.5)
    if bias is not None:
        s = s + bias
    if mask is not None:
        s = jnp.where(mask, s, NEG_INF)
    p = jax.nn.softmax(s, axis=-1).astype(v.dtype)
    return jnp.einsum('bhqk,bkhd->bqhd', p, v)


def neighbourhood_attention(h, hc, wqkv, wo, rpb, need_ctx):
    B, S, _ = h.shape
    H, dh = NA_HEADS, HEAD_DIM
    rows = S // GRID_W
    kh = min(NA_KH_MAX, rows)
    kw = NA_KW

    def proj(t):
        n = t.shape[1]
        q, k, v = jnp.split(t @ wqkv, 3, axis=-1)
        return q.reshape(B, n, H, dh), k.reshape(B, n, H, dh), v.reshape(B, n, H, dh)

    q, k, v = proj(h)
    qc, kc, vc = proj(hc)
    n_ctx = hc.shape[1]
    kg = k.reshape(B, rows, GRID_W, H, dh)
    vg = v.reshape(B, rows, GRID_W, H, dh)
    col = jnp.arange(GRID_W, dtype=jnp.int32)
    col_start = jnp.clip(col - kw // 2, 0, GRID_W - kw)
    col_ok = (col[None, :] >= col_start[:, None]) & (col[None, :] < col_start[:, None] + kw)
    dx_idx = jnp.clip(col[None, :] - col[:, None], -(kw - 1), kw - 1) + (kw - 1)
    mask = jnp.concatenate(
        [jnp.broadcast_to(col_ok[:, None, :], (GRID_W, kh, GRID_W)).reshape(GRID_W, kh * GRID_W),
         jnp.ones((GRID_W, n_ctx), dtype=bool)], axis=-1)
    rpb32 = rpb.astype(jnp.float32)
    ctx_bias = jnp.zeros((H, GRID_W, n_ctx), jnp.float32)

    def row_block(args):
        r, q_row = args
        rs = jnp.clip(r - kh // 2, 0, rows - kh)
        k_win = lax.dynamic_slice_in_dim(kg, rs, kh, axis=1).reshape(B, kh * GRID_W, H, dh)
        v_win = lax.dynamic_slice_in_dim(vg, rs, kh, axis=1).reshape(B, kh * GRID_W, H, dh)
        dy_idx = rs + jnp.arange(kh, dtype=jnp.int32) - r + (NA_KH_MAX - 1)
        bias = rpb32[:, dy_idx[None, :, None], dx_idx[:, None, :]].reshape(H, GRID_W, kh * GRID_W)
        bias = jnp.concatenate([bias, ctx_bias], axis=-1)
        o = _mha(q_row, jnp.concatenate([k_win, kc], axis=1), jnp.concatenate([v_win, vc], axis=1),
                 mask, bias)
        return o.reshape(B, GRID_W, H * dh)

    q_rows = jnp.moveaxis(q.reshape(B, rows, GRID_W, H, dh), 1, 0)
    y = _from_blocks(lax.map(row_block, (jnp.arange(rows, dtype=jnp.int32), q_rows))) @ wo
    yc = (_mha(qc, kc, vc).reshape(B, n_ctx, H * dh) @ wo) if need_ctx else None
    return y, yc


def _gqa_sink(q, k, v, sink, mask=None):
    s = jnp.einsum('bqkgd,bmkd->bkgqm', q, k).astype(jnp.float32) * (HEAD_DIM ** -0.5)
    if mask is not None:
        s = jnp.where(mask, s, NEG_INF)
    b, kv, g, n, _ = s.shape
    s = jnp.concatenate([s, jnp.broadcast_to(sink[None, :, :, None, None], (b, kv, g, n, 1))], axis=-1)
    p = jax.nn.softmax(s, axis=-1)[..., :-1].astype(v.dtype)
    o = jnp.einsum('bkgqm,bmkd->bqkgd', p, v)
    return o.reshape(b, n, -1)


def window_gqa_sink(h, hc, wqkv, wo, sinks, tables, need_ctx):
    B, S, _ = h.shape
    H, KV, dh = SWA_HEADS, SWA_KV_HEADS, HEAD_DIM
    G = H // KV
    n_ctx = hc.shape[1]

    def proj(t):
        n = t.shape[1]
        q, k, v = jnp.split(t @ wqkv, [H * dh, (H + KV) * dh], axis=-1)
        return q.reshape(B, n, H, dh), k.reshape(B, n, KV, dh), v.reshape(B, n, KV, dh)

    q, k, v = proj(h)
    q, k = rope_2d(q, tables), rope_2d(k, tables)
    q = q.reshape(B, S, KV, G, dh)
    qc, kc, vc = proj(hc)
    qc = qc.reshape(B, n_ctx, KV, G, dh)
    sink = sinks.astype(jnp.float32).reshape(KV, G)

    pad = SWA_WINDOW
    kp = jnp.pad(k, ((0, 0), (pad, pad), (0, 0), (0, 0)))
    vp = jnp.pad(v, ((0, 0), (pad, pad), (0, 0), (0, 0)))
    span = Q_BLOCK + 2 * pad
    qi = jnp.arange(Q_BLOCK, dtype=jnp.int32)
    kj = jnp.arange(span, dtype=jnp.int32)
    band = (kj[None, :] >= qi[:, None]) & (kj[None, :] <= qi[:, None] + 2 * SWA_WINDOW)
    ctx_ok = jnp.ones((Q_BLOCK, n_ctx), dtype=bool)

    def block(args):
        n, q_blk = args
        start = n * Q_BLOCK
        k_loc = lax.dynamic_slice_in_dim(kp, start, span, axis=1)
        v_loc = lax.dynamic_slice_in_dim(vp, start, span, axis=1)
        kpos = start - pad + kj
        valid = band & ((kpos >= 0) & (kpos < S))[None, :]
        mask = jnp.concatenate([valid, ctx_ok], axis=-1)
        return _gqa_sink(q_blk, jnp.concatenate([k_loc, kc], axis=1),
                         jnp.concatenate([v_loc, vc], axis=1), sink, mask)

    nb = S // Q_BLOCK
    y = _from_blocks(lax.map(block, (jnp.arange(nb, dtype=jnp.int32), _to_blocks(q, Q_BLOCK)))) @ wo
    yc = (_gqa_sink(qc, kc, vc, sink) @ wo) if need_ctx else None
    return y, yc


def conv_ffn(t, w_up, w_conv, b_conv, w_down):
    n = t.shape[1]
    u = t @ w_up
    half = CONV_W // 2
    up = jnp.pad(u, ((0, 0), (half, half), (0, 0)))
    acc = b_conv
    for tap in range(CONV_W):
        acc = acc + up[:, tap:tap + n] * w_conv[tap]
    a, g = jnp.split(acc, 2, axis=-1)
    return (jax.nn.silu(g) * a) @ w_down


def setup_inputs(seed: int = 0) -> dict:
    key = jax.random.key(seed)
    ks = iter(jax.random.split(key, 32))

    def nrm(shape, scale):
        return jax.random.normal(next(ks), shape, jnp.float32) * scale

    D = D_MODEL
    s = D ** -0.5
    return {
        "x": nrm((BATCH, SEQ, D), 1.0),
        "c": nrm((BATCH, D), 1.0),
        "ctx": nrm((BATCH, CTX_LEN, D), 1.0),
        "c_ctx": nrm((D,), 1.0),
        "ada_w": nrm((DEPTH, D, 6 * D), s),
        "ada_b": nrm((DEPTH, 6 * D), 0.02),
        "norm_mix": 1.0 + nrm((DEPTH, D), 0.05),
        "norm_ffn": 1.0 + nrm((DEPTH, D), 0.05),
        "norm_out": 1.0 + nrm((D,), 0.05),
        "ffn_up": nrm((DEPTH, D, 2 * D_FF), s),
        "ffn_conv": nrm((DEPTH, CONV_W, 2 * D_FF), CONV_W ** -0.5),
        "ffn_conv_b": nrm((DEPTH, 2 * D_FF), 0.02),
        "ffn_down": nrm((DEPTH, D_FF, D), D_FF ** -0.5),
        "a_wqkv": nrm((N_LAYERS_A, D, 3 * D), s),
        "a_wo": nrm((N_LAYERS_A, D, D), s),
        "a_lambda": nrm((N_LAYERS_A, 4, HEAD_DIM), 0.1),
        "a_subln": 1.0 + nrm((N_LAYERS_A, 2 * HEAD_DIM), 0.05),
        "b_wqkv": nrm((N_LAYERS_B, D, 3 * D), s),
        "b_wo": nrm((N_LAYERS_B, D, D), s),
        "b_rpb": nrm((N_LAYERS_B, NA_HEADS, 2 * NA_KH_MAX - 1, 2 * NA_KW - 1), 0.5),
        "c_wqkv": nrm((N_LAYERS_C, D, (SWA_HEADS + 2 * SWA_KV_HEADS) * HEAD_DIM), s),
        "c_wo": nrm((N_LAYERS_C, D, D), s),
        "c_sinks": nrm((N_LAYERS_C, SWA_HEADS), 0.5),
    }


def reference(x, c, ctx, c_ctx, ada_w, ada_b, norm_mix, norm_ffn, norm_out,
              ffn_up, ffn_conv, ffn_conv_b, ffn_down,
              a_wqkv, a_wo, a_lambda, a_subln,
              b_wqkv, b_wo, b_rpb,
              c_wqkv, c_wo, c_sinks):
    S = x.shape[1]
    tables = rope_tables(S, x.dtype)
    xc = ctx
    silu_c = jax.nn.silu(c)
    silu_cc = jax.nn.silu(c_ctx)
    for i in range(DEPTH):
        need_ctx = i < DEPTH - 1
        kind, j = i % N_MIXERS, i // N_MIXERS
        sh1, sc1, g1, sh2, sc2, g2 = jnp.split((silu_c @ ada_w[i] + ada_b[i])[:, None, :], 6, axis=-1)
        csh1, csc1, cg1, csh2, csc2, cg2 = jnp.split(silu_cc @ ada_w[i] + ada_b[i], 6, axis=-1)
        h = rmsnorm(x, norm_mix[i]) * (1.0 + sc1) + sh1
        hc = rmsnorm(xc, norm_mix[i]) * (1.0 + csc1) + csh1
        if kind == 0:
            y, yc = diff_attention(h, hc, a_wqkv[j], a_wo[j], a_lambda[j], a_subln[j], i, tables, need_ctx)
        elif kind == 1:
            y, yc = neighbourhood_attention(h, hc, b_wqkv[j], b_wo[j], b_rpb[j], need_ctx)
        else:
            y, yc = window_gqa_sink(h, hc, c_wqkv[j], c_wo[j], c_sinks[j], tables, need_ctx)
        x = x + g1 * y
        h2 = rmsnorm(x, norm_ffn[i]) * (1.0 + sc2) + sh2
        x = x + g2 * conv_ffn(h2, ffn_up[i], ffn_conv[i], ffn_conv_b[i], ffn_down[i])
        if need_ctx:
            xc = xc + cg1 * yc
            hc2 = rmsnorm(xc, norm_ffn[i]) * (1.0 + csc2) + csh2
            xc = xc + cg2 * conv_ffn(hc2, ffn_up[i], ffn_conv[i], ffn_conv_b[i], ffn_down[i])
    return rmsnorm(x, norm_out)
```

```python
import functools
import math

import numpy as np
import jax
import jax.numpy as jnp
from jax import lax
from jax.experimental import pallas as pl
from jax.experimental.pallas import tpu as pltpu

F32 = jnp.float32
BF16 = jnp.bfloat16

GRID_W = 64
HEAD_DIM = 64
ROPE_BASE = 10000.0
ROPE_PAIRS = HEAD_DIM // 4
NORM_EPS = 1e-6
NEG_INF = -1e30
N_MIXERS = 3
NA_KH = 8
NA_KW = 16
SWA_WINDOW = 128
CONV_W = 3
LOG2E = 1.4426950408889634
Q_SCALE = (HEAD_DIM ** -0.5) * LOG2E

LANES = 128
BF16_ROWS = 16
VMEM_LIMIT = 52 * 1024 * 1024

T_PROJ = 512
T_FFN = 1024
FF_CHUNK = 256
NA_ROWS = 4
SWA_TQ = 256
DIFF_TQ = 256
DIFF_KC = 512

PLAIN, SCALEQ, ROPEQ, ROPEK = 0, 1, 2, 3


def _cparams(*sem):
    return pltpu.CompilerParams(dimension_semantics=sem, vmem_limit_bytes=VMEM_LIMIT)


def _rms(x):
    return x * lax.rsqrt(jnp.mean(x * x, axis=-1, keepdims=True) + NORM_EPS)


def _ada_kernel(cs_ref, w_ref, b_ref, o_ref):
    cs = cs_ref[...]
    s = cs * (1.0 / (1.0 + jnp.exp(-cs)))
    w = w_ref[...]
    s_hi = s.astype(BF16)
    s_lo = (s - s_hi.astype(F32)).astype(BF16)
    w_hi = w.astype(BF16)
    w_lo = (w - w_hi.astype(F32)).astype(BF16)
    acc = jnp.dot(s_hi, w_hi, preferred_element_type=F32)
    acc = acc + jnp.dot(s_hi, w_lo, preferred_element_type=F32)
    acc = acc + jnp.dot(s_lo, w_hi, preferred_element_type=F32)
    o_ref[...] = acc + b_ref[...]


def _ada_call(cs, ada_w, ada_b):
    depth, d, n = ada_w.shape
    nt = 1536
    rows = cs.shape[0]
    return pl.pallas_call(
        _ada_kernel,
        out_shape=jax.ShapeDtypeStruct((depth, rows, n), F32),
        grid=(depth, n // nt),
        in_specs=[
            pl.BlockSpec((rows, d), lambda l, j: (0, 0)),
            pl.BlockSpec((None, d, nt), lambda l, j: (l, 0, j)),
            pl.BlockSpec((None, 1, nt), lambda l, j: (l, 0, j)),
        ],
        out_specs=pl.BlockSpec((None, rows, nt), lambda l, j: (l, 0, j)),
        compiler_params=_cparams("parallel", "parallel"),
        name="ada_mod",
    )(cs, ada_w, ada_b.reshape(depth, 1, n))


def _rope(r, c, s, first_half):
    partner = jnp.where(first_half, pltpu.roll(r, LANES - 16, 1), pltpu.roll(r, 16, 1))
    return r * c + partner * s


def _proj_kernel(*refs, kinds, nc, has_rope):
    if has_rope:
        x_ref, g_ref, sc_ref, sh_ref, w_ref, cq_ref, sq_ref, ck_ref, sk_ref, o_ref = refs
    else:
        x_ref, g_ref, sc_ref, sh_ref, w_ref, o_ref = refs
    t = x_ref.shape[0]
    nout = o_ref.shape[1]
    h = (_rms(x_ref[...]) * g_ref[...]) * (1.0 + sc_ref[...]) + sh_ref[...]
    hb = h.astype(BF16)
    if has_rope:
        lane = lax.broadcasted_iota(jnp.int32, (t, LANES), 1)
        first_half = (lane & 16) == 0
    for n0 in range(0, nout, nc):
        r = jnp.dot(hb, w_ref[:, n0:n0 + nc], preferred_element_type=F32)
        for j in range(nc // LANES):
            kind = kinds[n0 // LANES + j]
            rj = r[:, j * LANES:(j + 1) * LANES]
            if kind == ROPEQ:
                rj = _rope(rj, cq_ref[...], sq_ref[...], first_half)
            elif kind == ROPEK:
                rj = _rope(rj, ck_ref[...], sk_ref[...], first_half)
            elif kind == SCALEQ:
                rj = rj * Q_SCALE
            o_ref[:, n0 + j * LANES:n0 + (j + 1) * LANES] = rj.astype(BF16)


def _proj_call(x, gamma, mod, w, kinds, tables, *, tiles_per_seq, mod_row, name):
    n, d = x.shape
    nout = w.shape[1]
    t = T_PROJ
    has_rope = tables is not None

    def mrow(i):
        return mod_row if mod_row is not None else i // tiles_per_seq

    def tblock(i):
        return i % tiles_per_seq if mod_row is None else tiles_per_seq

    in_specs = [
        pl.BlockSpec((t, d), lambda i: (i, 0)),
        pl.BlockSpec((1, d), lambda i: (0, 0)),
        pl.BlockSpec((None, None, 1, d), lambda i: (mrow(i), 1, 0, 0)),
        pl.BlockSpec((None, None, 1, d), lambda i: (mrow(i), 0, 0, 0)),
        pl.BlockSpec((d, nout), lambda i: (0, 0)),
    ]
    args = [x, gamma.reshape(1, d), mod, mod, w]
    if has_rope:
        for tab in tables:
            in_specs.append(pl.BlockSpec((t, LANES), lambda i: (tblock(i), 0)))
            args.append(tab)
    return pl.pallas_call(
        functools.partial(_proj_kernel, kinds=tuple(kinds), nc=512, has_rope=has_rope),
        out_shape=jax.ShapeDtypeStruct((n, nout), BF16),
        grid=(n // t,),
        in_specs=in_specs,
        out_specs=pl.BlockSpec((t, nout), lambda i: (i, 0)),
        compiler_params=_cparams("parallel"),
        name=name,
    )(*args)


def _qk(q, k):
    return lax.dot_general(q, k, (((1,), (1,)), ((), ())), preferred_element_type=F32)


def _half_masks(shape):
    lane = lax.broadcasted_iota(jnp.int32, shape, 1)
    return lane < HEAD_DIM


def _split_heads(q, lo):
    qf = q.astype(F32)
    zero = jnp.zeros_like(qf)
    return jnp.where(lo, qf, zero).astype(q.dtype), jnp.where(lo, zero, qf).astype(q.dtype)


def _diff_lambda(lam_ref, lam_init):
    lv = lam_ref[...]
    d1 = jnp.sum(lv[0:1] * lv[1:2], axis=1, keepdims=True)
    d2 = jnp.sum(lv[2:3] * lv[3:4], axis=1, keepdims=True)
    return jnp.exp(d1) - jnp.exp(d2) + lam_init


def _online_update(qm, k, v, m, l, acc):
    s = _qk(qm, k)
    m_new = jnp.maximum(m, jnp.max(s, axis=1, keepdims=True))
    alpha = jnp.exp2(m - m_new)
    p = jnp.exp2(s - m_new)
    l = alpha * l + jnp.sum(p, axis=1, keepdims=True)
    acc = alpha * acc + jnp.dot(p.astype(BF16), v, preferred_element_type=F32)
    return m_new, l, acc


def _diff_finish(state, lam, subln_ref, lam_init, o_ref):
    m1, l1, a1, m2, l2, a2 = state
    o = a1 * (1.0 / l1) - lam * (a2 * (1.0 / l2))
    o = _rms(o) * subln_ref[...] * (1.0 - lam_init)
    o_ref[...] = o.astype(o_ref.dtype)


def _diff_init(tq):
    m = jnp.full((tq, 1), NEG_INF, F32)
    l = jnp.zeros((tq, 1), F32)
    a = jnp.zeros((tq, LANES), F32)
    return (m, l, a, m, l, a)


def _diff_kernel(lam_ref, subln_ref, q_ref, kl_ref, vl_ref, kc_ref, vc_ref, o_ref, *, lam_init, kchunk):
    tq = q_ref.shape[0]
    lam = _diff_lambda(lam_ref, lam_init)
    q1, q2 = _split_heads(q_ref[...], _half_masks((tq, LANES)))

    def step(k, v, state):
        m1, l1, a1, m2, l2, a2 = state
        m1, l1, a1 = _online_update(q1, k, v, m1, l1, a1)
        m2, l2, a2 = _online_update(q2, k, v, m2, l2, a2)
        return (m1, l1, a1, m2, l2, a2)

    def body(c, state):
        off = pl.multiple_of(c * kchunk, kchunk)
        return step(kl_ref[pl.ds(off, kchunk), :], vl_ref[pl.ds(off, kchunk), :], state)

    state = lax.fori_loop(0, kl_ref.shape[0] // kchunk, body, _diff_init(tq))
    state = step(kc_ref[...], vc_ref[...], state)
    _diff_finish(state, lam, subln_ref, lam_init, o_ref)


def _diff_ctx_kernel(lam_ref, subln_ref, q_ref, kc_ref, vc_ref, o_ref, *, lam_init):
    tq = q_ref.shape[0]
    lam = _diff_lambda(lam_ref, lam_init)
    q1, q2 = _split_heads(q_ref[...], _half_masks((tq, LANES)))
    m, l, a, _, _, _ = _diff_init(tq)
    s1 = _online_update(q1, kc_ref[...], vc_ref[...], m, l, a)
    s2 = _online_update(q2, kc_ref[...], vc_ref[...], m, l, a)
    _diff_finish(s1 + s2, lam, subln_ref, lam_init, o_ref)


def _diff_attention(qkv, qkvc, lam_vecs, subln, lam_init, batch, seq, nctx, need_ctx):
    d = qkv.shape[1] // 3
    nh = d // LANES
    tq = DIFF_TQ
    nq = seq // tq
    subln2 = subln.reshape(1, LANES)
    y = pl.pallas_call(
        functools.partial(_diff_kernel, lam_init=lam_init, kchunk=DIFF_KC),
        out_shape=jax.ShapeDtypeStruct((batch * seq, d), BF16),
        grid=(batch, nh, nq),
        in_specs=[
            pl.BlockSpec((4, HEAD_DIM), lambda b, h, i: (0, 0)),
            pl.BlockSpec((1, LANES), lambda b, h, i: (0, 0)),
            pl.BlockSpec((tq, LANES), lambda b, h, i: (b * nq + i, h)),
            pl.BlockSpec((seq, LANES), lambda b, h, i: (b, nh + h)),
            pl.BlockSpec((seq, LANES), lambda b, h, i: (b, 2 * nh + h)),
            pl.BlockSpec((nctx, LANES), lambda b, h, i: (b, nh + h)),
            pl.BlockSpec((nctx, LANES), lambda b, h, i: (b, 2 * nh + h)),
        ],
        out_specs=pl.BlockSpec((tq, LANES), lambda b, h, i: (b * nq + i, h)),
        compiler_params=_cparams("parallel", "parallel", "arbitrary"),
        name="diff_attn",
    )(lam_vecs, subln2, qkv, qkv, qkv, qkvc, qkvc)
    yc = None
    if need_ctx:
        yc = pl.pallas_call(
            functools.partial(_diff_ctx_kernel, lam_init=lam_init),
            out_shape=jax.ShapeDtypeStruct((batch * nctx, d), BF16),
            grid=(batch, nh),
            in_specs=[
                pl.BlockSpec((4, HEAD_DIM), lambda b, h: (0, 0)),
                pl.BlockSpec((1, LANES), lambda b, h: (0, 0)),
                pl.BlockSpec((nctx, LANES), lambda b, h: (b, h)),
                pl.BlockSpec((nctx, LANES), lambda b, h: (b, nh + h)),
                pl.BlockSpec((nctx, LANES), lambda b, h: (b, 2 * nh + h)),
            ],
            out_specs=pl.BlockSpec((nctx, LANES), lambda b, h: (b, h)),
            compiler_params=_cparams("parallel", "parallel"),
            name="diff_attn_ctx",
        )(lam_vecs, subln2, qkvc, qkvc, qkvc)
    return y, yc


def _softmax_pv(scores, values, extra_logit=None):
    m = jnp.max(scores[0], axis=1, keepdims=True)
    for s in scores[1:]:
        m = jnp.maximum(m, jnp.max(s, axis=1, keepdims=True))
    if extra_logit is not None:
        m = jnp.maximum(m, extra_logit)
    l = None
    acc = None
    for s, v in zip(scores, values):
        p = jnp.exp2(s - m)
        ps = jnp.sum(p, axis=1, keepdims=True)
        pv = jnp.dot(p.astype(BF16), v, preferred_element_type=F32)
        l = ps if l is None else l + ps
        acc = pv if acc is None else acc + pv
    if extra_logit is not None:
        l = l + jnp.exp2(extra_logit - m)
    return acc * (1.0 / l)


def _na_kernel(bias_ref, q_ref, k0_ref, k1_ref, k2_ref, v0_ref, v1_ref, v2_ref, kc_ref, vc_ref, o_ref):
    tq = q_ref.shape[0]
    lo = _half_masks((tq, LANES))
    qs = _split_heads(q_ref[...], lo)
    ks = (k0_ref[...], k1_ref[...], k2_ref[...])
    vs = [v0_ref[...], v1_ref[...], v2_ref[...], vc_ref[...]]
    blk = k0_ref.shape[0]
    outs = []
    for hh in range(2):
        scores = [_qk(qs[hh], ks[j]) + bias_ref[hh, :, j * blk:(j + 1) * blk] for j in range(3)]
        scores.append(_qk(qs[hh], kc_ref[...]))
        outs.append(_softmax_pv(scores, vs))
    o_ref[...] = jnp.where(lo, outs[0], outs[1]).astype(o_ref.dtype)


def _plain_ctx_kernel(q_ref, kc_ref, vc_ref, o_ref):
    tq = q_ref.shape[0]
    lo = _half_masks((tq, LANES))
    qs = _split_heads(q_ref[...], lo)
    outs = [_softmax_pv([_qk(qs[hh], kc_ref[...])], [vc_ref[...]]) for hh in range(2)]
    o_ref[...] = jnp.where(lo, outs[0], outs[1]).astype(o_ref.dtype)


def _na_bias_tables(rpb, rows):
    w = GRID_W
    nk = 3 * NA_ROWS
    a = np.arange(NA_ROWS)[:, None, None, None]
    cq = np.arange(w)[None, :, None, None]
    e = np.arange(nk)[None, None, :, None]
    cx = np.arange(w)[None, None, None, :]
    col_start = np.clip(cq - NA_KW // 2, 0, w - NA_KW)
    col_ok = (cx >= col_start) & (cx < col_start + NA_KW)
    dx = np.clip(cx - cq, -(NA_KW - 1), NA_KW - 1) + (NA_KW - 1)
    tabs = []
    for r0, k0 in ((0, 0), (NA_ROWS, 0), (rows - NA_ROWS, rows - nk)):
        r = r0 + a
        ry = k0 + e
        rs = np.clip(r - NA_KH // 2, 0, rows - NA_KH)
        ok = (ry >= rs) & (ry < rs + NA_KH) & col_ok
        dy = np.clip(ry - r + (NA_KH - 1), 0, 2 * NA_KH - 2)
        shape = (NA_ROWS, w, nk, w)
        dyb = np.broadcast_to(dy, shape).reshape(NA_ROWS * w, nk * w)
        dxb = np.broadcast_to(dx, shape).reshape(NA_ROWS * w, nk * w)
        okb = np.broadcast_to(ok, shape).reshape(NA_ROWS * w, nk * w)
        tabs.append(jnp.where(okb[None], rpb[:, dyb, dxb] * LOG2E, NEG_INF))
    return jnp.stack(tabs).astype(F32)


def _na_attention(qkv, qkvc, rpb, batch, seq, nctx, need_ctx):
    d = qkv.shape[1] // 3
    nb = d // LANES
    rows = seq // GRID_W
    tq = NA_ROWS * GRID_W
    ng = seq // tq
    bias = _na_bias_tables(rpb.astype(F32), rows)

    def cls(g):
        return jnp.where(g == 0, 0, jnp.where(g == ng - 1, 2, 1))

    def kblock(g, j):
        return jnp.clip(g - 1, 0, ng - 3) + j

    def kv_spec(col0, j):
        return pl.BlockSpec((tq, LANES), lambda hb, g, b: (b * ng + kblock(g, j), col0 + hb))

    y = pl.pallas_call(
        _na_kernel,
        out_shape=jax.ShapeDtypeStruct((batch * seq, d), BF16),
        grid=(nb, ng, batch),
        in_specs=[
            pl.BlockSpec((None, 2, tq, 3 * tq), lambda hb, g, b: (cls(g), hb, 0, 0)),
            pl.BlockSpec((tq, LANES), lambda hb, g, b: (b * ng + g, hb)),
            kv_spec(nb, 0), kv_spec(nb, 1), kv_spec(nb, 2),
            kv_spec(2 * nb, 0), kv_spec(2 * nb, 1), kv_spec(2 * nb, 2),
            pl.BlockSpec((nctx, LANES), lambda hb, g, b: (b, nb + hb)),
            pl.BlockSpec((nctx, LANES), lambda hb, g, b: (b, 2 * nb + hb)),
        ],
        out_specs=pl.BlockSpec((tq, LANES), lambda hb, g, b: (b * ng + g, hb)),
        compiler_params=_cparams("parallel", "parallel", "arbitrary"),
        name="na_attn",
    )(bias, qkv, qkv, qkv, qkv, qkv, qkv, qkv, qkvc, qkvc)
    yc = None
    if need_ctx:
        yc = pl.pallas_call(
            _plain_ctx_kernel,
            out_shape=jax.ShapeDtypeStruct((batch * nctx, d), BF16),
            grid=(batch, nb),
            in_specs=[
                pl.BlockSpec((nctx, LANES), lambda b, hb: (b, hb)),
                pl.BlockSpec((nctx, LANES), lambda b, hb: (b, nb + hb)),
                pl.BlockSpec((nctx, LANES), lambda b, hb: (b, 2 * nb + hb)),
            ],
            out_specs=pl.BlockSpec((nctx, LANES), lambda b, hb: (b, hb)),
            compiler_params=_cparams("parallel", "parallel"),
            name="na_attn_ctx",
        )(qkvc, qkvc, qkvc)
    return y, yc


def _swa_heads(q_ref, sinks_ref, g, score_fn, values, o_ref):
    tq = q_ref.shape[0]
    lo = _half_masks((tq, LANES))
    for jb in range(2):
        qs = _split_heads(q_ref[:, jb * LANES:(jb + 1) * LANES], lo)
        outs = []
        for hh in range(2):
            sink = jnp.full((1, 1), sinks_ref[g * 4 + jb * 2 + hh] * LOG2E, F32)
            outs.append(_softmax_pv(score_fn(qs[hh]), values, extra_logit=sink))
        o_ref[:, jb * LANES:(jb + 1) * LANES] = jnp.where(lo, outs[0], outs[1]).astype(o_ref.dtype)


def _swa_kernel(sinks_ref, q_ref, kl_ref, vl_ref, kc_ref, vc_ref, o_ref, *, span):
    g = pl.program_id(1)
    i = pl.program_id(2)
    tq = q_ref.shape[0]
    seq = kl_ref.shape[0]
    kstart = pl.multiple_of(jnp.clip(i * tq - SWA_WINDOW, 0, seq - span), SWA_WINDOW)
    kwin = kl_ref[pl.ds(kstart, span), :]
    vwin = vl_ref[pl.ds(kstart, span), :]
    qpos = i * tq + lax.broadcasted_iota(jnp.int32, (tq, span), 0)
    kpos = kstart + lax.broadcasted_iota(jnp.int32, (tq, span), 1)
    band = jnp.where(jnp.abs(qpos - kpos) <= SWA_WINDOW, 0.0, NEG_INF).astype(F32)
    kc = kc_ref[...]

    def score_fn(qm):
        return [_qk(qm, kwin) + band, _qk(qm, kc)]

    _swa_heads(q_ref, sinks_ref, g, score_fn, [vwin, vc_ref[...]], o_ref)


def _swa_ctx_kernel(sinks_ref, q_ref, kc_ref, vc_ref, o_ref):
    g = pl.program_id(1)
    kc = kc_ref[...]
    _swa_heads(q_ref, sinks_ref, g, lambda qm: [_qk(qm, kc)], [vc_ref[...]], o_ref)


def _swa_attention(qkv, qkvc, sinks, batch, seq, nctx, need_ctx, d):
    nkv = (qkv.shape[1] - d) // (2 * LANES)
    qb = d // nkv
    kcol = d // LANES
    tq = SWA_TQ
    nq = seq // tq
    span = tq + 2 * SWA_WINDOW
    smem = pl.BlockSpec(memory_space=pltpu.SMEM)
    sinks = sinks.astype(F32)
    y = pl.pallas_call(
        functools.partial(_swa_kernel, span=span),
        out_shape=jax.ShapeDtypeStruct((batch * seq, d), BF16),
        grid=(batch, nkv, nq),
        in_specs=[
            smem,
            pl.BlockSpec((tq, qb), lambda b, g, i: (b * nq + i, g)),
            pl.BlockSpec((seq, LANES), lambda b, g, i: (b, kcol + g)),
            pl.BlockSpec((seq, LANES), lambda b, g, i: (b, kcol + nkv + g)),
            pl.BlockSpec((nctx, LANES), lambda b, g, i: (b, kcol + g)),
            pl.BlockSpec((nctx, LANES), lambda b, g, i: (b, kcol + nkv + g)),
        ],
        out_specs=pl.BlockSpec((tq, qb), lambda b, g, i: (b * nq + i, g)),
        compiler_params=_cparams("parallel", "parallel", "arbitrary"),
        name="swa_attn",
    )(sinks, qkv, qkv, qkv, qkvc, qkvc)
    yc = None
    if need_ctx:
        yc = pl.pallas_call(
            _swa_ctx_kernel,
            out_shape=jax.ShapeDtypeStruct((batch * nctx, d), BF16),
            grid=(batch, nkv),
            in_specs=[
                smem,
                pl.BlockSpec((nctx, qb), lambda b, g: (b, g)),
                pl.BlockSpec((nctx, LANES), lambda b, g: (b, kcol + g)),
                pl.BlockSpec((nctx, LANES), lambda b, g: (b, kcol + nkv + g)),
            ],
            out_specs=pl.BlockSpec((nctx, qb), lambda b, g: (b, g)),
            compiler_params=_cparams("parallel", "parallel"),
            name="swa_attn_ctx",
        )(sinks, qkvc, qkvc, qkvc)
    return y, yc


def _oproj_kernel(o_ref, wo_ref, x_ref, g1_ref, nf_ref, sc2_ref, sh2_ref, xo_ref, h2_ref):
    y = jnp.dot(o_ref[...], wo_ref[...], preferred_element_type=F32)
    xn = x_ref[...] + g1_ref[...] * y
    xo_ref[...] = xn
    h2 = (_rms(xn) * nf_ref[...]) * (1.0 + sc2_ref[...]) + sh2_ref[...]
    h2_ref[...] = h2.astype(BF16)


def _oproj_call(o, wo, x, gamma, mod, *, tiles_per_seq, mod_row, name):
    n, d = x.shape
    t = T_PROJ

    def mrow(i):
        return mod_row if mod_row is not None else i // tiles_per_seq

    def mspec(which):
        return pl.BlockSpec((None, None, 1, d), lambda i: (mrow(i), which, 0, 0))

    return pl.pallas_call(
        _oproj_kernel,
        out_shape=(jax.ShapeDtypeStruct((n, d), F32), jax.ShapeDtypeStruct((n, d), BF16)),
        grid=(n // t,),
        in_specs=[
            pl.BlockSpec((t, d), lambda i: (i, 0)),
            pl.BlockSpec((d, d), lambda i: (0, 0)),
            pl.BlockSpec((t, d), lambda i: (i, 0)),
            mspec(2),
            pl.BlockSpec((1, d), lambda i: (0, 0)),
            mspec(4),
            mspec(3),
        ],
        out_specs=(pl.BlockSpec((t, d), lambda i: (i, 0)), pl.BlockSpec((t, d), lambda i: (i, 0))),
        compiler_params=_cparams("parallel"),
        name=name,
    )(o, wo, x, mod, gamma.reshape(1, d), mod, mod)


def _ffn_kernel(h_ref, hp_ref, hn_ref, wa_ref, wg_ref, ca_ref, cg_ref, ba_ref, bg_ref, wd_ref,
                x_ref, g2_ref, no_ref, o_ref, lhs, acc, *, nseq, final):
    i = pl.program_id(0)
    c = pl.program_id(1)
    t = h_ref.shape[0]
    halo = hp_ref.shape[0]

    @pl.when(c == 0)
    def _():
        keep_prev = ((i % nseq) != 0).astype(F32)
        keep_next = ((i % nseq) != nseq - 1).astype(F32)
        lhs[0:halo, :] = (hp_ref[...].astype(F32) * keep_prev).astype(BF16)
        lhs[halo:halo + t, :] = h_ref[...]
        lhs[halo + t:, :] = (hn_ref[...].astype(F32) * keep_next).astype(BF16)
        acc[...] = jnp.zeros_like(acc)

    def conv(w_ref, cw_ref, cb_ref):
        u = jnp.dot(lhs[...], w_ref[...], preferred_element_type=F32)
        cw = cw_ref[...]
        return (cb_ref[...] + u[halo - 1:halo - 1 + t] * cw[0:1]
                + u[halo:halo + t] * cw[1:2] + u[halo + 1:halo + 1 + t] * cw[2:3])

    a = conv(wa_ref, ca_ref, ba_ref)
    g = conv(wg_ref, cg_ref, bg_ref)
    act = (g * (1.0 / (1.0 + jnp.exp(-g)))) * a
    acc[...] += jnp.dot(act.astype(BF16), wd_ref[...], preferred_element_type=F32)

    @pl.when(c == pl.num_programs(1) - 1)
    def _():
        xn = x_ref[...] + g2_ref[...] * acc[...]
        if final:
            xn = _rms(xn) * no_ref[...]
        o_ref[...] = xn


def _ffn_call(h2, x, w_up, w_conv, b_conv, w_down, mod, norm_out, *, t, tiles_per_seq, mod_row, final, name):
    n, d = x.shape
    dff = w_down.shape[0]
    f = FF_CHUNK
    nch = dff // f
    halo = BF16_ROWS
    hb = t // halo
    last_hblock = n // halo - 1

    def mrow(i):
        return mod_row if mod_row is not None else i // tiles_per_seq

    b2 = b_conv.reshape(1, 2 * dff)
    return pl.pallas_call(
        functools.partial(_ffn_kernel, nseq=tiles_per_seq, final=final),
        out_shape=jax.ShapeDtypeStruct((n, d), F32),
        grid=(n // t, nch),
        in_specs=[
            pl.BlockSpec((t, d), lambda i, c: (i, 0)),
            pl.BlockSpec((halo, d), lambda i, c: (jnp.maximum(i * hb - 1, 0), 0)),
            pl.BlockSpec((halo, d), lambda i, c: (jnp.minimum((i + 1) * hb, last_hblock), 0)),
            pl.BlockSpec((d, f), lambda i, c: (0, c)),
            pl.BlockSpec((d, f), lambda i, c: (0, nch + c)),
            pl.BlockSpec((CONV_W, f), lambda i, c: (0, c)),
            pl.BlockSpec((CONV_W, f), lambda i, c: (0, nch + c)),
            pl.BlockSpec((1, f), lambda i, c: (0, c)),
            pl.BlockSpec((1, f), lambda i, c: (0, nch + c)),
            pl.BlockSpec((f, d), lambda i, c: (c, 0)),
            pl.BlockSpec((t, d), lambda i, c: (i, 0)),
            pl.BlockSpec((None, None, 1, d), lambda i, c: (mrow(i), 5, 0, 0)),
            pl.BlockSpec((1, d), lambda i, c: (0, 0)),
        ],
        out_specs=pl.BlockSpec((t, d), lambda i, c: (i, 0)),
        scratch_shapes=[pltpu.VMEM((t + 2 * halo, d), BF16), pltpu.VMEM((t, d), F32)],
        compiler_params=_cparams("parallel", "arbitrary"),
        name=name,
    )(h2, h2, h2, w_up, w_up, w_conv, w_conv, b2, b2, w_down, x, mod, norm_out.reshape(1, d))


def _rope_tables(seq, extra_rows):
    tpos = jnp.arange(seq, dtype=jnp.int32)
    row = (tpos // GRID_W).astype(F32)
    col = (tpos % GRID_W).astype(F32)
    inv = ROPE_BASE ** (-jnp.arange(ROPE_PAIRS, dtype=F32) / ROPE_PAIRS)
    ar = row[:, None] * inv[None, :]
    ac = col[:, None] * inv[None, :]
    cr, sr, cc, sc = jnp.cos(ar), jnp.sin(ar), jnp.cos(ac), jnp.sin(ac)
    cos = jnp.concatenate([cr, cr, cc, cc] * (LANES // HEAD_DIM), axis=1)
    sin = jnp.concatenate([-sr, sr, -sc, sc] * (LANES // HEAD_DIM), axis=1)
    cos = jnp.concatenate([cos, jnp.ones((extra_rows, LANES), F32)], axis=0)
    sin = jnp.concatenate([sin, jnp.zeros((extra_rows, LANES), F32)], axis=0)
    return cos * Q_SCALE, sin * Q_SCALE, cos, sin


def _dup_kv_columns(w, d, nkv):
    q = w[:, :d]
    k = w[:, d:d + nkv * HEAD_DIM].reshape(-1, nkv, 1, HEAD_DIM)
    v = w[:, d + nkv * HEAD_DIM:].reshape(-1, nkv, 1, HEAD_DIM)
    kd = jnp.broadcast_to(k, (w.shape[0], nkv, 2, HEAD_DIM)).reshape(w.shape[0], -1)
    vd = jnp.broadcast_to(v, (w.shape[0], nkv, 2, HEAD_DIM)).reshape(w.shape[0], -1)
    return jnp.concatenate([q, kd, vd], axis=1)


def kernel(x, c, ctx, c_ctx, ada_w, ada_b, norm_mix, norm_ffn, norm_out, ffn_up, ffn_conv, ffn_conv_b,
           ffn_down, a_wqkv, a_wo, a_lambda, a_subln, b_wqkv, b_wo, b_rpb, c_wqkv, c_wo, c_sinks):
    batch, seq, d = x.shape
    nctx = ctx.shape[1]
    depth = ada_w.shape[0]
    assert seq % T_FFN == 0 and seq % T_PROJ == 0 and (batch * nctx) % T_PROJ == 0

    mod_rows = 16
    cs = jnp.concatenate([c, c_ctx[None, :], jnp.zeros((mod_rows - batch - 1, d), F32)], axis=0)
    mod_all = _ada_call(cs, ada_w, ada_b).reshape(depth, mod_rows, 6, 1, d)
    ctx_row = batch

    tables = _rope_tables(seq, T_PROJ)
    lat_tiles = seq // T_PROJ
    nd = d // LANES

    xl = x.reshape(batch * seq, d)
    xc = ctx.reshape(batch * nctx, d)
    for i in range(depth):
        need_ctx = i < depth - 1
        kind, j = i % N_MIXERS, i // N_MIXERS
        mod = mod_all[i]
        if kind == 0:
            w = a_wqkv[j].astype(BF16)
            wo = a_wo[j].astype(BF16)
            kinds = [ROPEQ] * nd + [ROPEK] * nd + [PLAIN] * nd
            tabs = tables
        elif kind == 1:
            w = b_wqkv[j].astype(BF16)
            wo = b_wo[j].astype(BF16)
            kinds = [SCALEQ] * nd + [PLAIN] * (2 * nd)
            tabs = None
        else:
            nkv = (c_wqkv.shape[2] - d) // (2 * HEAD_DIM)
            w = _dup_kv_columns(c_wqkv[j], d, nkv).astype(BF16)
            wo = c_wo[j].astype(BF16)
            kinds = [ROPEQ] * nd + [ROPEK] * nkv + [PLAIN] * nkv
            tabs = tables
        qkv = _proj_call(xl, norm_mix[i], mod, w, kinds, tabs, tiles_per_seq=lat_tiles, mod_row=None,
                         name=f"proj{i}")
        qkvc = _proj_call(xc, norm_mix[i], mod, w, kinds, tabs, tiles_per_seq=lat_tiles, mod_row=ctx_row,
                          name=f"proj_ctx{i}")
        if kind == 0:
            lam_init = 0.8 - 0.6 * math.exp(-0.3 * i)
            y, yc = _diff_attention(qkv, qkvc, a_lambda[j].astype(F32), a_subln[j], lam_init,
                                    batch, seq, nctx, need_ctx)
        elif kind == 1:
            y, yc = _na_attention(qkv, qkvc, b_rpb[j], batch, seq, nctx, need_ctx)
        else:
            y, yc = _swa_attention(qkv, qkvc, c_sinks[j], batch, seq, nctx, need_ctx, d)

        w_up = ffn_up[i].astype(BF16)
        w_down = ffn_down[i].astype(BF16)
        xl, h2 = _oproj_call(y, wo, xl, norm_ffn[i], mod, tiles_per_seq=lat_tiles, mod_row=None,
                             name=f"oproj{i}")
        xl = _ffn_call(h2, xl, w_up, ffn_conv[i], ffn_conv_b[i], w_down, mod, norm_out,
                       t=T_FFN, tiles_per_seq=seq // T_FFN, mod_row=None, final=not need_ctx,
                       name=f"ffn{i}")
        if need_ctx:
            xc, h2c = _oproj_call(yc, wo, xc, norm_ffn[i], mod, tiles_per_seq=1, mod_row=ctx_row,
                                  name=f"oproj_ctx{i}")
            xc = _ffn_call(h2c, xc, w_up, ffn_conv[i], ffn_conv_b[i], w_down, mod, norm_out,
                           t=nctx, tiles_per_seq=1, mod_row=ctx_row, final=False, name=f"ffn_ctx{i}")
    return xl.reshape(batch, seq, d)
```

```python
import functools
import math

import numpy as np
import jax
import jax.numpy as jnp
from jax import lax
from jax.experimental import pallas as pl
from jax.experimental.pallas import tpu as pltpu

F32 = jnp.float32
BF16 = jnp.bfloat16

GRID_W = 64
HEAD_DIM = 64
ROPE_BASE = 10000.0
ROPE_PAIRS = HEAD_DIM // 4
NORM_EPS = 1e-6
NEG_INF = -1e30
N_MIXERS = 3
NA_KH = 8
NA_KW = 16
SWA_WINDOW = 128
CONV_W = 3
LOG2E = 1.4426950408889634
Q_SCALE = (HEAD_DIM ** -0.5) * LOG2E

LANES = 128
BF16_ROWS = 16
VMEM_LIMIT = 52 * 1024 * 1024

T_PROJ = 512
T_FFN = 1024
FF_CHUNK = 256
NA_ROWS = 4
SWA_TQ = 256
DIFF_TQ = 256

PLAIN, SCALEQ, ROPEQ, ROPEK = 0, 1, 2, 3


def _cparams(*sem):
    return pltpu.CompilerParams(dimension_semantics=sem, vmem_limit_bytes=VMEM_LIMIT)


def _rms(x):
    return x * lax.rsqrt(jnp.mean(x * x, axis=-1, keepdims=True) + NORM_EPS)


def _ada_kernel(cs_ref, w_ref, b_ref, o_ref):
    cs = cs_ref[...]
    s = cs * (1.0 / (1.0 + jnp.exp(-cs)))
    w = w_ref[...]
    s_hi = s.astype(BF16)
    s_lo = (s - s_hi.astype(F32)).astype(BF16)
    w_hi = w.astype(BF16)
    w_lo = (w - w_hi.astype(F32)).astype(BF16)
    acc = jnp.dot(s_hi, w_hi, preferred_element_type=F32)
    acc = acc + jnp.dot(s_hi, w_lo, preferred_element_type=F32)
    acc = acc + jnp.dot(s_lo, w_hi, preferred_element_type=F32)
    o_ref[...] = acc + b_ref[...]


def _ada_call(cs, ada_w, ada_b):
    depth, d, n = ada_w.shape
    nt = 1536
    rows = cs.shape[0]
    return pl.pallas_call(
        _ada_kernel,
        out_shape=jax.ShapeDtypeStruct((depth, rows, n), F32),
        grid=(depth, n // nt),
        in_specs=[
            pl.BlockSpec((rows, d), lambda l, j: (0, 0)),
            pl.BlockSpec((None, d, nt), lambda l, j: (l, 0, j)),
            pl.BlockSpec((None, 1, nt), lambda l, j: (l, 0, j)),
        ],
        out_specs=pl.BlockSpec((None, rows, nt), lambda l, j: (l, 0, j)),
        compiler_params=_cparams("parallel", "parallel"),
        name="ada_mod",
    )(cs, ada_w, ada_b.reshape(depth, 1, n))


def _rope(r, c, s, first_half):
    partner = jnp.where(first_half, pltpu.roll(r, LANES - 16, 1), pltpu.roll(r, 16, 1))
    return r * c + partner * s


def _proj_kernel(*refs, kinds, nc, has_rope):
    if has_rope:
        x_ref, g_ref, sc_ref, sh_ref, w_ref, cq_ref, sq_ref, ck_ref, sk_ref, o_ref = refs
    else:
        x_ref, g_ref, sc_ref, sh_ref, w_ref, o_ref = refs
    t = x_ref.shape[0]
    nout = o_ref.shape[1]
    h = (_rms(x_ref[...]) * g_ref[...]) * (1.0 + sc_ref[...]) + sh_ref[...]
    hb = h.astype(BF16)
    if has_rope:
        lane = lax.broadcasted_iota(jnp.int32, (t, LANES), 1)
        first_half = (lane & 16) == 0
    for n0 in range(0, nout, nc):
        r = jnp.dot(hb, w_ref[:, n0:n0 + nc], preferred_element_type=F32)
        for j in range(nc // LANES):
            kind = kinds[n0 // LANES + j]
            rj = r[:, j * LANES:(j + 1) * LANES]
            if kind == ROPEQ:
                rj = _rope(rj, cq_ref[...], sq_ref[...], first_half)
            elif kind == ROPEK:
                rj = _rope(rj, ck_ref[...], sk_ref[...], first_half)
            elif kind == SCALEQ:
                rj = rj * Q_SCALE
            o_ref[:, n0 + j * LANES:n0 + (j + 1) * LANES] = rj.astype(BF16)


def _proj_call(x, gamma, mod, w, kinds, tables, *, tiles_per_seq, mod_row, name):
    n, d = x.shape
    nout = w.shape[1]
    t = T_PROJ
    has_rope = tables is not None

    def mrow(i):
        return mod_row if mod_row is not None else i // tiles_per_seq

    def tblock(i):
        return i % tiles_per_seq if mod_row is None else tiles_per_seq

    in_specs = [
        pl.BlockSpec((t, d), lambda i: (i, 0)),
        pl.BlockSpec((1, d), lambda i: (0, 0)),
        pl.BlockSpec((None, None, 1, d), lambda i: (mrow(i), 1, 0, 0)),
        pl.BlockSpec((None, None, 1, d), lambda i: (mrow(i), 0, 0, 0)),
        pl.BlockSpec((d, nout), lambda i: (0, 0)),
    ]
    args = [x, gamma.reshape(1, d), mod, mod, w]
    if has_rope:
        for tab in tables:
            in_specs.append(pl.BlockSpec((t, LANES), lambda i: (tblock(i), 0)))
            args.append(tab)
    return pl.pallas_call(
        functools.partial(_proj_kernel, kinds=tuple(kinds), nc=512, has_rope=has_rope),
        out_shape=jax.ShapeDtypeStruct((n, nout), BF16),
        grid=(n // t,),
        in_specs=in_specs,
        out_specs=pl.BlockSpec((t, nout), lambda i: (i, 0)),
        compiler_params=_cparams("parallel"),
        name=name,
    )(*args)


def _qk(q, k):
    return lax.dot_general(q, k, (((1,), (1,)), ((), ())), preferred_element_type=F32)


def _half_masks(shape):
    lane = lax.broadcasted_iota(jnp.int32, shape, 1)
    return lane < HEAD_DIM


def _split_heads(q, lo):
    qf = q.astype(F32)
    zero = jnp.zeros_like(qf)
    return jnp.where(lo, qf, zero).astype(q.dtype), jnp.where(lo, zero, qf).astype(q.dtype)


def _diff_lambda(lam_ref, lam_init):
    lv = lam_ref[...]
    d1 = jnp.sum(lv[0:1] * lv[1:2], axis=1, keepdims=True)
    d2 = jnp.sum(lv[2:3] * lv[3:4], axis=1, keepdims=True)
    return jnp.exp(d1) - jnp.exp(d2) + lam_init


def _diff_finish(o1, o2, lam, subln_ref, lam_init, o_ref):
    o = _rms(o1 - lam * o2) * subln_ref[...] * (1.0 - lam_init)
    o_ref[...] = o.astype(o_ref.dtype)


def _diff_kernel(lam_ref, subln_ref, q_ref, kl_ref, vl_ref, kc_ref, vc_ref, o_ref, *, lam_init):
    tq = q_ref.shape[0]
    lam = _diff_lambda(lam_ref, lam_init)
    qs = _split_heads(q_ref[...], _half_masks((tq, LANES)))
    values = [vl_ref[...], vc_ref[...]]
    o1, o2 = [_softmax_pv([_qk(qm, kl_ref[...]), _qk(qm, kc_ref[...])], values) for qm in qs]
    _diff_finish(o1, o2, lam, subln_ref, lam_init, o_ref)


def _diff_ctx_kernel(lam_ref, subln_ref, q_ref, kc_ref, vc_ref, o_ref, *, lam_init):
    tq = q_ref.shape[0]
    lam = _diff_lambda(lam_ref, lam_init)
    qs = _split_heads(q_ref[...], _half_masks((tq, LANES)))
    o1, o2 = [_softmax_pv([_qk(qm, kc_ref[...])], [vc_ref[...]]) for qm in qs]
    _diff_finish(o1, o2, lam, subln_ref, lam_init, o_ref)


def _diff_attention(qkv, qkvc, lam_vecs, subln, lam_init, batch, seq, nctx, need_ctx):
    d = qkv.shape[1] // 3
    nh = d // LANES
    tq = DIFF_TQ
    nq = seq // tq
    subln2 = subln.reshape(1, LANES)
    y = pl.pallas_call(
        functools.partial(_diff_kernel, lam_init=lam_init),
        out_shape=jax.ShapeDtypeStruct((batch * seq, d), BF16),
        grid=(batch, nh, nq),
        in_specs=[
            pl.BlockSpec((4, HEAD_DIM), lambda b, h, i: (0, 0)),
            pl.BlockSpec((1, LANES), lambda b, h, i: (0, 0)),
            pl.BlockSpec((tq, LANES), lambda b, h, i: (b * nq + i, h)),
            pl.BlockSpec((seq, LANES), lambda b, h, i: (b, nh + h)),
            pl.BlockSpec((seq, LANES), lambda b, h, i: (b, 2 * nh + h)),
            pl.BlockSpec((nctx, LANES), lambda b, h, i: (b, nh + h)),
            pl.BlockSpec((nctx, LANES), lambda b, h, i: (b, 2 * nh + h)),
        ],
        out_specs=pl.BlockSpec((tq, LANES), lambda b, h, i: (b * nq + i, h)),
        compiler_params=_cparams("parallel", "parallel", "arbitrary"),
        name="diff_attn",
    )(lam_vecs, subln2, qkv, qkv, qkv, qkvc, qkvc)
    yc = None
    if need_ctx:
        yc = pl.pallas_call(
            functools.partial(_diff_ctx_kernel, lam_init=lam_init),
            out_shape=jax.ShapeDtypeStruct((batch * nctx, d), BF16),
            grid=(batch, nh),
            in_specs=[
                pl.BlockSpec((4, HEAD_DIM), lambda b, h: (0, 0)),
                pl.BlockSpec((1, LANES), lambda b, h: (0, 0)),
                pl.BlockSpec((nctx, LANES), lambda b, h: (b, h)),
                pl.BlockSpec((nctx, LANES), lambda b, h: (b, nh + h)),
                pl.BlockSpec((nctx, LANES), lambda b, h: (b, 2 * nh + h)),
            ],
            out_specs=pl.BlockSpec((nctx, LANES), lambda b, h: (b, h)),
            compiler_params=_cparams("parallel", "parallel"),
            name="diff_attn_ctx",
        )(lam_vecs, subln2, qkvc, qkvc, qkvc)
    return y, yc


def _softmax_pv(scores, values, extra_logit=None):
    m = jnp.max(scores[0], axis=1, keepdims=True)
    for s in scores[1:]:
        m = jnp.maximum(m, jnp.max(s, axis=1, keepdims=True))
    if extra_logit is not None:
        m = jnp.maximum(m, extra_logit)
    l = None
    acc = None
    for s, v in zip(scores, values):
        p = jnp.exp2(s - m)
        ps = jnp.sum(p, axis=1, keepdims=True)
        pv = jnp.dot(p.astype(BF16), v, preferred_element_type=F32)
        l = ps if l is None else l + ps
        acc = pv if acc is None else acc + pv
    if extra_logit is not None:
        l = l + jnp.exp2(extra_logit - m)
    return acc * (1.0 / l)


def _na_kernel(bias_ref, q_ref, k0_ref, k1_ref, k2_ref, v0_ref, v1_ref, v2_ref, kc_ref, vc_ref, o_ref):
    tq = q_ref.shape[0]
    lo = _half_masks((tq, LANES))
    qs = _split_heads(q_ref[...], lo)
    ks = (k0_ref[...], k1_ref[...], k2_ref[...])
    vs = [v0_ref[...], v1_ref[...], v2_ref[...], vc_ref[...]]
    blk = k0_ref.shape[0]
    outs = []
    for hh in range(2):
        scores = [_qk(qs[hh], ks[j]) + bias_ref[hh, :, j * blk:(j + 1) * blk] for j in range(3)]
        scores.append(_qk(qs[hh], kc_ref[...]))
        outs.append(_softmax_pv(scores, vs))
    o_ref[...] = jnp.where(lo, outs[0], outs[1]).astype(o_ref.dtype)


def _plain_ctx_kernel(q_ref, kc_ref, vc_ref, o_ref):
    tq = q_ref.shape[0]
    lo = _half_masks((tq, LANES))
    qs = _split_heads(q_ref[...], lo)
    outs = [_softmax_pv([_qk(qs[hh], kc_ref[...])], [vc_ref[...]]) for hh in range(2)]
    o_ref[...] = jnp.where(lo, outs[0], outs[1]).astype(o_ref.dtype)


def _na_bias_tables(rpb, rows):
    w = GRID_W
    nk = 3 * NA_ROWS
    nh = rpb.shape[0]
    edge = w - NA_KW
    ext = jnp.concatenate([jnp.repeat(rpb[..., :1], edge, axis=-1), rpb,
                           jnp.repeat(rpb[..., -1:], edge + 1, axis=-1)], axis=-1)
    flat = jnp.tile(ext, (1, 1, w + 1))[..., :w * (2 * w + 1)]
    hankel = flat.reshape(nh, -1, w, 2 * w + 1)[..., :w]
    toep = jnp.flip(hankel, axis=-2)
    cq = np.arange(w)[:, None]
    cx = np.arange(w)[None, :]
    col_start = np.clip(cq - NA_KW // 2, 0, w - NA_KW)
    col_ok = (cx >= col_start) & (cx < col_start + NA_KW)
    toep = jnp.where(col_ok, toep * LOG2E, NEG_INF)
    masked = jnp.full((nh, w, w), NEG_INF, F32)
    tabs = []
    for r0, k0 in ((0, 0), (NA_ROWS, 0), (rows - NA_ROWS, rows - nk)):
        row_blocks = []
        for a in range(NA_ROWS):
            r = r0 + a
            rs = min(max(r - NA_KH // 2, 0), rows - NA_KH)
            blocks = []
            for e in range(nk):
                ry = k0 + e
                blocks.append(toep[:, ry - r + NA_KH - 1] if rs <= ry < rs + NA_KH else masked)
            row_blocks.append(jnp.concatenate(blocks, axis=-1))
        tabs.append(jnp.concatenate(row_blocks, axis=-2))
    return jnp.stack(tabs).astype(F32)


def _na_attention(qkv, qkvc, rpb, batch, seq, nctx, need_ctx):
    d = qkv.shape[1] // 3
    nb = d // LANES
    rows = seq // GRID_W
    tq = NA_ROWS * GRID_W
    ng = seq // tq
    bias = _na_bias_tables(rpb.astype(F32), rows)

    def cls(g):
        return jnp.where(g == 0, 0, jnp.where(g == ng - 1, 2, 1))

    def kblock(g, j):
        return jnp.clip(g - 1, 0, ng - 3) + j

    def kv_spec(col0, j):
        return pl.BlockSpec((tq, LANES), lambda hb, g, b: (b * ng + kblock(g, j), col0 + hb))

    y = pl.pallas_call(
        _na_kernel,
        out_shape=jax.ShapeDtypeStruct((batch * seq, d), BF16),
        grid=(nb, ng, batch),
        in_specs=[
            pl.BlockSpec((None, 2, tq, 3 * tq), lambda hb, g, b: (cls(g), hb, 0, 0)),
            pl.BlockSpec((tq, LANES), lambda hb, g, b: (b * ng + g, hb)),
            kv_spec(nb, 0), kv_spec(nb, 1), kv_spec(nb, 2),
            kv_spec(2 * nb, 0), kv_spec(2 * nb, 1), kv_spec(2 * nb, 2),
            pl.BlockSpec((nctx, LANES), lambda hb, g, b: (b, nb + hb)),
            pl.BlockSpec((nctx, LANES), lambda hb, g, b: (b, 2 * nb + hb)),
        ],
        out_specs=pl.BlockSpec((tq, LANES), lambda hb, g, b: (b * ng + g, hb)),
        compiler_params=_cparams("parallel", "parallel", "arbitrary"),
        name="na_attn",
    )(bias, qkv, qkv, qkv, qkv, qkv, qkv, qkv, qkvc, qkvc)
    yc = None
    if need_ctx:
        yc = pl.pallas_call(
            _plain_ctx_kernel,
            out_shape=jax.ShapeDtypeStruct((batch * nctx, d), BF16),
            grid=(batch, nb),
            in_specs=[
                pl.BlockSpec((nctx, LANES), lambda b, hb: (b, hb)),
                pl.BlockSpec((nctx, LANES), lambda b, hb: (b, nb + hb)),
                pl.BlockSpec((nctx, LANES), lambda b, hb: (b, 2 * nb + hb)),
            ],
            out_specs=pl.BlockSpec((nctx, LANES), lambda b, hb: (b, hb)),
            compiler_params=_cparams("parallel", "parallel"),
            name="na_attn_ctx",
        )(qkvc, qkvc, qkvc)
    return y, yc


def _swa_heads(q_ref, sinks_ref, g, score_fn, values, o_ref):
    tq = q_ref.shape[0]
    lo = _half_masks((tq, LANES))
    for jb in range(2):
        qs = _split_heads(q_ref[:, jb * LANES:(jb + 1) * LANES], lo)
        outs = []
        for hh in range(2):
            sink = jnp.full((1, 1), sinks_ref[g * 4 + jb * 2 + hh] * LOG2E, F32)
            outs.append(_softmax_pv(score_fn(qs[hh]), values, extra_logit=sink))
        o_ref[:, jb * LANES:(jb + 1) * LANES] = jnp.where(lo, outs[0], outs[1]).astype(o_ref.dtype)


def _swa_kernel(sinks_ref, q_ref, kl_ref, vl_ref, kc_ref, vc_ref, o_ref, *, span):
    g = pl.program_id(1)
    i = pl.program_id(2)
    tq = q_ref.shape[0]
    seq = kl_ref.shape[0]
    kstart = pl.multiple_of(jnp.clip(i * tq - SWA_WINDOW, 0, seq - span), SWA_WINDOW)
    kwin = kl_ref[pl.ds(kstart, span), :]
    vwin = vl_ref[pl.ds(kstart, span), :]
    qpos = i * tq + lax.broadcasted_iota(jnp.int32, (tq, span), 0)
    kpos = kstart + lax.broadcasted_iota(jnp.int32, (tq, span), 1)
    band = jnp.where(jnp.abs(qpos - kpos) <= SWA_WINDOW, 0.0, NEG_INF).astype(F32)
    kc = kc_ref[...]

    def score_fn(qm):
        return [_qk(qm, kwin) + band, _qk(qm, kc)]

    _swa_heads(q_ref, sinks_ref, g, score_fn, [vwin, vc_ref[...]], o_ref)


def _swa_ctx_kernel(sinks_ref, q_ref, kc_ref, vc_ref, o_ref):
    g = pl.program_id(1)
    kc = kc_ref[...]
    _swa_heads(q_ref, sinks_ref, g, lambda qm: [_qk(qm, kc)], [vc_ref[...]], o_ref)


def _swa_attention(qkv, qkvc, sinks, batch, seq, nctx, need_ctx, d):
    nkv = (qkv.shape[1] - d) // (2 * LANES)
    qb = d // nkv
    kcol = d // LANES
    tq = SWA_TQ
    nq = seq // tq
    span = tq + 2 * SWA_WINDOW
    smem = pl.BlockSpec(memory_space=pltpu.SMEM)
    sinks = sinks.astype(F32)
    y = pl.pallas_call(
        functools.partial(_swa_kernel, span=span),
        out_shape=jax.ShapeDtypeStruct((batch * seq, d), BF16),
        grid=(batch, nkv, nq),
        in_specs=[
            smem,
            pl.BlockSpec((tq, qb), lambda b, g, i: (b * nq + i, g)),
            pl.BlockSpec((seq, LANES), lambda b, g, i: (b, kcol + g)),
            pl.BlockSpec((seq, LANES), lambda b, g, i: (b, kcol + nkv + g)),
            pl.BlockSpec((nctx, LANES), lambda b, g, i: (b, kcol + g)),
            pl.BlockSpec((nctx, LANES), lambda b, g, i: (b, kcol + nkv + g)),
        ],
        out_specs=pl.BlockSpec((tq, qb), lambda b, g, i: (b * nq + i, g)),
        compiler_params=_cparams("parallel", "parallel", "arbitrary"),
        name="swa_attn",
    )(sinks, qkv, qkv, qkv, qkvc, qkvc)
    yc = None
    if need_ctx:
        yc = pl.pallas_call(
            _swa_ctx_kernel,
            out_shape=jax.ShapeDtypeStruct((batch * nctx, d), BF16),
            grid=(batch, nkv),
            in_specs=[
                smem,
                pl.BlockSpec((nctx, qb), lambda b, g: (b, g)),
                pl.BlockSpec((nctx, LANES), lambda b, g: (b, kcol + g)),
                pl.BlockSpec((nctx, LANES), lambda b, g: (b, kcol + nkv + g)),
            ],
            out_specs=pl.BlockSpec((nctx, qb), lambda b, g: (b, g)),
            compiler_params=_cparams("parallel", "parallel"),
            name="swa_attn_ctx",
        )(sinks, qkvc, qkvc, qkvc)
    return y, yc


def _oproj_kernel(o_ref, wo_ref, x_ref, g1_ref, nf_ref, sc2_ref, sh2_ref, xo_ref, h2_ref):
    y = jnp.dot(o_ref[...], wo_ref[...], preferred_element_type=F32)
    xn = x_ref[...] + g1_ref[...] * y
    xo_ref[...] = xn
    h2 = (_rms(xn) * nf_ref[...]) * (1.0 + sc2_ref[...]) + sh2_ref[...]
    h2_ref[...] = h2.astype(BF16)


def _oproj_call(o, wo, x, gamma, mod, *, tiles_per_seq, mod_row, name):
    n, d = x.shape
    t = T_PROJ

    def mrow(i):
        return mod_row if mod_row is not None else i // tiles_per_seq

    def mspec(which):
        return pl.BlockSpec((None, None, 1, d), lambda i: (mrow(i), which, 0, 0))

    return pl.pallas_call(
        _oproj_kernel,
        out_shape=(jax.ShapeDtypeStruct((n, d), F32), jax.ShapeDtypeStruct((n, d), BF16)),
        grid=(n // t,),
        in_specs=[
            pl.BlockSpec((t, d), lambda i: (i, 0)),
            pl.BlockSpec((d, d), lambda i: (0, 0)),
            pl.BlockSpec((t, d), lambda i: (i, 0)),
            mspec(2),
            pl.BlockSpec((1, d), lambda i: (0, 0)),
            mspec(4),
            mspec(3),
        ],
        out_specs=(pl.BlockSpec((t, d), lambda i: (i, 0)), pl.BlockSpec((t, d), lambda i: (i, 0))),
        compiler_params=_cparams("parallel"),
        name=name,
    )(o, wo, x, mod, gamma.reshape(1, d), mod, mod)


def _ffn_kernel(h_ref, hp_ref, hn_ref, wa_ref, wg_ref, ca_ref, cg_ref, ba_ref, bg_ref, wd_ref,
                x_ref, g2_ref, no_ref, o_ref, lhs, acc, *, nseq, final):
    i = pl.program_id(0)
    c = pl.program_id(1)
    t = h_ref.shape[0]
    halo = hp_ref.shape[0]

    @pl.when(c == 0)
    def _():
        keep_prev = ((i % nseq) != 0).astype(F32)
        keep_next = ((i % nseq) != nseq - 1).astype(F32)
        lhs[0:halo, :] = (hp_ref[...].astype(F32) * keep_prev).astype(BF16)
        lhs[halo:halo + t, :] = h_ref[...]
        lhs[halo + t:, :] = (hn_ref[...].astype(F32) * keep_next).astype(BF16)
        acc[...] = jnp.zeros_like(acc)

    def conv(w_ref, cw_ref, cb_ref):
        u = jnp.dot(lhs[...], w_ref[...], preferred_element_type=F32)
        cw = cw_ref[...]
        return (cb_ref[...] + u[halo - 1:halo - 1 + t] * cw[0:1]
                + u[halo:halo + t] * cw[1:2] + u[halo + 1:halo + 1 + t] * cw[2:3])

    a = conv(wa_ref, ca_ref, ba_ref)
    g = conv(wg_ref, cg_ref, bg_ref)
    act = (g * (1.0 / (1.0 + jnp.exp(-g)))) * a
    acc[...] += jnp.dot(act.astype(BF16), wd_ref[...], preferred_element_type=F32)

    @pl.when(c == pl.num_programs(1) - 1)
    def _():
        xn = x_ref[...] + g2_ref[...] * acc[...]
        if final:
            xn = _rms(xn) * no_ref[...]
        o_ref[...] = xn


def _ffn_call(h2, x, w_up, w_conv, b_conv, w_down, mod, norm_out, *, t, tiles_per_seq, mod_row, final, name):
    n, d = x.shape
    dff = w_down.shape[0]
    f = FF_CHUNK
    nch = dff // f
    halo = BF16_ROWS
    hb = t // halo
    last_hblock = n // halo - 1

    def mrow(i):
        return mod_row if mod_row is not None else i // tiles_per_seq

    b2 = b_conv.reshape(1, 2 * dff)
    return pl.pallas_call(
        functools.partial(_ffn_kernel, nseq=tiles_per_seq, final=final),
        out_shape=jax.ShapeDtypeStruct((n, d), F32),
        grid=(n // t, nch),
        in_specs=[
            pl.BlockSpec((t, d), lambda i, c: (i, 0)),
            pl.BlockSpec((halo, d), lambda i, c: (jnp.maximum(i * hb - 1, 0), 0)),
            pl.BlockSpec((halo, d), lambda i, c: (jnp.minimum((i + 1) * hb, last_hblock), 0)),
            pl.BlockSpec((d, f), lambda i, c: (0, c)),
            pl.BlockSpec((d, f), lambda i, c: (0, nch + c)),
            pl.BlockSpec((CONV_W, f), lambda i, c: (0, c)),
            pl.BlockSpec((CONV_W, f), lambda i, c: (0, nch + c)),
            pl.BlockSpec((1, f), lambda i, c: (0, c)),
            pl.BlockSpec((1, f), lambda i, c: (0, nch + c)),
            pl.BlockSpec((f, d), lambda i, c: (c, 0)),
            pl.BlockSpec((t, d), lambda i, c: (i, 0)),
            pl.BlockSpec((None, None, 1, d), lambda i, c: (mrow(i), 5, 0, 0)),
            pl.BlockSpec((1, d), lambda i, c: (0, 0)),
        ],
        out_specs=pl.BlockSpec((t, d), lambda i, c: (i, 0)),
        scratch_shapes=[pltpu.VMEM((t + 2 * halo, d), BF16), pltpu.VMEM((t, d), F32)],
        compiler_params=_cparams("parallel", "arbitrary"),
        name=name,
    )(h2, h2, h2, w_up, w_up, w_conv, w_conv, b2, b2, w_down, x, mod, norm_out.reshape(1, d))


def _rope_tables(seq, extra_rows):
    tpos = jnp.arange(seq, dtype=jnp.int32)
    row = (tpos // GRID_W).astype(F32)
    col = (tpos % GRID_W).astype(F32)
    inv = ROPE_BASE ** (-jnp.arange(ROPE_PAIRS, dtype=F32) / ROPE_PAIRS)
    ar = row[:, None] * inv[None, :]
    ac = col[:, None] * inv[None, :]
    cr, sr, cc, sc = jnp.cos(ar), jnp.sin(ar), jnp.cos(ac), jnp.sin(ac)
    cos = jnp.concatenate([cr, cr, cc, cc] * (LANES // HEAD_DIM), axis=1)
    sin = jnp.concatenate([-sr, sr, -sc, sc] * (LANES // HEAD_DIM), axis=1)
    cos = jnp.concatenate([cos, jnp.ones((extra_rows, LANES), F32)], axis=0)
    sin = jnp.concatenate([sin, jnp.zeros((extra_rows, LANES), F32)], axis=0)
    return cos * Q_SCALE, sin * Q_SCALE, cos, sin


def _dup_kv_columns(w, d, nkv):
    q = w[:, :d]
    k = w[:, d:d + nkv * HEAD_DIM].reshape(-1, nkv, 1, HEAD_DIM)
    v = w[:, d + nkv * HEAD_DIM:].reshape(-1, nkv, 1, HEAD_DIM)
    kd = jnp.broadcast_to(k, (w.shape[0], nkv, 2, HEAD_DIM)).reshape(w.shape[0], -1)
    vd = jnp.broadcast_to(v, (w.shape[0], nkv, 2, HEAD_DIM)).reshape(w.shape[0], -1)
    return jnp.concatenate([q, kd, vd], axis=1)


def kernel(x, c, ctx, c_ctx, ada_w, ada_b, norm_mix, norm_ffn, norm_out, ffn_up, ffn_conv, ffn_conv_b,
           ffn_down, a_wqkv, a_wo, a_lambda, a_subln, b_wqkv, b_wo, b_rpb, c_wqkv, c_wo, c_sinks):
    batch, seq, d = x.shape
    nctx = ctx.shape[1]
    depth = ada_w.shape[0]
    assert seq % T_FFN == 0 and seq % T_PROJ == 0 and (batch * nctx) % T_PROJ == 0

    mod_rows = 16
    cs = jnp.concatenate([c, c_ctx[None, :], jnp.zeros((mod_rows - batch - 1, d), F32)], axis=0)
    mod_all = _ada_call(cs, ada_w, ada_b).reshape(depth, mod_rows, 6, 1, d)
    ctx_row = batch

    tables = _rope_tables(seq, T_PROJ)
    lat_tiles = seq // T_PROJ
    nd = d // LANES

    xl = x.reshape(batch * seq, d)
    xc = ctx.reshape(batch * nctx, d)
    for i in range(depth):
        need_ctx = i < depth - 1
        kind, j = i % N_MIXERS, i // N_MIXERS
        mod = mod_all[i]
        if kind == 0:
            w = a_wqkv[j].astype(BF16)
            wo = a_wo[j].astype(BF16)
            kinds = [ROPEQ] * nd + [ROPEK] * nd + [PLAIN] * nd
            tabs = tables
        elif kind == 1:
            w = b_wqkv[j].astype(BF16)
            wo = b_wo[j].astype(BF16)
            kinds = [SCALEQ] * nd + [PLAIN] * (2 * nd)
            tabs = None
        else:
            nkv = (c_wqkv.shape[2] - d) // (2 * HEAD_DIM)
            w = _dup_kv_columns(c_wqkv[j], d, nkv).astype(BF16)
            wo = c_wo[j].astype(BF16)
            kinds = [ROPEQ] * nd + [ROPEK] * nkv + [PLAIN] * nkv
            tabs = tables
        qkv = _proj_call(xl, norm_mix[i], mod, w, kinds, tabs, tiles_per_seq=lat_tiles, mod_row=None,
                         name=f"proj{i}")
        qkvc = _proj_call(xc, norm_mix[i], mod, w, kinds, tabs, tiles_per_seq=lat_tiles, mod_row=ctx_row,
                          name=f"proj_ctx{i}")
        if kind == 0:
            lam_init = 0.8 - 0.6 * math.exp(-0.3 * i)
            y, yc = _diff_attention(qkv, qkvc, a_lambda[j].astype(F32), a_subln[j], lam_init,
                                    batch, seq, nctx, need_ctx)
        elif kind == 1:
            y, yc = _na_attention(qkv, qkvc, b_rpb[j], batch, seq, nctx, need_ctx)
        else:
            y, yc = _swa_attention(qkv, qkvc, c_sinks[j], batch, seq, nctx, need_ctx, d)

        w_up = ffn_up[i].astype(BF16)
        w_down = ffn_down[i].astype(BF16)
        xl, h2 = _oproj_call(y, wo, xl, norm_ffn[i], mod, tiles_per_seq=lat_tiles, mod_row=None,
                             name=f"oproj{i}")
        xl = _ffn_call(h2, xl, w_up, ffn_conv[i], ffn_conv_b[i], w_down, mod, norm_out,
                       t=T_FFN, tiles_per_seq=seq // T_FFN, mod_row=None, final=not need_ctx,
                       name=f"ffn{i}")
        if need_ctx:
            xc, h2c = _oproj_call(yc, wo, xc, norm_ffn[i], mod, tiles_per_seq=1, mod_row=ctx_row,
                                  name=f"oproj_ctx{i}")
            xc = _ffn_call(h2c, xc, w_up, ffn_conv[i], ffn_conv_b[i], w_down, mod, norm_out,
                           t=nctx, tiles_per_seq=1, mod_row=ctx_row, final=False, name=f"ffn_ctx{i}")
    return xl.reshape(batch, seq, d)
```

```python
import functools
import math

import numpy as np
import jax
import jax.numpy as jnp
from jax import lax
from jax.experimental import pallas as pl
from jax.experimental.pallas import tpu as pltpu

F32 = jnp.float32
BF16 = jnp.bfloat16

GRID_W = 64
HEAD_DIM = 64
ROPE_BASE = 10000.0
ROPE_PAIRS = HEAD_DIM // 4
NORM_EPS = 1e-6
NEG_INF = -1e30
N_MIXERS = 3
NA_KH = 8
NA_KW = 16
SWA_WINDOW = 128
CONV_W = 3
LOG2E = 1.4426950408889634
Q_SCALE = (HEAD_DIM ** -0.5) * LOG2E

LANES = 128
BF16_ROWS = 16
VMEM_LIMIT = 52 * 1024 * 1024

T_PROJ = 512
T_FFN = 512
FF_CHUNK = 256
NA_ROWS = 4
SWA_TQ = 256
DIFF_TQ = 256

PLAIN, SCALEQ, ROPEQ, ROPEK = 0, 1, 2, 3


def _cparams(*sem):
    return pltpu.CompilerParams(dimension_semantics=sem, vmem_limit_bytes=VMEM_LIMIT)


def _rms(x):
    return x * lax.rsqrt(jnp.mean(x * x, axis=-1, keepdims=True) + NORM_EPS)


def _ada_kernel(cs_ref, w_ref, b_ref, o_ref):
    cs = cs_ref[...]
    s = cs * (1.0 / (1.0 + jnp.exp(-cs)))
    w = w_ref[...]
    s_hi = s.astype(BF16)
    s_lo = (s - s_hi.astype(F32)).astype(BF16)
    w_hi = w.astype(BF16)
    w_lo = (w - w_hi.astype(F32)).astype(BF16)
    acc = jnp.dot(s_hi, w_hi, preferred_element_type=F32)
    acc = acc + jnp.dot(s_hi, w_lo, preferred_element_type=F32)
    acc = acc + jnp.dot(s_lo, w_hi, preferred_element_type=F32)
    o_ref[...] = acc + b_ref[...]


def _ada_call(cs, ada_w, ada_b):
    depth, d, n = ada_w.shape
    nt = 1536
    rows = cs.shape[0]
    return pl.pallas_call(
        _ada_kernel,
        out_shape=jax.ShapeDtypeStruct((depth, rows, n), F32),
        grid=(depth, n // nt),
        in_specs=[
            pl.BlockSpec((rows, d), lambda l, j: (0, 0)),
            pl.BlockSpec((None, d, nt), lambda l, j: (l, 0, j)),
            pl.BlockSpec((None, 1, nt), lambda l, j: (l, 0, j)),
        ],
        out_specs=pl.BlockSpec((None, rows, nt), lambda l, j: (l, 0, j)),
        compiler_params=_cparams("parallel", "parallel"),
        name="ada_mod",
    )(cs, ada_w, ada_b.reshape(depth, 1, n))


def _rope(r, c, s, first_half):
    partner = jnp.where(first_half, pltpu.roll(r, LANES - 16, 1), pltpu.roll(r, 16, 1))
    return r * c + partner * s


def _proj_kernel(*refs, kinds, nc, has_rope):
    if has_rope:
        x_ref, g_ref, sc_ref, sh_ref, w_ref, cq_ref, sq_ref, ck_ref, sk_ref, o_ref = refs
    else:
        x_ref, g_ref, sc_ref, sh_ref, w_ref, o_ref = refs
    t = x_ref.shape[0]
    nout = o_ref.shape[1]
    h = (_rms(x_ref[...]) * g_ref[...]) * (1.0 + sc_ref[...]) + sh_ref[...]
    hb = h.astype(BF16)
    if has_rope:
        lane = lax.broadcasted_iota(jnp.int32, (t, LANES), 1)
        first_half = (lane & 16) == 0
    for n0 in range(0, nout, nc):
        r = jnp.dot(hb, w_ref[:, n0:n0 + nc], preferred_element_type=F32)
        for j in range(nc // LANES):
            kind = kinds[n0 // LANES + j]
            rj = r[:, j * LANES:(j + 1) * LANES]
            if kind == ROPEQ:
                rj = _rope(rj, cq_ref[...], sq_ref[...], first_half)
            elif kind == ROPEK:
                rj = _rope(rj, ck_ref[...], sk_ref[...], first_half)
            elif kind == SCALEQ:
                rj = rj * Q_SCALE
            o_ref[:, n0 + j * LANES:n0 + (j + 1) * LANES] = rj.astype(BF16)


def _proj_call(x, gamma, mod, w, kinds, tables, *, tiles_per_seq, mod_row, name):
    n, d = x.shape
    nout = w.shape[1]
    t = T_PROJ
    has_rope = tables is not None

    def mrow(i):
        return mod_row if mod_row is not None else i // tiles_per_seq

    def tblock(i):
        return i % tiles_per_seq if mod_row is None else tiles_per_seq

    in_specs = [
        pl.BlockSpec((t, d), lambda i: (i, 0)),
        pl.BlockSpec((1, d), lambda i: (0, 0)),
        pl.BlockSpec((None, None, 1, d), lambda i: (mrow(i), 1, 0, 0)),
        pl.BlockSpec((None, None, 1, d), lambda i: (mrow(i), 0, 0, 0)),
        pl.BlockSpec((d, nout), lambda i: (0, 0)),
    ]
    args = [x, gamma.reshape(1, d), mod, mod, w]
    if has_rope:
        for tab in tables:
            in_specs.append(pl.BlockSpec((t, LANES), lambda i: (tblock(i), 0)))
            args.append(tab)
    return pl.pallas_call(
        functools.partial(_proj_kernel, kinds=tuple(kinds), nc=512, has_rope=has_rope),
        out_shape=jax.ShapeDtypeStruct((n, nout), BF16),
        grid=(n // t,),
        in_specs=in_specs,
        out_specs=pl.BlockSpec((t, nout), lambda i: (i, 0)),
        compiler_params=_cparams("parallel"),
        name=name,
    )(*args)


def _qk(q, k):
    return lax.dot_general(q, k, (((1,), (1,)), ((), ())), preferred_element_type=F32)


def _half_masks(shape):
    lane = lax.broadcasted_iota(jnp.int32, shape, 1)
    return lane < HEAD_DIM


def _split_heads(q, lo):
    qf = q.astype(F32)
    zero = jnp.zeros_like(qf)
    return jnp.where(lo, qf, zero).astype(q.dtype), jnp.where(lo, zero, qf).astype(q.dtype)


def _diff_lambda(lam_ref, lam_init):
    lv = lam_ref[...]
    d1 = jnp.sum(lv[0:1] * lv[1:2], axis=1, keepdims=True)
    d2 = jnp.sum(lv[2:3] * lv[3:4], axis=1, keepdims=True)
    return jnp.exp(d1) - jnp.exp(d2) + lam_init


def _diff_finish(o1, o2, lam, subln_ref, lam_init, o_ref):
    o = _rms(o1 - lam * o2) * subln_ref[...] * (1.0 - lam_init)
    o_ref[...] = o.astype(o_ref.dtype)


def _diff_kernel(lam_ref, subln_ref, q_ref, kl_ref, vl_ref, kc_ref, vc_ref, o_ref, *, lam_init):
    tq = q_ref.shape[0]
    lam = _diff_lambda(lam_ref, lam_init)
    qs = _split_heads(q_ref[...], _half_masks((tq, LANES)))
    values = [vl_ref[...], vc_ref[...]]
    scores = [[_qk(qm, kl_ref[...]), _qk(qm, kc_ref[...])] for qm in qs]
    o1, o2 = [_softmax_pv(s, values) for s in scores]
    _diff_finish(o1, o2, lam, subln_ref, lam_init, o_ref)


def _diff_ctx_kernel(lam_ref, subln_ref, q_ref, kc_ref, vc_ref, o_ref, *, lam_init):
    tq = q_ref.shape[0]
    lam = _diff_lambda(lam_ref, lam_init)
    qs = _split_heads(q_ref[...], _half_masks((tq, LANES)))
    o1, o2 = [_softmax_pv([_qk(qm, kc_ref[...])], [vc_ref[...]]) for qm in qs]
    _diff_finish(o1, o2, lam, subln_ref, lam_init, o_ref)


def _diff_attention(qkv, qkvc, lam_vecs, subln, lam_init, batch, seq, nctx, need_ctx):
    d = qkv.shape[1] // 3
    nh = d // LANES
    tq = DIFF_TQ
    nq = seq // tq
    subln2 = subln.reshape(1, LANES)
    y = pl.pallas_call(
        functools.partial(_diff_kernel, lam_init=lam_init),
        out_shape=jax.ShapeDtypeStruct((batch * seq, d), BF16),
        grid=(batch, nh, nq),
        in_specs=[
            pl.BlockSpec((4, HEAD_DIM), lambda b, h, i: (0, 0)),
            pl.BlockSpec((1, LANES), lambda b, h, i: (0, 0)),
            pl.BlockSpec((tq, LANES), lambda b, h, i: (b * nq + i, h)),
            pl.BlockSpec((seq, LANES), lambda b, h, i: (b, nh + h)),
            pl.BlockSpec((seq, LANES), lambda b, h, i: (b, 2 * nh + h)),
            pl.BlockSpec((nctx, LANES), lambda b, h, i: (b, nh + h)),
            pl.BlockSpec((nctx, LANES), lambda b, h, i: (b, 2 * nh + h)),
        ],
        out_specs=pl.BlockSpec((tq, LANES), lambda b, h, i: (b * nq + i, h)),
        compiler_params=_cparams("parallel", "parallel", "arbitrary"),
        name="diff_attn",
    )(lam_vecs, subln2, qkv, qkv, qkv, qkvc, qkvc)
    yc = None
    if need_ctx:
        yc = pl.pallas_call(
            functools.partial(_diff_ctx_kernel, lam_init=lam_init),
            out_shape=jax.ShapeDtypeStruct((batch * nctx, d), BF16),
            grid=(batch, nh),
            in_specs=[
                pl.BlockSpec((4, HEAD_DIM), lambda b, h: (0, 0)),
                pl.BlockSpec((1, LANES), lambda b, h: (0, 0)),
                pl.BlockSpec((nctx, LANES), lambda b, h: (b, h)),
                pl.BlockSpec((nctx, LANES), lambda b, h: (b, nh + h)),
                pl.BlockSpec((nctx, LANES), lambda b, h: (b, 2 * nh + h)),
            ],
            out_specs=pl.BlockSpec((nctx, LANES), lambda b, h: (b, h)),
            compiler_params=_cparams("parallel", "parallel"),
            name="diff_attn_ctx",
        )(lam_vecs, subln2, qkvc, qkvc, qkvc)
    return y, yc


def _softmax_pv(scores, values, extra_logit=None):
    m = jnp.max(scores[0], axis=1, keepdims=True)
    for s in scores[1:]:
        m = jnp.maximum(m, jnp.max(s, axis=1, keepdims=True))
    if extra_logit is not None:
        m = jnp.maximum(m, extra_logit)
    l = None
    acc = None
    for s, v in zip(scores, values):
        p = jnp.exp2(s - m)
        ps = jnp.sum(p, axis=1, keepdims=True)
        pv = jnp.dot(p.astype(BF16), v, preferred_element_type=F32)
        l = ps if l is None else l + ps
        acc = pv if acc is None else acc + pv
    if extra_logit is not None:
        l = l + jnp.exp2(extra_logit - m)
    return acc * (1.0 / l)


def _na_kernel(bias_ref, q_ref, k0_ref, k1_ref, k2_ref, v0_ref, v1_ref, v2_ref, kc_ref, vc_ref, o_ref):
    tq = q_ref.shape[0]
    lo = _half_masks((tq, LANES))
    qs = _split_heads(q_ref[...], lo)
    ks = (k0_ref[...], k1_ref[...], k2_ref[...])
    vs = [v0_ref[...], v1_ref[...], v2_ref[...], vc_ref[...]]
    blk = k0_ref.shape[0]
    outs = []
    for hh in range(2):
        scores = [_qk(qs[hh], ks[j]) + bias_ref[hh, :, j * blk:(j + 1) * blk] for j in range(3)]
        scores.append(_qk(qs[hh], kc_ref[...]))
        outs.append(_softmax_pv(scores, vs))
    o_ref[...] = jnp.where(lo, outs[0], outs[1]).astype(o_ref.dtype)


def _plain_ctx_kernel(q_ref, kc_ref, vc_ref, o_ref):
    tq = q_ref.shape[0]
    lo = _half_masks((tq, LANES))
    qs = _split_heads(q_ref[...], lo)
    outs = [_softmax_pv([_qk(qs[hh], kc_ref[...])], [vc_ref[...]]) for hh in range(2)]
    o_ref[...] = jnp.where(lo, outs[0], outs[1]).astype(o_ref.dtype)


def _na_bias_tables(rpb, rows):
    w = GRID_W
    nk = 3 * NA_ROWS
    nh = rpb.shape[0]
    edge = w - NA_KW
    ext = jnp.concatenate([jnp.repeat(rpb[..., :1], edge, axis=-1), rpb,
                           jnp.repeat(rpb[..., -1:], edge + 1, axis=-1)], axis=-1)
    flat = jnp.tile(ext, (1, 1, w + 1))[..., :w * (2 * w + 1)]
    hankel = flat.reshape(nh, -1, w, 2 * w + 1)[..., :w]
    toep = jnp.flip(hankel, axis=-2)
    cq = np.arange(w)[:, None]
    cx = np.arange(w)[None, :]
    col_start = np.clip(cq - NA_KW // 2, 0, w - NA_KW)
    col_ok = (cx >= col_start) & (cx < col_start + NA_KW)
    toep = jnp.where(col_ok, toep * LOG2E, NEG_INF)
    masked = jnp.full((nh, w, w), NEG_INF, F32)
    tabs = []
    for r0, k0 in ((0, 0), (NA_ROWS, 0), (rows - NA_ROWS, rows - nk)):
        row_blocks = []
        for a in range(NA_ROWS):
            r = r0 + a
            rs = min(max(r - NA_KH // 2, 0), rows - NA_KH)
            blocks = []
            for e in range(nk):
                ry = k0 + e
                blocks.append(toep[:, ry - r + NA_KH - 1] if rs <= ry < rs + NA_KH else masked)
            row_blocks.append(jnp.concatenate(blocks, axis=-1))
        tabs.append(jnp.concatenate(row_blocks, axis=-2))
    return jnp.stack(tabs).astype(F32)


def _na_attention(qkv, qkvc, rpb, batch, seq, nctx, need_ctx):
    d = qkv.shape[1] // 3
    nb = d // LANES
    rows = seq // GRID_W
    tq = NA_ROWS * GRID_W
    ng = seq // tq
    bias = _na_bias_tables(rpb.astype(F32), rows)

    def cls(g):
        return jnp.where(g == 0, 0, jnp.where(g == ng - 1, 2, 1))

    def kblock(g, j):
        return jnp.clip(g - 1, 0, ng - 3) + j

    def kv_spec(col0, j):
        return pl.BlockSpec((tq, LANES), lambda hb, g, b: (b * ng + kblock(g, j), col0 + hb))

    y = pl.pallas_call(
        _na_kernel,
        out_shape=jax.ShapeDtypeStruct((batch * seq, d), BF16),
        grid=(nb, ng, batch),
        in_specs=[
            pl.BlockSpec((None, 2, tq, 3 * tq), lambda hb, g, b: (cls(g), hb, 0, 0)),
            pl.BlockSpec((tq, LANES), lambda hb, g, b: (b * ng + g, hb)),
            kv_spec(nb, 0), kv_spec(nb, 1), kv_spec(nb, 2),
            kv_spec(2 * nb, 0), kv_spec(2 * nb, 1), kv_spec(2 * nb, 2),
            pl.BlockSpec((nctx, LANES), lambda hb, g, b: (b, nb + hb)),
            pl.BlockSpec((nctx, LANES), lambda hb, g, b: (b, 2 * nb + hb)),
        ],
        out_specs=pl.BlockSpec((tq, LANES), lambda hb, g, b: (b * ng + g, hb)),
        compiler_params=_cparams("parallel", "parallel", "arbitrary"),
        name="na_attn",
    )(bias, qkv, qkv, qkv, qkv, qkv, qkv, qkv, qkvc, qkvc)
    yc = None
    if need_ctx:
        yc = pl.pallas_call(
            _plain_ctx_kernel,
            out_shape=jax.ShapeDtypeStruct((batch * nctx, d), BF16),
            grid=(batch, nb),
            in_specs=[
                pl.BlockSpec((nctx, LANES), lambda b, hb: (b, hb)),
                pl.BlockSpec((nctx, LANES), lambda b, hb: (b, nb + hb)),
                pl.BlockSpec((nctx, LANES), lambda b, hb: (b, 2 * nb + hb)),
            ],
            out_specs=pl.BlockSpec((nctx, LANES), lambda b, hb: (b, hb)),
            compiler_params=_cparams("parallel", "parallel"),
            name="na_attn_ctx",
        )(qkvc, qkvc, qkvc)
    return y, yc


def _swa_heads(q_ref, sinks_ref, g, score_fn, values, o_ref):
    tq = q_ref.shape[0]
    lo = _half_masks((tq, LANES))
    for jb in range(2):
        qs = _split_heads(q_ref[:, jb * LANES:(jb + 1) * LANES], lo)
        outs = []
        for hh in range(2):
            sink = jnp.full((1, 1), sinks_ref[g * 4 + jb * 2 + hh] * LOG2E, F32)
            outs.append(_softmax_pv(score_fn(qs[hh]), values, extra_logit=sink))
        o_ref[:, jb * LANES:(jb + 1) * LANES] = jnp.where(lo, outs[0], outs[1]).astype(o_ref.dtype)


def _swa_kernel(sinks_ref, q_ref, kl_ref, vl_ref, kc_ref, vc_ref, o_ref, *, span):
    g = pl.program_id(1)
    i = pl.program_id(2)
    tq = q_ref.shape[0]
    seq = kl_ref.shape[0]
    kstart = pl.multiple_of(jnp.clip(i * tq - SWA_WINDOW, 0, seq - span), SWA_WINDOW)
    kwin = kl_ref[pl.ds(kstart, span), :]
    vwin = vl_ref[pl.ds(kstart, span), :]
    qpos = i * tq + lax.broadcasted_iota(jnp.int32, (tq, span), 0)
    kpos = kstart + lax.broadcasted_iota(jnp.int32, (tq, span), 1)
    band = jnp.where(jnp.abs(qpos - kpos) <= SWA_WINDOW, 0.0, NEG_INF).astype(F32)
    kc = kc_ref[...]

    def score_fn(qm):
        return [_qk(qm, kwin) + band, _qk(qm, kc)]

    _swa_heads(q_ref, sinks_ref, g, score_fn, [vwin, vc_ref[...]], o_ref)


def _swa_ctx_kernel(sinks_ref, q_ref, kc_ref, vc_ref, o_ref):
    g = pl.program_id(1)
    kc = kc_ref[...]
    _swa_heads(q_ref, sinks_ref, g, lambda qm: [_qk(qm, kc)], [vc_ref[...]], o_ref)


def _swa_attention(qkv, qkvc, sinks, batch, seq, nctx, need_ctx, d):
    nkv = (qkv.shape[1] - d) // (2 * LANES)
    qb = d // nkv
    kcol = d // LANES
    tq = SWA_TQ
    nq = seq // tq
    span = tq + 2 * SWA_WINDOW
    smem = pl.BlockSpec(memory_space=pltpu.SMEM)
    sinks = sinks.astype(F32)
    y = pl.pallas_call(
        functools.partial(_swa_kernel, span=span),
        out_shape=jax.ShapeDtypeStruct((batch * seq, d), BF16),
        grid=(batch, nkv, nq),
        in_specs=[
            smem,
            pl.BlockSpec((tq, qb), lambda b, g, i: (b * nq + i, g)),
            pl.BlockSpec((seq, LANES), lambda b, g, i: (b, kcol + g)),
            pl.BlockSpec((seq, LANES), lambda b, g, i: (b, kcol + nkv + g)),
            pl.BlockSpec((nctx, LANES), lambda b, g, i: (b, kcol + g)),
            pl.BlockSpec((nctx, LANES), lambda b, g, i: (b, kcol + nkv + g)),
        ],
        out_specs=pl.BlockSpec((tq, qb), lambda b, g, i: (b * nq + i, g)),
        compiler_params=_cparams("parallel", "parallel", "arbitrary"),
        name="swa_attn",
    )(sinks, qkv, qkv, qkv, qkvc, qkvc)
    yc = None
    if need_ctx:
        yc = pl.pallas_call(
            _swa_ctx_kernel,
            out_shape=jax.ShapeDtypeStruct((batch * nctx, d), BF16),
            grid=(batch, nkv),
            in_specs=[
                smem,
                pl.BlockSpec((nctx, qb), lambda b, g: (b, g)),
                pl.BlockSpec((nctx, LANES), lambda b, g: (b, kcol + g)),
                pl.BlockSpec((nctx, LANES), lambda b, g: (b, kcol + nkv + g)),
            ],
            out_specs=pl.BlockSpec((nctx, qb), lambda b, g: (b, g)),
            compiler_params=_cparams("parallel", "parallel"),
            name="swa_attn_ctx",
        )(sinks, qkvc, qkvc, qkvc)
    return y, yc


def _oproj_kernel(o_ref, wo_ref, x_ref, g1_ref, nf_ref, sc2_ref, sh2_ref, xo_ref, h2_ref):
    y = jnp.dot(o_ref[...], wo_ref[...], preferred_element_type=F32)
    xn = x_ref[...] + g1_ref[...] * y
    xo_ref[...] = xn
    h2 = (_rms(xn) * nf_ref[...]) * (1.0 + sc2_ref[...]) + sh2_ref[...]
    h2_ref[...] = h2.astype(BF16)


def _oproj_call(o, wo, x, gamma, mod, *, tiles_per_seq, mod_row, name):
    n, d = x.shape
    t = T_PROJ

    def mrow(i):
        return mod_row if mod_row is not None else i // tiles_per_seq

    def mspec(which):
        return pl.BlockSpec((None, None, 1, d), lambda i: (mrow(i), which, 0, 0))

    return pl.pallas_call(
        _oproj_kernel,
        out_shape=(jax.ShapeDtypeStruct((n, d), F32), jax.ShapeDtypeStruct((n, d), BF16)),
        grid=(n // t,),
        in_specs=[
            pl.BlockSpec((t, d), lambda i: (i, 0)),
            pl.BlockSpec((d, d), lambda i: (0, 0)),
            pl.BlockSpec((t, d), lambda i: (i, 0)),
            mspec(2),
            pl.BlockSpec((1, d), lambda i: (0, 0)),
            mspec(4),
            mspec(3),
        ],
        out_specs=(pl.BlockSpec((t, d), lambda i: (i, 0)), pl.BlockSpec((t, d), lambda i: (i, 0))),
        compiler_params=_cparams("parallel"),
        name=name,
    )(o, wo, x, mod, gamma.reshape(1, d), mod, mod)


def _ffn_kernel(h_ref, hp_ref, hn_ref, wup_ref, cw_ref, cb_ref, wd_ref, x_ref, g2_ref, no_ref, o_ref,
                lhs, act, ubuf, *, nseq, final, f):
    i = pl.program_id(0)
    t = h_ref.shape[0]
    halo = hp_ref.shape[0]
    dff = wd_ref.shape[0]
    rows = t + 2 * halo

    keep_prev = ((i % nseq) != 0).astype(F32)
    keep_next = ((i % nseq) != nseq - 1).astype(F32)
    lhs[0:halo, :] = (hp_ref[...].astype(F32) * keep_prev).astype(BF16)
    lhs[halo:halo + t, :] = h_ref[...]
    lhs[halo + t:, :] = (hn_ref[...].astype(F32) * keep_next).astype(BF16)

    def conv(col0, slot):
        u = jnp.dot(lhs[...], wup_ref[:, col0:col0 + f], preferred_element_type=F32)
        outs = []
        for s in range(f // LANES):
            sl = slice(col0 + s * LANES, col0 + (s + 1) * LANES)
            buf = ubuf.at[slot, s]
            buf[pl.ds(0, rows, stride=2), :] = u[:, s * LANES:(s + 1) * LANES]
            taps = [buf[pl.ds(2 * (halo - 1 + k), t, stride=2), :] for k in range(CONV_W)]
            outs.append(cb_ref[:, sl] + taps[0] * cw_ref[0:1, sl] + taps[1] * cw_ref[1:2, sl]
                        + taps[2] * cw_ref[2:3, sl])
        return jnp.concatenate(outs, axis=1)

    for c in range(dff // f):
        slot = 2 * (c % 2)
        a = conv(c * f, slot)
        g = conv(dff + c * f, slot + 1)
        act[:, c * f:(c + 1) * f] = ((g * (1.0 / (1.0 + jnp.exp(-g)))) * a).astype(BF16)

    xn = x_ref[...] + g2_ref[...] * jnp.dot(act[...], wd_ref[...], preferred_element_type=F32)
    if final:
        xn = _rms(xn) * no_ref[...]
    o_ref[...] = xn


def _ffn_call(h2, x, w_up, w_conv, b_conv, w_down, mod, norm_out, *, t, tiles_per_seq, mod_row, final, name):
    n, d = x.shape
    dff = w_down.shape[0]
    f = FF_CHUNK
    halo = BF16_ROWS
    hb = t // halo
    last_hblock = n // halo - 1
    resident = pl.Buffered(1)

    def mrow(i):
        return mod_row if mod_row is not None else i // tiles_per_seq

    return pl.pallas_call(
        functools.partial(_ffn_kernel, nseq=tiles_per_seq, final=final, f=f),
        out_shape=jax.ShapeDtypeStruct((n, d), F32),
        grid=(n // t,),
        in_specs=[
            pl.BlockSpec((t, d), lambda i: (i, 0)),
            pl.BlockSpec((halo, d), lambda i: (jnp.maximum(i * hb - 1, 0), 0)),
            pl.BlockSpec((halo, d), lambda i: (jnp.minimum((i + 1) * hb, last_hblock), 0)),
            pl.BlockSpec((d, 2 * dff), lambda i: (0, 0), pipeline_mode=resident),
            pl.BlockSpec((CONV_W, 2 * dff), lambda i: (0, 0), pipeline_mode=resident),
            pl.BlockSpec((1, 2 * dff), lambda i: (0, 0), pipeline_mode=resident),
            pl.BlockSpec((dff, d), lambda i: (0, 0), pipeline_mode=resident),
            pl.BlockSpec((t, d), lambda i: (i, 0)),
            pl.BlockSpec((None, None, 1, d), lambda i: (mrow(i), 5, 0, 0)),
            pl.BlockSpec((1, d), lambda i: (0, 0)),
        ],
        out_specs=pl.BlockSpec((t, d), lambda i: (i, 0)),
        scratch_shapes=[pltpu.VMEM((t + 2 * halo, d), BF16), pltpu.VMEM((t, dff), BF16),
                        pltpu.VMEM((4, f // LANES, 2 * (t + 2 * halo), LANES), F32)],
        compiler_params=_cparams("parallel"),
        name=name,
    )(h2, h2, h2, w_up, w_conv, b_conv.reshape(1, 2 * dff), w_down, x, mod, norm_out.reshape(1, d))


def _rope_tables(seq, extra_rows):
    tpos = jnp.arange(seq, dtype=jnp.int32)
    row = (tpos // GRID_W).astype(F32)
    col = (tpos % GRID_W).astype(F32)
    inv = ROPE_BASE ** (-jnp.arange(ROPE_PAIRS, dtype=F32) / ROPE_PAIRS)
    ar = row[:, None] * inv[None, :]
    ac = col[:, None] * inv[None, :]
    cr, sr, cc, sc = jnp.cos(ar), jnp.sin(ar), jnp.cos(ac), jnp.sin(ac)
    cos = jnp.concatenate([cr, cr, cc, cc] * (LANES // HEAD_DIM), axis=1)
    sin = jnp.concatenate([-sr, sr, -sc, sc] * (LANES // HEAD_DIM), axis=1)
    cos = jnp.concatenate([cos, jnp.ones((extra_rows, LANES), F32)], axis=0)
    sin = jnp.concatenate([sin, jnp.zeros((extra_rows, LANES), F32)], axis=0)
    return cos * Q_SCALE, sin * Q_SCALE, cos, sin


def _dup_kv_columns(w, d, nkv):
    q = w[:, :d]
    k = w[:, d:d + nkv * HEAD_DIM].reshape(-1, nkv, 1, HEAD_DIM)
    v = w[:, d + nkv * HEAD_DIM:].reshape(-1, nkv, 1, HEAD_DIM)
    kd = jnp.broadcast_to(k, (w.shape[0], nkv, 2, HEAD_DIM)).reshape(w.shape[0], -1)
    vd = jnp.broadcast_to(v, (w.shape[0], nkv, 2, HEAD_DIM)).reshape(w.shape[0], -1)
    return jnp.concatenate([q, kd, vd], axis=1)


def kernel(x, c, ctx, c_ctx, ada_w, ada_b, norm_mix, norm_ffn, norm_out, ffn_up, ffn_conv, ffn_conv_b,
           ffn_down, a_wqkv, a_wo, a_lambda, a_subln, b_wqkv, b_wo, b_rpb, c_wqkv, c_wo, c_sinks):
    batch, seq, d = x.shape
    nctx = ctx.shape[1]
    depth = ada_w.shape[0]
    assert seq % T_FFN == 0 and seq % T_PROJ == 0 and (batch * nctx) % T_PROJ == 0

    mod_rows = 16
    cs = jnp.concatenate([c, c_ctx[None, :], jnp.zeros((mod_rows - batch - 1, d), F32)], axis=0)
    mod_all = _ada_call(cs, ada_w, ada_b).reshape(depth, mod_rows, 6, 1, d)
    ctx_row = batch

    tables = _rope_tables(seq, T_PROJ)
    lat_tiles = seq // T_PROJ
    nd = d // LANES

    xl = x.reshape(batch * seq, d)
    xc = ctx.reshape(batch * nctx, d)
    for i in range(depth):
        need_ctx = i < depth - 1
        kind, j = i % N_MIXERS, i // N_MIXERS
        mod = mod_all[i]
        if kind == 0:
            w = a_wqkv[j].astype(BF16)
            wo = a_wo[j].astype(BF16)
            kinds = [ROPEQ] * nd + [ROPEK] * nd + [PLAIN] * nd
            tabs = tables
        elif kind == 1:
            w = b_wqkv[j].astype(BF16)
            wo = b_wo[j].astype(BF16)
            kinds = [SCALEQ] * nd + [PLAIN] * (2 * nd)
            tabs = None
        else:
            nkv = (c_wqkv.shape[2] - d) // (2 * HEAD_DIM)
            w = _dup_kv_columns(c_wqkv[j], d, nkv).astype(BF16)
            wo = c_wo[j].astype(BF16)
            kinds = [ROPEQ] * nd + [ROPEK] * nkv + [PLAIN] * nkv
            tabs = tables
        qkv = _proj_call(xl, norm_mix[i], mod, w, kinds, tabs, tiles_per_seq=lat_tiles, mod_row=None,
                         name=f"proj{i}")
        qkvc = _proj_call(xc, norm_mix[i], mod, w, kinds, tabs, tiles_per_seq=lat_tiles, mod_row=ctx_row,
                          name=f"proj_ctx{i}")
        if kind == 0:
            lam_init = 0.8 - 0.6 * math.exp(-0.3 * i)
            y, yc = _diff_attention(qkv, qkvc, a_lambda[j].astype(F32), a_subln[j], lam_init,
                                    batch, seq, nctx, need_ctx)
        elif kind == 1:
            y, yc = _na_attention(qkv, qkvc, b_rpb[j], batch, seq, nctx, need_ctx)
        else:
            y, yc = _swa_attention(qkv, qkvc, c_sinks[j], batch, seq, nctx, need_ctx, d)

        w_up = ffn_up[i].astype(BF16)
        w_down = ffn_down[i].astype(BF16)
        xl, h2 = _oproj_call(y, wo, xl, norm_ffn[i], mod, tiles_per_seq=lat_tiles, mod_row=None,
                             name=f"oproj{i}")
        xl = _ffn_call(h2, xl, w_up, ffn_conv[i], ffn_conv_b[i], w_down, mod, norm_out,
                       t=T_FFN, tiles_per_seq=seq // T_FFN, mod_row=None, final=not need_ctx,
                       name=f"ffn{i}")
        if need_ctx:
            xc, h2c = _oproj_call(yc, wo, xc, norm_ffn[i], mod, tiles_per_seq=1, mod_row=ctx_row,
                                  name=f"oproj_ctx{i}")
            xc = _ffn_call(h2c, xc, w_up, ffn_conv[i], ffn_conv_b[i], w_down, mod, norm_out,
                           t=nctx, tiles_per_seq=1, mod_row=ctx_row, final=False, name=f"ffn_ctx{i}")
    return xl.reshape(batch, seq, d)
```

```python
import functools
import math

import numpy as np
import jax
import jax.numpy as jnp
from jax import lax
from jax.experimental import pallas as pl
from jax.experimental.pallas import tpu as pltpu

F32 = jnp.float32
BF16 = jnp.bfloat16

GRID_W = 64
HEAD_DIM = 64
ROPE_BASE = 10000.0
ROPE_PAIRS = HEAD_DIM // 4
NORM_EPS = 1e-6
NEG_INF = -1e30
N_MIXERS = 3
NA_KH = 8
NA_KW = 16
SWA_WINDOW = 128
CONV_W = 3
LOG2E = 1.4426950408889634
Q_SCALE = (HEAD_DIM ** -0.5) * LOG2E

LANES = 128
BF16_ROWS = 16
VMEM_LIMIT = 52 * 1024 * 1024

T_PROJ = 512
T_FFN = 512
FF_CHUNK = 256
NA_ROWS = 4
SWA_TQ = 256
DIFF_TQ = 256
DIFF_KC = 256

PLAIN, SCALEQ, ROPEQ, ROPEK = 0, 1, 2, 3


def _cparams(*sem):
    return pltpu.CompilerParams(dimension_semantics=sem, vmem_limit_bytes=VMEM_LIMIT)


def _rms(x):
    return x * lax.rsqrt(jnp.mean(x * x, axis=-1, keepdims=True) + NORM_EPS)


def _ada_kernel(cs_ref, w_ref, b_ref, o_ref):
    cs = cs_ref[...]
    s = cs * (1.0 / (1.0 + jnp.exp(-cs)))
    w = w_ref[...]
    s_hi = s.astype(BF16)
    s_lo = (s - s_hi.astype(F32)).astype(BF16)
    w_hi = w.astype(BF16)
    w_lo = (w - w_hi.astype(F32)).astype(BF16)
    acc = jnp.dot(s_hi, w_hi, preferred_element_type=F32)
    acc = acc + jnp.dot(s_hi, w_lo, preferred_element_type=F32)
    acc = acc + jnp.dot(s_lo, w_hi, preferred_element_type=F32)
    o_ref[...] = acc + b_ref[...]


def _ada_call(cs, ada_w, ada_b):
    depth, d, n = ada_w.shape
    nt = 1536
    rows = cs.shape[0]
    return pl.pallas_call(
        _ada_kernel,
        out_shape=jax.ShapeDtypeStruct((depth, rows, n), F32),
        grid=(depth, n // nt),
        in_specs=[
            pl.BlockSpec((rows, d), lambda l, j: (0, 0)),
            pl.BlockSpec((None, d, nt), lambda l, j: (l, 0, j)),
            pl.BlockSpec((None, 1, nt), lambda l, j: (l, 0, j)),
        ],
        out_specs=pl.BlockSpec((None, rows, nt), lambda l, j: (l, 0, j)),
        compiler_params=_cparams("parallel", "parallel"),
        name="ada_mod",
    )(cs, ada_w, ada_b.reshape(depth, 1, n))


def _rope(r, c, s, first_half):
    partner = jnp.where(first_half, pltpu.roll(r, LANES - 16, 1), pltpu.roll(r, 16, 1))
    return r * c + partner * s


def _proj_kernel(*refs, kinds, nc, has_rope):
    if has_rope:
        x_ref, g_ref, sc_ref, sh_ref, w_ref, cq_ref, sq_ref, ck_ref, sk_ref, o_ref = refs
    else:
        x_ref, g_ref, sc_ref, sh_ref, w_ref, o_ref = refs
    t = x_ref.shape[0]
    nout = o_ref.shape[1]
    h = (_rms(x_ref[...]) * g_ref[...]) * (1.0 + sc_ref[...]) + sh_ref[...]
    hb = h.astype(BF16)
    if has_rope:
        lane = lax.broadcasted_iota(jnp.int32, (t, LANES), 1)
        first_half = (lane & 16) == 0
    for n0 in range(0, nout, nc):
        r = jnp.dot(hb, w_ref[:, n0:n0 + nc], preferred_element_type=F32)
        for j in range(nc // LANES):
            kind = kinds[n0 // LANES + j]
            rj = r[:, j * LANES:(j + 1) * LANES]
            if kind == ROPEQ:
                rj = _rope(rj, cq_ref[...], sq_ref[...], first_half)
            elif kind == ROPEK:
                rj = _rope(rj, ck_ref[...], sk_ref[...], first_half)
            elif kind == SCALEQ:
                rj = rj * Q_SCALE
            o_ref[:, n0 + j * LANES:n0 + (j + 1) * LANES] = rj.astype(BF16)


def _proj_call(x, gamma, mod, w, kinds, tables, *, tiles_per_seq, mod_row, name):
    n, d = x.shape
    nout = w.shape[1]
    t = T_PROJ
    has_rope = tables is not None

    def mrow(i):
        return mod_row if mod_row is not None else i // tiles_per_seq

    def tblock(i):
        return i % tiles_per_seq if mod_row is None else tiles_per_seq

    in_specs = [
        pl.BlockSpec((t, d), lambda i: (i, 0)),
        pl.BlockSpec((1, d), lambda i: (0, 0)),
        pl.BlockSpec((None, None, 1, d), lambda i: (mrow(i), 1, 0, 0)),
        pl.BlockSpec((None, None, 1, d), lambda i: (mrow(i), 0, 0, 0)),
        pl.BlockSpec((d, nout), lambda i: (0, 0)),
    ]
    args = [x, gamma.reshape(1, d), mod, mod, w]
    if has_rope:
        for tab in tables:
            in_specs.append(pl.BlockSpec((t, LANES), lambda i: (tblock(i), 0)))
            args.append(tab)
    return pl.pallas_call(
        functools.partial(_proj_kernel, kinds=tuple(kinds), nc=512, has_rope=has_rope),
        out_shape=jax.ShapeDtypeStruct((n, nout), BF16),
        grid=(n // t,),
        in_specs=in_specs,
        out_specs=pl.BlockSpec((t, nout), lambda i: (i, 0)),
        compiler_params=_cparams("parallel"),
        name=name,
    )(*args)


def _qk(q, k):
    return lax.dot_general(q, k, (((1,), (1,)), ((), ())), preferred_element_type=F32)


def _half_masks(shape):
    lane = lax.broadcasted_iota(jnp.int32, shape, 1)
    return lane < HEAD_DIM


def _split_heads(q, lo):
    qf = q.astype(F32)
    zero = jnp.zeros_like(qf)
    return jnp.where(lo, qf, zero).astype(q.dtype), jnp.where(lo, zero, qf).astype(q.dtype)


def _diff_lambda(lam_ref, lam_init):
    lv = lam_ref[...]
    d1 = jnp.sum(lv[0:1] * lv[1:2], axis=1, keepdims=True)
    d2 = jnp.sum(lv[2:3] * lv[3:4], axis=1, keepdims=True)
    return jnp.exp(d1) - jnp.exp(d2) + lam_init


def _diff_finish(o1, o2, lam, subln_ref, lam_init, o_ref):
    o = _rms(o1 - lam * o2) * subln_ref[...] * (1.0 - lam_init)
    o_ref[...] = o.astype(o_ref.dtype)


def _diff_kernel(lam_ref, subln_ref, q_ref, kl_ref, vl_ref, kc_ref, vc_ref, o_ref,
                 vaug, s_scr, p_scr, m_scr, *, lam_init, tq, kc):
    seq = q_ref.shape[0]
    nkeys = vaug.shape[0]
    nlat = seq // kc
    nchunk = nkeys // kc
    ntile = seq // tq
    lam = _diff_lambda(lam_ref, lam_init)
    lo = _half_masks((tq, LANES))

    vaug[0:seq, 0:LANES] = vl_ref[...]
    vaug[seq:, 0:LANES] = vc_ref[...]
    vaug[:, LANES:] = jnp.ones((nkeys, LANES), BF16)

    def kchunk(c):
        if c < nlat:
            return kl_ref[c * kc:(c + 1) * kc, :]
        return kc_ref[(c - nlat) * kc:(c - nlat + 1) * kc, :]

    def half(m, tile, do_qk, do_exp, do_pv):
        cols = lambda c: slice(c * kc, (c + 1) * kc)
        if do_qk:
            q = q_ref[pl.ds(pl.multiple_of(tile * tq, tq), tq), :].astype(F32)
            zero = jnp.zeros_like(q)
            qm = (jnp.where(lo, q, zero) if m == 0 else jnp.where(lo, zero, q)).astype(BF16)
            mrun = jnp.full((tq, LANES), NEG_INF, F32)
        if do_exp:
            mb = jnp.concatenate([m_scr[1 - m]] * (kc // LANES), axis=1)
        if do_pv:
            acc = jnp.zeros((tq, 2 * LANES), F32)
        for c in range(nchunk):
            if do_qk:
                s = _qk(qm, kchunk(c))
                s_scr[m, :, cols(c)] = s
                for u in range(kc // LANES):
                    mrun = jnp.maximum(mrun, s[:, u * LANES:(u + 1) * LANES])
            if do_exp:
                p_scr[1 - m, :, cols(c)] = jnp.exp2(s_scr[1 - m, :, cols(c)] - mb).astype(BF16)
            if do_pv:
                acc = acc + jnp.dot(p_scr[m, :, cols(c)], vaug[cols(c), :], preferred_element_type=F32)
        if do_qk:
            m_scr[m] = jnp.broadcast_to(jnp.max(mrun, axis=1, keepdims=True), (tq, LANES))
        if do_pv:
            return acc[:, :LANES] * (1.0 / acc[:, LANES:])
        return None

    def emit(tile, o1, o2):
        o = _rms(o1 - lam * o2) * subln_ref[...] * (1.0 - lam_init)
        o_ref[pl.ds(pl.multiple_of(tile * tq, tq), tq), :] = o.astype(o_ref.dtype)

    half(0, 0, True, False, False)
    half(1, 0, True, True, False)

    def body(t, carry):
        o1 = half(0, t, True, True, True)
        o2 = half(1, t, True, True, True)
        emit(t - 1, o1, o2)
        return carry

    lax.fori_loop(1, ntile, body, 0)
    o1 = half(0, 0, False, True, True)
    o2 = half(1, 0, False, False, True)
    emit(ntile - 1, o1, o2)


def _diff_ctx_kernel(lam_ref, subln_ref, q_ref, kc_ref, vc_ref, o_ref, *, lam_init):
    tq = q_ref.shape[0]
    lam = _diff_lambda(lam_ref, lam_init)
    qs = _split_heads(q_ref[...], _half_masks((tq, LANES)))
    o1, o2 = [_softmax_pv([_qk(qm, kc_ref[...])], [vc_ref[...]]) for qm in qs]
    _diff_finish(o1, o2, lam, subln_ref, lam_init, o_ref)


def _diff_attention(qkv, qkvc, lam_vecs, subln, lam_init, batch, seq, nctx, need_ctx):
    d = qkv.shape[1] // 3
    nh = d // LANES
    tq = DIFF_TQ
    kc = DIFF_KC
    assert seq % tq == 0 and seq % kc == 0 and nctx % kc == 0 and seq // tq >= 2
    nkeys = seq + nctx
    subln2 = subln.reshape(1, LANES)
    y = pl.pallas_call(
        functools.partial(_diff_kernel, lam_init=lam_init, tq=tq, kc=kc),
        out_shape=jax.ShapeDtypeStruct((batch * seq, d), BF16),
        grid=(batch, nh),
        in_specs=[
            pl.BlockSpec((4, HEAD_DIM), lambda b, h: (0, 0)),
            pl.BlockSpec((1, LANES), lambda b, h: (0, 0)),
            pl.BlockSpec((seq, LANES), lambda b, h: (b, h)),
            pl.BlockSpec((seq, LANES), lambda b, h: (b, nh + h)),
            pl.BlockSpec((seq, LANES), lambda b, h: (b, 2 * nh + h)),
            pl.BlockSpec((nctx, LANES), lambda b, h: (b, nh + h)),
            pl.BlockSpec((nctx, LANES), lambda b, h: (b, 2 * nh + h)),
        ],
        out_specs=pl.BlockSpec((seq, LANES), lambda b, h: (b, h)),
        scratch_shapes=[pltpu.VMEM((nkeys, 2 * LANES), BF16), pltpu.VMEM((2, tq, nkeys), F32),
                        pltpu.VMEM((2, tq, nkeys), BF16), pltpu.VMEM((2, tq, LANES), F32)],
        compiler_params=_cparams("parallel", "parallel"),
        name="diff_attn",
    )(lam_vecs, subln2, qkv, qkv, qkv, qkvc, qkvc)
    yc = None
    if need_ctx:
        yc = pl.pallas_call(
            functools.partial(_diff_ctx_kernel, lam_init=lam_init),
            out_shape=jax.ShapeDtypeStruct((batch * nctx, d), BF16),
            grid=(batch, nh),
            in_specs=[
                pl.BlockSpec((4, HEAD_DIM), lambda b, h: (0, 0)),
                pl.BlockSpec((1, LANES), lambda b, h: (0, 0)),
                pl.BlockSpec((nctx, LANES), lambda b, h: (b, h)),
                pl.BlockSpec((nctx, LANES), lambda b, h: (b, nh + h)),
                pl.BlockSpec((nctx, LANES), lambda b, h: (b, 2 * nh + h)),
            ],
            out_specs=pl.BlockSpec((nctx, LANES), lambda b, h: (b, h)),
            compiler_params=_cparams("parallel", "parallel"),
            name="diff_attn_ctx",
        )(lam_vecs, subln2, qkvc, qkvc, qkvc)
    return y, yc


def _softmax_pv(scores, values, extra_logit=None):
    m = jnp.max(scores[0], axis=1, keepdims=True)
    for s in scores[1:]:
        m = jnp.maximum(m, jnp.max(s, axis=1, keepdims=True))
    if extra_logit is not None:
        m = jnp.maximum(m, extra_logit)
    l = None
    acc = None
    for s, v in zip(scores, values):
        p = jnp.exp2(s - m)
        ps = jnp.sum(p, axis=1, keepdims=True)
        pv = jnp.dot(p.astype(BF16), v, preferred_element_type=F32)
        l = ps if l is None else l + ps
        acc = pv if acc is None else acc + pv
    if extra_logit is not None:
        l = l + jnp.exp2(extra_logit - m)
    return acc * (1.0 / l)


def _na_kernel(bias_ref, q_ref, k0_ref, k1_ref, k2_ref, v0_ref, v1_ref, v2_ref, kc_ref, vc_ref, o_ref):
    tq = q_ref.shape[0]
    lo = _half_masks((tq, LANES))
    qs = _split_heads(q_ref[...], lo)
    ks = (k0_ref[...], k1_ref[...], k2_ref[...])
    vs = [v0_ref[...], v1_ref[...], v2_ref[...], vc_ref[...]]
    blk = k0_ref.shape[0]
    outs = []
    for hh in range(2):
        scores = [_qk(qs[hh], ks[j]) + bias_ref[hh, :, j * blk:(j + 1) * blk] for j in range(3)]
        scores.append(_qk(qs[hh], kc_ref[...]))
        outs.append(_softmax_pv(scores, vs))
    o_ref[...] = jnp.where(lo, outs[0], outs[1]).astype(o_ref.dtype)


def _plain_ctx_kernel(q_ref, kc_ref, vc_ref, o_ref):
    tq = q_ref.shape[0]
    lo = _half_masks((tq, LANES))
    qs = _split_heads(q_ref[...], lo)
    outs = [_softmax_pv([_qk(qs[hh], kc_ref[...])], [vc_ref[...]]) for hh in range(2)]
    o_ref[...] = jnp.where(lo, outs[0], outs[1]).astype(o_ref.dtype)


def _na_bias_tables(rpb, rows):
    w = GRID_W
    nk = 3 * NA_ROWS
    nh = rpb.shape[0]
    edge = w - NA_KW
    ext = jnp.concatenate([jnp.repeat(rpb[..., :1], edge, axis=-1), rpb,
                           jnp.repeat(rpb[..., -1:], edge + 1, axis=-1)], axis=-1)
    flat = jnp.tile(ext, (1, 1, w + 1))[..., :w * (2 * w + 1)]
    hankel = flat.reshape(nh, -1, w, 2 * w + 1)[..., :w]
    toep = jnp.flip(hankel, axis=-2)
    cq = np.arange(w)[:, None]
    cx = np.arange(w)[None, :]
    col_start = np.clip(cq - NA_KW // 2, 0, w - NA_KW)
    col_ok = (cx >= col_start) & (cx < col_start + NA_KW)
    toep = jnp.where(col_ok, toep * LOG2E, NEG_INF)
    masked = jnp.full((nh, w, w), NEG_INF, F32)
    tabs = []
    for r0, k0 in ((0, 0), (NA_ROWS, 0), (rows - NA_ROWS, rows - nk)):
        row_blocks = []
        for a in range(NA_ROWS):
            r = r0 + a
            rs = min(max(r - NA_KH // 2, 0), rows - NA_KH)
            blocks = []
            for e in range(nk):
                ry = k0 + e
                blocks.append(toep[:, ry - r + NA_KH - 1] if rs <= ry < rs + NA_KH else masked)
            row_blocks.append(jnp.concatenate(blocks, axis=-1))
        tabs.append(jnp.concatenate(row_blocks, axis=-2))
    return jnp.stack(tabs).astype(F32)


def _na_attention(qkv, qkvc, rpb, batch, seq, nctx, need_ctx):
    d = qkv.shape[1] // 3
    nb = d // LANES
    rows = seq // GRID_W
    tq = NA_ROWS * GRID_W
    ng = seq // tq
    bias = _na_bias_tables(rpb.astype(F32), rows)

    def cls(g):
        return jnp.where(g == 0, 0, jnp.where(g == ng - 1, 2, 1))

    def kblock(g, j):
        return jnp.clip(g - 1, 0, ng - 3) + j

    def kv_spec(col0, j):
        return pl.BlockSpec((tq, LANES), lambda hb, g, b: (b * ng + kblock(g, j), col0 + hb))

    y = pl.pallas_call(
        _na_kernel,
        out_shape=jax.ShapeDtypeStruct((batch * seq, d), BF16),
        grid=(nb, ng, batch),
        in_specs=[
            pl.BlockSpec((None, 2, tq, 3 * tq), lambda hb, g, b: (cls(g), hb, 0, 0)),
            pl.BlockSpec((tq, LANES), lambda hb, g, b: (b * ng + g, hb)),
            kv_spec(nb, 0), kv_spec(nb, 1), kv_spec(nb, 2),
            kv_spec(2 * nb, 0), kv_spec(2 * nb, 1), kv_spec(2 * nb, 2),
            pl.BlockSpec((nctx, LANES), lambda hb, g, b: (b, nb + hb)),
            pl.BlockSpec((nctx, LANES), lambda hb, g, b: (b, 2 * nb + hb)),
        ],
        out_specs=pl.BlockSpec((tq, LANES), lambda hb, g, b: (b * ng + g, hb)),
        compiler_params=_cparams("parallel", "parallel", "arbitrary"),
        name="na_attn",
    )(bias, qkv, qkv, qkv, qkv, qkv, qkv, qkv, qkvc, qkvc)
    yc = None
    if need_ctx:
        yc = pl.pallas_call(
            _plain_ctx_kernel,
            out_shape=jax.ShapeDtypeStruct((batch * nctx, d), BF16),
            grid=(batch, nb),
            in_specs=[
                pl.BlockSpec((nctx, LANES), lambda b, hb: (b, hb)),
                pl.BlockSpec((nctx, LANES), lambda b, hb: (b, nb + hb)),
                pl.BlockSpec((nctx, LANES), lambda b, hb: (b, 2 * nb + hb)),
            ],
            out_specs=pl.BlockSpec((nctx, LANES), lambda b, hb: (b, hb)),
            compiler_params=_cparams("parallel", "parallel"),
            name="na_attn_ctx",
        )(qkvc, qkvc, qkvc)
    return y, yc


def _swa_heads(q_ref, sinks_ref, g, score_fn, values, o_ref):
    tq = q_ref.shape[0]
    lo = _half_masks((tq, LANES))
    for jb in range(2):
        qs = _split_heads(q_ref[:, jb * LANES:(jb + 1) * LANES], lo)
        outs = []
        for hh in range(2):
            sink = jnp.full((1, 1), sinks_ref[g * 4 + jb * 2 + hh] * LOG2E, F32)
            outs.append(_softmax_pv(score_fn(qs[hh]), values, extra_logit=sink))
        o_ref[:, jb * LANES:(jb + 1) * LANES] = jnp.where(lo, outs[0], outs[1]).astype(o_ref.dtype)


def _swa_kernel(sinks_ref, q_ref, kl_ref, vl_ref, kc_ref, vc_ref, o_ref, *, span):
    g = pl.program_id(1)
    i = pl.program_id(2)
    tq = q_ref.shape[0]
    seq = kl_ref.shape[0]
    kstart = pl.multiple_of(jnp.clip(i * tq - SWA_WINDOW, 0, seq - span), SWA_WINDOW)
    kwin = kl_ref[pl.ds(kstart, span), :]
    vwin = vl_ref[pl.ds(kstart, span), :]
    qpos = i * tq + lax.broadcasted_iota(jnp.int32, (tq, span), 0)
    kpos = kstart + lax.broadcasted_iota(jnp.int32, (tq, span), 1)
    band = jnp.where(jnp.abs(qpos - kpos) <= SWA_WINDOW, 0.0, NEG_INF).astype(F32)
    kc = kc_ref[...]

    def score_fn(qm):
        return [_qk(qm, kwin) + band, _qk(qm, kc)]

    _swa_heads(q_ref, sinks_ref, g, score_fn, [vwin, vc_ref[...]], o_ref)


def _swa_ctx_kernel(sinks_ref, q_ref, kc_ref, vc_ref, o_ref):
    g = pl.program_id(1)
    kc = kc_ref[...]
    _swa_heads(q_ref, sinks_ref, g, lambda qm: [_qk(qm, kc)], [vc_ref[...]], o_ref)


def _swa_attention(qkv, qkvc, sinks, batch, seq, nctx, need_ctx, d):
    nkv = (qkv.shape[1] - d) // (2 * LANES)
    qb = d // nkv
    kcol = d // LANES
    tq = SWA_TQ
    nq = seq // tq
    span = tq + 2 * SWA_WINDOW
    smem = pl.BlockSpec(memory_space=pltpu.SMEM)
    sinks = sinks.astype(F32)
    y = pl.pallas_call(
        functools.partial(_swa_kernel, span=span),
        out_shape=jax.ShapeDtypeStruct((batch * seq, d), BF16),
        grid=(batch, nkv, nq),
        in_specs=[
            smem,
            pl.BlockSpec((tq, qb), lambda b, g, i: (b * nq + i, g)),
            pl.BlockSpec((seq, LANES), lambda b, g, i: (b, kcol + g)),
            pl.BlockSpec((seq, LANES), lambda b, g, i: (b, kcol + nkv + g)),
            pl.BlockSpec((nctx, LANES), lambda b, g, i: (b, kcol + g)),
            pl.BlockSpec((nctx, LANES), lambda b, g, i: (b, kcol + nkv + g)),
        ],
        out_specs=pl.BlockSpec((tq, qb), lambda b, g, i: (b * nq + i, g)),
        compiler_params=_cparams("parallel", "parallel", "arbitrary"),
        name="swa_attn",
    )(sinks, qkv, qkv, qkv, qkvc, qkvc)
    yc = None
    if need_ctx:
        yc = pl.pallas_call(
            _swa_ctx_kernel,
            out_shape=jax.ShapeDtypeStruct((batch * nctx, d), BF16),
            grid=(batch, nkv),
            in_specs=[
                smem,
                pl.BlockSpec((nctx, qb), lambda b, g: (b, g)),
                pl.BlockSpec((nctx, LANES), lambda b, g: (b, kcol + g)),
                pl.BlockSpec((nctx, LANES), lambda b, g: (b, kcol + nkv + g)),
            ],
            out_specs=pl.BlockSpec((nctx, qb), lambda b, g: (b, g)),
            compiler_params=_cparams("parallel", "parallel"),
            name="swa_attn_ctx",
        )(sinks, qkvc, qkvc, qkvc)
    return y, yc


def _oproj_kernel(o_ref, wo_ref, x_ref, g1_ref, nf_ref, sc2_ref, sh2_ref, xo_ref, h2_ref):
    y = jnp.dot(o_ref[...], wo_ref[...], preferred_element_type=F32)
    xn = x_ref[...] + g1_ref[...] * y
    xo_ref[...] = xn
    h2 = (_rms(xn) * nf_ref[...]) * (1.0 + sc2_ref[...]) + sh2_ref[...]
    h2_ref[...] = h2.astype(BF16)


def _oproj_call(o, wo, x, gamma, mod, *, tiles_per_seq, mod_row, name):
    n, d = x.shape
    t = T_PROJ

    def mrow(i):
        return mod_row if mod_row is not None else i // tiles_per_seq

    def mspec(which):
        return pl.BlockSpec((None, None, 1, d), lambda i: (mrow(i), which, 0, 0))

    return pl.pallas_call(
        _oproj_kernel,
        out_shape=(jax.ShapeDtypeStruct((n, d), F32), jax.ShapeDtypeStruct((n, d), BF16)),
        grid=(n // t,),
        in_specs=[
            pl.BlockSpec((t, d), lambda i: (i, 0)),
            pl.BlockSpec((d, d), lambda i: (0, 0)),
            pl.BlockSpec((t, d), lambda i: (i, 0)),
            mspec(2),
            pl.BlockSpec((1, d), lambda i: (0, 0)),
            mspec(4),
            mspec(3),
        ],
        out_specs=(pl.BlockSpec((t, d), lambda i: (i, 0)), pl.BlockSpec((t, d), lambda i: (i, 0))),
        compiler_params=_cparams("parallel"),
        name=name,
    )(o, wo, x, mod, gamma.reshape(1, d), mod, mod)


def _ffn_kernel(h_ref, hp_ref, hn_ref, wup_ref, cw_ref, cb_ref, wd_ref, x_ref, g2_ref, no_ref, o_ref,
                lhs, act, ubuf, *, nseq, final, f):
    i = pl.program_id(0)
    t = h_ref.shape[0]
    halo = hp_ref.shape[0]
    dff = wd_ref.shape[0]
    rows = t + 2 * halo

    keep_prev = ((i % nseq) != 0).astype(F32)
    keep_next = ((i % nseq) != nseq - 1).astype(F32)
    lhs[0:halo, :] = (hp_ref[...].astype(F32) * keep_prev).astype(BF16)
    lhs[halo:halo + t, :] = h_ref[...]
    lhs[halo + t:, :] = (hn_ref[...].astype(F32) * keep_next).astype(BF16)

    def conv(col0, slot):
        u = jnp.dot(lhs[...], wup_ref[:, col0:col0 + f], preferred_element_type=F32)
        outs = []
        for s in range(f // LANES):
            sl = slice(col0 + s * LANES, col0 + (s + 1) * LANES)
            buf = ubuf.at[slot, s]
            buf[pl.ds(0, rows, stride=2), :] = u[:, s * LANES:(s + 1) * LANES]
            taps = [buf[pl.ds(2 * (halo - 1 + k), t, stride=2), :] for k in range(CONV_W)]
            outs.append(cb_ref[:, sl] + taps[0] * cw_ref[0:1, sl] + taps[1] * cw_ref[1:2, sl]
                        + taps[2] * cw_ref[2:3, sl])
        return jnp.concatenate(outs, axis=1)

    for c in range(dff // f):
        slot = 2 * (c % 2)
        a = conv(c * f, slot)
        g = conv(dff + c * f, slot + 1)
        act[:, c * f:(c + 1) * f] = ((g * (1.0 / (1.0 + jnp.exp(-g)))) * a).astype(BF16)

    xn = x_ref[...] + g2_ref[...] * jnp.dot(act[...], wd_ref[...], preferred_element_type=F32)
    if final:
        xn = _rms(xn) * no_ref[...]
    o_ref[...] = xn


def _ffn_call(h2, x, w_up, w_conv, b_conv, w_down, mod, norm_out, *, t, tiles_per_seq, mod_row, final, name):
    n, d = x.shape
    dff = w_down.shape[0]
    f = FF_CHUNK
    halo = BF16_ROWS
    hb = t // halo
    last_hblock = n // halo - 1
    resident = pl.Buffered(1)

    def mrow(i):
        return mod_row if mod_row is not None else i // tiles_per_seq

    return pl.pallas_call(
        functools.partial(_ffn_kernel, nseq=tiles_per_seq, final=final, f=f),
        out_shape=jax.ShapeDtypeStruct((n, d), F32),
        grid=(n // t,),
        in_specs=[
            pl.BlockSpec((t, d), lambda i: (i, 0)),
            pl.BlockSpec((halo, d), lambda i: (jnp.maximum(i * hb - 1, 0), 0)),
            pl.BlockSpec((halo, d), lambda i: (jnp.minimum((i + 1) * hb, last_hblock), 0)),
            pl.BlockSpec((d, 2 * dff), lambda i: (0, 0), pipeline_mode=resident),
            pl.BlockSpec((CONV_W, 2 * dff), lambda i: (0, 0), pipeline_mode=resident),
            pl.BlockSpec((1, 2 * dff), lambda i: (0, 0), pipeline_mode=resident),
            pl.BlockSpec((dff, d), lambda i: (0, 0), pipeline_mode=resident),
            pl.BlockSpec((t, d), lambda i: (i, 0)),
            pl.BlockSpec((None, None, 1, d), lambda i: (mrow(i), 5, 0, 0)),
            pl.BlockSpec((1, d), lambda i: (0, 0)),
        ],
        out_specs=pl.BlockSpec((t, d), lambda i: (i, 0)),
        scratch_shapes=[pltpu.VMEM((t + 2 * halo, d), BF16), pltpu.VMEM((t, dff), BF16),
                        pltpu.VMEM((4, f // LANES, 2 * (t + 2 * halo), LANES), F32)],
        compiler_params=_cparams("parallel"),
        name=name,
    )(h2, h2, h2, w_up, w_conv, b_conv.reshape(1, 2 * dff), w_down, x, mod, norm_out.reshape(1, d))


def _rope_tables(seq, extra_rows):
    tpos = jnp.arange(seq, dtype=jnp.int32)
    row = (tpos // GRID_W).astype(F32)
    col = (tpos % GRID_W).astype(F32)
    inv = ROPE_BASE ** (-jnp.arange(ROPE_PAIRS, dtype=F32) / ROPE_PAIRS)
    ar = row[:, None] * inv[None, :]
    ac = col[:, None] * inv[None, :]
    cr, sr, cc, sc = jnp.cos(ar), jnp.sin(ar), jnp.cos(ac), jnp.sin(ac)
    cos = jnp.concatenate([cr, cr, cc, cc] * (LANES // HEAD_DIM), axis=1)
    sin = jnp.concatenate([-sr, sr, -sc, sc] * (LANES // HEAD_DIM), axis=1)
    cos = jnp.concatenate([cos, jnp.ones((extra_rows, LANES), F32)], axis=0)
    sin = jnp.concatenate([sin, jnp.zeros((extra_rows, LANES), F32)], axis=0)
    return cos * Q_SCALE, sin * Q_SCALE, cos, sin


def _dup_kv_columns(w, d, nkv):
    q = w[:, :d]
    k = w[:, d:d + nkv * HEAD_DIM].reshape(-1, nkv, 1, HEAD_DIM)
    v = w[:, d + nkv * HEAD_DIM:].reshape(-1, nkv, 1, HEAD_DIM)
    kd = jnp.broadcast_to(k, (w.shape[0], nkv, 2, HEAD_DIM)).reshape(w.shape[0], -1)
    vd = jnp.broadcast_to(v, (w.shape[0], nkv, 2, HEAD_DIM)).reshape(w.shape[0], -1)
    return jnp.concatenate([q, kd, vd], axis=1)


def kernel(x, c, ctx, c_ctx, ada_w, ada_b, norm_mix, norm_ffn, norm_out, ffn_up, ffn_conv, ffn_conv_b,
           ffn_down, a_wqkv, a_wo, a_lambda, a_subln, b_wqkv, b_wo, b_rpb, c_wqkv, c_wo, c_sinks):
    batch, seq, d = x.shape
    nctx = ctx.shape[1]
    depth = ada_w.shape[0]
    assert seq % T_FFN == 0 and seq % T_PROJ == 0 and (batch * nctx) % T_PROJ == 0

    mod_rows = 16
    cs = jnp.concatenate([c, c_ctx[None, :], jnp.zeros((mod_rows - batch - 1, d), F32)], axis=0)
    mod_all = _ada_call(cs, ada_w, ada_b).reshape(depth, mod_rows, 6, 1, d)
    ctx_row = batch

    tables = _rope_tables(seq, T_PROJ)
    lat_tiles = seq // T_PROJ
    nd = d // LANES

    xl = x.reshape(batch * seq, d)
    xc = ctx.reshape(batch * nctx, d)
    for i in range(depth):
        need_ctx = i < depth - 1
        kind, j = i % N_MIXERS, i // N_MIXERS
        mod = mod_all[i]
        if kind == 0:
            w = a_wqkv[j].astype(BF16)
            wo = a_wo[j].astype(BF16)
            kinds = [ROPEQ] * nd + [ROPEK] * nd + [PLAIN] * nd
            tabs = tables
        elif kind == 1:
            w = b_wqkv[j].astype(BF16)
            wo = b_wo[j].astype(BF16)
            kinds = [SCALEQ] * nd + [PLAIN] * (2 * nd)
            tabs = None
        else:
            nkv = (c_wqkv.shape[2] - d) // (2 * HEAD_DIM)
            w = _dup_kv_columns(c_wqkv[j], d, nkv).astype(BF16)
            wo = c_wo[j].astype(BF16)
            kinds = [ROPEQ] * nd + [ROPEK] * nkv + [PLAIN] * nkv
            tabs = tables
        qkv = _proj_call(xl, norm_mix[i], mod, w, kinds, tabs, tiles_per_seq=lat_tiles, mod_row=None,
                         name=f"proj{i}")
        qkvc = _proj_call(xc, norm_mix[i], mod, w, kinds, tabs, tiles_per_seq=lat_tiles, mod_row=ctx_row,
                          name=f"proj_ctx{i}")
        if kind == 0:
            lam_init = 0.8 - 0.6 * math.exp(-0.3 * i)
            y, yc = _diff_attention(qkv, qkvc, a_lambda[j].astype(F32), a_subln[j], lam_init,
                                    batch, seq, nctx, need_ctx)
        elif kind == 1:
            y, yc = _na_attention(qkv, qkvc, b_rpb[j], batch, seq, nctx, need_ctx)
        else:
            y, yc = _swa_attention(qkv, qkvc, c_sinks[j], batch, seq, nctx, need_ctx, d)

        w_up = ffn_up[i].astype(BF16)
        w_down = ffn_down[i].astype(BF16)
        xl, h2 = _oproj_call(y, wo, xl, norm_ffn[i], mod, tiles_per_seq=lat_tiles, mod_row=None,
                             name=f"oproj{i}")
        xl = _ffn_call(h2, xl, w_up, ffn_conv[i], ffn_conv_b[i], w_down, mod, norm_out,
                       t=T_FFN, tiles_per_seq=seq // T_FFN, mod_row=None, final=not need_ctx,
                       name=f"ffn{i}")
        if need_ctx:
            xc, h2c = _oproj_call(yc, wo, xc, norm_ffn[i], mod, tiles_per_seq=1, mod_row=ctx_row,
                                  name=f"oproj_ctx{i}")
            xc = _ffn_call(h2c, xc, w_up, ffn_conv[i], ffn_conv_b[i], w_down, mod, norm_out,
                           t=nctx, tiles_per_seq=1, mod_row=ctx_row, final=False, name=f"ffn_ctx{i}")
    return xl.reshape(batch, seq, d)
```

```python
import functools
import math

import numpy as np
import jax
import jax.numpy as jnp
from jax import lax
from jax.experimental import pallas as pl
from jax.experimental.pallas import tpu as pltpu

F32 = jnp.float32
BF16 = jnp.bfloat16

GRID_W = 64
HEAD_DIM = 64
ROPE_BASE = 10000.0
ROPE_PAIRS = HEAD_DIM // 4
NORM_EPS = 1e-6
NEG_INF = -1e30
N_MIXERS = 3
NA_KH = 8
NA_KW = 16
SWA_WINDOW = 128
CONV_W = 3
LOG2E = 1.4426950408889634
Q_SCALE = (HEAD_DIM ** -0.5) * LOG2E

LANES = 128
BF16_ROWS = 16
F32_ROWS = 8
VMEM_LIMIT = 52 * 1024 * 1024

T_PROJ = 512
T_FFN = 512
FF_CHUNK = 256
NA_ROWS = 4
SWA_TQ = 256
DIFF_TQ = 256
DIFF_KC = 256

PLAIN, SCALEQ, ROPEQ, ROPEK = 0, 1, 2, 3


def _cparams(*sem):
    return pltpu.CompilerParams(dimension_semantics=sem, vmem_limit_bytes=VMEM_LIMIT)


def _rms(x):
    return x * lax.rsqrt(jnp.mean(x * x, axis=-1, keepdims=True) + NORM_EPS)


def _ada_kernel(cs_ref, w_ref, b_ref, o_ref):
    cs = cs_ref[...]
    s = cs * (1.0 / (1.0 + jnp.exp(-cs)))
    w = w_ref[...]
    s_hi = s.astype(BF16)
    s_lo = (s - s_hi.astype(F32)).astype(BF16)
    w_hi = w.astype(BF16)
    w_lo = (w - w_hi.astype(F32)).astype(BF16)
    acc = jnp.dot(s_hi, w_hi, preferred_element_type=F32)
    acc = acc + jnp.dot(s_hi, w_lo, preferred_element_type=F32)
    acc = acc + jnp.dot(s_lo, w_hi, preferred_element_type=F32)
    o_ref[...] = acc + b_ref[...]


def _ada_call(cs, ada_w, ada_b):
    depth, d, n = ada_w.shape
    nt = 1536
    rows = cs.shape[0]
    return pl.pallas_call(
        _ada_kernel,
        out_shape=jax.ShapeDtypeStruct((depth, rows, n), F32),
        grid=(depth, n // nt),
        in_specs=[
            pl.BlockSpec((rows, d), lambda l, j: (0, 0)),
            pl.BlockSpec((None, d, nt), lambda l, j: (l, 0, j)),
            pl.BlockSpec((None, 1, nt), lambda l, j: (l, 0, j)),
        ],
        out_specs=pl.BlockSpec((None, rows, nt), lambda l, j: (l, 0, j)),
        compiler_params=_cparams("parallel", "parallel"),
        name="ada_mod",
    )(cs, ada_w, ada_b.reshape(depth, 1, n))


def _rope(r, c, s, first_half):
    partner = jnp.where(first_half, pltpu.roll(r, LANES - 16, 1), pltpu.roll(r, 16, 1))
    return r * c + partner * s


def _proj_kernel(*refs, kinds, nc, has_rope):
    if has_rope:
        x_ref, g_ref, sc_ref, sh_ref, w_ref, cq_ref, sq_ref, ck_ref, sk_ref, o_ref = refs
    else:
        x_ref, g_ref, sc_ref, sh_ref, w_ref, o_ref = refs
    t = x_ref.shape[0]
    nout = o_ref.shape[1]
    h = (_rms(x_ref[...]) * g_ref[...]) * (1.0 + sc_ref[...]) + sh_ref[...]
    hb = h.astype(BF16)
    if has_rope:
        lane = lax.broadcasted_iota(jnp.int32, (t, LANES), 1)
        first_half = (lane & 16) == 0
    for n0 in range(0, nout, nc):
        r = jnp.dot(hb, w_ref[:, n0:n0 + nc], preferred_element_type=F32)
        for j in range(nc // LANES):
            kind = kinds[n0 // LANES + j]
            rj = r[:, j * LANES:(j + 1) * LANES]
            if kind == ROPEQ:
                rj = _rope(rj, cq_ref[...], sq_ref[...], first_half)
            elif kind == ROPEK:
                rj = _rope(rj, ck_ref[...], sk_ref[...], first_half)
            elif kind == SCALEQ:
                rj = rj * Q_SCALE
            o_ref[:, n0 + j * LANES:n0 + (j + 1) * LANES] = rj.astype(BF16)


def _proj_call(x, gamma, mod, w, kinds, tables, *, tiles_per_seq, mod_row, name):
    n, d = x.shape
    nout = w.shape[1]
    t = T_PROJ
    has_rope = tables is not None

    def mrow(i):
        return mod_row if mod_row is not None else i // tiles_per_seq

    def tblock(i):
        return i % tiles_per_seq if mod_row is None else tiles_per_seq

    in_specs = [
        pl.BlockSpec((t, d), lambda i: (i, 0)),
        pl.BlockSpec((1, d), lambda i: (0, 0)),
        pl.BlockSpec((None, None, 1, d), lambda i: (mrow(i), 1, 0, 0)),
        pl.BlockSpec((None, None, 1, d), lambda i: (mrow(i), 0, 0, 0)),
        pl.BlockSpec((d, nout), lambda i: (0, 0)),
    ]
    args = [x, gamma.reshape(1, d), mod, mod, w]
    if has_rope:
        for tab in tables:
            in_specs.append(pl.BlockSpec((t, LANES), lambda i: (tblock(i), 0)))
            args.append(tab)
    return pl.pallas_call(
        functools.partial(_proj_kernel, kinds=tuple(kinds), nc=512, has_rope=has_rope),
        out_shape=jax.ShapeDtypeStruct((n, nout), BF16),
        grid=(n // t,),
        in_specs=in_specs,
        out_specs=pl.BlockSpec((t, nout), lambda i: (i, 0)),
        compiler_params=_cparams("parallel"),
        name=name,
    )(*args)


def _qk(q, k):
    return lax.dot_general(q, k, (((1,), (1,)), ((), ())), preferred_element_type=F32)


def _half_masks(shape):
    lane = lax.broadcasted_iota(jnp.int32, shape, 1)
    return lane < HEAD_DIM


def _split_heads(q, lo):
    qf = q.astype(F32)
    zero = jnp.zeros_like(qf)
    return jnp.where(lo, qf, zero).astype(q.dtype), jnp.where(lo, zero, qf).astype(q.dtype)


def _diff_lambda(lam_ref, lam_init):
    lv = lam_ref[...]
    d1 = jnp.sum(lv[0:1] * lv[1:2], axis=1, keepdims=True)
    d2 = jnp.sum(lv[2:3] * lv[3:4], axis=1, keepdims=True)
    return jnp.exp(d1) - jnp.exp(d2) + lam_init


def _diff_finish(o1, o2, lam, subln_ref, lam_init, o_ref):
    o = _rms(o1 - lam * o2) * subln_ref[...] * (1.0 - lam_init)
    o_ref[...] = o.astype(o_ref.dtype)


def _diff_kernel(lam_ref, subln_ref, q_ref, kl_ref, vl_ref, kc_ref, vc_ref, o_ref,
                 vt, s_scr, p_scr, m_scr, o_scr, *, lam_init, tq, kc):
    seq = q_ref.shape[0]
    nkeys = vt.shape[1]
    nlat = seq // kc
    nchunk = nkeys // kc
    ntile = seq // tq
    sub = m_scr.shape[1]
    lam = _diff_lambda(lam_ref, lam_init)
    lo = _half_masks((tq, LANES))
    rows = lambda c: slice(c * kc, (c + 1) * kc)

    def chunk_of(lat_ref, ctx_ref, c):
        if c < nlat:
            return lat_ref[rows(c), :]
        return ctx_ref[rows(c - nlat), :]

    eye = (lax.broadcasted_iota(jnp.int32, (LANES, LANES), 0)
           == lax.broadcasted_iota(jnp.int32, (LANES, LANES), 1)).astype(F32).astype(BF16)
    for c in range(nchunk):
        vt[0:LANES, rows(c)] = _qk(eye, chunk_of(vl_ref, vc_ref, c)).astype(BF16)
    vt[LANES:, :] = jnp.ones((vt.shape[0] - LANES, nkeys), BF16)

    def half(m, tile, do_qk, do_exp, do_pv):
        if do_qk:
            q = q_ref[pl.ds(pl.multiple_of(tile * tq, tq), tq), :].astype(F32)
            zero = jnp.zeros_like(q)
            qm = (jnp.where(lo, q, zero) if m == 0 else jnp.where(lo, zero, q)).astype(BF16)
            mrun = jnp.full((sub, tq), NEG_INF, F32)
        if do_exp:
            mb = m_scr[1 - m, 0:1, :]
        if do_pv:
            acc = jnp.zeros((vt.shape[0], tq), F32)
        for c in range(nchunk):
            if do_qk:
                s = _qk(chunk_of(kl_ref, kc_ref, c), qm)
                s_scr[m, rows(c), :] = s
                for u in range(kc // sub):
                    mrun = jnp.maximum(mrun, s[u * sub:(u + 1) * sub, :])
            if do_exp:
                p_scr[1 - m, rows(c), :] = jnp.exp2(s_scr[1 - m, rows(c), :] - mb).astype(BF16)
            if do_pv:
                acc = acc + jnp.dot(vt[:, rows(c)], p_scr[m, rows(c), :], preferred_element_type=F32)
        if do_qk:
            m_scr[m] = jnp.broadcast_to(jnp.max(mrun, axis=0, keepdims=True), (sub, tq))
        if do_pv:
            return acc[:LANES] * (1.0 / acc[LANES:LANES + 1])
        return None

    def emit(tile):
        o = _rms(o_scr[...].T) * subln_ref[...] * (1.0 - lam_init)
        o_ref[pl.ds(pl.multiple_of(tile * tq, tq), tq), :] = o.astype(o_ref.dtype)

    o_scr[...] = jnp.zeros_like(o_scr)
    half(0, 0, True, False, False)
    half(1, 0, True, True, False)

    def body(t, carry):
        emit(jnp.maximum(t - 2, 0))
        o1 = half(0, t, True, True, True)
        o2 = half(1, t, True, True, True)
        o_scr[...] = o1 - lam * o2
        return carry

    lax.fori_loop(1, ntile, body, 0)
    emit(ntile - 2)
    o1 = half(0, 0, False, True, True)
    o2 = half(1, 0, False, False, True)
    o_scr[...] = o1 - lam * o2
    emit(ntile - 1)


def _diff_ctx_kernel(lam_ref, subln_ref, q_ref, kc_ref, vc_ref, o_ref, *, lam_init):
    tq = q_ref.shape[0]
    lam = _diff_lambda(lam_ref, lam_init)
    qs = _split_heads(q_ref[...], _half_masks((tq, LANES)))
    o1, o2 = [_softmax_pv([_qk(qm, kc_ref[...])], [vc_ref[...]]) for qm in qs]
    _diff_finish(o1, o2, lam, subln_ref, lam_init, o_ref)


def _diff_attention(qkv, qkvc, lam_vecs, subln, lam_init, batch, seq, nctx, need_ctx):
    d = qkv.shape[1] // 3
    nh = d // LANES
    tq = DIFF_TQ
    kc = DIFF_KC
    assert seq % tq == 0 and seq % kc == 0 and nctx % kc == 0 and seq // tq >= 2
    nkeys = seq + nctx
    subln2 = subln.reshape(1, LANES)
    y = pl.pallas_call(
        functools.partial(_diff_kernel, lam_init=lam_init, tq=tq, kc=kc),
        out_shape=jax.ShapeDtypeStruct((batch * seq, d), BF16),
        grid=(batch, nh),
        in_specs=[
            pl.BlockSpec((4, HEAD_DIM), lambda b, h: (0, 0)),
            pl.BlockSpec((1, LANES), lambda b, h: (0, 0)),
            pl.BlockSpec((seq, LANES), lambda b, h: (b, h)),
            pl.BlockSpec((seq, LANES), lambda b, h: (b, nh + h)),
            pl.BlockSpec((seq, LANES), lambda b, h: (b, 2 * nh + h)),
            pl.BlockSpec((nctx, LANES), lambda b, h: (b, nh + h)),
            pl.BlockSpec((nctx, LANES), lambda b, h: (b, 2 * nh + h)),
        ],
        out_specs=pl.BlockSpec((seq, LANES), lambda b, h: (b, h)),
        scratch_shapes=[pltpu.VMEM((LANES + BF16_ROWS, nkeys), BF16), pltpu.VMEM((2, nkeys, tq), F32),
                        pltpu.VMEM((2, nkeys, tq), BF16), pltpu.VMEM((2, F32_ROWS, tq), F32),
                        pltpu.VMEM((LANES, tq), F32)],
        compiler_params=_cparams("parallel", "parallel"),
        name="diff_attn",
    )(lam_vecs, subln2, qkv, qkv, qkv, qkvc, qkvc)
    yc = None
    if need_ctx:
        yc = pl.pallas_call(
            functools.partial(_diff_ctx_kernel, lam_init=lam_init),
            out_shape=jax.ShapeDtypeStruct((batch * nctx, d), BF16),
            grid=(batch, nh),
            in_specs=[
                pl.BlockSpec((4, HEAD_DIM), lambda b, h: (0, 0)),
                pl.BlockSpec((1, LANES), lambda b, h: (0, 0)),
                pl.BlockSpec((nctx, LANES), lambda b, h: (b, h)),
                pl.BlockSpec((nctx, LANES), lambda b, h: (b, nh + h)),
                pl.BlockSpec((nctx, LANES), lambda b, h: (b, 2 * nh + h)),
            ],
            out_specs=pl.BlockSpec((nctx, LANES), lambda b, h: (b, h)),
            compiler_params=_cparams("parallel", "parallel"),
            name="diff_attn_ctx",
        )(lam_vecs, subln2, qkvc, qkvc, qkvc)
    return y, yc


def _softmax_pv(scores, values, extra_logit=None):
    m = jnp.max(scores[0], axis=1, keepdims=True)
    for s in scores[1:]:
        m = jnp.maximum(m, jnp.max(s, axis=1, keepdims=True))
    if extra_logit is not None:
        m = jnp.maximum(m, extra_logit)
    l = None
    acc = None
    for s, v in zip(scores, values):
        p = jnp.exp2(s - m)
        ps = jnp.sum(p, axis=1, keepdims=True)
        pv = jnp.dot(p.astype(BF16), v, preferred_element_type=F32)
        l = ps if l is None else l + ps
        acc = pv if acc is None else acc + pv
    if extra_logit is not None:
        l = l + jnp.exp2(extra_logit - m)
    return acc * (1.0 / l)


def _na_kernel(bias_ref, q_ref, k0_ref, k1_ref, k2_ref, v0_ref, v1_ref, v2_ref, kc_ref, vc_ref, o_ref):
    tq = q_ref.shape[0]
    lo = _half_masks((tq, LANES))
    qs = _split_heads(q_ref[...], lo)
    ks = (k0_ref[...], k1_ref[...], k2_ref[...])
    vs = [v0_ref[...], v1_ref[...], v2_ref[...], vc_ref[...]]
    blk = k0_ref.shape[0]
    outs = []
    for hh in range(2):
        scores = [_qk(qs[hh], ks[j]) + bias_ref[hh, :, j * blk:(j + 1) * blk] for j in range(3)]
        scores.append(_qk(qs[hh], kc_ref[...]))
        outs.append(_softmax_pv(scores, vs))
    o_ref[...] = jnp.where(lo, outs[0], outs[1]).astype(o_ref.dtype)


def _plain_ctx_kernel(q_ref, kc_ref, vc_ref, o_ref):
    tq = q_ref.shape[0]
    lo = _half_masks((tq, LANES))
    qs = _split_heads(q_ref[...], lo)
    outs = [_softmax_pv([_qk(qs[hh], kc_ref[...])], [vc_ref[...]]) for hh in range(2)]
    o_ref[...] = jnp.where(lo, outs[0], outs[1]).astype(o_ref.dtype)


def _na_bias_tables(rpb, rows):
    w = GRID_W
    nk = 3 * NA_ROWS
    nh = rpb.shape[0]
    edge = w - NA_KW
    ext = jnp.concatenate([jnp.repeat(rpb[..., :1], edge, axis=-1), rpb,
                           jnp.repeat(rpb[..., -1:], edge + 1, axis=-1)], axis=-1)
    flat = jnp.tile(ext, (1, 1, w + 1))[..., :w * (2 * w + 1)]
    hankel = flat.reshape(nh, -1, w, 2 * w + 1)[..., :w]
    toep = jnp.flip(hankel, axis=-2)
    cq = np.arange(w)[:, None]
    cx = np.arange(w)[None, :]
    col_start = np.clip(cq - NA_KW // 2, 0, w - NA_KW)
    col_ok = (cx >= col_start) & (cx < col_start + NA_KW)
    toep = jnp.where(col_ok, toep * LOG2E, NEG_INF)
    masked = jnp.full((nh, w, w), NEG_INF, F32)
    tabs = []
    for r0, k0 in ((0, 0), (NA_ROWS, 0), (rows - NA_ROWS, rows - nk)):
        row_blocks = []
        for a in range(NA_ROWS):
            r = r0 + a
            rs = min(max(r - NA_KH // 2, 0), rows - NA_KH)
            blocks = []
            for e in range(nk):
                ry = k0 + e
                blocks.append(toep[:, ry - r + NA_KH - 1] if rs <= ry < rs + NA_KH else masked)
            row_blocks.append(jnp.concatenate(blocks, axis=-1))
        tabs.append(jnp.concatenate(row_blocks, axis=-2))
    return jnp.stack(tabs).astype(F32)


def _na_attention(qkv, qkvc, rpb, batch, seq, nctx, need_ctx):
    d = qkv.shape[1] // 3
    nb = d // LANES
    rows = seq // GRID_W
    tq = NA_ROWS * GRID_W
    ng = seq // tq
    bias = _na_bias_tables(rpb.astype(F32), rows)

    def cls(g):
        return jnp.where(g == 0, 0, jnp.where(g == ng - 1, 2, 1))

    def kblock(g, j):
        return jnp.clip(g - 1, 0, ng - 3) + j

    def kv_spec(col0, j):
        return pl.BlockSpec((tq, LANES), lambda hb, g, b: (b * ng + kblock(g, j), col0 + hb))

    y = pl.pallas_call(
        _na_kernel,
        out_shape=jax.ShapeDtypeStruct((batch * seq, d), BF16),
        grid=(nb, ng, batch),
        in_specs=[
            pl.BlockSpec((None, 2, tq, 3 * tq), lambda hb, g, b: (cls(g), hb, 0, 0)),
            pl.BlockSpec((tq, LANES), lambda hb, g, b: (b * ng + g, hb)),
            kv_spec(nb, 0), kv_spec(nb, 1), kv_spec(nb, 2),
            kv_spec(2 * nb, 0), kv_spec(2 * nb, 1), kv_spec(2 * nb, 2),
            pl.BlockSpec((nctx, LANES), lambda hb, g, b: (b, nb + hb)),
            pl.BlockSpec((nctx, LANES), lambda hb, g, b: (b, 2 * nb + hb)),
        ],
        out_specs=pl.BlockSpec((tq, LANES), lambda hb, g, b: (b * ng + g, hb)),
        compiler_params=_cparams("parallel", "parallel", "arbitrary"),
        name="na_attn",
    )(bias, qkv, qkv, qkv, qkv, qkv, qkv, qkv, qkvc, qkvc)
    yc = None
    if need_ctx:
        yc = pl.pallas_call(
            _plain_ctx_kernel,
            out_shape=jax.ShapeDtypeStruct((batch * nctx, d), BF16),
            grid=(batch, nb),
            in_specs=[
                pl.BlockSpec((nctx, LANES), lambda b, hb: (b, hb)),
                pl.BlockSpec((nctx, LANES), lambda b, hb: (b, nb + hb)),
                pl.BlockSpec((nctx, LANES), lambda b, hb: (b, 2 * nb + hb)),
            ],
            out_specs=pl.BlockSpec((nctx, LANES), lambda b, hb: (b, hb)),
            compiler_params=_cparams("parallel", "parallel"),
            name="na_attn_ctx",
        )(qkvc, qkvc, qkvc)
    return y, yc


def _swa_heads(q_ref, sinks_ref, g, score_fn, values, o_ref):
    tq = q_ref.shape[0]
    lo = _half_masks((tq, LANES))
    for jb in range(2):
        qs = _split_heads(q_ref[:, jb * LANES:(jb + 1) * LANES], lo)
        outs = []
        for hh in range(2):
            sink = jnp.full((1, 1), sinks_ref[g * 4 + jb * 2 + hh] * LOG2E, F32)
            outs.append(_softmax_pv(score_fn(qs[hh]), values, extra_logit=sink))
        o_ref[:, jb * LANES:(jb + 1) * LANES] = jnp.where(lo, outs[0], outs[1]).astype(o_ref.dtype)


def _swa_kernel(sinks_ref, q_ref, kl_ref, vl_ref, kc_ref, vc_ref, o_ref, *, span):
    g = pl.program_id(1)
    i = pl.program_id(2)
    tq = q_ref.shape[0]
    seq = kl_ref.shape[0]
    kstart = pl.multiple_of(jnp.clip(i * tq - SWA_WINDOW, 0, seq - span), SWA_WINDOW)
    kwin = kl_ref[pl.ds(kstart, span), :]
    vwin = vl_ref[pl.ds(kstart, span), :]
    qpos = i * tq + lax.broadcasted_iota(jnp.int32, (tq, span), 0)
    kpos = kstart + lax.broadcasted_iota(jnp.int32, (tq, span), 1)
    band = jnp.where(jnp.abs(qpos - kpos) <= SWA_WINDOW, 0.0, NEG_INF).astype(F32)
    kc = kc_ref[...]

    def score_fn(qm):
        return [_qk(qm, kwin) + band, _qk(qm, kc)]

    _swa_heads(q_ref, sinks_ref, g, score_fn, [vwin, vc_ref[...]], o_ref)


def _swa_ctx_kernel(sinks_ref, q_ref, kc_ref, vc_ref, o_ref):
    g = pl.program_id(1)
    kc = kc_ref[...]
    _swa_heads(q_ref, sinks_ref, g, lambda qm: [_qk(qm, kc)], [vc_ref[...]], o_ref)


def _swa_attention(qkv, qkvc, sinks, batch, seq, nctx, need_ctx, d):
    nkv = (qkv.shape[1] - d) // (2 * LANES)
    qb = d // nkv
    kcol = d // LANES
    tq = SWA_TQ
    nq = seq // tq
    span = tq + 2 * SWA_WINDOW
    smem = pl.BlockSpec(memory_space=pltpu.SMEM)
    sinks = sinks.astype(F32)
    y = pl.pallas_call(
        functools.partial(_swa_kernel, span=span),
        out_shape=jax.ShapeDtypeStruct((batch * seq, d), BF16),
        grid=(batch, nkv, nq),
        in_specs=[
            smem,
            pl.BlockSpec((tq, qb), lambda b, g, i: (b * nq + i, g)),
            pl.BlockSpec((seq, LANES), lambda b, g, i: (b, kcol + g)),
            pl.BlockSpec((seq, LANES), lambda b, g, i: (b, kcol + nkv + g)),
            pl.BlockSpec((nctx, LANES), lambda b, g, i: (b, kcol + g)),
            pl.BlockSpec((nctx, LANES), lambda b, g, i: (b, kcol + nkv + g)),
        ],
        out_specs=pl.BlockSpec((tq, qb), lambda b, g, i: (b * nq + i, g)),
        compiler_params=_cparams("parallel", "parallel", "arbitrary"),
        name="swa_attn",
    )(sinks, qkv, qkv, qkv, qkvc, qkvc)
    yc = None
    if need_ctx:
        yc = pl.pallas_call(
            _swa_ctx_kernel,
            out_shape=jax.ShapeDtypeStruct((batch * nctx, d), BF16),
            grid=(batch, nkv),
            in_specs=[
                smem,
                pl.BlockSpec((nctx, qb), lambda b, g: (b, g)),
                pl.BlockSpec((nctx, LANES), lambda b, g: (b, kcol + g)),
                pl.BlockSpec((nctx, LANES), lambda b, g: (b, kcol + nkv + g)),
            ],
            out_specs=pl.BlockSpec((nctx, qb), lambda b, g: (b, g)),
            compiler_params=_cparams("parallel", "parallel"),
            name="swa_attn_ctx",
        )(sinks, qkvc, qkvc, qkvc)
    return y, yc


def _oproj_kernel(o_ref, wo_ref, x_ref, g1_ref, nf_ref, sc2_ref, sh2_ref, xo_ref, h2_ref):
    y = jnp.dot(o_ref[...], wo_ref[...], preferred_element_type=F32)
    xn = x_ref[...] + g1_ref[...] * y
    xo_ref[...] = xn
    h2 = (_rms(xn) * nf_ref[...]) * (1.0 + sc2_ref[...]) + sh2_ref[...]
    h2_ref[...] = h2.astype(BF16)


def _oproj_call(o, wo, x, gamma, mod, *, tiles_per_seq, mod_row, name):
    n, d = x.shape
    t = T_PROJ

    def mrow(i):
        return mod_row if mod_row is not None else i // tiles_per_seq

    def mspec(which):
        return pl.BlockSpec((None, None, 1, d), lambda i: (mrow(i), which, 0, 0))

    return pl.pallas_call(
        _oproj_kernel,
        out_shape=(jax.ShapeDtypeStruct((n, d), F32), jax.ShapeDtypeStruct((n, d), BF16)),
        grid=(n // t,),
        in_specs=[
            pl.BlockSpec((t, d), lambda i: (i, 0)),
            pl.BlockSpec((d, d), lambda i: (0, 0)),
            pl.BlockSpec((t, d), lambda i: (i, 0)),
            mspec(2),
            pl.BlockSpec((1, d), lambda i: (0, 0)),
            mspec(4),
            mspec(3),
        ],
        out_specs=(pl.BlockSpec((t, d), lambda i: (i, 0)), pl.BlockSpec((t, d), lambda i: (i, 0))),
        compiler_params=_cparams("parallel"),
        name=name,
    )(o, wo, x, mod, gamma.reshape(1, d), mod, mod)


def _ffn_kernel(h_ref, hp_ref, hn_ref, wup_ref, cw_ref, cb_ref, wd_ref, x_ref, g2_ref, no_ref, o_ref,
                lhs, act, ubuf, *, nseq, final, f):
    i = pl.program_id(0)
    t = h_ref.shape[0]
    halo = hp_ref.shape[0]
    dff = wd_ref.shape[0]
    rows = t + 2 * halo

    keep_prev = ((i % nseq) != 0).astype(F32)
    keep_next = ((i % nseq) != nseq - 1).astype(F32)
    lhs[0:halo, :] = (hp_ref[...].astype(F32) * keep_prev).astype(BF16)
    lhs[halo:halo + t, :] = h_ref[...]
    lhs[halo + t:, :] = (hn_ref[...].astype(F32) * keep_next).astype(BF16)

    def conv(col0, slot):
        u = jnp.dot(lhs[...], wup_ref[:, col0:col0 + f], preferred_element_type=F32)
        outs = []
        for s in range(f // LANES):
            sl = slice(col0 + s * LANES, col0 + (s + 1) * LANES)
            buf = ubuf.at[slot, s]
            buf[pl.ds(0, rows, stride=2), :] = u[:, s * LANES:(s + 1) * LANES]
            taps = [buf[pl.ds(2 * (halo - 1 + k), t, stride=2), :] for k in range(CONV_W)]
            outs.append(cb_ref[:, sl] + taps[0] * cw_ref[0:1, sl] + taps[1] * cw_ref[1:2, sl]
                        + taps[2] * cw_ref[2:3, sl])
        return jnp.concatenate(outs, axis=1)

    for c in range(dff // f):
        slot = 2 * (c % 2)
        a = conv(c * f, slot)
        g = conv(dff + c * f, slot + 1)
        act[:, c * f:(c + 1) * f] = ((g * (1.0 / (1.0 + jnp.exp(-g)))) * a).astype(BF16)

    xn = x_ref[...] + g2_ref[...] * jnp.dot(act[...], wd_ref[...], preferred_element_type=F32)
    if final:
        xn = _rms(xn) * no_ref[...]
    o_ref[...] = xn


def _ffn_call(h2, x, w_up, w_conv, b_conv, w_down, mod, norm_out, *, t, tiles_per_seq, mod_row, final, name):
    n, d = x.shape
    dff = w_down.shape[0]
    f = FF_CHUNK
    halo = BF16_ROWS
    hb = t // halo
    last_hblock = n // halo - 1
    resident = pl.Buffered(1)

    def mrow(i):
        return mod_row if mod_row is not None else i // tiles_per_seq

    return pl.pallas_call(
        functools.partial(_ffn_kernel, nseq=tiles_per_seq, final=final, f=f),
        out_shape=jax.ShapeDtypeStruct((n, d), F32),
        grid=(n // t,),
        in_specs=[
            pl.BlockSpec((t, d), lambda i: (i, 0)),
            pl.BlockSpec((halo, d), lambda i: (jnp.maximum(i * hb - 1, 0), 0)),
            pl.BlockSpec((halo, d), lambda i: (jnp.minimum((i + 1) * hb, last_hblock), 0)),
            pl.BlockSpec((d, 2 * dff), lambda i: (0, 0), pipeline_mode=resident),
            pl.BlockSpec((CONV_W, 2 * dff), lambda i: (0, 0), pipeline_mode=resident),
            pl.BlockSpec((1, 2 * dff), lambda i: (0, 0), pipeline_mode=resident),
            pl.BlockSpec((dff, d), lambda i: (0, 0), pipeline_mode=resident),
            pl.BlockSpec((t, d), lambda i: (i, 0)),
            pl.BlockSpec((None, None, 1, d), lambda i: (mrow(i), 5, 0, 0)),
            pl.BlockSpec((1, d), lambda i: (0, 0)),
        ],
        out_specs=pl.BlockSpec((t, d), lambda i: (i, 0)),
        scratch_shapes=[pltpu.VMEM((t + 2 * halo, d), BF16), pltpu.VMEM((t, dff), BF16),
                        pltpu.VMEM((4, f // LANES, 2 * (t + 2 * halo), LANES), F32)],
        compiler_params=_cparams("parallel"),
        name=name,
    )(h2, h2, h2, w_up, w_conv, b_conv.reshape(1, 2 * dff), w_down, x, mod, norm_out.reshape(1, d))


def _rope_tables(seq, extra_rows):
    tpos = jnp.arange(seq, dtype=jnp.int32)
    row = (tpos // GRID_W).astype(F32)
    col = (tpos % GRID_W).astype(F32)
    inv = ROPE_BASE ** (-jnp.arange(ROPE_PAIRS, dtype=F32) / ROPE_PAIRS)
    ar = row[:, None] * inv[None, :]
    ac = col[:, None] * inv[None, :]
    cr, sr, cc, sc = jnp.cos(ar), jnp.sin(ar), jnp.cos(ac), jnp.sin(ac)
    cos = jnp.concatenate([cr, cr, cc, cc] * (LANES // HEAD_DIM), axis=1)
    sin = jnp.concatenate([-sr, sr, -sc, sc] * (LANES // HEAD_DIM), axis=1)
    cos = jnp.concatenate([cos, jnp.ones((extra_rows, LANES), F32)], axis=0)
    sin = jnp.concatenate([sin, jnp.zeros((extra_rows, LANES), F32)], axis=0)
    return cos * Q_SCALE, sin * Q_SCALE, cos, sin


def _dup_kv_columns(w, d, nkv):
    q = w[:, :d]
    k = w[:, d:d + nkv * HEAD_DIM].reshape(-1, nkv, 1, HEAD_DIM)
    v = w[:, d + nkv * HEAD_DIM:].reshape(-1, nkv, 1, HEAD_DIM)
    kd = jnp.broadcast_to(k, (w.shape[0], nkv, 2, HEAD_DIM)).reshape(w.shape[0], -1)
    vd = jnp.broadcast_to(v, (w.shape[0], nkv, 2, HEAD_DIM)).reshape(w.shape[0], -1)
    return jnp.concatenate([q, kd, vd], axis=1)


def kernel(x, c, ctx, c_ctx, ada_w, ada_b, norm_mix, norm_ffn, norm_out, ffn_up, ffn_conv, ffn_conv_b,
           ffn_down, a_wqkv, a_wo, a_lambda, a_subln, b_wqkv, b_wo, b_rpb, c_wqkv, c_wo, c_sinks):
    batch, seq, d = x.shape
    nctx = ctx.shape[1]
    depth = ada_w.shape[0]
    assert seq % T_FFN == 0 and seq % T_PROJ == 0 and (batch * nctx) % T_PROJ == 0

    mod_rows = 16
    cs = jnp.concatenate([c, c_ctx[None, :], jnp.zeros((mod_rows - batch - 1, d), F32)], axis=0)
    mod_all = _ada_call(cs, ada_w, ada_b).reshape(depth, mod_rows, 6, 1, d)
    ctx_row = batch

    tables = _rope_tables(seq, T_PROJ)
    lat_tiles = seq // T_PROJ
    nd = d // LANES

    xl = x.reshape(batch * seq, d)
    xc = ctx.reshape(batch * nctx, d)
    for i in range(depth):
        need_ctx = i < depth - 1
        kind, j = i % N_MIXERS, i // N_MIXERS
        mod = mod_all[i]
        if kind == 0:
            w = a_wqkv[j].astype(BF16)
            wo = a_wo[j].astype(BF16)
            kinds = [ROPEQ] * nd + [ROPEK] * nd + [PLAIN] * nd
            tabs = tables
        elif kind == 1:
            w = b_wqkv[j].astype(BF16)
            wo = b_wo[j].astype(BF16)
            kinds = [SCALEQ] * nd + [PLAIN] * (2 * nd)
            tabs = None
        else:
            nkv = (c_wqkv.shape[2] - d) // (2 * HEAD_DIM)
            w = _dup_kv_columns(c_wqkv[j], d, nkv).astype(BF16)
            wo = c_wo[j].astype(BF16)
            kinds = [ROPEQ] * nd + [ROPEK] * nkv + [PLAIN] * nkv
            tabs = tables
        qkv = _proj_call(xl, norm_mix[i], mod, w, kinds, tabs, tiles_per_seq=lat_tiles, mod_row=None,
                         name=f"proj{i}")
        qkvc = _proj_call(xc, norm_mix[i], mod, w, kinds, tabs, tiles_per_seq=lat_tiles, mod_row=ctx_row,
                          name=f"proj_ctx{i}")
        if kind == 0:
            lam_init = 0.8 - 0.6 * math.exp(-0.3 * i)
            y, yc = _diff_attention(qkv, qkvc, a_lambda[j].astype(F32), a_subln[j], lam_init,
                                    batch, seq, nctx, need_ctx)
        elif kind == 1:
            y, yc = _na_attention(qkv, qkvc, b_rpb[j], batch, seq, nctx, need_ctx)
        else:
            y, yc = _swa_attention(qkv, qkvc, c_sinks[j], batch, seq, nctx, need_ctx, d)

        w_up = ffn_up[i].astype(BF16)
        w_down = ffn_down[i].astype(BF16)
        xl, h2 = _oproj_call(y, wo, xl, norm_ffn[i], mod, tiles_per_seq=lat_tiles, mod_row=None,
                             name=f"oproj{i}")
        xl = _ffn_call(h2, xl, w_up, ffn_conv[i], ffn_conv_b[i], w_down, mod, norm_out,
                       t=T_FFN, tiles_per_seq=seq // T_FFN, mod_row=None, final=not need_ctx,
                       name=f"ffn{i}")
        if need_ctx:
            xc, h2c = _oproj_call(yc, wo, xc, norm_ffn[i], mod, tiles_per_seq=1, mod_row=ctx_row,
                                  name=f"oproj_ctx{i}")
            xc = _ffn_call(h2c, xc, w_up, ffn_conv[i], ffn_conv_b[i], w_down, mod, norm_out,
                           t=nctx, tiles_per_seq=1, mod_row=ctx_row, final=False, name=f"ffn_ctx{i}")
    return xl.reshape(batch, seq, d)
```

```python
import functools
import math

import numpy as np
import jax
import jax.numpy as jnp
from jax import lax
from jax.experimental import pallas as pl
from jax.experimental.pallas import tpu as pltpu

F32 = jnp.float32
BF16 = jnp.bfloat16

GRID_W = 64
HEAD_DIM = 64
ROPE_BASE = 10000.0
ROPE_PAIRS = HEAD_DIM // 4
NORM_EPS = 1e-6
NEG_INF = -1e30
N_MIXERS = 3
NA_KH = 8
NA_KW = 16
SWA_WINDOW = 128
CONV_W = 3
LOG2E = 1.4426950408889634
Q_SCALE = (HEAD_DIM ** -0.5) * LOG2E

LANES = 128
BF16_ROWS = 16
F32_ROWS = 8
VMEM_LIMIT = 52 * 1024 * 1024

T_PROJ = 512
T_FFN = 512
FF_CHUNK = 256
NA_ROWS = 4
SWA_TQ = 256
DIFF_TQ = 256
DIFF_KC = 256

PLAIN, SCALEQ, ROPEQ, ROPEK = 0, 1, 2, 3


def _cparams(*sem):
    return pltpu.CompilerParams(dimension_semantics=sem, vmem_limit_bytes=VMEM_LIMIT)


def _rms(x):
    return x * lax.rsqrt(jnp.mean(x * x, axis=-1, keepdims=True) + NORM_EPS)


def _ada_kernel(cs_ref, w_ref, b_ref, o_ref):
    cs = cs_ref[...]
    s = cs * (1.0 / (1.0 + jnp.exp(-cs)))
    w = w_ref[...]
    s_hi = s.astype(BF16)
    s_lo = (s - s_hi.astype(F32)).astype(BF16)
    w_hi = w.astype(BF16)
    w_lo = (w - w_hi.astype(F32)).astype(BF16)
    acc = jnp.dot(s_hi, w_hi, preferred_element_type=F32)
    acc = acc + jnp.dot(s_hi, w_lo, preferred_element_type=F32)
    acc = acc + jnp.dot(s_lo, w_hi, preferred_element_type=F32)
    o_ref[...] = acc + b_ref[...]


def _ada_call(cs, ada_w, ada_b):
    depth, d, n = ada_w.shape
    nt = 1536
    rows = cs.shape[0]
    return pl.pallas_call(
        _ada_kernel,
        out_shape=jax.ShapeDtypeStruct((depth, rows, n), F32),
        grid=(depth, n // nt),
        in_specs=[
            pl.BlockSpec((rows, d), lambda l, j: (0, 0)),
            pl.BlockSpec((None, d, nt), lambda l, j: (l, 0, j)),
            pl.BlockSpec((None, 1, nt), lambda l, j: (l, 0, j)),
        ],
        out_specs=pl.BlockSpec((None, rows, nt), lambda l, j: (l, 0, j)),
        compiler_params=_cparams("parallel", "parallel"),
        name="ada_mod",
    )(cs, ada_w, ada_b.reshape(depth, 1, n))


def _rope(r, c, s, first_half):
    partner = jnp.where(first_half, pltpu.roll(r, LANES - 16, 1), pltpu.roll(r, 16, 1))
    return r * c + partner * s


def _proj_kernel(*refs, kinds, nc, has_rope):
    if has_rope:
        x_ref, g_ref, sc_ref, sh_ref, w_ref, cq_ref, sq_ref, ck_ref, sk_ref, o_ref = refs
    else:
        x_ref, g_ref, sc_ref, sh_ref, w_ref, o_ref = refs
    t = x_ref.shape[0]
    nout = o_ref.shape[1]
    h = (_rms(x_ref[...]) * g_ref[...]) * (1.0 + sc_ref[...]) + sh_ref[...]
    hb = h.astype(BF16)
    if has_rope:
        lane = lax.broadcasted_iota(jnp.int32, (t, LANES), 1)
        first_half = (lane & 16) == 0
    for n0 in range(0, nout, nc):
        r = jnp.dot(hb, w_ref[:, n0:n0 + nc], preferred_element_type=F32)
        for j in range(nc // LANES):
            kind = kinds[n0 // LANES + j]
            rj = r[:, j * LANES:(j + 1) * LANES]
            if kind == ROPEQ:
                rj = _rope(rj, cq_ref[...], sq_ref[...], first_half)
            elif kind == ROPEK:
                rj = _rope(rj, ck_ref[...], sk_ref[...], first_half)
            elif kind == SCALEQ:
                rj = rj * Q_SCALE
            o_ref[:, n0 + j * LANES:n0 + (j + 1) * LANES] = rj.astype(BF16)


def _proj_call(x, gamma, mod, w, kinds, tables, *, tiles_per_seq, mod_row, name):
    n, d = x.shape
    nout = w.shape[1]
    t = T_PROJ
    has_rope = tables is not None

    def mrow(i):
        return mod_row if mod_row is not None else i // tiles_per_seq

    def tblock(i):
        return i % tiles_per_seq if mod_row is None else tiles_per_seq

    in_specs = [
        pl.BlockSpec((t, d), lambda i: (i, 0)),
        pl.BlockSpec((1, d), lambda i: (0, 0)),
        pl.BlockSpec((None, None, 1, d), lambda i: (mrow(i), 1, 0, 0)),
        pl.BlockSpec((None, None, 1, d), lambda i: (mrow(i), 0, 0, 0)),
        pl.BlockSpec((d, nout), lambda i: (0, 0)),
    ]
    args = [x, gamma.reshape(1, d), mod, mod, w]
    if has_rope:
        for tab in tables:
            in_specs.append(pl.BlockSpec((t, LANES), lambda i: (tblock(i), 0)))
            args.append(tab)
    return pl.pallas_call(
        functools.partial(_proj_kernel, kinds=tuple(kinds), nc=512, has_rope=has_rope),
        out_shape=jax.ShapeDtypeStruct((n, nout), BF16),
        grid=(n // t,),
        in_specs=in_specs,
        out_specs=pl.BlockSpec((t, nout), lambda i: (i, 0)),
        compiler_params=_cparams("parallel"),
        name=name,
    )(*args)


def _qk(q, k):
    return lax.dot_general(q, k, (((1,), (1,)), ((), ())), preferred_element_type=F32)


def _half_masks(shape):
    lane = lax.broadcasted_iota(jnp.int32, shape, 1)
    return lane < HEAD_DIM


def _split_heads(q, lo):
    qf = q.astype(F32)
    zero = jnp.zeros_like(qf)
    return jnp.where(lo, qf, zero).astype(q.dtype), jnp.where(lo, zero, qf).astype(q.dtype)


def _diff_lambda(lam_ref, lam_init):
    lv = lam_ref[...]
    d1 = jnp.sum(lv[0:1] * lv[1:2], axis=1, keepdims=True)
    d2 = jnp.sum(lv[2:3] * lv[3:4], axis=1, keepdims=True)
    return jnp.exp(d1) - jnp.exp(d2) + lam_init


def _diff_finish(o1, o2, lam, subln_ref, lam_init, o_ref):
    o = _rms(o1 - lam * o2) * subln_ref[...] * (1.0 - lam_init)
    o_ref[...] = o.astype(o_ref.dtype)


def _diff_kernel(lam_ref, subln_ref, q_ref, kl_ref, vl_ref, kc_ref, vc_ref, o_ref,
                 vt, s_scr, p_scr, m_scr, o_scr, *, lam_init, tq, kc):
    seq = q_ref.shape[0]
    nkeys = vt.shape[1]
    nlat = seq // kc
    nchunk = nkeys // kc
    ntile = seq // tq
    sub = m_scr.shape[1]
    lam = _diff_lambda(lam_ref, lam_init)
    lo = _half_masks((tq, LANES))
    rows = lambda c: slice(c * kc, (c + 1) * kc)

    def chunk_of(lat_ref, ctx_ref, c):
        if c < nlat:
            return lat_ref[rows(c), :]
        return ctx_ref[rows(c - nlat), :]

    eye = (lax.broadcasted_iota(jnp.int32, (LANES, LANES), 0)
           == lax.broadcasted_iota(jnp.int32, (LANES, LANES), 1)).astype(F32).astype(BF16)
    for c in range(nchunk):
        vt[0:LANES, rows(c)] = _qk(eye, chunk_of(vl_ref, vc_ref, c)).astype(BF16)
    vt[LANES:, :] = jnp.ones((vt.shape[0] - LANES, nkeys), BF16)

    def half(m, tile, do_qk, do_exp, do_pv):
        if do_qk:
            q = q_ref[pl.ds(pl.multiple_of(tile * tq, tq), tq), :].astype(F32)
            zero = jnp.zeros_like(q)
            qm = (jnp.where(lo, q, zero) if m == 0 else jnp.where(lo, zero, q)).astype(BF16)
            mrun = jnp.full((sub, tq), NEG_INF, F32)
        if do_exp:
            mb = m_scr[1 - m, 0:1, :]
        if do_pv:
            acc = jnp.zeros((vt.shape[0], tq), F32)
        for c in range(nchunk):
            if do_qk:
                s = _qk(chunk_of(kl_ref, kc_ref, c), qm)
                s_scr[m, rows(c), :] = s
                for u in range(kc // sub):
                    mrun = jnp.maximum(mrun, s[u * sub:(u + 1) * sub, :])
            if do_exp:
                p_scr[1 - m, rows(c), :] = jnp.exp2(s_scr[1 - m, rows(c), :] - mb).astype(BF16)
            if do_pv:
                acc = acc + jnp.dot(vt[:, rows(c)], p_scr[m, rows(c), :], preferred_element_type=F32)
        if do_qk:
            m_scr[m] = jnp.broadcast_to(jnp.max(mrun, axis=0, keepdims=True), (sub, tq))
        if do_pv:
            return acc[:LANES] * (1.0 / acc[LANES:LANES + 1])
        return None

    def emit(tile):
        o = _rms(o_scr[...].T) * subln_ref[...] * (1.0 - lam_init)
        o_ref[pl.ds(pl.multiple_of(tile * tq, tq), tq), :] = o.astype(o_ref.dtype)

    o_scr[...] = jnp.zeros_like(o_scr)
    half(0, 0, True, False, False)
    half(1, 0, True, True, False)

    def body(t, carry):
        emit(jnp.maximum(t - 2, 0))
        o1 = half(0, t, True, True, True)
        o2 = half(1, t, True, True, True)
        o_scr[...] = o1 - lam * o2
        return carry

    lax.fori_loop(1, ntile, body, 0)
    emit(ntile - 2)
    o1 = half(0, 0, False, True, True)
    o2 = half(1, 0, False, False, True)
    o_scr[...] = o1 - lam * o2
    emit(ntile - 1)


def _diff_ctx_kernel(lam_ref, subln_ref, q_ref, kc_ref, vc_ref, o_ref, *, lam_init):
    tq = q_ref.shape[0]
    lam = _diff_lambda(lam_ref, lam_init)
    qs = _split_heads(q_ref[...], _half_masks((tq, LANES)))
    o1, o2 = [_softmax_pv([_qk(qm, kc_ref[...])], [vc_ref[...]]) for qm in qs]
    _diff_finish(o1, o2, lam, subln_ref, lam_init, o_ref)


def _diff_attention(qkv, qkvc, lam_vecs, subln, lam_init, batch, seq, nctx, need_ctx):
    d = qkv.shape[1] // 3
    nh = d // LANES
    tq = DIFF_TQ
    kc = DIFF_KC
    assert seq % tq == 0 and seq % kc == 0 and nctx % kc == 0 and seq // tq >= 2
    nkeys = seq + nctx
    subln2 = subln.reshape(1, LANES)
    y = pl.pallas_call(
        functools.partial(_diff_kernel, lam_init=lam_init, tq=tq, kc=kc),
        out_shape=jax.ShapeDtypeStruct((batch * seq, d), BF16),
        grid=(batch, nh),
        in_specs=[
            pl.BlockSpec((4, HEAD_DIM), lambda b, h: (0, 0)),
            pl.BlockSpec((1, LANES), lambda b, h: (0, 0)),
            pl.BlockSpec((seq, LANES), lambda b, h: (b, h)),
            pl.BlockSpec((seq, LANES), lambda b, h: (b, nh + h)),
            pl.BlockSpec((seq, LANES), lambda b, h: (b, 2 * nh + h)),
            pl.BlockSpec((nctx, LANES), lambda b, h: (b, nh + h)),
            pl.BlockSpec((nctx, LANES), lambda b, h: (b, 2 * nh + h)),
        ],
        out_specs=pl.BlockSpec((seq, LANES), lambda b, h: (b, h)),
        scratch_shapes=[pltpu.VMEM((LANES + BF16_ROWS, nkeys), BF16), pltpu.VMEM((2, nkeys, tq), F32),
                        pltpu.VMEM((2, nkeys, tq), BF16), pltpu.VMEM((2, F32_ROWS, tq), F32),
                        pltpu.VMEM((LANES, tq), F32)],
        compiler_params=_cparams("parallel", "parallel"),
        name="diff_attn",
    )(lam_vecs, subln2, qkv, qkv, qkv, qkvc, qkvc)
    yc = None
    if need_ctx:
        yc = pl.pallas_call(
            functools.partial(_diff_ctx_kernel, lam_init=lam_init),
            out_shape=jax.ShapeDtypeStruct((batch * nctx, d), BF16),
            grid=(batch, nh),
            in_specs=[
                pl.BlockSpec((4, HEAD_DIM), lambda b, h: (0, 0)),
                pl.BlockSpec((1, LANES), lambda b, h: (0, 0)),
                pl.BlockSpec((nctx, LANES), lambda b, h: (b, h)),
                pl.BlockSpec((nctx, LANES), lambda b, h: (b, nh + h)),
                pl.BlockSpec((nctx, LANES), lambda b, h: (b, 2 * nh + h)),
            ],
            out_specs=pl.BlockSpec((nctx, LANES), lambda b, h: (b, h)),
            compiler_params=_cparams("parallel", "parallel"),
            name="diff_attn_ctx",
        )(lam_vecs, subln2, qkvc, qkvc, qkvc)
    return y, yc


def _softmax_pv(scores, values, extra_logit=None):
    m = jnp.max(scores[0], axis=1, keepdims=True)
    for s in scores[1:]:
        m = jnp.maximum(m, jnp.max(s, axis=1, keepdims=True))
    if extra_logit is not None:
        m = jnp.maximum(m, extra_logit)
    l = None
    acc = None
    for s, v in zip(scores, values):
        p = jnp.exp2(s - m)
        ps = jnp.sum(p, axis=1, keepdims=True)
        pv = jnp.dot(p.astype(BF16), v, preferred_element_type=F32)
        l = ps if l is None else l + ps
        acc = pv if acc is None else acc + pv
    if extra_logit is not None:
        l = l + jnp.exp2(extra_logit - m)
    return acc * (1.0 / l)


WIN_HEADS = 4


def _identity(n):
    return (lax.broadcasted_iota(jnp.int32, (n, n), 0)
            == lax.broadcasted_iota(jnp.int32, (n, n), 1)).astype(F32).astype(BF16)


def _fill_vt(vt, head0, chunk, v_block, n_heads):
    tr = _qk(_identity(LANES), v_block).astype(BF16)
    for hh in range(n_heads):
        vt[head0 + hh, chunk, 0:HEAD_DIM, :] = tr[hh * HEAD_DIM:(hh + 1) * HEAD_DIM]
        vt[head0 + hh, chunk, HEAD_DIM:, :] = jnp.ones((vt.shape[2] - HEAD_DIM, vt.shape[3]), BF16)


def _window_pipeline(q_ref, o_ref, s_scr, p_scr, m_scr, e_scr, o_scr, *, tq, kc, nchunk,
                     scores, values, extra_logit):
    ntile = q_ref.shape[0] // tq
    sub = m_scr.shape[1]
    lo = _half_masks((tq, LANES))
    rows = lambda c: slice(c * kc, (c + 1) * kc)
    lanes = lambda jb: slice(jb * LANES, (jb + 1) * LANES)
    tile_rows = lambda t: pl.ds(pl.multiple_of(t * tq, tq), tq)

    def half(j, t, do_qk, do_exp, do_pv):
        slot = j % 2
        jp, jv = (j - 1) % WIN_HEADS, (j - 2) % WIN_HEADS
        tv = t if j >= 2 else t - 1
        if do_qk:
            q = q_ref[tile_rows(t), lanes(j // 2)].astype(F32)
            zero = jnp.zeros_like(q)
            qm = (jnp.where(lo, q, zero) if j % 2 == 0 else jnp.where(lo, zero, q)).astype(BF16)
            mrun = jnp.full((sub, tq), NEG_INF, F32)
        if do_exp:
            mb = m_scr[1 - slot, 0:1, :]
        if do_pv:
            acc = jnp.zeros((o_scr.shape[1] + BF16_ROWS, tq), F32)
        for c in range(nchunk):
            if do_qk:
                s = scores(qm, t, j, c)
                s_scr[slot, rows(c), :] = s
                for u in range(kc // sub):
                    mrun = jnp.maximum(mrun, s[u * sub:(u + 1) * sub, :])
            if do_exp:
                p_scr[1 - slot, rows(c), :] = jnp.exp2(s_scr[1 - slot, rows(c), :] - mb).astype(BF16)
            if do_pv:
                acc = acc + jnp.dot(values(tv, jv, c), p_scr[slot, rows(c), :], preferred_element_type=F32)
        if do_qk:
            m = jnp.max(mrun, axis=0, keepdims=True)
            if extra_logit is not None:
                m = jnp.maximum(m, extra_logit(j))
            m_scr[slot] = jnp.broadcast_to(m, (sub, tq))
        if do_exp and extra_logit is not None:
            e_scr[1 - slot] = jnp.exp2(extra_logit(jp) - m_scr[1 - slot])
        if do_pv:
            l = acc[HEAD_DIM:HEAD_DIM + 1]
            if extra_logit is not None:
                l = l + e_scr[slot, 0:1, :]
            o_scr[jv] = acc[:HEAD_DIM] * (1.0 / l)

    def emit(t):
        for jb in range(WIN_HEADS // 2):
            o = jnp.concatenate([o_scr[2 * jb], o_scr[2 * jb + 1]], axis=0).T
            o_ref[tile_rows(t), lanes(jb)] = o.astype(o_ref.dtype)

    half(0, 0, True, False, False)
    half(1, 0, True, True, False)
    half(2, 0, True, True, True)
    half(3, 0, True, True, True)

    def body(t, carry):
        half(0, t, True, True, True)
        half(1, t, True, True, True)
        emit(t - 1)
        half(2, t, True, True, True)
        half(3, t, True, True, True)
        return carry

    lax.fori_loop(1, ntile, body, 0)
    half(0, ntile, False, True, True)
    half(1, ntile, False, False, True)
    emit(ntile - 1)


def _tile_class(t, ntile):
    return jnp.where(t == 0, 0, jnp.where(t == ntile - 1, 2, 1))


def _na_kernel(bias_ref, q_ref, k_ref, v_ref, kc_ref, vc_ref, o_ref, vt, s_scr, p_scr, m_scr, o_scr,
               *, tq, kc):
    seq, nctx = q_ref.shape[0], kc_ref.shape[0]
    ntile, nlat, nwin = seq // tq, seq // kc, 3 * tq // kc
    rows = lambda c: slice(c * kc, (c + 1) * kc)
    lanes = lambda jb: slice(jb * LANES, (jb + 1) * LANES)
    for jb in range(WIN_HEADS // 2):
        for c in range(nlat):
            _fill_vt(vt, 2 * jb, c, v_ref[rows(c), lanes(jb)], 2)
        for c in range(nctx // kc):
            _fill_vt(vt, 2 * jb, nlat + c, vc_ref[rows(c), lanes(jb)], 2)

    def win_start(t):
        return pl.multiple_of(jnp.clip((t - 1) * tq, 0, seq - 3 * tq), tq)

    def scores(qm, t, j, c):
        if c >= nwin:
            return _qk(kc_ref[rows(c - nwin), lanes(j // 2)], qm)
        k = k_ref[pl.ds(win_start(t) + c * kc, kc), lanes(j // 2)]
        return _qk(k, qm) + bias_ref[_tile_class(t, ntile), j, rows(c), :]

    def values(t, j, c):
        return vt[j, nlat + c - nwin] if c >= nwin else vt[j, win_start(t) // kc + c]

    _window_pipeline(q_ref, o_ref, s_scr, p_scr, m_scr, None, o_scr, tq=tq, kc=kc,
                     nchunk=nwin + nctx // kc, scores=scores, values=values, extra_logit=None)


def _plain_ctx_kernel(q_ref, kc_ref, vc_ref, o_ref):
    tq = q_ref.shape[0]
    lo = _half_masks((tq, LANES))
    qs = _split_heads(q_ref[...], lo)
    outs = [_softmax_pv([_qk(qs[hh], kc_ref[...])], [vc_ref[...]]) for hh in range(2)]
    o_ref[...] = jnp.where(lo, outs[0], outs[1]).astype(o_ref.dtype)


def _na_bias_tables(rpb, rows):
    w = GRID_W
    nk = 3 * NA_ROWS
    nh = rpb.shape[0]
    edge = w - NA_KW
    ext = jnp.concatenate([jnp.repeat(rpb[..., :1], edge, axis=-1), rpb,
                           jnp.repeat(rpb[..., -1:], edge + 1, axis=-1)], axis=-1)
    flat = jnp.tile(ext, (1, 1, w + 1))[..., :w * (2 * w + 1)]
    hankel = flat.reshape(nh, -1, w, 2 * w + 1)[..., :w]
    toep = jnp.flip(hankel, axis=-2)
    cq = np.arange(w)[:, None]
    cx = np.arange(w)[None, :]
    col_start = np.clip(cq - NA_KW // 2, 0, w - NA_KW)
    col_ok = (cx >= col_start) & (cx < col_start + NA_KW)
    toep = jnp.where(col_ok, toep * LOG2E, NEG_INF)
    masked = jnp.full((nh, w, w), NEG_INF, F32)
    tabs = []
    for r0, k0 in ((0, 0), (NA_ROWS, 0), (rows - NA_ROWS, rows - nk)):
        row_blocks = []
        for a in range(NA_ROWS):
            r = r0 + a
            rs = min(max(r - NA_KH // 2, 0), rows - NA_KH)
            blocks = []
            for e in range(nk):
                ry = k0 + e
                blocks.append(toep[:, ry - r + NA_KH - 1] if rs <= ry < rs + NA_KH else masked)
            row_blocks.append(jnp.concatenate(blocks, axis=-1))
        tabs.append(jnp.concatenate(row_blocks, axis=-2))
    return jnp.swapaxes(jnp.stack(tabs), -1, -2).astype(F32)


def _win_scratch(nheads, nchunks, nkeys_item, tq, kc, with_extra):
    shapes = [pltpu.VMEM((nheads, nchunks, HEAD_DIM + BF16_ROWS, kc), BF16),
              pltpu.VMEM((2, nkeys_item, tq), F32), pltpu.VMEM((2, nkeys_item, tq), BF16),
              pltpu.VMEM((2, F32_ROWS, tq), F32)]
    if with_extra:
        shapes.append(pltpu.VMEM((2, F32_ROWS, tq), F32))
    shapes.append(pltpu.VMEM((WIN_HEADS, HEAD_DIM, tq), F32))
    return shapes


def _na_attention(qkv, qkvc, rpb, batch, seq, nctx, need_ctx):
    d = qkv.shape[1] // 3
    nb = d // LANES
    rows = seq // GRID_W
    tq = NA_ROWS * GRID_W
    kc = tq
    qw = WIN_HEADS * HEAD_DIM
    ngrp = d // qw
    assert seq // tq >= 3 and nctx % kc == 0
    bias = _na_bias_tables(rpb.astype(F32), rows)
    y = pl.pallas_call(
        functools.partial(_na_kernel, tq=tq, kc=kc),
        out_shape=jax.ShapeDtypeStruct((batch * seq, d), BF16),
        grid=(ngrp, batch),
        in_specs=[
            pl.BlockSpec((3, WIN_HEADS, 3 * tq, tq), lambda hg, b: (0, hg, 0, 0),
                         pipeline_mode=pl.Buffered(1)),
            pl.BlockSpec((seq, qw), lambda hg, b: (b, hg)),
            pl.BlockSpec((seq, qw), lambda hg, b: (b, ngrp + hg)),
            pl.BlockSpec((seq, qw), lambda hg, b: (b, 2 * ngrp + hg)),
            pl.BlockSpec((nctx, qw), lambda hg, b: (b, ngrp + hg)),
            pl.BlockSpec((nctx, qw), lambda hg, b: (b, 2 * ngrp + hg)),
        ],
        out_specs=pl.BlockSpec((seq, qw), lambda hg, b: (b, hg)),
        scratch_shapes=_win_scratch(WIN_HEADS, (seq + nctx) // kc, 3 * tq + nctx, tq, kc, False),
        compiler_params=_cparams("parallel", "arbitrary"),
        name="na_attn",
    )(bias, qkv, qkv, qkv, qkvc, qkvc)
    yc = None
    if need_ctx:
        yc = pl.pallas_call(
            _plain_ctx_kernel,
            out_shape=jax.ShapeDtypeStruct((batch * nctx, d), BF16),
            grid=(batch, nb),
            in_specs=[
                pl.BlockSpec((nctx, LANES), lambda b, hb: (b, hb)),
                pl.BlockSpec((nctx, LANES), lambda b, hb: (b, nb + hb)),
                pl.BlockSpec((nctx, LANES), lambda b, hb: (b, 2 * nb + hb)),
            ],
            out_specs=pl.BlockSpec((nctx, LANES), lambda b, hb: (b, hb)),
            compiler_params=_cparams("parallel", "parallel"),
            name="na_attn_ctx",
        )(qkvc, qkvc, qkvc)
    return y, yc


def _swa_heads(q_ref, sinks_ref, g, score_fn, values, o_ref):
    tq = q_ref.shape[0]
    lo = _half_masks((tq, LANES))
    for jb in range(2):
        qs = _split_heads(q_ref[:, jb * LANES:(jb + 1) * LANES], lo)
        outs = []
        for hh in range(2):
            sink = jnp.full((1, 1), sinks_ref[g * 4 + jb * 2 + hh] * LOG2E, F32)
            outs.append(_softmax_pv(score_fn(qs[hh]), values, extra_logit=sink))
        o_ref[:, jb * LANES:(jb + 1) * LANES] = jnp.where(lo, outs[0], outs[1]).astype(o_ref.dtype)


def _swa_kernel(sinks_ref, band_ref, q_ref, k_ref, v_ref, kc_ref, vc_ref, o_ref,
                vt, s_scr, p_scr, m_scr, e_scr, o_scr, *, tq, kc, span):
    g = pl.program_id(1)
    seq, nctx = q_ref.shape[0], kc_ref.shape[0]
    ntile, nlat, nwin = seq // tq, seq // kc, span // kc
    rows = lambda c: slice(c * kc, (c + 1) * kc)
    for c in range(nlat):
        _fill_vt(vt, 0, c, v_ref[rows(c), :], 1)
    for c in range(nctx // kc):
        _fill_vt(vt, 0, nlat + c, vc_ref[rows(c), :], 1)

    def win_start(t):
        return pl.multiple_of(jnp.clip(t * tq - SWA_WINDOW, 0, seq - span), SWA_WINDOW)

    def scores(qm, t, j, c):
        if c >= nwin:
            return _qk(kc_ref[rows(c - nwin), :], qm)
        k = k_ref[pl.ds(win_start(t) + c * kc, kc), :]
        return _qk(k, qm) + band_ref[_tile_class(t, ntile), rows(c), :]

    def values(t, j, c):
        return vt[0, nlat + c - nwin] if c >= nwin else vt[0, win_start(t) // kc + c]

    def sink(j):
        return jnp.full((1, 1), sinks_ref[g * WIN_HEADS + j] * LOG2E, F32)

    _window_pipeline(q_ref, o_ref, s_scr, p_scr, m_scr, e_scr, o_scr, tq=tq, kc=kc,
                     nchunk=nwin + nctx // kc, scores=scores, values=values, extra_logit=sink)


def _swa_band_tables(seq, tq, span):
    ntile = seq // tq
    tabs = []
    for t in (0, 1, ntile - 1):
        start = min(max(t * tq - SWA_WINDOW, 0), seq - span)
        kpos = start + np.arange(span)[:, None]
        qpos = t * tq + np.arange(tq)[None, :]
        tabs.append(np.where(np.abs(qpos - kpos) <= SWA_WINDOW, 0.0, NEG_INF))
    return jnp.asarray(np.stack(tabs), F32)


def _swa_ctx_kernel(sinks_ref, q_ref, kc_ref, vc_ref, o_ref):
    g = pl.program_id(1)
    kc = kc_ref[...]
    _swa_heads(q_ref, sinks_ref, g, lambda qm: [_qk(qm, kc)], [vc_ref[...]], o_ref)


def _swa_attention(qkv, qkvc, sinks, batch, seq, nctx, need_ctx, d):
    nkv = (qkv.shape[1] - d) // (2 * LANES)
    qb = d // nkv
    kcol = d // LANES
    tq = SWA_TQ
    kc = SWA_WINDOW
    span = tq + 2 * SWA_WINDOW
    assert qb == WIN_HEADS * HEAD_DIM and seq // tq >= 3 and nctx % kc == 0
    smem = pl.BlockSpec(memory_space=pltpu.SMEM)
    sinks = sinks.astype(F32)
    y = pl.pallas_call(
        functools.partial(_swa_kernel, tq=tq, kc=kc, span=span),
        out_shape=jax.ShapeDtypeStruct((batch * seq, d), BF16),
        grid=(batch, nkv),
        in_specs=[
            smem,
            pl.BlockSpec((3, span, tq), lambda b, g: (0, 0, 0), pipeline_mode=pl.Buffered(1)),
            pl.BlockSpec((seq, qb), lambda b, g: (b, g)),
            pl.BlockSpec((seq, LANES), lambda b, g: (b, kcol + g)),
            pl.BlockSpec((seq, LANES), lambda b, g: (b, kcol + nkv + g)),
            pl.BlockSpec((nctx, LANES), lambda b, g: (b, kcol + g)),
            pl.BlockSpec((nctx, LANES), lambda b, g: (b, kcol + nkv + g)),
        ],
        out_specs=pl.BlockSpec((seq, qb), lambda b, g: (b, g)),
        scratch_shapes=_win_scratch(1, (seq + nctx) // kc, span + nctx, tq, kc, True),
        compiler_params=_cparams("parallel", "parallel"),
        name="swa_attn",
    )(sinks, _swa_band_tables(seq, tq, span), qkv, qkv, qkv, qkvc, qkvc)
    yc = None
    if need_ctx:
        yc = pl.pallas_call(
            _swa_ctx_kernel,
            out_shape=jax.ShapeDtypeStruct((batch * nctx, d), BF16),
            grid=(batch, nkv),
            in_specs=[
                smem,
                pl.BlockSpec((nctx, qb), lambda b, g: (b, g)),
                pl.BlockSpec((nctx, LANES), lambda b, g: (b, kcol + g)),
                pl.BlockSpec((nctx, LANES), lambda b, g: (b, kcol + nkv + g)),
            ],
            out_specs=pl.BlockSpec((nctx, qb), lambda b, g: (b, g)),
            compiler_params=_cparams("parallel", "parallel"),
            name="swa_attn_ctx",
        )(sinks, qkvc, qkvc, qkvc)
    return y, yc


def _oproj_kernel(o_ref, wo_ref, x_ref, g1_ref, nf_ref, sc2_ref, sh2_ref, xo_ref, h2_ref):
    y = jnp.dot(o_ref[...], wo_ref[...], preferred_element_type=F32)
    xn = x_ref[...] + g1_ref[...] * y
    xo_ref[...] = xn
    h2 = (_rms(xn) * nf_ref[...]) * (1.0 + sc2_ref[...]) + sh2_ref[...]
    h2_ref[...] = h2.astype(BF16)


def _oproj_call(o, wo, x, gamma, mod, *, tiles_per_seq, mod_row, name):
    n, d = x.shape
    t = T_PROJ

    def mrow(i):
        return mod_row if mod_row is not None else i // tiles_per_seq

    def mspec(which):
        return pl.BlockSpec((None, None, 1, d), lambda i: (mrow(i), which, 0, 0))

    return pl.pallas_call(
        _oproj_kernel,
        out_shape=(jax.ShapeDtypeStruct((n, d), F32), jax.ShapeDtypeStruct((n, d), BF16)),
        grid=(n // t,),
        in_specs=[
            pl.BlockSpec((t, d), lambda i: (i, 0)),
            pl.BlockSpec((d, d), lambda i: (0, 0)),
            pl.BlockSpec((t, d), lambda i: (i, 0)),
            mspec(2),
            pl.BlockSpec((1, d), lambda i: (0, 0)),
            mspec(4),
            mspec(3),
        ],
        out_specs=(pl.BlockSpec((t, d), lambda i: (i, 0)), pl.BlockSpec((t, d), lambda i: (i, 0))),
        compiler_params=_cparams("parallel"),
        name=name,
    )(o, wo, x, mod, gamma.reshape(1, d), mod, mod)


def _ffn_kernel(h_ref, hp_ref, hn_ref, wup_ref, cw_ref, cb_ref, wd_ref, x_ref, g2_ref, no_ref, o_ref,
                lhs, act, ubuf, *, nseq, final, f):
    i = pl.program_id(0)
    t = h_ref.shape[0]
    halo = hp_ref.shape[0]
    dff = wd_ref.shape[0]
    rows = t + 2 * halo

    keep_prev = ((i % nseq) != 0).astype(F32)
    keep_next = ((i % nseq) != nseq - 1).astype(F32)
    lhs[0:halo, :] = (hp_ref[...].astype(F32) * keep_prev).astype(BF16)
    lhs[halo:halo + t, :] = h_ref[...]
    lhs[halo + t:, :] = (hn_ref[...].astype(F32) * keep_next).astype(BF16)

    def conv(col0, slot):
        u = jnp.dot(lhs[...], wup_ref[:, col0:col0 + f], preferred_element_type=F32)
        outs = []
        for s in range(f // LANES):
            sl = slice(col0 + s * LANES, col0 + (s + 1) * LANES)
            buf = ubuf.at[slot, s]
            buf[pl.ds(0, rows, stride=2), :] = u[:, s * LANES:(s + 1) * LANES]
            taps = [buf[pl.ds(2 * (halo - 1 + k), t, stride=2), :] for k in range(CONV_W)]
            outs.append(cb_ref[:, sl] + taps[0] * cw_ref[0:1, sl] + taps[1] * cw_ref[1:2, sl]
                        + taps[2] * cw_ref[2:3, sl])
        return jnp.concatenate(outs, axis=1)

    for c in range(dff // f):
        slot = 2 * (c % 2)
        a = conv(c * f, slot)
        g = conv(dff + c * f, slot + 1)
        act[:, c * f:(c + 1) * f] = ((g * (1.0 / (1.0 + jnp.exp(-g)))) * a).astype(BF16)

    xn = x_ref[...] + g2_ref[...] * jnp.dot(act[...], wd_ref[...], preferred_element_type=F32)
    if final:
        xn = _rms(xn) * no_ref[...]
    o_ref[...] = xn


def _ffn_call(h2, x, w_up, w_conv, b_conv, w_down, mod, norm_out, *, t, tiles_per_seq, mod_row, final, name):
    n, d = x.shape
    dff = w_down.shape[0]
    f = FF_CHUNK
    halo = BF16_ROWS
    hb = t // halo
    last_hblock = n // halo - 1
    resident = pl.Buffered(1)

    def mrow(i):
        return mod_row if mod_row is not None else i // tiles_per_seq

    return pl.pallas_call(
        functools.partial(_ffn_kernel, nseq=tiles_per_seq, final=final, f=f),
        out_shape=jax.ShapeDtypeStruct((n, d), F32),
        grid=(n // t,),
        in_specs=[
            pl.BlockSpec((t, d), lambda i: (i, 0)),
            pl.BlockSpec((halo, d), lambda i: (jnp.maximum(i * hb - 1, 0), 0)),
            pl.BlockSpec((halo, d), lambda i: (jnp.minimum((i + 1) * hb, last_hblock), 0)),
            pl.BlockSpec((d, 2 * dff), lambda i: (0, 0), pipeline_mode=resident),
            pl.BlockSpec((CONV_W, 2 * dff), lambda i: (0, 0), pipeline_mode=resident),
            pl.BlockSpec((1, 2 * dff), lambda i: (0, 0), pipeline_mode=resident),
            pl.BlockSpec((dff, d), lambda i: (0, 0), pipeline_mode=resident),
            pl.BlockSpec((t, d), lambda i: (i, 0)),
            pl.BlockSpec((None, None, 1, d), lambda i: (mrow(i), 5, 0, 0)),
            pl.BlockSpec((1, d), lambda i: (0, 0)),
        ],
        out_specs=pl.BlockSpec((t, d), lambda i: (i, 0)),
        scratch_shapes=[pltpu.VMEM((t + 2 * halo, d), BF16), pltpu.VMEM((t, dff), BF16),
                        pltpu.VMEM((4, f // LANES, 2 * (t + 2 * halo), LANES), F32)],
        compiler_params=_cparams("parallel"),
        name=name,
    )(h2, h2, h2, w_up, w_conv, b_conv.reshape(1, 2 * dff), w_down, x, mod, norm_out.reshape(1, d))


def _rope_tables(seq, extra_rows):
    tpos = jnp.arange(seq, dtype=jnp.int32)
    row = (tpos // GRID_W).astype(F32)
    col = (tpos % GRID_W).astype(F32)
    inv = ROPE_BASE ** (-jnp.arange(ROPE_PAIRS, dtype=F32) / ROPE_PAIRS)
    ar = row[:, None] * inv[None, :]
    ac = col[:, None] * inv[None, :]
    cr, sr, cc, sc = jnp.cos(ar), jnp.sin(ar), jnp.cos(ac), jnp.sin(ac)
    cos = jnp.concatenate([cr, cr, cc, cc] * (LANES // HEAD_DIM), axis=1)
    sin = jnp.concatenate([-sr, sr, -sc, sc] * (LANES // HEAD_DIM), axis=1)
    cos = jnp.concatenate([cos, jnp.ones((extra_rows, LANES), F32)], axis=0)
    sin = jnp.concatenate([sin, jnp.zeros((extra_rows, LANES), F32)], axis=0)
    return cos * Q_SCALE, sin * Q_SCALE, cos, sin


def _dup_kv_columns(w, d, nkv):
    q = w[:, :d]
    k = w[:, d:d + nkv * HEAD_DIM].reshape(-1, nkv, 1, HEAD_DIM)
    v = w[:, d + nkv * HEAD_DIM:].reshape(-1, nkv, 1, HEAD_DIM)
    kd = jnp.broadcast_to(k, (w.shape[0], nkv, 2, HEAD_DIM)).reshape(w.shape[0], -1)
    vd = jnp.broadcast_to(v, (w.shape[0], nkv, 2, HEAD_DIM)).reshape(w.shape[0], -1)
    return jnp.concatenate([q, kd, vd], axis=1)


def kernel(x, c, ctx, c_ctx, ada_w, ada_b, norm_mix, norm_ffn, norm_out, ffn_up, ffn_conv, ffn_conv_b,
           ffn_down, a_wqkv, a_wo, a_lambda, a_subln, b_wqkv, b_wo, b_rpb, c_wqkv, c_wo, c_sinks):
    batch, seq, d = x.shape
    nctx = ctx.shape[1]
    depth = ada_w.shape[0]
    assert seq % T_FFN == 0 and seq % T_PROJ == 0 and (batch * nctx) % T_PROJ == 0

    mod_rows = 16
    cs = jnp.concatenate([c, c_ctx[None, :], jnp.zeros((mod_rows - batch - 1, d), F32)], axis=0)
    mod_all = _ada_call(cs, ada_w, ada_b).reshape(depth, mod_rows, 6, 1, d)
    ctx_row = batch

    tables = _rope_tables(seq, T_PROJ)
    lat_tiles = seq // T_PROJ
    nd = d // LANES

    xl = x.reshape(batch * seq, d)
    xc = ctx.reshape(batch * nctx, d)
    for i in range(depth):
        need_ctx = i < depth - 1
        kind, j = i % N_MIXERS, i // N_MIXERS
        mod = mod_all[i]
        if kind == 0:
            w = a_wqkv[j].astype(BF16)
            wo = a_wo[j].astype(BF16)
            kinds = [ROPEQ] * nd + [ROPEK] * nd + [PLAIN] * nd
            tabs = tables
        elif kind == 1:
            w = b_wqkv[j].astype(BF16)
            wo = b_wo[j].astype(BF16)
            kinds = [SCALEQ] * nd + [PLAIN] * (2 * nd)
            tabs = None
        else:
            nkv = (c_wqkv.shape[2] - d) // (2 * HEAD_DIM)
            w = _dup_kv_columns(c_wqkv[j], d, nkv).astype(BF16)
            wo = c_wo[j].astype(BF16)
            kinds = [ROPEQ] * nd + [ROPEK] * nkv + [PLAIN] * nkv
            tabs = tables
        qkv = _proj_call(xl, norm_mix[i], mod, w, kinds, tabs, tiles_per_seq=lat_tiles, mod_row=None,
                         name=f"proj{i}")
        qkvc = _proj_call(xc, norm_mix[i], mod, w, kinds, tabs, tiles_per_seq=lat_tiles, mod_row=ctx_row,
                          name=f"proj_ctx{i}")
        if kind == 0:
            lam_init = 0.8 - 0.6 * math.exp(-0.3 * i)
            y, yc = _diff_attention(qkv, qkvc, a_lambda[j].astype(F32), a_subln[j], lam_init,
                                    batch, seq, nctx, need_ctx)
        elif kind == 1:
            y, yc = _na_attention(qkv, qkvc, b_rpb[j], batch, seq, nctx, need_ctx)
        else:
            y, yc = _swa_attention(qkv, qkvc, c_sinks[j], batch, seq, nctx, need_ctx, d)

        w_up = ffn_up[i].astype(BF16)
        w_down = ffn_down[i].astype(BF16)
        xl, h2 = _oproj_call(y, wo, xl, norm_ffn[i], mod, tiles_per_seq=lat_tiles, mod_row=None,
                             name=f"oproj{i}")
        xl = _ffn_call(h2, xl, w_up, ffn_conv[i], ffn_conv_b[i], w_down, mod, norm_out,
                       t=T_FFN, tiles_per_seq=seq // T_FFN, mod_row=None, final=not need_ctx,
                       name=f"ffn{i}")
        if need_ctx:
            xc, h2c = _oproj_call(yc, wo, xc, norm_ffn[i], mod, tiles_per_seq=1, mod_row=ctx_row,
                                  name=f"oproj_ctx{i}")
            xc = _ffn_call(h2c, xc, w_up, ffn_conv[i], ffn_conv_b[i], w_down, mod, norm_out,
                           t=nctx, tiles_per_seq=1, mod_row=ctx_row, final=False, name=f"ffn_ctx{i}")
    return xl.reshape(batch, seq, d)
```

```python
import functools
import math

import numpy as np
import jax
import jax.numpy as jnp
from jax import lax
from jax.experimental import pallas as pl
from jax.experimental.pallas import tpu as pltpu

F32 = jnp.float32
BF16 = jnp.bfloat16

GRID_W = 64
HEAD_DIM = 64
ROPE_BASE = 10000.0
ROPE_PAIRS = HEAD_DIM // 4
NORM_EPS = 1e-6
NEG_INF = -1e30
N_MIXERS = 3
NA_KH = 8
NA_KW = 16
SWA_WINDOW = 128
CONV_W = 3
LOG2E = 1.4426950408889634
Q_SCALE = (HEAD_DIM ** -0.5) * LOG2E

LANES = 128
BF16_ROWS = 16
F32_ROWS = 8
VMEM_LIMIT = 52 * 1024 * 1024

T_PROJ = 512
T_FFN = 512
FF_CHUNK = 256
NA_ROWS = 4
SWA_TQ = 256
DIFF_TQ = 256
DIFF_KC = 256

PLAIN, SCALEQ, ROPEQ, ROPEK = 0, 1, 2, 3


def _cparams(*sem):
    return pltpu.CompilerParams(dimension_semantics=sem, vmem_limit_bytes=VMEM_LIMIT)


def _rms(x):
    return x * lax.rsqrt(jnp.mean(x * x, axis=-1, keepdims=True) + NORM_EPS)


def _ada_kernel(cs_ref, w_ref, b_ref, o_ref):
    cs = cs_ref[...]
    s = cs * (1.0 / (1.0 + jnp.exp(-cs)))
    w = w_ref[...]
    s_hi = s.astype(BF16)
    s_lo = (s - s_hi.astype(F32)).astype(BF16)
    w_hi = w.astype(BF16)
    w_lo = (w - w_hi.astype(F32)).astype(BF16)
    acc = jnp.dot(s_hi, w_hi, preferred_element_type=F32)
    acc = acc + jnp.dot(s_hi, w_lo, preferred_element_type=F32)
    acc = acc + jnp.dot(s_lo, w_hi, preferred_element_type=F32)
    o_ref[...] = acc + b_ref[...]


def _ada_call(cs, ada_w, ada_b):
    depth, d, n = ada_w.shape
    nt = 1536
    rows = cs.shape[0]
    return pl.pallas_call(
        _ada_kernel,
        out_shape=jax.ShapeDtypeStruct((depth, rows, n), F32),
        grid=(depth, n // nt),
        in_specs=[
            pl.BlockSpec((rows, d), lambda l, j: (0, 0)),
            pl.BlockSpec((None, d, nt), lambda l, j: (l, 0, j)),
            pl.BlockSpec((None, 1, nt), lambda l, j: (l, 0, j)),
        ],
        out_specs=pl.BlockSpec((None, rows, nt), lambda l, j: (l, 0, j)),
        compiler_params=_cparams("parallel", "parallel"),
        name="ada_mod",
    )(cs, ada_w, ada_b.reshape(depth, 1, n))


def _rope(r, c, s, first_half):
    partner = jnp.where(first_half, pltpu.roll(r, LANES - 16, 1), pltpu.roll(r, 16, 1))
    return r * c + partner * s


def _proj_kernel(*refs, kinds, nc, has_rope):
    if has_rope:
        x_ref, g_ref, sc_ref, sh_ref, w_ref, cq_ref, sq_ref, ck_ref, sk_ref, o_ref = refs
    else:
        x_ref, g_ref, sc_ref, sh_ref, w_ref, o_ref = refs
    t = x_ref.shape[0]
    nout = o_ref.shape[1]
    h = (_rms(x_ref[...]) * g_ref[...]) * (1.0 + sc_ref[...]) + sh_ref[...]
    hb = h.astype(BF16)
    if has_rope:
        lane = lax.broadcasted_iota(jnp.int32, (t, LANES), 1)
        first_half = (lane & 16) == 0
    for n0 in range(0, nout, nc):
        r = jnp.dot(hb, w_ref[:, n0:n0 + nc], preferred_element_type=F32)
        for j in range(nc // LANES):
            kind = kinds[n0 // LANES + j]
            rj = r[:, j * LANES:(j + 1) * LANES]
            if kind == ROPEQ:
                rj = _rope(rj, cq_ref[...], sq_ref[...], first_half)
            elif kind == ROPEK:
                rj = _rope(rj, ck_ref[...], sk_ref[...], first_half)
            elif kind == SCALEQ:
                rj = rj * Q_SCALE
            o_ref[:, n0 + j * LANES:n0 + (j + 1) * LANES] = rj.astype(BF16)


def _proj_call(x, gamma, mod, w, kinds, tables, *, tiles_per_seq, mod_row, name):
    n, d = x.shape
    nout = w.shape[1]
    t = T_PROJ
    has_rope = tables is not None

    def mrow(i):
        return mod_row if mod_row is not None else i // tiles_per_seq

    def tblock(i):
        return i % tiles_per_seq if mod_row is None else tiles_per_seq

    in_specs = [
        pl.BlockSpec((t, d), lambda i: (i, 0)),
        pl.BlockSpec((1, d), lambda i: (0, 0)),
        pl.BlockSpec((None, None, 1, d), lambda i: (mrow(i), 1, 0, 0)),
        pl.BlockSpec((None, None, 1, d), lambda i: (mrow(i), 0, 0, 0)),
        pl.BlockSpec((d, nout), lambda i: (0, 0)),
    ]
    args = [x, gamma.reshape(1, d), mod, mod, w]
    if has_rope:
        for tab in tables:
            in_specs.append(pl.BlockSpec((t, LANES), lambda i: (tblock(i), 0)))
            args.append(tab)
    return pl.pallas_call(
        functools.partial(_proj_kernel, kinds=tuple(kinds), nc=512, has_rope=has_rope),
        out_shape=jax.ShapeDtypeStruct((n, nout), BF16),
        grid=(n // t,),
        in_specs=in_specs,
        out_specs=pl.BlockSpec((t, nout), lambda i: (i, 0)),
        compiler_params=_cparams("parallel"),
        name=name,
    )(*args)


def _qk(q, k):
    return lax.dot_general(q, k, (((1,), (1,)), ((), ())), preferred_element_type=F32)


def _half_masks(shape):
    lane = lax.broadcasted_iota(jnp.int32, shape, 1)
    return lane < HEAD_DIM


def _split_heads(q, lo):
    qf = q.astype(F32)
    zero = jnp.zeros_like(qf)
    return jnp.where(lo, qf, zero).astype(q.dtype), jnp.where(lo, zero, qf).astype(q.dtype)


def _diff_lambda(lam_ref, lam_init):
    lv = lam_ref[...]
    d1 = jnp.sum(lv[0:1] * lv[1:2], axis=1, keepdims=True)
    d2 = jnp.sum(lv[2:3] * lv[3:4], axis=1, keepdims=True)
    return jnp.exp(d1) - jnp.exp(d2) + lam_init


def _diff_finish(o1, o2, lam, subln_ref, lam_init, o_ref):
    o = _rms(o1 - lam * o2) * subln_ref[...] * (1.0 - lam_init)
    o_ref[...] = o.astype(o_ref.dtype)


def _diff_kernel(lam_ref, subln_ref, q_ref, kl_ref, vl_ref, kc_ref, vc_ref, o_ref,
                 vt, s_scr, p_scr, m_scr, o_scr, *, lam_init, tq, kc):
    seq = q_ref.shape[0]
    nkeys = vt.shape[1]
    nlat = seq // kc
    nchunk = nkeys // kc
    ntile = seq // tq
    sub = m_scr.shape[1]
    lam = _diff_lambda(lam_ref, lam_init)
    lo = _half_masks((tq, LANES))
    rows = lambda c: slice(c * kc, (c + 1) * kc)

    def chunk_of(lat_ref, ctx_ref, c):
        if c < nlat:
            return lat_ref[rows(c), :]
        return ctx_ref[rows(c - nlat), :]

    eye = (lax.broadcasted_iota(jnp.int32, (LANES, LANES), 0)
           == lax.broadcasted_iota(jnp.int32, (LANES, LANES), 1)).astype(F32).astype(BF16)
    for c in range(nchunk):
        vt[0:LANES, rows(c)] = _qk(eye, chunk_of(vl_ref, vc_ref, c)).astype(BF16)
    vt[LANES:, :] = jnp.ones((vt.shape[0] - LANES, nkeys), BF16)

    def half(m, tile, do_qk, do_exp, do_pv):
        if do_qk:
            q = q_ref[pl.ds(pl.multiple_of(tile * tq, tq), tq), :].astype(F32)
            zero = jnp.zeros_like(q)
            qm = (jnp.where(lo, q, zero) if m == 0 else jnp.where(lo, zero, q)).astype(BF16)
            mrun = jnp.full((sub, tq), NEG_INF, F32)
        if do_exp:
            mb = m_scr[1 - m, 0:1, :]
        if do_pv:
            acc = jnp.zeros((vt.shape[0], tq), F32)
        for c in range(nchunk):
            if do_qk:
                s = _qk(chunk_of(kl_ref, kc_ref, c), qm)
                s_scr[m, rows(c), :] = s
                for u in range(kc // sub):
                    mrun = jnp.maximum(mrun, s[u * sub:(u + 1) * sub, :])
            if do_pv:
                acc = acc + jnp.dot(vt[:, rows(c)], p_scr[1 - m, rows(c), :], preferred_element_type=F32)
            if do_exp:
                p_scr[1 - m, rows(c), :] = jnp.exp2(s_scr[1 - m, rows(c), :] - mb).astype(BF16)
        if do_qk:
            m_scr[m] = jnp.broadcast_to(jnp.max(mrun, axis=0, keepdims=True), (sub, tq))
        if do_pv:
            return acc[:LANES] * (1.0 / acc[LANES:LANES + 1])
        return None

    def emit(tile, o2):
        o = _rms((o_scr[...] - lam * o2).T) * subln_ref[...] * (1.0 - lam_init)
        o_ref[pl.ds(pl.multiple_of(tile * tq, tq), tq), :] = o.astype(o_ref.dtype)

    half(0, 0, True, False, False)
    half(1, 0, True, True, False)
    half(0, 1, True, True, False)
    o_scr[...] = half(1, 1, True, True, True)

    def body(t, carry):
        emit(t - 2, half(0, t, True, True, True))
        o_scr[...] = half(1, t, True, True, True)
        return carry

    lax.fori_loop(2, ntile, body, 0)
    emit(ntile - 2, half(0, ntile, False, True, True))
    o_scr[...] = half(1, ntile, False, False, True)
    emit(ntile - 1, half(0, ntile + 1, False, False, True))


def _diff_ctx_kernel(lam_ref, subln_ref, q_ref, kc_ref, vc_ref, o_ref, *, lam_init):
    tq = q_ref.shape[0]
    lam = _diff_lambda(lam_ref, lam_init)
    qs = _split_heads(q_ref[...], _half_masks((tq, LANES)))
    o1, o2 = [_softmax_pv([_qk(qm, kc_ref[...])], [vc_ref[...]]) for qm in qs]
    _diff_finish(o1, o2, lam, subln_ref, lam_init, o_ref)


def _diff_attention(qkv, qkvc, lam_vecs, subln, lam_init, batch, seq, nctx, need_ctx):
    d = qkv.shape[1] // 3
    nh = d // LANES
    tq = DIFF_TQ
    kc = DIFF_KC
    assert seq % tq == 0 and seq % kc == 0 and nctx % kc == 0 and seq // tq >= 2
    nkeys = seq + nctx
    subln2 = subln.reshape(1, LANES)
    y = pl.pallas_call(
        functools.partial(_diff_kernel, lam_init=lam_init, tq=tq, kc=kc),
        out_shape=jax.ShapeDtypeStruct((batch * seq, d), BF16),
        grid=(batch, nh),
        in_specs=[
            pl.BlockSpec((4, HEAD_DIM), lambda b, h: (0, 0)),
            pl.BlockSpec((1, LANES), lambda b, h: (0, 0)),
            pl.BlockSpec((seq, LANES), lambda b, h: (b, h)),
            pl.BlockSpec((seq, LANES), lambda b, h: (b, nh + h)),
            pl.BlockSpec((seq, LANES), lambda b, h: (b, 2 * nh + h)),
            pl.BlockSpec((nctx, LANES), lambda b, h: (b, nh + h)),
            pl.BlockSpec((nctx, LANES), lambda b, h: (b, 2 * nh + h)),
        ],
        out_specs=pl.BlockSpec((seq, LANES), lambda b, h: (b, h)),
        scratch_shapes=[pltpu.VMEM((LANES + BF16_ROWS, nkeys), BF16), pltpu.VMEM((2, nkeys, tq), F32),
                        pltpu.VMEM((2, nkeys, tq), BF16), pltpu.VMEM((2, F32_ROWS, tq), F32),
                        pltpu.VMEM((LANES, tq), F32)],
        compiler_params=_cparams("parallel", "parallel"),
        name="diff_attn",
    )(lam_vecs, subln2, qkv, qkv, qkv, qkvc, qkvc)
    yc = None
    if need_ctx:
        yc = pl.pallas_call(
            functools.partial(_diff_ctx_kernel, lam_init=lam_init),
            out_shape=jax.ShapeDtypeStruct((batch * nctx, d), BF16),
            grid=(batch, nh),
            in_specs=[
                pl.BlockSpec((4, HEAD_DIM), lambda b, h: (0, 0)),
                pl.BlockSpec((1, LANES), lambda b, h: (0, 0)),
                pl.BlockSpec((nctx, LANES), lambda b, h: (b, h)),
                pl.BlockSpec((nctx, LANES), lambda b, h: (b, nh + h)),
                pl.BlockSpec((nctx, LANES), lambda b, h: (b, 2 * nh + h)),
            ],
            out_specs=pl.BlockSpec((nctx, LANES), lambda b, h: (b, h)),
            compiler_params=_cparams("parallel", "parallel"),
            name="diff_attn_ctx",
        )(lam_vecs, subln2, qkvc, qkvc, qkvc)
    return y, yc


def _softmax_pv(scores, values, extra_logit=None):
    m = jnp.max(scores[0], axis=1, keepdims=True)
    for s in scores[1:]:
        m = jnp.maximum(m, jnp.max(s, axis=1, keepdims=True))
    if extra_logit is not None:
        m = jnp.maximum(m, extra_logit)
    l = None
    acc = None
    for s, v in zip(scores, values):
        p = jnp.exp2(s - m)
        ps = jnp.sum(p, axis=1, keepdims=True)
        pv = jnp.dot(p.astype(BF16), v, preferred_element_type=F32)
        l = ps if l is None else l + ps
        acc = pv if acc is None else acc + pv
    if extra_logit is not None:
        l = l + jnp.exp2(extra_logit - m)
    return acc * (1.0 / l)


WIN_HEADS = 4


def _identity(n):
    return (lax.broadcasted_iota(jnp.int32, (n, n), 0)
            == lax.broadcasted_iota(jnp.int32, (n, n), 1)).astype(F32).astype(BF16)


def _fill_vt(vt, head0, chunk, v_block, n_heads):
    tr = _qk(_identity(LANES), v_block).astype(BF16)
    for hh in range(n_heads):
        vt[head0 + hh, chunk, 0:HEAD_DIM, :] = tr[hh * HEAD_DIM:(hh + 1) * HEAD_DIM]
        vt[head0 + hh, chunk, HEAD_DIM:, :] = jnp.ones((vt.shape[2] - HEAD_DIM, vt.shape[3]), BF16)


def _window_pipeline(q_ref, o_ref, s_scr, p_scr, m_scr, e_scr, o_scr, *, tq, kc, nchunk,
                     scores, values, extra_logit):
    ntile = q_ref.shape[0] // tq
    sub = m_scr.shape[1]
    lo = _half_masks((tq, LANES))
    rows = lambda c: slice(c * kc, (c + 1) * kc)
    lanes = lambda jb: slice(jb * LANES, (jb + 1) * LANES)
    tile_rows = lambda t: pl.ds(pl.multiple_of(t * tq, tq), tq)

    def half(j, t, do_qk, do_exp, do_pv):
        slot = j % 2
        jp, jv = (j - 1) % WIN_HEADS, (j - 2) % WIN_HEADS
        tv = t if j >= 2 else t - 1
        if do_qk:
            q = q_ref[tile_rows(t), lanes(j // 2)].astype(F32)
            zero = jnp.zeros_like(q)
            qm = (jnp.where(lo, q, zero) if j % 2 == 0 else jnp.where(lo, zero, q)).astype(BF16)
            mrun = jnp.full((sub, tq), NEG_INF, F32)
        if do_exp:
            mb = m_scr[1 - slot, 0:1, :]
        if do_pv:
            acc = jnp.zeros((o_scr.shape[1] + BF16_ROWS, tq), F32)
        for c in range(nchunk):
            if do_qk:
                s = scores(qm, t, j, c)
                s_scr[slot, rows(c), :] = s
                for u in range(kc // sub):
                    mrun = jnp.maximum(mrun, s[u * sub:(u + 1) * sub, :])
            if do_exp:
                p_scr[1 - slot, rows(c), :] = jnp.exp2(s_scr[1 - slot, rows(c), :] - mb).astype(BF16)
            if do_pv:
                acc = acc + jnp.dot(values(tv, jv, c), p_scr[slot, rows(c), :], preferred_element_type=F32)
        if do_qk:
            m = jnp.max(mrun, axis=0, keepdims=True)
            if extra_logit is not None:
                m = jnp.maximum(m, extra_logit(j))
            m_scr[slot] = jnp.broadcast_to(m, (sub, tq))
        if do_exp and extra_logit is not None:
            e_scr[1 - slot] = jnp.exp2(extra_logit(jp) - m_scr[1 - slot])
        if do_pv:
            l = acc[HEAD_DIM:HEAD_DIM + 1]
            if extra_logit is not None:
                l = l + e_scr[slot, 0:1, :]
            o_scr[jv] = acc[:HEAD_DIM] * (1.0 / l)

    def emit(t):
        for jb in range(WIN_HEADS // 2):
            o = jnp.concatenate([o_scr[2 * jb], o_scr[2 * jb + 1]], axis=0).T
            o_ref[tile_rows(t), lanes(jb)] = o.astype(o_ref.dtype)

    half(0, 0, True, False, False)
    half(1, 0, True, True, False)
    half(2, 0, True, True, True)
    half(3, 0, True, True, True)

    def body(t, carry):
        half(0, t, True, True, True)
        half(1, t, True, True, True)
        emit(t - 1)
        half(2, t, True, True, True)
        half(3, t, True, True, True)
        return carry

    lax.fori_loop(1, ntile, body, 0)
    half(0, ntile, False, True, True)
    half(1, ntile, False, False, True)
    emit(ntile - 1)


def _tile_class(t, ntile):
    return jnp.where(t == 0, 0, jnp.where(t == ntile - 1, 2, 1))


def _na_kernel(bias_ref, q_ref, k_ref, v_ref, kc_ref, vc_ref, o_ref, vt, s_scr, p_scr, m_scr, o_scr,
               *, tq, kc):
    seq, nctx = q_ref.shape[0], kc_ref.shape[0]
    ntile, nlat, nwin = seq // tq, seq // kc, 3 * tq // kc
    rows = lambda c: slice(c * kc, (c + 1) * kc)
    lanes = lambda jb: slice(jb * LANES, (jb + 1) * LANES)
    for jb in range(WIN_HEADS // 2):
        for c in range(nlat):
            _fill_vt(vt, 2 * jb, c, v_ref[rows(c), lanes(jb)], 2)
        for c in range(nctx // kc):
            _fill_vt(vt, 2 * jb, nlat + c, vc_ref[rows(c), lanes(jb)], 2)

    def win_start(t):
        return pl.multiple_of(jnp.clip((t - 1) * tq, 0, seq - 3 * tq), tq)

    def scores(qm, t, j, c):
        if c >= nwin:
            return _qk(kc_ref[rows(c - nwin), lanes(j // 2)], qm)
        k = k_ref[pl.ds(win_start(t) + c * kc, kc), lanes(j // 2)]
        return _qk(k, qm) + bias_ref[_tile_class(t, ntile), j, rows(c), :]

    def values(t, j, c):
        return vt[j, nlat + c - nwin] if c >= nwin else vt[j, win_start(t) // kc + c]

    _window_pipeline(q_ref, o_ref, s_scr, p_scr, m_scr, None, o_scr, tq=tq, kc=kc,
                     nchunk=nwin + nctx // kc, scores=scores, values=values, extra_logit=None)


def _plain_ctx_kernel(q_ref, kc_ref, vc_ref, o_ref):
    tq = q_ref.shape[0]
    lo = _half_masks((tq, LANES))
    qs = _split_heads(q_ref[...], lo)
    outs = [_softmax_pv([_qk(qs[hh], kc_ref[...])], [vc_ref[...]]) for hh in range(2)]
    o_ref[...] = jnp.where(lo, outs[0], outs[1]).astype(o_ref.dtype)


def _na_bias_tables(rpb, rows):
    w = GRID_W
    nk = 3 * NA_ROWS
    nh = rpb.shape[0]
    edge = w - NA_KW
    ext = jnp.concatenate([jnp.repeat(rpb[..., :1], edge, axis=-1), rpb,
                           jnp.repeat(rpb[..., -1:], edge + 1, axis=-1)], axis=-1)
    flat = jnp.tile(ext, (1, 1, w + 1))[..., :w * (2 * w + 1)]
    hankel = flat.reshape(nh, -1, w, 2 * w + 1)[..., :w]
    toep = jnp.flip(hankel, axis=-2)
    cq = np.arange(w)[:, None]
    cx = np.arange(w)[None, :]
    col_start = np.clip(cq - NA_KW // 2, 0, w - NA_KW)
    col_ok = (cx >= col_start) & (cx < col_start + NA_KW)
    toep = jnp.where(col_ok, toep * LOG2E, NEG_INF)
    masked = jnp.full((nh, w, w), NEG_INF, F32)
    tabs = []
    for r0, k0 in ((0, 0), (NA_ROWS, 0), (rows - NA_ROWS, rows - nk)):
        row_blocks = []
        for a in range(NA_ROWS):
            r = r0 + a
            rs = min(max(r - NA_KH // 2, 0), rows - NA_KH)
            blocks = []
            for e in range(nk):
                ry = k0 + e
                blocks.append(toep[:, ry - r + NA_KH - 1] if rs <= ry < rs + NA_KH else masked)
            row_blocks.append(jnp.concatenate(blocks, axis=-1))
        tabs.append(jnp.concatenate(row_blocks, axis=-2))
    return jnp.swapaxes(jnp.stack(tabs), -1, -2).astype(F32)


def _win_scratch(nheads, nchunks, nkeys_item, tq, kc, with_extra):
    shapes = [pltpu.VMEM((nheads, nchunks, HEAD_DIM + BF16_ROWS, kc), BF16),
              pltpu.VMEM((2, nkeys_item, tq), F32), pltpu.VMEM((2, nkeys_item, tq), BF16),
              pltpu.VMEM((2, F32_ROWS, tq), F32)]
    if with_extra:
        shapes.append(pltpu.VMEM((2, F32_ROWS, tq), F32))
    shapes.append(pltpu.VMEM((WIN_HEADS, HEAD_DIM, tq), F32))
    return shapes


def _na_attention(qkv, qkvc, rpb, batch, seq, nctx, need_ctx):
    d = qkv.shape[1] // 3
    nb = d // LANES
    rows = seq // GRID_W
    tq = NA_ROWS * GRID_W
    kc = tq
    qw = WIN_HEADS * HEAD_DIM
    ngrp = d // qw
    assert seq // tq >= 3 and nctx % kc == 0
    bias = _na_bias_tables(rpb.astype(F32), rows)
    y = pl.pallas_call(
        functools.partial(_na_kernel, tq=tq, kc=kc),
        out_shape=jax.ShapeDtypeStruct((batch * seq, d), BF16),
        grid=(ngrp, batch),
        in_specs=[
            pl.BlockSpec((3, WIN_HEADS, 3 * tq, tq), lambda hg, b: (0, hg, 0, 0),
                         pipeline_mode=pl.Buffered(1)),
            pl.BlockSpec((seq, qw), lambda hg, b: (b, hg)),
            pl.BlockSpec((seq, qw), lambda hg, b: (b, ngrp + hg)),
            pl.BlockSpec((seq, qw), lambda hg, b: (b, 2 * ngrp + hg)),
            pl.BlockSpec((nctx, qw), lambda hg, b: (b, ngrp + hg)),
            pl.BlockSpec((nctx, qw), lambda hg, b: (b, 2 * ngrp + hg)),
        ],
        out_specs=pl.BlockSpec((seq, qw), lambda hg, b: (b, hg)),
        scratch_shapes=_win_scratch(WIN_HEADS, (seq + nctx) // kc, 3 * tq + nctx, tq, kc, False),
        compiler_params=_cparams("parallel", "arbitrary"),
        name="na_attn",
    )(bias, qkv, qkv, qkv, qkvc, qkvc)
    yc = None
    if need_ctx:
        yc = pl.pallas_call(
            _plain_ctx_kernel,
            out_shape=jax.ShapeDtypeStruct((batch * nctx, d), BF16),
            grid=(batch, nb),
            in_specs=[
                pl.BlockSpec((nctx, LANES), lambda b, hb: (b, hb)),
                pl.BlockSpec((nctx, LANES), lambda b, hb: (b, nb + hb)),
                pl.BlockSpec((nctx, LANES), lambda b, hb: (b, 2 * nb + hb)),
            ],
            out_specs=pl.BlockSpec((nctx, LANES), lambda b, hb: (b, hb)),
            compiler_params=_cparams("parallel", "parallel"),
            name="na_attn_ctx",
        )(qkvc, qkvc, qkvc)
    return y, yc


def _swa_heads(q_ref, sinks_ref, g, score_fn, values, o_ref):
    tq = q_ref.shape[0]
    lo = _half_masks((tq, LANES))
    for jb in range(2):
        qs = _split_heads(q_ref[:, jb * LANES:(jb + 1) * LANES], lo)
        outs = []
        for hh in range(2):
            sink = jnp.full((1, 1), sinks_ref[g * 4 + jb * 2 + hh] * LOG2E, F32)
            outs.append(_softmax_pv(score_fn(qs[hh]), values, extra_logit=sink))
        o_ref[:, jb * LANES:(jb + 1) * LANES] = jnp.where(lo, outs[0], outs[1]).astype(o_ref.dtype)


def _swa_kernel(sinks_ref, band_ref, q_ref, k_ref, v_ref, kc_ref, vc_ref, o_ref,
                vt, s_scr, p_scr, m_scr, e_scr, o_scr, *, tq, kc, span):
    g = pl.program_id(1)
    seq, nctx = q_ref.shape[0], kc_ref.shape[0]
    ntile, nlat, nwin = seq // tq, seq // kc, span // kc
    rows = lambda c: slice(c * kc, (c + 1) * kc)
    for c in range(nlat):
        _fill_vt(vt, 0, c, v_ref[rows(c), :], 1)
    for c in range(nctx // kc):
        _fill_vt(vt, 0, nlat + c, vc_ref[rows(c), :], 1)

    def win_start(t):
        return pl.multiple_of(jnp.clip(t * tq - SWA_WINDOW, 0, seq - span), SWA_WINDOW)

    def scores(qm, t, j, c):
        if c >= nwin:
            return _qk(kc_ref[rows(c - nwin), :], qm)
        k = k_ref[pl.ds(win_start(t) + c * kc, kc), :]
        return _qk(k, qm) + band_ref[_tile_class(t, ntile), rows(c), :]

    def values(t, j, c):
        return vt[0, nlat + c - nwin] if c >= nwin else vt[0, win_start(t) // kc + c]

    def sink(j):
        return jnp.full((1, 1), sinks_ref[g * WIN_HEADS + j] * LOG2E, F32)

    _window_pipeline(q_ref, o_ref, s_scr, p_scr, m_scr, e_scr, o_scr, tq=tq, kc=kc,
                     nchunk=nwin + nctx // kc, scores=scores, values=values, extra_logit=sink)


def _swa_band_tables(seq, tq, span):
    ntile = seq // tq
    tabs = []
    for t in (0, 1, ntile - 1):
        start = min(max(t * tq - SWA_WINDOW, 0), seq - span)
        kpos = start + np.arange(span)[:, None]
        qpos = t * tq + np.arange(tq)[None, :]
        tabs.append(np.where(np.abs(qpos - kpos) <= SWA_WINDOW, 0.0, NEG_INF))
    return jnp.asarray(np.stack(tabs), F32)


def _swa_ctx_kernel(sinks_ref, q_ref, kc_ref, vc_ref, o_ref):
    g = pl.program_id(1)
    kc = kc_ref[...]
    _swa_heads(q_ref, sinks_ref, g, lambda qm: [_qk(qm, kc)], [vc_ref[...]], o_ref)


def _swa_attention(qkv, qkvc, sinks, batch, seq, nctx, need_ctx, d):
    nkv = (qkv.shape[1] - d) // (2 * LANES)
    qb = d // nkv
    kcol = d // LANES
    tq = SWA_TQ
    kc = SWA_WINDOW
    span = tq + 2 * SWA_WINDOW
    assert qb == WIN_HEADS * HEAD_DIM and seq // tq >= 3 and nctx % kc == 0
    smem = pl.BlockSpec(memory_space=pltpu.SMEM)
    sinks = sinks.astype(F32)
    y = pl.pallas_call(
        functools.partial(_swa_kernel, tq=tq, kc=kc, span=span),
        out_shape=jax.ShapeDtypeStruct((batch * seq, d), BF16),
        grid=(batch, nkv),
        in_specs=[
            smem,
            pl.BlockSpec((3, span, tq), lambda b, g: (0, 0, 0), pipeline_mode=pl.Buffered(1)),
            pl.BlockSpec((seq, qb), lambda b, g: (b, g)),
            pl.BlockSpec((seq, LANES), lambda b, g: (b, kcol + g)),
            pl.BlockSpec((seq, LANES), lambda b, g: (b, kcol + nkv + g)),
            pl.BlockSpec((nctx, LANES), lambda b, g: (b, kcol + g)),
            pl.BlockSpec((nctx, LANES), lambda b, g: (b, kcol + nkv + g)),
        ],
        out_specs=pl.BlockSpec((seq, qb), lambda b, g: (b, g)),
        scratch_shapes=_win_scratch(1, (seq + nctx) // kc, span + nctx, tq, kc, True),
        compiler_params=_cparams("parallel", "parallel"),
        name="swa_attn",
    )(sinks, _swa_band_tables(seq, tq, span), qkv, qkv, qkv, qkvc, qkvc)
    yc = None
    if need_ctx:
        yc = pl.pallas_call(
            _swa_ctx_kernel,
            out_shape=jax.ShapeDtypeStruct((batch * nctx, d), BF16),
            grid=(batch, nkv),
            in_specs=[
                smem,
                pl.BlockSpec((nctx, qb), lambda b, g: (b, g)),
                pl.BlockSpec((nctx, LANES), lambda b, g: (b, kcol + g)),
                pl.BlockSpec((nctx, LANES), lambda b, g: (b, kcol + nkv + g)),
            ],
            out_specs=pl.BlockSpec((nctx, qb), lambda b, g: (b, g)),
            compiler_params=_cparams("parallel", "parallel"),
            name="swa_attn_ctx",
        )(sinks, qkvc, qkvc, qkvc)
    return y, yc


def _oproj_kernel(o_ref, wo_ref, x_ref, g1_ref, nf_ref, sc2_ref, sh2_ref, xo_ref, h2_ref):
    y = jnp.dot(o_ref[...], wo_ref[...], preferred_element_type=F32)
    xn = x_ref[...] + g1_ref[...] * y
    xo_ref[...] = xn
    h2 = (_rms(xn) * nf_ref[...]) * (1.0 + sc2_ref[...]) + sh2_ref[...]
    h2_ref[...] = h2.astype(BF16)


def _oproj_call(o, wo, x, gamma, mod, *, tiles_per_seq, mod_row, name):
    n, d = x.shape
    t = T_PROJ

    def mrow(i):
        return mod_row if mod_row is not None else i // tiles_per_seq

    def mspec(which):
        return pl.BlockSpec((None, None, 1, d), lambda i: (mrow(i), which, 0, 0))

    return pl.pallas_call(
        _oproj_kernel,
        out_shape=(jax.ShapeDtypeStruct((n, d), F32), jax.ShapeDtypeStruct((n, d), BF16)),
        grid=(n // t,),
        in_specs=[
            pl.BlockSpec((t, d), lambda i: (i, 0)),
            pl.BlockSpec((d, d), lambda i: (0, 0)),
            pl.BlockSpec((t, d), lambda i: (i, 0)),
            mspec(2),
            pl.BlockSpec((1, d), lambda i: (0, 0)),
            mspec(4),
            mspec(3),
        ],
        out_specs=(pl.BlockSpec((t, d), lambda i: (i, 0)), pl.BlockSpec((t, d), lambda i: (i, 0))),
        compiler_params=_cparams("parallel"),
        name=name,
    )(o, wo, x, mod, gamma.reshape(1, d), mod, mod)


def _ffn_kernel(h_ref, hp_ref, hn_ref, wup_ref, cw_ref, cb_ref, wd_ref, x_ref, g2_ref, no_ref, o_ref,
                lhs, act, ubuf, *, nseq, final, f):
    i = pl.program_id(0)
    t = h_ref.shape[0]
    halo = hp_ref.shape[0]
    dff = wd_ref.shape[0]
    rows = t + 2 * halo

    keep_prev = ((i % nseq) != 0).astype(F32)
    keep_next = ((i % nseq) != nseq - 1).astype(F32)
    lhs[0:halo, :] = (hp_ref[...].astype(F32) * keep_prev).astype(BF16)
    lhs[halo:halo + t, :] = h_ref[...]
    lhs[halo + t:, :] = (hn_ref[...].astype(F32) * keep_next).astype(BF16)

    def conv(col0, slot):
        u = jnp.dot(lhs[...], wup_ref[:, col0:col0 + f], preferred_element_type=F32)
        outs = []
        for s in range(f // LANES):
            sl = slice(col0 + s * LANES, col0 + (s + 1) * LANES)
            buf = ubuf.at[slot, s]
            buf[pl.ds(0, rows, stride=2), :] = u[:, s * LANES:(s + 1) * LANES]
            taps = [buf[pl.ds(2 * (halo - 1 + k), t, stride=2), :] for k in range(CONV_W)]
            outs.append(cb_ref[:, sl] + taps[0] * cw_ref[0:1, sl] + taps[1] * cw_ref[1:2, sl]
                        + taps[2] * cw_ref[2:3, sl])
        return jnp.concatenate(outs, axis=1)

    for c in range(dff // f):
        slot = 2 * (c % 2)
        a = conv(c * f, slot)
        g = conv(dff + c * f, slot + 1)
        act[:, c * f:(c + 1) * f] = ((g * (1.0 / (1.0 + jnp.exp(-g)))) * a).astype(BF16)

    xn = x_ref[...] + g2_ref[...] * jnp.dot(act[...], wd_ref[...], preferred_element_type=F32)
    if final:
        xn = _rms(xn) * no_ref[...]
    o_ref[...] = xn


def _ffn_call(h2, x, w_up, w_conv, b_conv, w_down, mod, norm_out, *, t, tiles_per_seq, mod_row, final, name):
    n, d = x.shape
    dff = w_down.shape[0]
    f = FF_CHUNK
    halo = BF16_ROWS
    hb = t // halo
    last_hblock = n // halo - 1
    resident = pl.Buffered(1)

    def mrow(i):
        return mod_row if mod_row is not None else i // tiles_per_seq

    return pl.pallas_call(
        functools.partial(_ffn_kernel, nseq=tiles_per_seq, final=final, f=f),
        out_shape=jax.ShapeDtypeStruct((n, d), F32),
        grid=(n // t,),
        in_specs=[
            pl.BlockSpec((t, d), lambda i: (i, 0)),
            pl.BlockSpec((halo, d), lambda i: (jnp.maximum(i * hb - 1, 0), 0)),
            pl.BlockSpec((halo, d), lambda i: (jnp.minimum((i + 1) * hb, last_hblock), 0)),
            pl.BlockSpec((d, 2 * dff), lambda i: (0, 0), pipeline_mode=resident),
            pl.BlockSpec((CONV_W, 2 * dff), lambda i: (0, 0), pipeline_mode=resident),
            pl.BlockSpec((1, 2 * dff), lambda i: (0, 0), pipeline_mode=resident),
            pl.BlockSpec((dff, d), lambda i: (0, 0), pipeline_mode=resident),
            pl.BlockSpec((t, d), lambda i: (i, 0)),
            pl.BlockSpec((None, None, 1, d), lambda i: (mrow(i), 5, 0, 0)),
            pl.BlockSpec((1, d), lambda i: (0, 0)),
        ],
        out_specs=pl.BlockSpec((t, d), lambda i: (i, 0)),
        scratch_shapes=[pltpu.VMEM((t + 2 * halo, d), BF16), pltpu.VMEM((t, dff), BF16),
                        pltpu.VMEM((4, f // LANES, 2 * (t + 2 * halo), LANES), F32)],
        compiler_params=_cparams("parallel"),
        name=name,
    )(h2, h2, h2, w_up, w_conv, b_conv.reshape(1, 2 * dff), w_down, x, mod, norm_out.reshape(1, d))


def _rope_tables(seq, extra_rows):
    tpos = jnp.arange(seq, dtype=jnp.int32)
    row = (tpos // GRID_W).astype(F32)
    col = (tpos % GRID_W).astype(F32)
    inv = ROPE_BASE ** (-jnp.arange(ROPE_PAIRS, dtype=F32) / ROPE_PAIRS)
    ar = row[:, None] * inv[None, :]
    ac = col[:, None] * inv[None, :]
    cr, sr, cc, sc = jnp.cos(ar), jnp.sin(ar), jnp.cos(ac), jnp.sin(ac)
    cos = jnp.concatenate([cr, cr, cc, cc] * (LANES // HEAD_DIM), axis=1)
    sin = jnp.concatenate([-sr, sr, -sc, sc] * (LANES // HEAD_DIM), axis=1)
    cos = jnp.concatenate([cos, jnp.ones((extra_rows, LANES), F32)], axis=0)
    sin = jnp.concatenate([sin, jnp.zeros((extra_rows, LANES), F32)], axis=0)
    return cos * Q_SCALE, sin * Q_SCALE, cos, sin


def _dup_kv_columns(w, d, nkv):
    q = w[:, :d]
    k = w[:, d:d + nkv * HEAD_DIM].reshape(-1, nkv, 1, HEAD_DIM)
    v = w[:, d + nkv * HEAD_DIM:].reshape(-1, nkv, 1, HEAD_DIM)
    kd = jnp.broadcast_to(k, (w.shape[0], nkv, 2, HEAD_DIM)).reshape(w.shape[0], -1)
    vd = jnp.broadcast_to(v, (w.shape[0], nkv, 2, HEAD_DIM)).reshape(w.shape[0], -1)
    return jnp.concatenate([q, kd, vd], axis=1)


def kernel(x, c, ctx, c_ctx, ada_w, ada_b, norm_mix, norm_ffn, norm_out, ffn_up, ffn_conv, ffn_conv_b,
           ffn_down, a_wqkv, a_wo, a_lambda, a_subln, b_wqkv, b_wo, b_rpb, c_wqkv, c_wo, c_sinks):
    batch, seq, d = x.shape
    nctx = ctx.shape[1]
    depth = ada_w.shape[0]
    assert seq % T_FFN == 0 and seq % T_PROJ == 0 and (batch * nctx) % T_PROJ == 0

    mod_rows = 16
    cs = jnp.concatenate([c, c_ctx[None, :], jnp.zeros((mod_rows - batch - 1, d), F32)], axis=0)
    mod_all = _ada_call(cs, ada_w, ada_b).reshape(depth, mod_rows, 6, 1, d)
    ctx_row = batch

    tables = _rope_tables(seq, T_PROJ)
    lat_tiles = seq // T_PROJ
    nd = d // LANES

    xl = x.reshape(batch * seq, d)
    xc = ctx.reshape(batch * nctx, d)
    for i in range(depth):
        need_ctx = i < depth - 1
        kind, j = i % N_MIXERS, i // N_MIXERS
        mod = mod_all[i]
        if kind == 0:
            w = a_wqkv[j].astype(BF16)
            wo = a_wo[j].astype(BF16)
            kinds = [ROPEQ] * nd + [ROPEK] * nd + [PLAIN] * nd
            tabs = tables
        elif kind == 1:
            w = b_wqkv[j].astype(BF16)
            wo = b_wo[j].astype(BF16)
            kinds = [SCALEQ] * nd + [PLAIN] * (2 * nd)
            tabs = None
        else:
            nkv = (c_wqkv.shape[2] - d) // (2 * HEAD_DIM)
            w = _dup_kv_columns(c_wqkv[j], d, nkv).astype(BF16)
            wo = c_wo[j].astype(BF16)
            kinds = [ROPEQ] * nd + [ROPEK] * nkv + [PLAIN] * nkv
            tabs = tables
        qkv = _proj_call(xl, norm_mix[i], mod, w, kinds, tabs, tiles_per_seq=lat_tiles, mod_row=None,
                         name=f"proj{i}")
        qkvc = _proj_call(xc, norm_mix[i], mod, w, kinds, tabs, tiles_per_seq=lat_tiles, mod_row=ctx_row,
                          name=f"proj_ctx{i}")
        if kind == 0:
            lam_init = 0.8 - 0.6 * math.exp(-0.3 * i)
            y, yc = _diff_attention(qkv, qkvc, a_lambda[j].astype(F32), a_subln[j], lam_init,
                                    batch, seq, nctx, need_ctx)
        elif kind == 1:
            y, yc = _na_attention(qkv, qkvc, b_rpb[j], batch, seq, nctx, need_ctx)
        else:
            y, yc = _swa_attention(qkv, qkvc, c_sinks[j], batch, seq, nctx, need_ctx, d)

        w_up = ffn_up[i].astype(BF16)
        w_down = ffn_down[i].astype(BF16)
        xl, h2 = _oproj_call(y, wo, xl, norm_ffn[i], mod, tiles_per_seq=lat_tiles, mod_row=None,
                             name=f"oproj{i}")
        xl = _ffn_call(h2, xl, w_up, ffn_conv[i], ffn_conv_b[i], w_down, mod, norm_out,
                       t=T_FFN, tiles_per_seq=seq // T_FFN, mod_row=None, final=not need_ctx,
                       name=f"ffn{i}")
        if need_ctx:
            xc, h2c = _oproj_call(yc, wo, xc, norm_ffn[i], mod, tiles_per_seq=1, mod_row=ctx_row,
                                  name=f"oproj_ctx{i}")
            xc = _ffn_call(h2c, xc, w_up, ffn_conv[i], ffn_conv_b[i], w_down, mod, norm_out,
                           t=nctx, tiles_per_seq=1, mod_row=ctx_row, final=False, name=f"ffn_ctx{i}")
    return xl.reshape(batch, seq, d)
```

```python
import functools
import math

import numpy as np
import jax
import jax.numpy as jnp
from jax import lax
from jax.experimental import pallas as pl
from jax.experimental.pallas import tpu as pltpu

F32 = jnp.float32
BF16 = jnp.bfloat16

GRID_W = 64
HEAD_DIM = 64
ROPE_BASE = 10000.0
ROPE_PAIRS = HEAD_DIM // 4
NORM_EPS = 1e-6
NEG_INF = -1e30
N_MIXERS = 3
NA_KH = 8
NA_KW = 16
SWA_WINDOW = 128
CONV_W = 3
LOG2E = 1.4426950408889634
Q_SCALE = (HEAD_DIM ** -0.5) * LOG2E

LANES = 128
BF16_ROWS = 16
F32_ROWS = 8
VMEM_LIMIT = 52 * 1024 * 1024

T_PROJ = 512
T_FFN = 512
FF_CHUNK = 256
NA_ROWS = 4
SWA_TQ = 256
DIFF_TQ = 256
DIFF_KC = 256

PLAIN, SCALEQ, ROPEQ, ROPEK = 0, 1, 2, 3


def _cparams(*sem):
    return pltpu.CompilerParams(dimension_semantics=sem, vmem_limit_bytes=VMEM_LIMIT)


def _rms(x):
    return x * lax.rsqrt(jnp.mean(x * x, axis=-1, keepdims=True) + NORM_EPS)


def _ada_kernel(cs_ref, w_ref, b_ref, o_ref):
    cs = cs_ref[...]
    s = cs * (1.0 / (1.0 + jnp.exp(-cs)))
    w = w_ref[...]
    s_hi = s.astype(BF16)
    s_lo = (s - s_hi.astype(F32)).astype(BF16)
    w_hi = w.astype(BF16)
    w_lo = (w - w_hi.astype(F32)).astype(BF16)
    acc = jnp.dot(s_hi, w_hi, preferred_element_type=F32)
    acc = acc + jnp.dot(s_hi, w_lo, preferred_element_type=F32)
    acc = acc + jnp.dot(s_lo, w_hi, preferred_element_type=F32)
    o_ref[...] = acc + b_ref[...]


def _ada_call(cs, ada_w, ada_b):
    depth, d, n = ada_w.shape
    nt = 1536
    rows = cs.shape[0]
    return pl.pallas_call(
        _ada_kernel,
        out_shape=jax.ShapeDtypeStruct((depth, rows, n), F32),
        grid=(depth, n // nt),
        in_specs=[
            pl.BlockSpec((rows, d), lambda l, j: (0, 0)),
            pl.BlockSpec((None, d, nt), lambda l, j: (l, 0, j)),
            pl.BlockSpec((None, 1, nt), lambda l, j: (l, 0, j)),
        ],
        out_specs=pl.BlockSpec((None, rows, nt), lambda l, j: (l, 0, j)),
        compiler_params=_cparams("parallel", "parallel"),
        name="ada_mod",
    )(cs, ada_w, ada_b.reshape(depth, 1, n))


def _rope(r, c, s, first_half):
    partner = jnp.where(first_half, pltpu.roll(r, LANES - 16, 1), pltpu.roll(r, 16, 1))
    return r * c + partner * s


def _proj_kernel(*refs, kinds, nc, has_rope):
    if has_rope:
        x_ref, g_ref, sc_ref, sh_ref, w_ref, cq_ref, sq_ref, ck_ref, sk_ref, o_ref = refs
    else:
        x_ref, g_ref, sc_ref, sh_ref, w_ref, o_ref = refs
    t = x_ref.shape[0]
    nout = o_ref.shape[1]
    h = (_rms(x_ref[...]) * g_ref[...]) * (1.0 + sc_ref[...]) + sh_ref[...]
    hb = h.astype(BF16)
    if has_rope:
        lane = lax.broadcasted_iota(jnp.int32, (t, LANES), 1)
        first_half = (lane & 16) == 0
    for n0 in range(0, nout, nc):
        r = jnp.dot(hb, w_ref[:, n0:n0 + nc], preferred_element_type=F32)
        for j in range(nc // LANES):
            kind = kinds[n0 // LANES + j]
            rj = r[:, j * LANES:(j + 1) * LANES]
            if kind == ROPEQ:
                rj = _rope(rj, cq_ref[...], sq_ref[...], first_half)
            elif kind == ROPEK:
                rj = _rope(rj, ck_ref[...], sk_ref[...], first_half)
            elif kind == SCALEQ:
                rj = rj * Q_SCALE
            o_ref[:, n0 + j * LANES:n0 + (j + 1) * LANES] = rj.astype(BF16)


def _proj_call(x, gamma, mod, w, kinds, tables, *, tiles_per_seq, mod_row, name):
    n, d = x.shape
    nout = w.shape[1]
    t = T_PROJ
    has_rope = tables is not None

    def mrow(i):
        return mod_row if mod_row is not None else i // tiles_per_seq

    def tblock(i):
        return i % tiles_per_seq if mod_row is None else tiles_per_seq

    in_specs = [
        pl.BlockSpec((t, d), lambda i: (i, 0)),
        pl.BlockSpec((1, d), lambda i: (0, 0)),
        pl.BlockSpec((None, None, 1, d), lambda i: (mrow(i), 1, 0, 0)),
        pl.BlockSpec((None, None, 1, d), lambda i: (mrow(i), 0, 0, 0)),
        pl.BlockSpec((d, nout), lambda i: (0, 0)),
    ]
    args = [x, gamma.reshape(1, d), mod, mod, w]
    if has_rope:
        for tab in tables:
            in_specs.append(pl.BlockSpec((t, LANES), lambda i: (tblock(i), 0)))
            args.append(tab)
    return pl.pallas_call(
        functools.partial(_proj_kernel, kinds=tuple(kinds), nc=512, has_rope=has_rope),
        out_shape=jax.ShapeDtypeStruct((n, nout), BF16),
        grid=(n // t,),
        in_specs=in_specs,
        out_specs=pl.BlockSpec((t, nout), lambda i: (i, 0)),
        compiler_params=_cparams("parallel"),
        name=name,
    )(*args)


def _qk(q, k):
    return lax.dot_general(q, k, (((1,), (1,)), ((), ())), preferred_element_type=F32)


def _half_masks(shape):
    lane = lax.broadcasted_iota(jnp.int32, shape, 1)
    return lane < HEAD_DIM


def _split_heads(q, lo):
    qf = q.astype(F32)
    zero = jnp.zeros_like(qf)
    return jnp.where(lo, qf, zero).astype(q.dtype), jnp.where(lo, zero, qf).astype(q.dtype)


def _diff_lambda(lam_ref, lam_init):
    lv = lam_ref[...]
    d1 = jnp.sum(lv[0:1] * lv[1:2], axis=1, keepdims=True)
    d2 = jnp.sum(lv[2:3] * lv[3:4], axis=1, keepdims=True)
    return jnp.exp(d1) - jnp.exp(d2) + lam_init


def _diff_finish(o1, o2, lam, subln_ref, lam_init, o_ref):
    o = _rms(o1 - lam * o2) * subln_ref[...] * (1.0 - lam_init)
    o_ref[...] = o.astype(o_ref.dtype)


def _diff_kernel(lam_ref, subln_ref, q_ref, kl_ref, vl_ref, kc_ref, vc_ref, o_ref,
                 vt, s_scr, p_scr, m_scr, o_scr, *, lam_init, tq, kc):
    seq = q_ref.shape[0]
    nkeys = vt.shape[1]
    nlat = seq // kc
    nchunk = nkeys // kc
    ntile = seq // tq
    sub = m_scr.shape[1]
    lam = _diff_lambda(lam_ref, lam_init)
    lo = _half_masks((tq, LANES))
    rows = lambda c: slice(c * kc, (c + 1) * kc)

    def chunk_of(lat_ref, ctx_ref, c):
        if c < nlat:
            return lat_ref[rows(c), :]
        return ctx_ref[rows(c - nlat), :]

    eye = (lax.broadcasted_iota(jnp.int32, (LANES, LANES), 0)
           == lax.broadcasted_iota(jnp.int32, (LANES, LANES), 1)).astype(F32).astype(BF16)
    for c in range(nchunk):
        vt[0:LANES, rows(c)] = _qk(eye, chunk_of(vl_ref, vc_ref, c)).astype(BF16)
    vt[LANES:, :] = jnp.ones((vt.shape[0] - LANES, nkeys), BF16)

    def half(m, tile, do_qk, do_exp, do_pv):
        if do_qk:
            q = q_ref[pl.ds(pl.multiple_of(tile * tq, tq), tq), :].astype(F32)
            zero = jnp.zeros_like(q)
            qm = (jnp.where(lo, q, zero) if m == 0 else jnp.where(lo, zero, q)).astype(BF16)
            mrun = jnp.full((sub, tq), NEG_INF, F32)
        if do_exp:
            mb = m_scr[1 - m, 0:1, :]
        if do_pv:
            acc = jnp.zeros((vt.shape[0], tq), F32)
        for c in range(nchunk):
            if do_qk:
                s = _qk(chunk_of(kl_ref, kc_ref, c), qm)
                s_scr[m, rows(c), :] = s
                for u in range(kc // sub):
                    mrun = jnp.maximum(mrun, s[u * sub:(u + 1) * sub, :])
            if do_pv:
                acc = acc + jnp.dot(vt[:, rows(c)], p_scr[1 - m, rows(c), :], preferred_element_type=F32)
            if do_exp:
                p_scr[1 - m, rows(c), :] = jnp.exp2(s_scr[1 - m, rows(c), :] - mb).astype(BF16)
        if do_qk:
            m_scr[m] = jnp.broadcast_to(jnp.max(mrun, axis=0, keepdims=True), (sub, tq))
        if do_pv:
            return acc[:LANES] * (1.0 / acc[LANES:LANES + 1])
        return None

    def emit(tile, o2):
        o = _rms((o_scr[...] - lam * o2).T) * subln_ref[...] * (1.0 - lam_init)
        o_ref[pl.ds(pl.multiple_of(tile * tq, tq), tq), :] = o.astype(o_ref.dtype)

    half(0, 0, True, False, False)
    half(1, 0, True, True, False)
    half(0, 1, True, True, False)
    o_scr[...] = half(1, 1, True, True, True)

    def body(t, carry):
        emit(t - 2, half(0, t, True, True, True))
        o_scr[...] = half(1, t, True, True, True)
        return carry

    lax.fori_loop(2, ntile, body, 0)
    emit(ntile - 2, half(0, ntile, False, True, True))
    o_scr[...] = half(1, ntile, False, False, True)
    emit(ntile - 1, half(0, ntile + 1, False, False, True))


def _diff_ctx_kernel(lam_ref, subln_ref, q_ref, kc_ref, vc_ref, o_ref, *, lam_init):
    tq = q_ref.shape[0]
    lam = _diff_lambda(lam_ref, lam_init)
    qs = _split_heads(q_ref[...], _half_masks((tq, LANES)))
    o1, o2 = [_softmax_pv([_qk(qm, kc_ref[...])], [vc_ref[...]]) for qm in qs]
    _diff_finish(o1, o2, lam, subln_ref, lam_init, o_ref)


def _diff_attention(qkv, qkvc, lam_vecs, subln, lam_init, batch, seq, nctx, need_ctx):
    d = qkv.shape[1] // 3
    nh = d // LANES
    tq = DIFF_TQ
    kc = DIFF_KC
    assert seq % tq == 0 and seq % kc == 0 and nctx % kc == 0 and seq // tq >= 2
    nkeys = seq + nctx
    subln2 = subln.reshape(1, LANES)
    y = pl.pallas_call(
        functools.partial(_diff_kernel, lam_init=lam_init, tq=tq, kc=kc),
        out_shape=jax.ShapeDtypeStruct((batch * seq, d), BF16),
        grid=(batch, nh),
        in_specs=[
            pl.BlockSpec((4, HEAD_DIM), lambda b, h: (0, 0)),
            pl.BlockSpec((1, LANES), lambda b, h: (0, 0)),
            pl.BlockSpec((seq, LANES), lambda b, h: (b, h)),
            pl.BlockSpec((seq, LANES), lambda b, h: (b, nh + h)),
            pl.BlockSpec((seq, LANES), lambda b, h: (b, 2 * nh + h)),
            pl.BlockSpec((nctx, LANES), lambda b, h: (b, nh + h)),
            pl.BlockSpec((nctx, LANES), lambda b, h: (b, 2 * nh + h)),
        ],
        out_specs=pl.BlockSpec((seq, LANES), lambda b, h: (b, h)),
        scratch_shapes=[pltpu.VMEM((LANES + BF16_ROWS, nkeys), BF16), pltpu.VMEM((2, nkeys, tq), F32),
                        pltpu.VMEM((2, nkeys, tq), BF16), pltpu.VMEM((2, F32_ROWS, tq), F32),
                        pltpu.VMEM((LANES, tq), F32)],
        compiler_params=_cparams("parallel", "parallel"),
        name="diff_attn",
    )(lam_vecs, subln2, qkv, qkv, qkv, qkvc, qkvc)
    yc = None
    if need_ctx:
        yc = pl.pallas_call(
            functools.partial(_diff_ctx_kernel, lam_init=lam_init),
            out_shape=jax.ShapeDtypeStruct((batch * nctx, d), BF16),
            grid=(batch, nh),
            in_specs=[
                pl.BlockSpec((4, HEAD_DIM), lambda b, h: (0, 0)),
                pl.BlockSpec((1, LANES), lambda b, h: (0, 0)),
                pl.BlockSpec((nctx, LANES), lambda b, h: (b, h)),
                pl.BlockSpec((nctx, LANES), lambda b, h: (b, nh + h)),
                pl.BlockSpec((nctx, LANES), lambda b, h: (b, 2 * nh + h)),
            ],
            out_specs=pl.BlockSpec((nctx, LANES), lambda b, h: (b, h)),
            compiler_params=_cparams("parallel", "parallel"),
            name="diff_attn_ctx",
        )(lam_vecs, subln2, qkvc, qkvc, qkvc)
    return y, yc


def _softmax_pv(scores, values, extra_logit=None):
    m = jnp.max(scores[0], axis=1, keepdims=True)
    for s in scores[1:]:
        m = jnp.maximum(m, jnp.max(s, axis=1, keepdims=True))
    if extra_logit is not None:
        m = jnp.maximum(m, extra_logit)
    l = None
    acc = None
    for s, v in zip(scores, values):
        p = jnp.exp2(s - m)
        ps = jnp.sum(p, axis=1, keepdims=True)
        pv = jnp.dot(p.astype(BF16), v, preferred_element_type=F32)
        l = ps if l is None else l + ps
        acc = pv if acc is None else acc + pv
    if extra_logit is not None:
        l = l + jnp.exp2(extra_logit - m)
    return acc * (1.0 / l)


def _identity(n):
    return (lax.broadcasted_iota(jnp.int32, (n, n), 0)
            == lax.broadcasted_iota(jnp.int32, (n, n), 1)).astype(F32).astype(BF16)


def _fill_vt(vt, head0, chunk, v_block, n_heads):
    tr = _qk(_identity(LANES), v_block).astype(BF16)
    for hh in range(n_heads):
        vt[head0 + hh, chunk, 0:HEAD_DIM, :] = tr[hh * HEAD_DIM:(hh + 1) * HEAD_DIM]
        vt[head0 + hh, chunk, HEAD_DIM:, :] = jnp.ones((vt.shape[2] - HEAD_DIM, vt.shape[3]), BF16)


def _window_pipeline(q_ref, o_ref, s_scr, p_scr, m_scr, e_scr, o_scr, *, tq, kc, nchunk,
                     scores, values, extra_logit):
    ntile = q_ref.shape[0] // tq
    sub = m_scr.shape[1]
    lo = _half_masks((tq, LANES))
    rows = lambda c: slice(c * kc, (c + 1) * kc)
    tile_rows = lambda t: pl.ds(pl.multiple_of(t * tq, tq), tq)

    def half(j, t, do_qk, do_exp, do_pv):
        jo = 1 - j
        tv = t - 2 + j
        if do_qk:
            q = q_ref[tile_rows(t), :].astype(F32)
            zero = jnp.zeros_like(q)
            qm = (jnp.where(lo, q, zero) if j == 0 else jnp.where(lo, zero, q)).astype(BF16)
            mrun = jnp.full((sub, tq), NEG_INF, F32)
        if do_exp:
            mb = m_scr[jo, 0:1, :]
        if do_pv:
            acc = jnp.zeros((o_scr.shape[0] + BF16_ROWS, tq), F32)
        for c in range(nchunk):
            if do_qk:
                s = scores(qm, t, j, c)
                s_scr[j, rows(c), :] = s
                for u in range(kc // sub):
                    mrun = jnp.maximum(mrun, s[u * sub:(u + 1) * sub, :])
            if do_pv:
                acc = acc + jnp.dot(values(tv, jo, c), p_scr[jo, rows(c), :], preferred_element_type=F32)
            if do_exp:
                p_scr[jo, rows(c), :] = jnp.exp2(s_scr[jo, rows(c), :] - mb).astype(BF16)
        if do_qk:
            m = jnp.max(mrun, axis=0, keepdims=True)
            if extra_logit is not None:
                m = jnp.maximum(m, extra_logit(j))
            m_scr[j] = jnp.broadcast_to(m, (sub, tq))
        o = None
        if do_pv:
            l = acc[HEAD_DIM:HEAD_DIM + 1]
            if extra_logit is not None:
                l = l + e_scr[jo, 0:1, :]
            o = acc[:HEAD_DIM] * (1.0 / l)
        if do_exp and extra_logit is not None:
            e_scr[jo] = jnp.exp2(extra_logit(jo) - m_scr[jo])
        return o

    def emit(t, o1):
        o = jnp.concatenate([o_scr[...], o1], axis=0).T
        o_ref[tile_rows(t), :] = o.astype(o_ref.dtype)

    half(0, 0, True, False, False)
    half(1, 0, True, True, False)
    half(0, 1, True, True, False)
    o_scr[...] = half(1, 1, True, True, True)

    def body(t, carry):
        emit(t - 2, half(0, t, True, True, True))
        o_scr[...] = half(1, t, True, True, True)
        return carry

    lax.fori_loop(2, ntile, body, 0)
    emit(ntile - 2, half(0, ntile, False, True, True))
    o_scr[...] = half(1, ntile, False, False, True)
    emit(ntile - 1, half(0, ntile + 1, False, False, True))


def _tile_class(t, ntile):
    return jnp.where(t == 0, 0, jnp.where(t == ntile - 1, 2, 1))


def _na_kernel(bias_ref, q_ref, k_ref, v_ref, kc_ref, vc_ref, o_ref, vt, s_scr, p_scr, m_scr, o_scr,
               *, tq, kc):
    seq, nctx = q_ref.shape[0], kc_ref.shape[0]
    ntile, nlat, nwin = seq // tq, seq // kc, 3 * tq // kc
    rows = lambda c: slice(c * kc, (c + 1) * kc)
    for c in range(nlat):
        _fill_vt(vt, 0, c, v_ref[rows(c), :], 2)
    for c in range(nctx // kc):
        _fill_vt(vt, 0, nlat + c, vc_ref[rows(c), :], 2)

    def win_start(t):
        return pl.multiple_of(jnp.clip((t - 1) * tq, 0, seq - 3 * tq), tq)

    def scores(qm, t, j, c):
        if c >= nwin:
            return _qk(kc_ref[rows(c - nwin), :], qm)
        k = k_ref[pl.ds(win_start(t) + c * kc, kc), :]
        return _qk(k, qm) + bias_ref[_tile_class(t, ntile), j, rows(c), :]

    def values(t, j, c):
        return vt[j, nlat + c - nwin] if c >= nwin else vt[j, win_start(t) // kc + c]

    _window_pipeline(q_ref, o_ref, s_scr, p_scr, m_scr, None, o_scr, tq=tq, kc=kc,
                     nchunk=nwin + nctx // kc, scores=scores, values=values, extra_logit=None)


def _plain_ctx_kernel(q_ref, kc_ref, vc_ref, o_ref):
    tq = q_ref.shape[0]
    lo = _half_masks((tq, LANES))
    qs = _split_heads(q_ref[...], lo)
    outs = [_softmax_pv([_qk(qs[hh], kc_ref[...])], [vc_ref[...]]) for hh in range(2)]
    o_ref[...] = jnp.where(lo, outs[0], outs[1]).astype(o_ref.dtype)


def _na_bias_tables(rpb, rows):
    w = GRID_W
    nk = 3 * NA_ROWS
    nh = rpb.shape[0]
    edge = w - NA_KW
    ext = jnp.concatenate([jnp.repeat(rpb[..., :1], edge, axis=-1), rpb,
                           jnp.repeat(rpb[..., -1:], edge + 1, axis=-1)], axis=-1)
    flat = jnp.tile(ext, (1, 1, w + 1))[..., :w * (2 * w + 1)]
    hankel = flat.reshape(nh, -1, w, 2 * w + 1)[..., :w]
    toep = jnp.flip(hankel, axis=-2)
    cq = np.arange(w)[:, None]
    cx = np.arange(w)[None, :]
    col_start = np.clip(cq - NA_KW // 2, 0, w - NA_KW)
    col_ok = (cx >= col_start) & (cx < col_start + NA_KW)
    toep = jnp.where(col_ok, toep * LOG2E, NEG_INF)
    masked = jnp.full((nh, w, w), NEG_INF, F32)
    tabs = []
    for r0, k0 in ((0, 0), (NA_ROWS, 0), (rows - NA_ROWS, rows - nk)):
        row_blocks = []
        for a in range(NA_ROWS):
            r = r0 + a
            rs = min(max(r - NA_KH // 2, 0), rows - NA_KH)
            blocks = []
            for e in range(nk):
                ry = k0 + e
                blocks.append(toep[:, ry - r + NA_KH - 1] if rs <= ry < rs + NA_KH else masked)
            row_blocks.append(jnp.concatenate(blocks, axis=-1))
        tabs.append(jnp.concatenate(row_blocks, axis=-2))
    return jnp.swapaxes(jnp.stack(tabs), -1, -2).astype(F32)


def _win_scratch(nheads, nchunks, nkeys_item, tq, kc, with_extra):
    shapes = [pltpu.VMEM((nheads, nchunks, HEAD_DIM + BF16_ROWS, kc), BF16),
              pltpu.VMEM((2, nkeys_item, tq), F32), pltpu.VMEM((2, nkeys_item, tq), BF16),
              pltpu.VMEM((2, F32_ROWS, tq), F32)]
    if with_extra:
        shapes.append(pltpu.VMEM((2, F32_ROWS, tq), F32))
    shapes.append(pltpu.VMEM((HEAD_DIM, tq), F32))
    return shapes


def _na_attention(qkv, qkvc, rpb, batch, seq, nctx, need_ctx):
    d = qkv.shape[1] // 3
    nb = d // LANES
    rows = seq // GRID_W
    tq = NA_ROWS * GRID_W
    kc = tq
    assert seq // tq >= 3 and nctx % kc == 0
    bias = _na_bias_tables(rpb.astype(F32), rows)
    y = pl.pallas_call(
        functools.partial(_na_kernel, tq=tq, kc=kc),
        out_shape=jax.ShapeDtypeStruct((batch * seq, d), BF16),
        grid=(nb, batch),
        in_specs=[
            pl.BlockSpec((3, 2, 3 * tq, tq), lambda hb, b: (0, hb, 0, 0), pipeline_mode=pl.Buffered(1)),
            pl.BlockSpec((seq, LANES), lambda hb, b: (b, hb)),
            pl.BlockSpec((seq, LANES), lambda hb, b: (b, nb + hb)),
            pl.BlockSpec((seq, LANES), lambda hb, b: (b, 2 * nb + hb)),
            pl.BlockSpec((nctx, LANES), lambda hb, b: (b, nb + hb)),
            pl.BlockSpec((nctx, LANES), lambda hb, b: (b, 2 * nb + hb)),
        ],
        out_specs=pl.BlockSpec((seq, LANES), lambda hb, b: (b, hb)),
        scratch_shapes=_win_scratch(2, (seq + nctx) // kc, 3 * tq + nctx, tq, kc, False),
        compiler_params=_cparams("parallel", "arbitrary"),
        name="na_attn",
    )(bias, qkv, qkv, qkv, qkvc, qkvc)
    yc = None
    if need_ctx:
        yc = pl.pallas_call(
            _plain_ctx_kernel,
            out_shape=jax.ShapeDtypeStruct((batch * nctx, d), BF16),
            grid=(batch, nb),
            in_specs=[
                pl.BlockSpec((nctx, LANES), lambda b, hb: (b, hb)),
                pl.BlockSpec((nctx, LANES), lambda b, hb: (b, nb + hb)),
                pl.BlockSpec((nctx, LANES), lambda b, hb: (b, 2 * nb + hb)),
            ],
            out_specs=pl.BlockSpec((nctx, LANES), lambda b, hb: (b, hb)),
            compiler_params=_cparams("parallel", "parallel"),
            name="na_attn_ctx",
        )(qkvc, qkvc, qkvc)
    return y, yc


def _swa_heads(q_ref, sinks_ref, g, score_fn, values, o_ref):
    tq = q_ref.shape[0]
    lo = _half_masks((tq, LANES))
    for jb in range(2):
        qs = _split_heads(q_ref[:, jb * LANES:(jb + 1) * LANES], lo)
        outs = []
        for hh in range(2):
            sink = jnp.full((1, 1), sinks_ref[g * 4 + jb * 2 + hh] * LOG2E, F32)
            outs.append(_softmax_pv(score_fn(qs[hh]), values, extra_logit=sink))
        o_ref[:, jb * LANES:(jb + 1) * LANES] = jnp.where(lo, outs[0], outs[1]).astype(o_ref.dtype)


def _swa_kernel(sinks_ref, band_ref, q_ref, k_ref, v_ref, kc_ref, vc_ref, o_ref,
                vt, s_scr, p_scr, m_scr, e_scr, o_scr, *, tq, kc, span):
    head0 = (pl.program_id(1) * 2 + pl.program_id(2)) * 2
    seq, nctx = q_ref.shape[0], kc_ref.shape[0]
    ntile, nlat, nwin = seq // tq, seq // kc, span // kc
    rows = lambda c: slice(c * kc, (c + 1) * kc)
    for c in range(nlat):
        _fill_vt(vt, 0, c, v_ref[rows(c), :], 1)
    for c in range(nctx // kc):
        _fill_vt(vt, 0, nlat + c, vc_ref[rows(c), :], 1)

    def win_start(t):
        return pl.multiple_of(jnp.clip(t * tq - SWA_WINDOW, 0, seq - span), SWA_WINDOW)

    def scores(qm, t, j, c):
        if c >= nwin:
            return _qk(kc_ref[rows(c - nwin), :], qm)
        k = k_ref[pl.ds(win_start(t) + c * kc, kc), :]
        return _qk(k, qm) + band_ref[_tile_class(t, ntile), rows(c), :]

    def values(t, j, c):
        return vt[0, nlat + c - nwin] if c >= nwin else vt[0, win_start(t) // kc + c]

    def sink(j):
        return jnp.full((1, 1), sinks_ref[head0 + j] * LOG2E, F32)

    _window_pipeline(q_ref, o_ref, s_scr, p_scr, m_scr, e_scr, o_scr, tq=tq, kc=kc,
                     nchunk=nwin + nctx // kc, scores=scores, values=values, extra_logit=sink)


def _swa_band_tables(seq, tq, span):
    ntile = seq // tq
    tabs = []
    for t in (0, 1, ntile - 1):
        start = min(max(t * tq - SWA_WINDOW, 0), seq - span)
        kpos = start + np.arange(span)[:, None]
        qpos = t * tq + np.arange(tq)[None, :]
        tabs.append(np.where(np.abs(qpos - kpos) <= SWA_WINDOW, 0.0, NEG_INF))
    return jnp.asarray(np.stack(tabs), F32)


def _swa_ctx_kernel(sinks_ref, q_ref, kc_ref, vc_ref, o_ref):
    g = pl.program_id(1)
    kc = kc_ref[...]
    _swa_heads(q_ref, sinks_ref, g, lambda qm: [_qk(qm, kc)], [vc_ref[...]], o_ref)


def _swa_attention(qkv, qkvc, sinks, batch, seq, nctx, need_ctx, d):
    nkv = (qkv.shape[1] - d) // (2 * LANES)
    qb = d // nkv
    kcol = d // LANES
    tq = SWA_TQ
    kc = SWA_WINDOW
    span = tq + 2 * SWA_WINDOW
    assert qb == 2 * LANES and seq // tq >= 3 and nctx % kc == 0
    smem = pl.BlockSpec(memory_space=pltpu.SMEM)
    sinks = sinks.astype(F32)
    y = pl.pallas_call(
        functools.partial(_swa_kernel, tq=tq, kc=kc, span=span),
        out_shape=jax.ShapeDtypeStruct((batch * seq, d), BF16),
        grid=(batch, nkv, 2),
        in_specs=[
            smem,
            pl.BlockSpec((3, span, tq), lambda b, g, jb: (0, 0, 0), pipeline_mode=pl.Buffered(1)),
            pl.BlockSpec((seq, LANES), lambda b, g, jb: (b, 2 * g + jb)),
            pl.BlockSpec((seq, LANES), lambda b, g, jb: (b, kcol + g)),
            pl.BlockSpec((seq, LANES), lambda b, g, jb: (b, kcol + nkv + g)),
            pl.BlockSpec((nctx, LANES), lambda b, g, jb: (b, kcol + g)),
            pl.BlockSpec((nctx, LANES), lambda b, g, jb: (b, kcol + nkv + g)),
        ],
        out_specs=pl.BlockSpec((seq, LANES), lambda b, g, jb: (b, 2 * g + jb)),
        scratch_shapes=_win_scratch(1, (seq + nctx) // kc, span + nctx, tq, kc, True),
        compiler_params=_cparams("parallel", "parallel", "arbitrary"),
        name="swa_attn",
    )(sinks, _swa_band_tables(seq, tq, span), qkv, qkv, qkv, qkvc, qkvc)
    yc = None
    if need_ctx:
        yc = pl.pallas_call(
            _swa_ctx_kernel,
            out_shape=jax.ShapeDtypeStruct((batch * nctx, d), BF16),
            grid=(batch, nkv),
            in_specs=[
                smem,
                pl.BlockSpec((nctx, qb), lambda b, g: (b, g)),
                pl.BlockSpec((nctx, LANES), lambda b, g: (b, kcol + g)),
                pl.BlockSpec((nctx, LANES), lambda b, g: (b, kcol + nkv + g)),
            ],
            out_specs=pl.BlockSpec((nctx, qb), lambda b, g: (b, g)),
            compiler_params=_cparams("parallel", "parallel"),
            name="swa_attn_ctx",
        )(sinks, qkvc, qkvc, qkvc)
    return y, yc


def _oproj_kernel(o_ref, wo_ref, x_ref, g1_ref, nf_ref, sc2_ref, sh2_ref, xo_ref, h2_ref):
    y = jnp.dot(o_ref[...], wo_ref[...], preferred_element_type=F32)
    xn = x_ref[...] + g1_ref[...] * y
    xo_ref[...] = xn
    h2 = (_rms(xn) * nf_ref[...]) * (1.0 + sc2_ref[...]) + sh2_ref[...]
    h2_ref[...] = h2.astype(BF16)


def _oproj_call(o, wo, x, gamma, mod, *, tiles_per_seq, mod_row, name):
    n, d = x.shape
    t = T_PROJ

    def mrow(i):
        return mod_row if mod_row is not None else i // tiles_per_seq

    def mspec(which):
        return pl.BlockSpec((None, None, 1, d), lambda i: (mrow(i), which, 0, 0))

    return pl.pallas_call(
        _oproj_kernel,
        out_shape=(jax.ShapeDtypeStruct((n, d), F32), jax.ShapeDtypeStruct((n, d), BF16)),
        grid=(n // t,),
        in_specs=[
            pl.BlockSpec((t, d), lambda i: (i, 0)),
            pl.BlockSpec((d, d), lambda i: (0, 0)),
            pl.BlockSpec((t, d), lambda i: (i, 0)),
            mspec(2),
            pl.BlockSpec((1, d), lambda i: (0, 0)),
            mspec(4),
            mspec(3),
        ],
        out_specs=(pl.BlockSpec((t, d), lambda i: (i, 0)), pl.BlockSpec((t, d), lambda i: (i, 0))),
        compiler_params=_cparams("parallel"),
        name=name,
    )(o, wo, x, mod, gamma.reshape(1, d), mod, mod)


def _ffn_kernel(h_ref, hp_ref, hn_ref, wup_ref, cw_ref, cb_ref, wd_ref, x_ref, g2_ref, no_ref, o_ref,
                lhs, act, ubuf, *, nseq, final, f):
    i = pl.program_id(0)
    t = h_ref.shape[0]
    halo = hp_ref.shape[0]
    dff = wd_ref.shape[0]
    rows = t + 2 * halo

    keep_prev = ((i % nseq) != 0).astype(F32)
    keep_next = ((i % nseq) != nseq - 1).astype(F32)
    lhs[0:halo, :] = (hp_ref[...].astype(F32) * keep_prev).astype(BF16)
    lhs[halo:halo + t, :] = h_ref[...]
    lhs[halo + t:, :] = (hn_ref[...].astype(F32) * keep_next).astype(BF16)

    def conv(col0, slot):
        u = jnp.dot(lhs[...], wup_ref[:, col0:col0 + f], preferred_element_type=F32)
        outs = []
        for s in range(f // LANES):
            sl = slice(col0 + s * LANES, col0 + (s + 1) * LANES)
            buf = ubuf.at[slot, s]
            buf[pl.ds(0, rows, stride=2), :] = u[:, s * LANES:(s + 1) * LANES]
            taps = [buf[pl.ds(2 * (halo - 1 + k), t, stride=2), :] for k in range(CONV_W)]
            outs.append(cb_ref[:, sl] + taps[0] * cw_ref[0:1, sl] + taps[1] * cw_ref[1:2, sl]
                        + taps[2] * cw_ref[2:3, sl])
        return jnp.concatenate(outs, axis=1)

    for c in range(dff // f):
        slot = 2 * (c % 2)
        a = conv(c * f, slot)
        g = conv(dff + c * f, slot + 1)
        act[:, c * f:(c + 1) * f] = ((g * (1.0 / (1.0 + jnp.exp(-g)))) * a).astype(BF16)

    xn = x_ref[...] + g2_ref[...] * jnp.dot(act[...], wd_ref[...], preferred_element_type=F32)
    if final:
        xn = _rms(xn) * no_ref[...]
    o_ref[...] = xn


def _ffn_call(h2, x, w_up, w_conv, b_conv, w_down, mod, norm_out, *, t, tiles_per_seq, mod_row, final, name):
    n, d = x.shape
    dff = w_down.shape[0]
    f = FF_CHUNK
    halo = BF16_ROWS
    hb = t // halo
    last_hblock = n // halo - 1
    resident = pl.Buffered(1)

    def mrow(i):
        return mod_row if mod_row is not None else i // tiles_per_seq

    return pl.pallas_call(
        functools.partial(_ffn_kernel, nseq=tiles_per_seq, final=final, f=f),
        out_shape=jax.ShapeDtypeStruct((n, d), F32),
        grid=(n // t,),
        in_specs=[
            pl.BlockSpec((t, d), lambda i: (i, 0)),
            pl.BlockSpec((halo, d), lambda i: (jnp.maximum(i * hb - 1, 0), 0)),
            pl.BlockSpec((halo, d), lambda i: (jnp.minimum((i + 1) * hb, last_hblock), 0)),
            pl.BlockSpec((d, 2 * dff), lambda i: (0, 0), pipeline_mode=resident),
            pl.BlockSpec((CONV_W, 2 * dff), lambda i: (0, 0), pipeline_mode=resident),
            pl.BlockSpec((1, 2 * dff), lambda i: (0, 0), pipeline_mode=resident),
            pl.BlockSpec((dff, d), lambda i: (0, 0), pipeline_mode=resident),
            pl.BlockSpec((t, d), lambda i: (i, 0)),
            pl.BlockSpec((None, None, 1, d), lambda i: (mrow(i), 5, 0, 0)),
            pl.BlockSpec((1, d), lambda i: (0, 0)),
        ],
        out_specs=pl.BlockSpec((t, d), lambda i: (i, 0)),
        scratch_shapes=[pltpu.VMEM((t + 2 * halo, d), BF16), pltpu.VMEM((t, dff), BF16),
                        pltpu.VMEM((4, f // LANES, 2 * (t + 2 * halo), LANES), F32)],
        compiler_params=_cparams("parallel"),
        name=name,
    )(h2, h2, h2, w_up, w_conv, b_conv.reshape(1, 2 * dff), w_down, x, mod, norm_out.reshape(1, d))


def _rope_tables(seq, extra_rows):
    tpos = jnp.arange(seq, dtype=jnp.int32)
    row = (tpos // GRID_W).astype(F32)
    col = (tpos % GRID_W).astype(F32)
    inv = ROPE_BASE ** (-jnp.arange(ROPE_PAIRS, dtype=F32) / ROPE_PAIRS)
    ar = row[:, None] * inv[None, :]
    ac = col[:, None] * inv[None, :]
    cr, sr, cc, sc = jnp.cos(ar), jnp.sin(ar), jnp.cos(ac), jnp.sin(ac)
    cos = jnp.concatenate([cr, cr, cc, cc] * (LANES // HEAD_DIM), axis=1)
    sin = jnp.concatenate([-sr, sr, -sc, sc] * (LANES // HEAD_DIM), axis=1)
    cos = jnp.concatenate([cos, jnp.ones((extra_rows, LANES), F32)], axis=0)
    sin = jnp.concatenate([sin, jnp.zeros((extra_rows, LANES), F32)], axis=0)
    return cos * Q_SCALE, sin * Q_SCALE, cos, sin


def _dup_kv_columns(w, d, nkv):
    q = w[:, :d]
    k = w[:, d:d + nkv * HEAD_DIM].reshape(-1, nkv, 1, HEAD_DIM)
    v = w[:, d + nkv * HEAD_DIM:].reshape(-1, nkv, 1, HEAD_DIM)
    kd = jnp.broadcast_to(k, (w.shape[0], nkv, 2, HEAD_DIM)).reshape(w.shape[0], -1)
    vd = jnp.broadcast_to(v, (w.shape[0], nkv, 2, HEAD_DIM)).reshape(w.shape[0], -1)
    return jnp.concatenate([q, kd, vd], axis=1)


def kernel(x, c, ctx, c_ctx, ada_w, ada_b, norm_mix, norm_ffn, norm_out, ffn_up, ffn_conv, ffn_conv_b,
           ffn_down, a_wqkv, a_wo, a_lambda, a_subln, b_wqkv, b_wo, b_rpb, c_wqkv, c_wo, c_sinks):
    batch, seq, d = x.shape
    nctx = ctx.shape[1]
    depth = ada_w.shape[0]
    assert seq % T_FFN == 0 and seq % T_PROJ == 0 and (batch * nctx) % T_PROJ == 0

    mod_rows = 16
    cs = jnp.concatenate([c, c_ctx[None, :], jnp.zeros((mod_rows - batch - 1, d), F32)], axis=0)
    mod_all = _ada_call(cs, ada_w, ada_b).reshape(depth, mod_rows, 6, 1, d)
    ctx_row = batch

    tables = _rope_tables(seq, T_PROJ)
    lat_tiles = seq // T_PROJ
    nd = d // LANES

    xl = x.reshape(batch * seq, d)
    xc = ctx.reshape(batch * nctx, d)
    for i in range(depth):
        need_ctx = i < depth - 1
        kind, j = i % N_MIXERS, i // N_MIXERS
        mod = mod_all[i]
        if kind == 0:
            w = a_wqkv[j].astype(BF16)
            wo = a_wo[j].astype(BF16)
            kinds = [ROPEQ] * nd + [ROPEK] * nd + [PLAIN] * nd
            tabs = tables
        elif kind == 1:
            w = b_wqkv[j].astype(BF16)
            wo = b_wo[j].astype(BF16)
            kinds = [SCALEQ] * nd + [PLAIN] * (2 * nd)
            tabs = None
        else:
            nkv = (c_wqkv.shape[2] - d) // (2 * HEAD_DIM)
            w = _dup_kv_columns(c_wqkv[j], d, nkv).astype(BF16)
            wo = c_wo[j].astype(BF16)
            kinds = [ROPEQ] * nd + [ROPEK] * nkv + [PLAIN] * nkv
            tabs = tables
        qkv = _proj_call(xl, norm_mix[i], mod, w, kinds, tabs, tiles_per_seq=lat_tiles, mod_row=None,
                         name=f"proj{i}")
        qkvc = _proj_call(xc, norm_mix[i], mod, w, kinds, tabs, tiles_per_seq=lat_tiles, mod_row=ctx_row,
                          name=f"proj_ctx{i}")
        if kind == 0:
            lam_init = 0.8 - 0.6 * math.exp(-0.3 * i)
            y, yc = _diff_attention(qkv, qkvc, a_lambda[j].astype(F32), a_subln[j], lam_init,
                                    batch, seq, nctx, need_ctx)
        elif kind == 1:
            y, yc = _na_attention(qkv, qkvc, b_rpb[j], batch, seq, nctx, need_ctx)
        else:
            y, yc = _swa_attention(qkv, qkvc, c_sinks[j], batch, seq, nctx, need_ctx, d)

        w_up = ffn_up[i].astype(BF16)
        w_down = ffn_down[i].astype(BF16)
        xl, h2 = _oproj_call(y, wo, xl, norm_ffn[i], mod, tiles_per_seq=lat_tiles, mod_row=None,
                             name=f"oproj{i}")
        xl = _ffn_call(h2, xl, w_up, ffn_conv[i], ffn_conv_b[i], w_down, mod, norm_out,
                       t=T_FFN, tiles_per_seq=seq // T_FFN, mod_row=None, final=not need_ctx,
                       name=f"ffn{i}")
        if need_ctx:
            xc, h2c = _oproj_call(yc, wo, xc, norm_ffn[i], mod, tiles_per_seq=1, mod_row=ctx_row,
                                  name=f"oproj_ctx{i}")
            xc = _ffn_call(h2c, xc, w_up, ffn_conv[i], ffn_conv_b[i], w_down, mod, norm_out,
                           t=nctx, tiles_per_seq=1, mod_row=ctx_row, final=False, name=f"ffn_ctx{i}")
    return xl.reshape(batch, seq, d)
```

```python
import functools
import math

import numpy as np
import jax
import jax.numpy as jnp
from jax import lax
from jax.experimental import pallas as pl
from jax.experimental.pallas import tpu as pltpu

F32 = jnp.float32
BF16 = jnp.bfloat16

GRID_W = 64
HEAD_DIM = 64
ROPE_BASE = 10000.0
ROPE_PAIRS = HEAD_DIM // 4
NORM_EPS = 1e-6
NEG_INF = -1e30
N_MIXERS = 3
NA_KH = 8
NA_KW = 16
SWA_WINDOW = 128
CONV_W = 3
LOG2E = 1.4426950408889634
Q_SCALE = (HEAD_DIM ** -0.5) * LOG2E

LANES = 128
BF16_ROWS = 16
F32_ROWS = 8
VMEM_LIMIT = 52 * 1024 * 1024

T_PROJ = 512
T_FFN = 512
FF_CHUNK = 256
NA_ROWS = 4
SWA_TQ = 256
DIFF_TQ = 256
DIFF_KC = 256

PLAIN, SCALEQ, ROPEQ, ROPEK = 0, 1, 2, 3


def _cparams(*sem):
    return pltpu.CompilerParams(dimension_semantics=sem, vmem_limit_bytes=VMEM_LIMIT)


def _rms(x):
    return x * lax.rsqrt(jnp.mean(x * x, axis=-1, keepdims=True) + NORM_EPS)


def _ada_kernel(cs_ref, w_ref, b_ref, o_ref):
    cs = cs_ref[...]
    s = cs * (1.0 / (1.0 + jnp.exp(-cs)))
    w = w_ref[...]
    s_hi = s.astype(BF16)
    s_lo = (s - s_hi.astype(F32)).astype(BF16)
    w_hi = w.astype(BF16)
    w_lo = (w - w_hi.astype(F32)).astype(BF16)
    acc = jnp.dot(s_hi, w_hi, preferred_element_type=F32)
    acc = acc + jnp.dot(s_hi, w_lo, preferred_element_type=F32)
    acc = acc + jnp.dot(s_lo, w_hi, preferred_element_type=F32)
    o_ref[...] = acc + b_ref[...]


def _ada_call(cs, ada_w, ada_b):
    depth, d, n = ada_w.shape
    nt = 1536
    rows = cs.shape[0]
    return pl.pallas_call(
        _ada_kernel,
        out_shape=jax.ShapeDtypeStruct((depth, rows, n), F32),
        grid=(depth, n // nt),
        in_specs=[
            pl.BlockSpec((rows, d), lambda l, j: (0, 0)),
            pl.BlockSpec((None, d, nt), lambda l, j: (l, 0, j)),
            pl.BlockSpec((None, 1, nt), lambda l, j: (l, 0, j)),
        ],
        out_specs=pl.BlockSpec((None, rows, nt), lambda l, j: (l, 0, j)),
        compiler_params=_cparams("parallel", "parallel"),
        name="ada_mod",
    )(cs, ada_w, ada_b.reshape(depth, 1, n))


def _rope(r, c, s, first_half):
    partner = jnp.where(first_half, pltpu.roll(r, LANES - 16, 1), pltpu.roll(r, 16, 1))
    return r * c + partner * s


def _proj_kernel(*refs, kinds, nc, has_rope):
    if has_rope:
        x_ref, g_ref, sc_ref, sh_ref, w_ref, cq_ref, sq_ref, ck_ref, sk_ref, o_ref = refs
    else:
        x_ref, g_ref, sc_ref, sh_ref, w_ref, o_ref = refs
    t = x_ref.shape[0]
    nout = o_ref.shape[1]
    h = (_rms(x_ref[...]) * g_ref[...]) * (1.0 + sc_ref[...]) + sh_ref[...]
    hb = h.astype(BF16)
    if has_rope:
        lane = lax.broadcasted_iota(jnp.int32, (t, LANES), 1)
        first_half = (lane & 16) == 0
    for n0 in range(0, nout, nc):
        r = jnp.dot(hb, w_ref[:, n0:n0 + nc], preferred_element_type=F32)
        for j in range(nc // LANES):
            kind = kinds[n0 // LANES + j]
            rj = r[:, j * LANES:(j + 1) * LANES]
            if kind == ROPEQ:
                rj = _rope(rj, cq_ref[...], sq_ref[...], first_half)
            elif kind == ROPEK:
                rj = _rope(rj, ck_ref[...], sk_ref[...], first_half)
            elif kind == SCALEQ:
                rj = rj * Q_SCALE
            o_ref[:, n0 + j * LANES:n0 + (j + 1) * LANES] = rj.astype(BF16)


def _proj_call(x, gamma, mod, w, kinds, tables, *, tiles_per_seq, mod_row, name):
    n, d = x.shape
    nout = w.shape[1]
    t = T_PROJ
    has_rope = tables is not None

    def mrow(i):
        return mod_row if mod_row is not None else i // tiles_per_seq

    def tblock(i):
        return i % tiles_per_seq if mod_row is None else tiles_per_seq

    in_specs = [
        pl.BlockSpec((t, d), lambda i: (i, 0)),
        pl.BlockSpec((1, d), lambda i: (0, 0)),
        pl.BlockSpec((None, None, 1, d), lambda i: (mrow(i), 1, 0, 0)),
        pl.BlockSpec((None, None, 1, d), lambda i: (mrow(i), 0, 0, 0)),
        pl.BlockSpec((d, nout), lambda i: (0, 0)),
    ]
    args = [x, gamma.reshape(1, d), mod, mod, w]
    if has_rope:
        for tab in tables:
            in_specs.append(pl.BlockSpec((t, LANES), lambda i: (tblock(i), 0)))
            args.append(tab)
    return pl.pallas_call(
        functools.partial(_proj_kernel, kinds=tuple(kinds), nc=512, has_rope=has_rope),
        out_shape=jax.ShapeDtypeStruct((n, nout), BF16),
        grid=(n // t,),
        in_specs=in_specs,
        out_specs=pl.BlockSpec((t, nout), lambda i: (i, 0)),
        compiler_params=_cparams("parallel"),
        name=name,
    )(*args)


def _qk(q, k):
    return lax.dot_general(q, k, (((1,), (1,)), ((), ())), preferred_element_type=F32)


def _half_masks(shape):
    lane = lax.broadcasted_iota(jnp.int32, shape, 1)
    return lane < HEAD_DIM


def _split_heads(q, lo):
    qf = q.astype(F32)
    zero = jnp.zeros_like(qf)
    return jnp.where(lo, qf, zero).astype(q.dtype), jnp.where(lo, zero, qf).astype(q.dtype)


def _diff_lambda(lam_ref, lam_init):
    lv = lam_ref[...]
    d1 = jnp.sum(lv[0:1] * lv[1:2], axis=1, keepdims=True)
    d2 = jnp.sum(lv[2:3] * lv[3:4], axis=1, keepdims=True)
    return jnp.exp(d1) - jnp.exp(d2) + lam_init


def _diff_finish(o1, o2, lam, subln_ref, lam_init, o_ref):
    o = _rms(o1 - lam * o2) * subln_ref[...] * (1.0 - lam_init)
    o_ref[...] = o.astype(o_ref.dtype)


def _diff_kernel(lam_ref, subln_ref, q_ref, kl_ref, vl_ref, kc_ref, vc_ref, o_ref,
                 vt, s_scr, p_scr, m_scr, o_scr, *, lam_init, tq, kc):
    seq, nctx = q_ref.shape[0], kc_ref.shape[0]
    nkeys = vt.shape[1]
    nlat = seq // kc
    kcc = min(kc, nctx)
    nchunk = nlat + nctx // kcc
    ntile = seq // tq
    sub = m_scr.shape[1]
    lam = _diff_lambda(lam_ref, lam_init)
    lo = _half_masks((tq, LANES))

    def rows(c):
        if c < nlat:
            return slice(c * kc, (c + 1) * kc)
        return slice(seq + (c - nlat) * kcc, seq + (c - nlat + 1) * kcc)

    def chunk_of(lat_ref, ctx_ref, c):
        if c < nlat:
            return lat_ref[rows(c), :]
        return ctx_ref[(c - nlat) * kcc:(c - nlat + 1) * kcc, :]

    eye = (lax.broadcasted_iota(jnp.int32, (LANES, LANES), 0)
           == lax.broadcasted_iota(jnp.int32, (LANES, LANES), 1)).astype(F32).astype(BF16)
    for c in range(nchunk):
        vt[0:LANES, rows(c)] = _qk(eye, chunk_of(vl_ref, vc_ref, c)).astype(BF16)
    vt[LANES:, :] = jnp.ones((vt.shape[0] - LANES, nkeys), BF16)

    def half(m, tile, do_qk, do_exp, do_pv):
        if do_qk:
            q = q_ref[pl.ds(pl.multiple_of(tile * tq, tq), tq), :].astype(F32)
            zero = jnp.zeros_like(q)
            qm = (jnp.where(lo, q, zero) if m == 0 else jnp.where(lo, zero, q)).astype(BF16)
            mrun = jnp.full((sub, tq), NEG_INF, F32)
        if do_exp:
            mb = m_scr[1 - m, 0:1, :]
        if do_pv:
            acc = jnp.zeros((vt.shape[0], tq), F32)
        for c in range(nchunk):
            if do_qk:
                s = _qk(chunk_of(kl_ref, kc_ref, c), qm)
                s_scr[m, rows(c), :] = s
                for u in range(s.shape[0] // sub):
                    mrun = jnp.maximum(mrun, s[u * sub:(u + 1) * sub, :])
            if do_pv:
                acc = acc + jnp.dot(vt[:, rows(c)], p_scr[1 - m, rows(c), :], preferred_element_type=F32)
            if do_exp:
                p_scr[1 - m, rows(c), :] = jnp.exp2(s_scr[1 - m, rows(c), :] - mb).astype(BF16)
        if do_qk:
            m_scr[m] = jnp.broadcast_to(jnp.max(mrun, axis=0, keepdims=True), (sub, tq))
        if do_pv:
            return acc[:LANES] * (1.0 / acc[LANES:LANES + 1])
        return None

    def emit(tile, o2):
        o = _rms((o_scr[...] - lam * o2).T) * subln_ref[...] * (1.0 - lam_init)
        o_ref[pl.ds(pl.multiple_of(tile * tq, tq), tq), :] = o.astype(o_ref.dtype)

    half(0, 0, True, False, False)
    half(1, 0, True, True, False)
    half(0, 1, True, True, False)
    o_scr[...] = half(1, 1, True, True, True)

    def body(t, carry):
        emit(t - 2, half(0, t, True, True, True))
        o_scr[...] = half(1, t, True, True, True)
        return carry

    lax.fori_loop(2, ntile, body, 0)
    emit(ntile - 2, half(0, ntile, False, True, True))
    o_scr[...] = half(1, ntile, False, False, True)
    emit(ntile - 1, half(0, ntile + 1, False, False, True))


def _diff_ctx_kernel(lam_ref, subln_ref, q_ref, kc_ref, vc_ref, o_ref, *, lam_init):
    tq = q_ref.shape[0]
    lam = _diff_lambda(lam_ref, lam_init)
    qs = _split_heads(q_ref[...], _half_masks((tq, LANES)))
    o1, o2 = [_softmax_pv([_qk(qm, kc_ref[...])], [vc_ref[...]]) for qm in qs]
    _diff_finish(o1, o2, lam, subln_ref, lam_init, o_ref)


def _diff_attention(qkv, qkvc, lam_vecs, subln, lam_init, batch, seq, nctx, need_ctx):
    d = qkv.shape[1] // 3
    nh = d // LANES
    tq = DIFF_TQ
    kc = DIFF_KC
    assert seq % tq == 0 and seq % kc == 0 and nctx % min(kc, nctx) == 0 and seq // tq >= 2
    nkeys = seq + nctx
    subln2 = subln.reshape(1, LANES)
    y = pl.pallas_call(
        functools.partial(_diff_kernel, lam_init=lam_init, tq=tq, kc=kc),
        out_shape=jax.ShapeDtypeStruct((batch * seq, d), BF16),
        grid=(batch, nh),
        in_specs=[
            pl.BlockSpec((4, HEAD_DIM), lambda b, h: (0, 0)),
            pl.BlockSpec((1, LANES), lambda b, h: (0, 0)),
            pl.BlockSpec((seq, LANES), lambda b, h: (b, h)),
            pl.BlockSpec((seq, LANES), lambda b, h: (b, nh + h)),
            pl.BlockSpec((seq, LANES), lambda b, h: (b, 2 * nh + h)),
            pl.BlockSpec((nctx, LANES), lambda b, h: (b, nh + h)),
            pl.BlockSpec((nctx, LANES), lambda b, h: (b, 2 * nh + h)),
        ],
        out_specs=pl.BlockSpec((seq, LANES), lambda b, h: (b, h)),
        scratch_shapes=[pltpu.VMEM((LANES + BF16_ROWS, nkeys), BF16), pltpu.VMEM((2, nkeys, tq), F32),
                        pltpu.VMEM((2, nkeys, tq), BF16), pltpu.VMEM((2, F32_ROWS, tq), F32),
                        pltpu.VMEM((LANES, tq), F32)],
        compiler_params=_cparams("parallel", "parallel"),
        name="diff_attn",
    )(lam_vecs, subln2, qkv, qkv, qkv, qkvc, qkvc)
    yc = None
    if need_ctx:
        yc = pl.pallas_call(
            functools.partial(_diff_ctx_kernel, lam_init=lam_init),
            out_shape=jax.ShapeDtypeStruct((batch * nctx, d), BF16),
            grid=(batch, nh),
            in_specs=[
                pl.BlockSpec((4, HEAD_DIM), lambda b, h: (0, 0)),
                pl.BlockSpec((1, LANES), lambda b, h: (0, 0)),
                pl.BlockSpec((nctx, LANES), lambda b, h: (b, h)),
                pl.BlockSpec((nctx, LANES), lambda b, h: (b, nh + h)),
                pl.BlockSpec((nctx, LANES), lambda b, h: (b, 2 * nh + h)),
            ],
            out_specs=pl.BlockSpec((nctx, LANES), lambda b, h: (b, h)),
            compiler_params=_cparams("parallel", "parallel"),
            name="diff_attn_ctx",
        )(lam_vecs, subln2, qkvc, qkvc, qkvc)
    return y, yc


def _softmax_pv(scores, values, extra_logit=None):
    m = jnp.max(scores[0], axis=1, keepdims=True)
    for s in scores[1:]:
        m = jnp.maximum(m, jnp.max(s, axis=1, keepdims=True))
    if extra_logit is not None:
        m = jnp.maximum(m, extra_logit)
    l = None
    acc = None
    for s, v in zip(scores, values):
        p = jnp.exp2(s - m)
        ps = jnp.sum(p, axis=1, keepdims=True)
        pv = jnp.dot(p.astype(BF16), v, preferred_element_type=F32)
        l = ps if l is None else l + ps
        acc = pv if acc is None else acc + pv
    if extra_logit is not None:
        l = l + jnp.exp2(extra_logit - m)
    return acc * (1.0 / l)


def _identity(n):
    return (lax.broadcasted_iota(jnp.int32, (n, n), 0)
            == lax.broadcasted_iota(jnp.int32, (n, n), 1)).astype(F32).astype(BF16)


def _fill_vt(vt, head0, chunk, v_block, n_heads):
    tr = _qk(_identity(LANES), v_block).astype(BF16)
    for hh in range(n_heads):
        vt[head0 + hh, chunk, 0:HEAD_DIM, :] = tr[hh * HEAD_DIM:(hh + 1) * HEAD_DIM]
        vt[head0 + hh, chunk, HEAD_DIM:, :] = jnp.ones((vt.shape[2] - HEAD_DIM, vt.shape[3]), BF16)


def _window_pipeline(q_ref, o_ref, s_scr, p_scr, m_scr, e_scr, o_scr, *, tq, kc, nchunk,
                     scores, values, extra_logit):
    ntile = q_ref.shape[0] // tq
    sub = m_scr.shape[1]
    lo = _half_masks((tq, LANES))
    rows = lambda c: slice(c * kc, (c + 1) * kc)
    tile_rows = lambda t: pl.ds(pl.multiple_of(t * tq, tq), tq)

    def half(j, t, do_qk, do_exp, do_pv):
        jo = 1 - j
        tv = t - 2 + j
        if do_qk:
            q = q_ref[tile_rows(t), :].astype(F32)
            zero = jnp.zeros_like(q)
            qm = (jnp.where(lo, q, zero) if j == 0 else jnp.where(lo, zero, q)).astype(BF16)
            mrun = jnp.full((sub, tq), NEG_INF, F32)
        if do_exp:
            mb = m_scr[jo, 0:1, :]
        if do_pv:
            acc = jnp.zeros((o_scr.shape[0] + BF16_ROWS, tq), F32)
        for c in range(nchunk):
            if do_qk:
                s = scores(qm, t, j, c)
                s_scr[j, rows(c), :] = s
                for u in range(kc // sub):
                    mrun = jnp.maximum(mrun, s[u * sub:(u + 1) * sub, :])
            if do_pv:
                acc = acc + jnp.dot(values(tv, jo, c), p_scr[jo, rows(c), :], preferred_element_type=F32)
            if do_exp:
                p_scr[jo, rows(c), :] = jnp.exp2(s_scr[jo, rows(c), :] - mb).astype(BF16)
        if do_qk:
            m = jnp.max(mrun, axis=0, keepdims=True)
            if extra_logit is not None:
                m = jnp.maximum(m, extra_logit(j))
            m_scr[j] = jnp.broadcast_to(m, (sub, tq))
        o = None
        if do_pv:
            l = acc[HEAD_DIM:HEAD_DIM + 1]
            if extra_logit is not None:
                l = l + e_scr[jo, 0:1, :]
            o = acc[:HEAD_DIM] * (1.0 / l)
        if do_exp and extra_logit is not None:
            e_scr[jo] = jnp.exp2(extra_logit(jo) - m_scr[jo])
        return o

    def emit(t, o1):
        o = jnp.concatenate([o_scr[...], o1], axis=0).T
        o_ref[tile_rows(t), :] = o.astype(o_ref.dtype)

    half(0, 0, True, False, False)
    half(1, 0, True, True, False)
    half(0, 1, True, True, False)
    o_scr[...] = half(1, 1, True, True, True)

    for t in range(2, ntile):
        emit(t - 2, half(0, t, True, True, True))
        o_scr[...] = half(1, t, True, True, True)
    emit(ntile - 2, half(0, ntile, False, True, True))
    o_scr[...] = half(1, ntile, False, False, True)
    emit(ntile - 1, half(0, ntile + 1, False, False, True))


def _tile_class(t, ntile):
    return jnp.where(t == 0, 0, jnp.where(t == ntile - 1, 2, 1))


def _na_kernel(bias_ref, q_ref, k_ref, v_ref, kc_ref, vc_ref, o_ref, vt, s_scr, p_scr, m_scr, o_scr,
               *, tq, kc):
    seq, nctx = q_ref.shape[0], kc_ref.shape[0]
    ntile, nlat, nwin = seq // tq, seq // kc, 3 * tq // kc
    rows = lambda c: slice(c * kc, (c + 1) * kc)
    for c in range(nlat):
        _fill_vt(vt, 0, c, v_ref[rows(c), :], 2)
    for c in range(nctx // kc):
        _fill_vt(vt, 0, nlat + c, vc_ref[rows(c), :], 2)

    def win_start(t):
        return pl.multiple_of(jnp.clip((t - 1) * tq, 0, seq - 3 * tq), tq)

    def scores(qm, t, j, c):
        if c >= nwin:
            return _qk(kc_ref[rows(c - nwin), :], qm)
        k = k_ref[pl.ds(win_start(t) + c * kc, kc), :]
        return _qk(k, qm) + bias_ref[_tile_class(t, ntile), j, rows(c), :]

    def values(t, j, c):
        return vt[j, nlat + c - nwin] if c >= nwin else vt[j, win_start(t) // kc + c]

    _window_pipeline(q_ref, o_ref, s_scr, p_scr, m_scr, None, o_scr, tq=tq, kc=kc,
                     nchunk=nwin + nctx // kc, scores=scores, values=values, extra_logit=None)


def _plain_ctx_kernel(q_ref, kc_ref, vc_ref, o_ref):
    tq = q_ref.shape[0]
    lo = _half_masks((tq, LANES))
    qs = _split_heads(q_ref[...], lo)
    outs = [_softmax_pv([_qk(qs[hh], kc_ref[...])], [vc_ref[...]]) for hh in range(2)]
    o_ref[...] = jnp.where(lo, outs[0], outs[1]).astype(o_ref.dtype)


def _na_bias_tables(rpb, rows):
    w = GRID_W
    nk = 3 * NA_ROWS
    nh = rpb.shape[0]
    edge = w - NA_KW
    ext = jnp.concatenate([jnp.repeat(rpb[..., :1], edge, axis=-1), rpb,
                           jnp.repeat(rpb[..., -1:], edge + 1, axis=-1)], axis=-1)
    flat = jnp.tile(ext, (1, 1, w + 1))[..., :w * (2 * w + 1)]
    hankel = flat.reshape(nh, -1, w, 2 * w + 1)[..., :w]
    toep = jnp.flip(hankel, axis=-2)
    cq = np.arange(w)[:, None]
    cx = np.arange(w)[None, :]
    col_start = np.clip(cq - NA_KW // 2, 0, w - NA_KW)
    col_ok = (cx >= col_start) & (cx < col_start + NA_KW)
    toep = jnp.where(col_ok, toep * LOG2E, NEG_INF)
    masked = jnp.full((nh, w, w), NEG_INF, F32)
    tabs = []
    for r0, k0 in ((0, 0), (NA_ROWS, 0), (rows - NA_ROWS, rows - nk)):
        row_blocks = []
        for a in range(NA_ROWS):
            r = r0 + a
            rs = min(max(r - NA_KH // 2, 0), rows - NA_KH)
            blocks = []
            for e in range(nk):
                ry = k0 + e
                blocks.append(toep[:, ry - r + NA_KH - 1] if rs <= ry < rs + NA_KH else masked)
            row_blocks.append(jnp.concatenate(blocks, axis=-1))
        tabs.append(jnp.concatenate(row_blocks, axis=-2))
    return jnp.swapaxes(jnp.stack(tabs), -1, -2).astype(F32)


def _win_scratch(nheads, nchunks, nkeys_item, tq, kc, with_extra):
    shapes = [pltpu.VMEM((nheads, nchunks, HEAD_DIM + BF16_ROWS, kc), BF16),
              pltpu.VMEM((2, nkeys_item, tq), F32), pltpu.VMEM((2, nkeys_item, tq), BF16),
              pltpu.VMEM((2, F32_ROWS, tq), F32)]
    if with_extra:
        shapes.append(pltpu.VMEM((2, F32_ROWS, tq), F32))
    shapes.append(pltpu.VMEM((HEAD_DIM, tq), F32))
    return shapes


def _na_attention(qkv, qkvc, rpb, batch, seq, nctx, need_ctx):
    d = qkv.shape[1] // 3
    nb = d // LANES
    rows = seq // GRID_W
    tq = NA_ROWS * GRID_W
    kc = tq
    assert seq // tq >= 3 and nctx % kc == 0
    bias = _na_bias_tables(rpb.astype(F32), rows)
    y = pl.pallas_call(
        functools.partial(_na_kernel, tq=tq, kc=kc),
        out_shape=jax.ShapeDtypeStruct((batch * seq, d), BF16),
        grid=(nb, batch),
        in_specs=[
            pl.BlockSpec((3, 2, 3 * tq, tq), lambda hb, b: (0, hb, 0, 0), pipeline_mode=pl.Buffered(1)),
            pl.BlockSpec((seq, LANES), lambda hb, b: (b, hb)),
            pl.BlockSpec((seq, LANES), lambda hb, b: (b, nb + hb)),
            pl.BlockSpec((seq, LANES), lambda hb, b: (b, 2 * nb + hb)),
            pl.BlockSpec((nctx, LANES), lambda hb, b: (b, nb + hb)),
            pl.BlockSpec((nctx, LANES), lambda hb, b: (b, 2 * nb + hb)),
        ],
        out_specs=pl.BlockSpec((seq, LANES), lambda hb, b: (b, hb)),
        scratch_shapes=_win_scratch(2, (seq + nctx) // kc, 3 * tq + nctx, tq, kc, False),
        compiler_params=_cparams("parallel", "arbitrary"),
        name="na_attn",
    )(bias, qkv, qkv, qkv, qkvc, qkvc)
    yc = None
    if need_ctx:
        yc = pl.pallas_call(
            _plain_ctx_kernel,
            out_shape=jax.ShapeDtypeStruct((batch * nctx, d), BF16),
            grid=(batch, nb),
            in_specs=[
                pl.BlockSpec((nctx, LANES), lambda b, hb: (b, hb)),
                pl.BlockSpec((nctx, LANES), lambda b, hb: (b, nb + hb)),
                pl.BlockSpec((nctx, LANES), lambda b, hb: (b, 2 * nb + hb)),
            ],
            out_specs=pl.BlockSpec((nctx, LANES), lambda b, hb: (b, hb)),
            compiler_params=_cparams("parallel", "parallel"),
            name="na_attn_ctx",
        )(qkvc, qkvc, qkvc)
    return y, yc


def _swa_heads(q_ref, sinks_ref, g, score_fn, values, o_ref):
    tq = q_ref.shape[0]
    lo = _half_masks((tq, LANES))
    for jb in range(2):
        qs = _split_heads(q_ref[:, jb * LANES:(jb + 1) * LANES], lo)
        outs = []
        for hh in range(2):
            sink = jnp.full((1, 1), sinks_ref[g * 4 + jb * 2 + hh] * LOG2E, F32)
            outs.append(_softmax_pv(score_fn(qs[hh]), values, extra_logit=sink))
        o_ref[:, jb * LANES:(jb + 1) * LANES] = jnp.where(lo, outs[0], outs[1]).astype(o_ref.dtype)


def _swa_kernel(sinks_ref, band_ref, q_ref, k_ref, v_ref, kc_ref, vc_ref, o_ref,
                vt, s_scr, p_scr, m_scr, e_scr, o_scr, *, tq, kc, span):
    head0 = (pl.program_id(1) * 2 + pl.program_id(2)) * 2
    seq, nctx = q_ref.shape[0], kc_ref.shape[0]
    ntile, nlat, nwin = seq // tq, seq // kc, span // kc
    rows = lambda c: slice(c * kc, (c + 1) * kc)
    for c in range(nlat):
        _fill_vt(vt, 0, c, v_ref[rows(c), :], 1)
    for c in range(nctx // kc):
        _fill_vt(vt, 0, nlat + c, vc_ref[rows(c), :], 1)

    def win_start(t):
        return pl.multiple_of(jnp.clip(t * tq - SWA_WINDOW, 0, seq - span), SWA_WINDOW)

    def scores(qm, t, j, c):
        if c >= nwin:
            return _qk(kc_ref[rows(c - nwin), :], qm)
        k = k_ref[pl.ds(win_start(t) + c * kc, kc), :]
        return _qk(k, qm) + band_ref[_tile_class(t, ntile), rows(c), :]

    def values(t, j, c):
        return vt[0, nlat + c - nwin] if c >= nwin else vt[0, win_start(t) // kc + c]

    def sink(j):
        return jnp.full((1, 1), sinks_ref[head0 + j] * LOG2E, F32)

    _window_pipeline(q_ref, o_ref, s_scr, p_scr, m_scr, e_scr, o_scr, tq=tq, kc=kc,
                     nchunk=nwin + nctx // kc, scores=scores, values=values, extra_logit=sink)


def _swa_band_tables(seq, tq, span):
    ntile = seq // tq
    tabs = []
    for t in (0, 1, ntile - 1):
        start = min(max(t * tq - SWA_WINDOW, 0), seq - span)
        kpos = start + np.arange(span)[:, None]
        qpos = t * tq + np.arange(tq)[None, :]
        tabs.append(np.where(np.abs(qpos - kpos) <= SWA_WINDOW, 0.0, NEG_INF))
    return jnp.asarray(np.stack(tabs), F32)


def _swa_ctx_kernel(sinks_ref, q_ref, kc_ref, vc_ref, o_ref):
    g = pl.program_id(1)
    kc = kc_ref[...]
    _swa_heads(q_ref, sinks_ref, g, lambda qm: [_qk(qm, kc)], [vc_ref[...]], o_ref)


def _swa_attention(qkv, qkvc, sinks, batch, seq, nctx, need_ctx, d):
    nkv = (qkv.shape[1] - d) // (2 * LANES)
    qb = d // nkv
    kcol = d // LANES
    tq = SWA_TQ
    kc = SWA_WINDOW
    span = tq + 2 * SWA_WINDOW
    assert qb == 2 * LANES and seq // tq >= 3 and nctx % kc == 0
    smem = pl.BlockSpec(memory_space=pltpu.SMEM)
    sinks = sinks.astype(F32)
    y = pl.pallas_call(
        functools.partial(_swa_kernel, tq=tq, kc=kc, span=span),
        out_shape=jax.ShapeDtypeStruct((batch * seq, d), BF16),
        grid=(batch, nkv, 2),
        in_specs=[
            smem,
            pl.BlockSpec((3, span, tq), lambda b, g, jb: (0, 0, 0), pipeline_mode=pl.Buffered(1)),
            pl.BlockSpec((seq, LANES), lambda b, g, jb: (b, 2 * g + jb)),
            pl.BlockSpec((seq, LANES), lambda b, g, jb: (b, kcol + g)),
            pl.BlockSpec((seq, LANES), lambda b, g, jb: (b, kcol + nkv + g)),
            pl.BlockSpec((nctx, LANES), lambda b, g, jb: (b, kcol + g)),
            pl.BlockSpec((nctx, LANES), lambda b, g, jb: (b, kcol + nkv + g)),
        ],
        out_specs=pl.BlockSpec((seq, LANES), lambda b, g, jb: (b, 2 * g + jb)),
        scratch_shapes=_win_scratch(1, (seq + nctx) // kc, span + nctx, tq, kc, True),
        compiler_params=_cparams("parallel", "parallel", "arbitrary"),
        name="swa_attn",
    )(sinks, _swa_band_tables(seq, tq, span), qkv, qkv, qkv, qkvc, qkvc)
    yc = None
    if need_ctx:
        yc = pl.pallas_call(
            _swa_ctx_kernel,
            out_shape=jax.ShapeDtypeStruct((batch * nctx, d), BF16),
            grid=(batch, nkv),
            in_specs=[
                smem,
                pl.BlockSpec((nctx, qb), lambda b, g: (b, g)),
                pl.BlockSpec((nctx, LANES), lambda b, g: (b, kcol + g)),
                pl.BlockSpec((nctx, LANES), lambda b, g: (b, kcol + nkv + g)),
            ],
            out_specs=pl.BlockSpec((nctx, qb), lambda b, g: (b, g)),
            compiler_params=_cparams("parallel", "parallel"),
            name="swa_attn_ctx",
        )(sinks, qkvc, qkvc, qkvc)
    return y, yc


def _oproj_kernel(o_ref, wo_ref, x_ref, g1_ref, nf_ref, sc2_ref, sh2_ref, xo_ref, h2_ref):
    y = jnp.dot(o_ref[...], wo_ref[...], preferred_element_type=F32)
    xn = x_ref[...] + g1_ref[...] * y
    xo_ref[...] = xn
    h2 = (_rms(xn) * nf_ref[...]) * (1.0 + sc2_ref[...]) + sh2_ref[...]
    h2_ref[...] = h2.astype(BF16)


def _oproj_call(o, wo, x, gamma, mod, *, tiles_per_seq, mod_row, name):
    n, d = x.shape
    t = T_PROJ

    def mrow(i):
        return mod_row if mod_row is not None else i // tiles_per_seq

    def mspec(which):
        return pl.BlockSpec((None, None, 1, d), lambda i: (mrow(i), which, 0, 0))

    return pl.pallas_call(
        _oproj_kernel,
        out_shape=(jax.ShapeDtypeStruct((n, d), F32), jax.ShapeDtypeStruct((n, d), BF16)),
        grid=(n // t,),
        in_specs=[
            pl.BlockSpec((t, d), lambda i: (i, 0)),
            pl.BlockSpec((d, d), lambda i: (0, 0)),
            pl.BlockSpec((t, d), lambda i: (i, 0)),
            mspec(2),
            pl.BlockSpec((1, d), lambda i: (0, 0)),
            mspec(4),
            mspec(3),
        ],
        out_specs=(pl.BlockSpec((t, d), lambda i: (i, 0)), pl.BlockSpec((t, d), lambda i: (i, 0))),
        compiler_params=_cparams("parallel"),
        name=name,
    )(o, wo, x, mod, gamma.reshape(1, d), mod, mod)


def _ffn_kernel(h_ref, hp_ref, hn_ref, wup_ref, cw_ref, cb_ref, wd_ref, x_ref, g2_ref, no_ref, o_ref,
                lhs, act, ubuf, *, nseq, final, f):
    i = pl.program_id(0)
    t = h_ref.shape[0]
    halo = hp_ref.shape[0]
    dff = wd_ref.shape[0]
    rows = t + 2 * halo

    keep_prev = ((i % nseq) != 0).astype(F32)
    keep_next = ((i % nseq) != nseq - 1).astype(F32)
    lhs[0:halo, :] = (hp_ref[...].astype(F32) * keep_prev).astype(BF16)
    lhs[halo:halo + t, :] = h_ref[...]
    lhs[halo + t:, :] = (hn_ref[...].astype(F32) * keep_next).astype(BF16)

    def conv(col0, slot):
        u = jnp.dot(lhs[...], wup_ref[:, col0:col0 + f], preferred_element_type=F32)
        outs = []
        for s in range(f // LANES):
            sl = slice(col0 + s * LANES, col0 + (s + 1) * LANES)
            buf = ubuf.at[slot, s]
            buf[pl.ds(0, rows, stride=2), :] = u[:, s * LANES:(s + 1) * LANES]
            taps = [buf[pl.ds(2 * (halo - 1 + k), t, stride=2), :] for k in range(CONV_W)]
            outs.append(cb_ref[:, sl] + taps[0] * cw_ref[0:1, sl] + taps[1] * cw_ref[1:2, sl]
                        + taps[2] * cw_ref[2:3, sl])
        return jnp.concatenate(outs, axis=1)

    for c in range(dff // f):
        slot = 2 * (c % 2)
        a = conv(c * f, slot)
        g = conv(dff + c * f, slot + 1)
        act[:, c * f:(c + 1) * f] = ((g * (1.0 / (1.0 + jnp.exp(-g)))) * a).astype(BF16)

    xn = x_ref[...] + g2_ref[...] * jnp.dot(act[...], wd_ref[...], preferred_element_type=F32)
    if final:
        xn = _rms(xn) * no_ref[...]
    o_ref[...] = xn


def _ffn_call(h2, x, w_up, w_conv, b_conv, w_down, mod, norm_out, *, t, tiles_per_seq, mod_row, final, name):
    n, d = x.shape
    dff = w_down.shape[0]
    f = FF_CHUNK
    halo = BF16_ROWS
    hb = t // halo
    last_hblock = n // halo - 1
    resident = pl.Buffered(1)

    def mrow(i):
        return mod_row if mod_row is not None else i // tiles_per_seq

    return pl.pallas_call(
        functools.partial(_ffn_kernel, nseq=tiles_per_seq, final=final, f=f),
        out_shape=jax.ShapeDtypeStruct((n, d), F32),
        grid=(n // t,),
        in_specs=[
            pl.BlockSpec((t, d), lambda i: (i, 0)),
            pl.BlockSpec((halo, d), lambda i: (jnp.maximum(i * hb - 1, 0), 0)),
            pl.BlockSpec((halo, d), lambda i: (jnp.minimum((i + 1) * hb, last_hblock), 0)),
            pl.BlockSpec((d, 2 * dff), lambda i: (0, 0), pipeline_mode=resident),
            pl.BlockSpec((CONV_W, 2 * dff), lambda i: (0, 0), pipeline_mode=resident),
            pl.BlockSpec((1, 2 * dff), lambda i: (0, 0), pipeline_mode=resident),
            pl.BlockSpec((dff, d), lambda i: (0, 0), pipeline_mode=resident),
            pl.BlockSpec((t, d), lambda i: (i, 0)),
            pl.BlockSpec((None, None, 1, d), lambda i: (mrow(i), 5, 0, 0)),
            pl.BlockSpec((1, d), lambda i: (0, 0)),
        ],
        out_specs=pl.BlockSpec((t, d), lambda i: (i, 0)),
        scratch_shapes=[pltpu.VMEM((t + 2 * halo, d), BF16), pltpu.VMEM((t, dff), BF16),
                        pltpu.VMEM((4, f // LANES, 2 * (t + 2 * halo), LANES), F32)],
        compiler_params=_cparams("parallel"),
        name=name,
    )(h2, h2, h2, w_up, w_conv, b_conv.reshape(1, 2 * dff), w_down, x, mod, norm_out.reshape(1, d))


def _rope_tables(seq, extra_rows):
    tpos = jnp.arange(seq, dtype=jnp.int32)
    row = (tpos // GRID_W).astype(F32)
    col = (tpos % GRID_W).astype(F32)
    inv = ROPE_BASE ** (-jnp.arange(ROPE_PAIRS, dtype=F32) / ROPE_PAIRS)
    ar = row[:, None] * inv[None, :]
    ac = col[:, None] * inv[None, :]
    cr, sr, cc, sc = jnp.cos(ar), jnp.sin(ar), jnp.cos(ac), jnp.sin(ac)
    cos = jnp.concatenate([cr, cr, cc, cc] * (LANES // HEAD_DIM), axis=1)
    sin = jnp.concatenate([-sr, sr, -sc, sc] * (LANES // HEAD_DIM), axis=1)
    cos = jnp.concatenate([cos, jnp.ones((extra_rows, LANES), F32)], axis=0)
    sin = jnp.concatenate([sin, jnp.zeros((extra_rows, LANES), F32)], axis=0)
    return cos * Q_SCALE, sin * Q_SCALE, cos, sin


def _dup_kv_columns(w, d, nkv):
    q = w[:, :d]
    k = w[:, d:d + nkv * HEAD_DIM].reshape(-1, nkv, 1, HEAD_DIM)
    v = w[:, d + nkv * HEAD_DIM:].reshape(-1, nkv, 1, HEAD_DIM)
    kd = jnp.broadcast_to(k, (w.shape[0], nkv, 2, HEAD_DIM)).reshape(w.shape[0], -1)
    vd = jnp.broadcast_to(v, (w.shape[0], nkv, 2, HEAD_DIM)).reshape(w.shape[0], -1)
    return jnp.concatenate([q, kd, vd], axis=1)


def kernel(x, c, ctx, c_ctx, ada_w, ada_b, norm_mix, norm_ffn, norm_out, ffn_up, ffn_conv, ffn_conv_b,
           ffn_down, a_wqkv, a_wo, a_lambda, a_subln, b_wqkv, b_wo, b_rpb, c_wqkv, c_wo, c_sinks):
    batch, seq, d = x.shape
    nctx = ctx.shape[1]
    depth = ada_w.shape[0]
    assert seq % T_FFN == 0 and seq % T_PROJ == 0 and (batch * nctx) % T_PROJ == 0

    mod_rows = 16
    cs = jnp.concatenate([c, c_ctx[None, :], jnp.zeros((mod_rows - batch - 1, d), F32)], axis=0)
    mod_all = _ada_call(cs, ada_w, ada_b).reshape(depth, mod_rows, 6, 1, d)
    ctx_row = batch

    tables = _rope_tables(seq, T_PROJ)
    lat_tiles = seq // T_PROJ
    nd = d // LANES

    xl = x.reshape(batch * seq, d)
    xc = ctx.reshape(batch * nctx, d)
    for i in range(depth):
        need_ctx = i < depth - 1
        kind, j = i % N_MIXERS, i // N_MIXERS
        mod = mod_all[i]
        if kind == 0:
            w = a_wqkv[j].astype(BF16)
            wo = a_wo[j].astype(BF16)
            kinds = [ROPEQ] * nd + [ROPEK] * nd + [PLAIN] * nd
            tabs = tables
        elif kind == 1:
            w = b_wqkv[j].astype(BF16)
            wo = b_wo[j].astype(BF16)
            kinds = [SCALEQ] * nd + [PLAIN] * (2 * nd)
            tabs = None
        else:
            nkv = (c_wqkv.shape[2] - d) // (2 * HEAD_DIM)
            w = _dup_kv_columns(c_wqkv[j], d, nkv).astype(BF16)
            wo = c_wo[j].astype(BF16)
            kinds = [ROPEQ] * nd + [ROPEK] * nkv + [PLAIN] * nkv
            tabs = tables
        qkv = _proj_call(xl, norm_mix[i], mod, w, kinds, tabs, tiles_per_seq=lat_tiles, mod_row=None,
                         name=f"proj{i}")
        qkvc = _proj_call(xc, norm_mix[i], mod, w, kinds, tabs, tiles_per_seq=lat_tiles, mod_row=ctx_row,
                          name=f"proj_ctx{i}")
        if kind == 0:
            lam_init = 0.8 - 0.6 * math.exp(-0.3 * i)
            y, yc = _diff_attention(qkv, qkvc, a_lambda[j].astype(F32), a_subln[j], lam_init,
                                    batch, seq, nctx, need_ctx)
        elif kind == 1:
            y, yc = _na_attention(qkv, qkvc, b_rpb[j], batch, seq, nctx, need_ctx)
        else:
            y, yc = _swa_attention(qkv, qkvc, c_sinks[j], batch, seq, nctx, need_ctx, d)

        w_up = ffn_up[i].astype(BF16)
        w_down = ffn_down[i].astype(BF16)
        xl, h2 = _oproj_call(y, wo, xl, norm_ffn[i], mod, tiles_per_seq=lat_tiles, mod_row=None,
                             name=f"oproj{i}")
        xl = _ffn_call(h2, xl, w_up, ffn_conv[i], ffn_conv_b[i], w_down, mod, norm_out,
                       t=T_FFN, tiles_per_seq=seq // T_FFN, mod_row=None, final=not need_ctx,
                       name=f"ffn{i}")
        if need_ctx:
            xc, h2c = _oproj_call(yc, wo, xc, norm_ffn[i], mod, tiles_per_seq=1, mod_row=ctx_row,
                                  name=f"oproj_ctx{i}")
            xc = _ffn_call(h2c, xc, w_up, ffn_conv[i], ffn_conv_b[i], w_down, mod, norm_out,
                           t=nctx, tiles_per_seq=1, mod_row=ctx_row, final=False, name=f"ffn_ctx{i}")
    return xl.reshape(batch, seq, d)
```

```python
import functools
import math

import numpy as np
import jax
import jax.numpy as jnp
from jax import lax
from jax.experimental import pallas as pl
from jax.experimental.pallas import tpu as pltpu

F32 = jnp.float32
BF16 = jnp.bfloat16

GRID_W = 64
HEAD_DIM = 64
ROPE_BASE = 10000.0
ROPE_PAIRS = HEAD_DIM // 4
NORM_EPS = 1e-6
NEG_INF = -1e30
N_MIXERS = 3
NA_KH = 8
NA_KW = 16
SWA_WINDOW = 128
CONV_W = 3
LOG2E = 1.4426950408889634
Q_SCALE = (HEAD_DIM ** -0.5) * LOG2E

LANES = 128
BF16_ROWS = 16
F32_ROWS = 8
VMEM_LIMIT = 52 * 1024 * 1024

T_PROJ = 512
T_FFN = 512
FF_CHUNK = 256
NA_ROWS = 4
SWA_TQ = 256
DIFF_TQ = 256
DIFF_KC = 256

PLAIN, SCALEQ, ROPEQ, ROPEK = 0, 1, 2, 3


def _cparams(*sem):
    return pltpu.CompilerParams(dimension_semantics=sem, vmem_limit_bytes=VMEM_LIMIT)


def _rms(x):
    return x * lax.rsqrt(jnp.mean(x * x, axis=-1, keepdims=True) + NORM_EPS)


def _ada_kernel(cs_ref, w_ref, b_ref, o_ref):
    cs = cs_ref[...]
    s = cs * (1.0 / (1.0 + jnp.exp(-cs)))
    w = w_ref[...]
    s_hi = s.astype(BF16)
    s_lo = (s - s_hi.astype(F32)).astype(BF16)
    w_hi = w.astype(BF16)
    w_lo = (w - w_hi.astype(F32)).astype(BF16)
    acc = jnp.dot(s_hi, w_hi, preferred_element_type=F32)
    acc = acc + jnp.dot(s_hi, w_lo, preferred_element_type=F32)
    acc = acc + jnp.dot(s_lo, w_hi, preferred_element_type=F32)
    o_ref[...] = acc + b_ref[...]


def _ada_call(cs, ada_w, ada_b):
    depth, d, n = ada_w.shape
    nt = 1536
    rows = cs.shape[0]
    return pl.pallas_call(
        _ada_kernel,
        out_shape=jax.ShapeDtypeStruct((depth, rows, n), F32),
        grid=(depth, n // nt),
        in_specs=[
            pl.BlockSpec((rows, d), lambda l, j: (0, 0)),
            pl.BlockSpec((None, d, nt), lambda l, j: (l, 0, j)),
            pl.BlockSpec((None, 1, nt), lambda l, j: (l, 0, j)),
        ],
        out_specs=pl.BlockSpec((None, rows, nt), lambda l, j: (l, 0, j)),
        compiler_params=_cparams("parallel", "parallel"),
        name="ada_mod",
    )(cs, ada_w, ada_b.reshape(depth, 1, n))


def _rope(r, c, s, first_half):
    partner = jnp.where(first_half, pltpu.roll(r, LANES - 16, 1), pltpu.roll(r, 16, 1))
    return r * c + partner * s


def _proj_kernel(*refs, kinds, nc, has_rope):
    if has_rope:
        x_ref, g_ref, sc_ref, sh_ref, w_ref, cq_ref, sq_ref, ck_ref, sk_ref, o_ref = refs
    else:
        x_ref, g_ref, sc_ref, sh_ref, w_ref, o_ref = refs
    t = x_ref.shape[0]
    nout = o_ref.shape[1]
    h = (_rms(x_ref[...]) * g_ref[...]) * (1.0 + sc_ref[...]) + sh_ref[...]
    hb = h.astype(BF16)
    if has_rope:
        lane = lax.broadcasted_iota(jnp.int32, (t, LANES), 1)
        first_half = (lane & 16) == 0
    for n0 in range(0, nout, nc):
        r = jnp.dot(hb, w_ref[:, n0:n0 + nc], preferred_element_type=F32)
        for j in range(nc // LANES):
            kind = kinds[n0 // LANES + j]
            rj = r[:, j * LANES:(j + 1) * LANES]
            if kind == ROPEQ:
                rj = _rope(rj, cq_ref[...], sq_ref[...], first_half)
            elif kind == ROPEK:
                rj = _rope(rj, ck_ref[...], sk_ref[...], first_half)
            elif kind == SCALEQ:
                rj = rj * Q_SCALE
            o_ref[:, n0 + j * LANES:n0 + (j + 1) * LANES] = rj.astype(BF16)


def _proj_call(x, gamma, mod, w, kinds, tables, *, tiles_per_seq, mod_row, name):
    n, d = x.shape
    nout = w.shape[1]
    t = T_PROJ
    has_rope = tables is not None

    def mrow(i):
        return mod_row if mod_row is not None else i // tiles_per_seq

    def tblock(i):
        return i % tiles_per_seq if mod_row is None else tiles_per_seq

    in_specs = [
        pl.BlockSpec((t, d), lambda i: (i, 0)),
        pl.BlockSpec((1, d), lambda i: (0, 0)),
        pl.BlockSpec((None, None, 1, d), lambda i: (mrow(i), 1, 0, 0)),
        pl.BlockSpec((None, None, 1, d), lambda i: (mrow(i), 0, 0, 0)),
        pl.BlockSpec((d, nout), lambda i: (0, 0)),
    ]
    args = [x, gamma.reshape(1, d), mod, mod, w]
    if has_rope:
        for tab in tables:
            in_specs.append(pl.BlockSpec((t, LANES), lambda i: (tblock(i), 0)))
            args.append(tab)
    return pl.pallas_call(
        functools.partial(_proj_kernel, kinds=tuple(kinds), nc=512, has_rope=has_rope),
        out_shape=jax.ShapeDtypeStruct((n, nout), BF16),
        grid=(n // t,),
        in_specs=in_specs,
        out_specs=pl.BlockSpec((t, nout), lambda i: (i, 0)),
        compiler_params=_cparams("parallel"),
        name=name,
    )(*args)


def _qk(q, k):
    return lax.dot_general(q, k, (((1,), (1,)), ((), ())), preferred_element_type=F32)


def _half_masks(shape):
    lane = lax.broadcasted_iota(jnp.int32, shape, 1)
    return lane < HEAD_DIM


def _split_heads(q, lo):
    qf = q.astype(F32)
    zero = jnp.zeros_like(qf)
    return jnp.where(lo, qf, zero).astype(q.dtype), jnp.where(lo, zero, qf).astype(q.dtype)


def _diff_lambda(lam_ref, lam_init):
    lv = lam_ref[...]
    d1 = jnp.sum(lv[0:1] * lv[1:2], axis=1, keepdims=True)
    d2 = jnp.sum(lv[2:3] * lv[3:4], axis=1, keepdims=True)
    return jnp.exp(d1) - jnp.exp(d2) + lam_init


def _diff_finish(o1, o2, lam, subln_ref, lam_init, o_ref):
    o = _rms(o1 - lam * o2) * subln_ref[...] * (1.0 - lam_init)
    o_ref[...] = o.astype(o_ref.dtype)


def _diff_kernel(lam_ref, subln_ref, q_ref, kl_ref, vl_ref, kc_ref, vc_ref, o_ref,
                 vt, s_scr, p_scr, m_scr, o_scr, *, lam_init, tq, kc):
    seq, nctx = q_ref.shape[0], kc_ref.shape[0]
    nkeys = vt.shape[1]
    nlat = seq // kc
    kcc = min(kc, nctx)
    nchunk = nlat + nctx // kcc
    ntile = seq // tq
    sub = m_scr.shape[1]
    lam = _diff_lambda(lam_ref, lam_init)
    lo = _half_masks((tq, LANES))

    def rows(c):
        if c < nlat:
            return slice(c * kc, (c + 1) * kc)
        return slice(seq + (c - nlat) * kcc, seq + (c - nlat + 1) * kcc)

    def chunk_of(lat_ref, ctx_ref, c):
        if c < nlat:
            return lat_ref[rows(c), :]
        return ctx_ref[(c - nlat) * kcc:(c - nlat + 1) * kcc, :]

    eye = (lax.broadcasted_iota(jnp.int32, (LANES, LANES), 0)
           == lax.broadcasted_iota(jnp.int32, (LANES, LANES), 1)).astype(F32).astype(BF16)
    for c in range(nchunk):
        vt[0:LANES, rows(c)] = _qk(eye, chunk_of(vl_ref, vc_ref, c)).astype(BF16)
    vt[LANES:, :] = jnp.ones((vt.shape[0] - LANES, nkeys), BF16)

    def half(m, tile, do_qk, do_exp, do_pv):
        if do_qk:
            q = q_ref[pl.ds(pl.multiple_of(tile * tq, tq), tq), :].astype(F32)
            zero = jnp.zeros_like(q)
            qm = (jnp.where(lo, q, zero) if m == 0 else jnp.where(lo, zero, q)).astype(BF16)
            mrun = jnp.full((sub, tq), NEG_INF, F32)
        if do_exp:
            mb = m_scr[1 - m, 0:1, :]
        if do_pv:
            acc = jnp.zeros((vt.shape[0], tq), F32)
        for c in range(nchunk):
            if do_qk:
                s = _qk(chunk_of(kl_ref, kc_ref, c), qm)
                s_scr[m, rows(c), :] = s
                for u in range(s.shape[0] // sub):
                    mrun = jnp.maximum(mrun, s[u * sub:(u + 1) * sub, :])
            if do_pv:
                acc = acc + jnp.dot(vt[:, rows(c)], p_scr[1 - m, rows(c), :], preferred_element_type=F32)
            if do_exp:
                p_scr[1 - m, rows(c), :] = jnp.exp2(s_scr[1 - m, rows(c), :] - mb).astype(BF16)
        if do_qk:
            m_scr[m] = jnp.broadcast_to(jnp.max(mrun, axis=0, keepdims=True), (sub, tq))
        if do_pv:
            return acc[:LANES] * (1.0 / acc[LANES:LANES + 1])
        return None

    def emit(tile, o2):
        o = _rms((o_scr[...] - lam * o2).T) * subln_ref[...] * (1.0 - lam_init)
        o_ref[pl.ds(pl.multiple_of(tile * tq, tq), tq), :] = o.astype(o_ref.dtype)

    half(0, 0, True, False, False)
    half(1, 0, True, True, False)
    half(0, 1, True, True, False)
    o_scr[...] = half(1, 1, True, True, True)

    def body(t, carry):
        emit(t - 2, half(0, t, True, True, True))
        o_scr[...] = half(1, t, True, True, True)
        return carry

    lax.fori_loop(2, ntile, body, 0)
    emit(ntile - 2, half(0, ntile, False, True, True))
    o_scr[...] = half(1, ntile, False, False, True)
    emit(ntile - 1, half(0, ntile + 1, False, False, True))


def _diff_ctx_kernel(lam_ref, subln_ref, q_ref, kc_ref, vc_ref, o_ref, *, lam_init):
    tq = q_ref.shape[0]
    lam = _diff_lambda(lam_ref, lam_init)
    lo = _half_masks((tq, LANES))
    for h in range(q_ref.shape[1] // LANES):
        cols = slice(h * LANES, (h + 1) * LANES)
        qs = _split_heads(q_ref[:, cols], lo)
        o1, o2 = [_softmax_pv([_qk(qm, kc_ref[:, cols])], [vc_ref[:, cols]]) for qm in qs]
        _diff_finish(o1, o2, lam, subln_ref, lam_init, o_ref.at[:, cols])


def _diff_attention(qkv, qkvc, lam_vecs, subln, lam_init, batch, seq, nctx, need_ctx):
    d = qkv.shape[1] // 3
    nh = d // LANES
    tq = DIFF_TQ
    kc = DIFF_KC
    assert seq % tq == 0 and seq % kc == 0 and nctx % min(kc, nctx) == 0 and seq // tq >= 2
    nkeys = seq + nctx
    subln2 = subln.reshape(1, LANES)
    y = pl.pallas_call(
        functools.partial(_diff_kernel, lam_init=lam_init, tq=tq, kc=kc),
        out_shape=jax.ShapeDtypeStruct((batch * seq, d), BF16),
        grid=(batch, nh),
        in_specs=[
            pl.BlockSpec((4, HEAD_DIM), lambda b, h: (0, 0)),
            pl.BlockSpec((1, LANES), lambda b, h: (0, 0)),
            pl.BlockSpec((seq, LANES), lambda b, h: (b, h)),
            pl.BlockSpec((seq, LANES), lambda b, h: (b, nh + h)),
            pl.BlockSpec((seq, LANES), lambda b, h: (b, 2 * nh + h)),
            pl.BlockSpec((nctx, LANES), lambda b, h: (b, nh + h)),
            pl.BlockSpec((nctx, LANES), lambda b, h: (b, 2 * nh + h)),
        ],
        out_specs=pl.BlockSpec((seq, LANES), lambda b, h: (b, h)),
        scratch_shapes=[pltpu.VMEM((LANES + BF16_ROWS, nkeys), BF16), pltpu.VMEM((2, nkeys, tq), F32),
                        pltpu.VMEM((2, nkeys, tq), BF16), pltpu.VMEM((2, F32_ROWS, tq), F32),
                        pltpu.VMEM((LANES, tq), F32)],
        compiler_params=_cparams("parallel", "parallel"),
        name="diff_attn",
    )(lam_vecs, subln2, qkv, qkv, qkv, qkvc, qkvc)
    yc = None
    if need_ctx:
        yc = pl.pallas_call(
            functools.partial(_diff_ctx_kernel, lam_init=lam_init),
            out_shape=jax.ShapeDtypeStruct((batch * nctx, d), BF16),
            grid=(batch,),
            in_specs=[
                pl.BlockSpec((4, HEAD_DIM), lambda b: (0, 0)),
                pl.BlockSpec((1, LANES), lambda b: (0, 0)),
                pl.BlockSpec((nctx, d), lambda b: (b, 0)),
                pl.BlockSpec((nctx, d), lambda b: (b, 1)),
                pl.BlockSpec((nctx, d), lambda b: (b, 2)),
            ],
            out_specs=pl.BlockSpec((nctx, d), lambda b: (b, 0)),
            compiler_params=_cparams("parallel"),
            name="diff_attn_ctx",
        )(lam_vecs, subln2, qkvc, qkvc, qkvc)
    return y, yc


def _softmax_pv(scores, values, extra_logit=None):
    m = jnp.max(scores[0], axis=1, keepdims=True)
    for s in scores[1:]:
        m = jnp.maximum(m, jnp.max(s, axis=1, keepdims=True))
    if extra_logit is not None:
        m = jnp.maximum(m, extra_logit)
    l = None
    acc = None
    for s, v in zip(scores, values):
        p = jnp.exp2(s - m)
        ps = jnp.sum(p, axis=1, keepdims=True)
        pv = jnp.dot(p.astype(BF16), v, preferred_element_type=F32)
        l = ps if l is None else l + ps
        acc = pv if acc is None else acc + pv
    if extra_logit is not None:
        l = l + jnp.exp2(extra_logit - m)
    return acc * (1.0 / l)


def _identity(n):
    return (lax.broadcasted_iota(jnp.int32, (n, n), 0)
            == lax.broadcasted_iota(jnp.int32, (n, n), 1)).astype(F32).astype(BF16)


def _fill_vt(vt, head0, chunk, v_block, n_heads):
    tr = _qk(_identity(LANES), v_block).astype(BF16)
    for hh in range(n_heads):
        vt[head0 + hh, chunk, 0:HEAD_DIM, :] = tr[hh * HEAD_DIM:(hh + 1) * HEAD_DIM]
        vt[head0 + hh, chunk, HEAD_DIM:, :] = jnp.ones((vt.shape[2] - HEAD_DIM, vt.shape[3]), BF16)


def _window_pipeline(q_ref, o_ref, s_scr, p_scr, m_scr, e_scr, o_scr, *, tq, kc, nchunk,
                     scores, values, extra_logit):
    ntile = q_ref.shape[0] // tq
    sub = m_scr.shape[1]
    lo = _half_masks((tq, LANES))
    rows = lambda c: slice(c * kc, (c + 1) * kc)
    tile_rows = lambda t: pl.ds(pl.multiple_of(t * tq, tq), tq)

    def half(j, t, do_qk, do_exp, do_pv):
        jo = 1 - j
        tv = t - 2 + j
        if do_qk:
            q = q_ref[tile_rows(t), :].astype(F32)
            zero = jnp.zeros_like(q)
            qm = (jnp.where(lo, q, zero) if j == 0 else jnp.where(lo, zero, q)).astype(BF16)
            mrun = jnp.full((sub, tq), NEG_INF, F32)
        if do_exp:
            mb = m_scr[jo, 0:1, :]
        if do_pv:
            acc = jnp.zeros((o_scr.shape[0] + BF16_ROWS, tq), F32)
        for c in range(nchunk):
            if do_qk:
                s = scores(qm, t, j, c)
                s_scr[j, rows(c), :] = s
                for u in range(kc // sub):
                    mrun = jnp.maximum(mrun, s[u * sub:(u + 1) * sub, :])
            if do_pv:
                acc = acc + jnp.dot(values(tv, jo, c), p_scr[jo, rows(c), :], preferred_element_type=F32)
            if do_exp:
                p_scr[jo, rows(c), :] = jnp.exp2(s_scr[jo, rows(c), :] - mb).astype(BF16)
        if do_qk:
            m = jnp.max(mrun, axis=0, keepdims=True)
            if extra_logit is not None:
                m = jnp.maximum(m, extra_logit(j))
            m_scr[j] = jnp.broadcast_to(m, (sub, tq))
        o = None
        if do_pv:
            l = acc[HEAD_DIM:HEAD_DIM + 1]
            if extra_logit is not None:
                l = l + e_scr[jo, 0:1, :]
            o = acc[:HEAD_DIM] * (1.0 / l)
        if do_exp and extra_logit is not None:
            e_scr[jo] = jnp.exp2(extra_logit(jo) - m_scr[jo])
        return o

    def emit(t, o1):
        o = jnp.concatenate([o_scr[...], o1], axis=0).T
        o_ref[tile_rows(t), :] = o.astype(o_ref.dtype)

    half(0, 0, True, False, False)
    half(1, 0, True, True, False)
    half(0, 1, True, True, False)
    o_scr[...] = half(1, 1, True, True, True)

    for t in range(2, ntile):
        emit(t - 2, half(0, t, True, True, True))
        o_scr[...] = half(1, t, True, True, True)
    emit(ntile - 2, half(0, ntile, False, True, True))
    o_scr[...] = half(1, ntile, False, False, True)
    emit(ntile - 1, half(0, ntile + 1, False, False, True))


def _tile_class(t, ntile):
    return jnp.where(t == 0, 0, jnp.where(t == ntile - 1, 2, 1))


def _na_kernel(bias_ref, q_ref, k_ref, v_ref, kc_ref, vc_ref, o_ref, vt, s_scr, p_scr, m_scr, o_scr,
               *, tq, kc):
    seq, nctx = q_ref.shape[0], kc_ref.shape[0]
    ntile, nlat, nwin = seq // tq, seq // kc, 3 * tq // kc
    rows = lambda c: slice(c * kc, (c + 1) * kc)
    for c in range(nlat):
        _fill_vt(vt, 0, c, v_ref[rows(c), :], 2)
    for c in range(nctx // kc):
        _fill_vt(vt, 0, nlat + c, vc_ref[rows(c), :], 2)

    def win_start(t):
        return pl.multiple_of(jnp.clip((t - 1) * tq, 0, seq - 3 * tq), tq)

    def scores(qm, t, j, c):
        if c >= nwin:
            return _qk(kc_ref[rows(c - nwin), :], qm)
        k = k_ref[pl.ds(win_start(t) + c * kc, kc), :]
        return _qk(k, qm) + bias_ref[_tile_class(t, ntile), j, rows(c), :]

    def values(t, j, c):
        return vt[j, nlat + c - nwin] if c >= nwin else vt[j, win_start(t) // kc + c]

    _window_pipeline(q_ref, o_ref, s_scr, p_scr, m_scr, None, o_scr, tq=tq, kc=kc,
                     nchunk=nwin + nctx // kc, scores=scores, values=values, extra_logit=None)


def _plain_ctx_kernel(q_ref, kc_ref, vc_ref, o_ref):
    tq = q_ref.shape[0]
    lo = _half_masks((tq, LANES))
    for hb in range(q_ref.shape[1] // LANES):
        cols = slice(hb * LANES, (hb + 1) * LANES)
        qs = _split_heads(q_ref[:, cols], lo)
        outs = [_softmax_pv([_qk(qs[hh], kc_ref[:, cols])], [vc_ref[:, cols]]) for hh in range(2)]
        o_ref[:, cols] = jnp.where(lo, outs[0], outs[1]).astype(o_ref.dtype)


def _na_bias_tables(rpb, rows):
    w = GRID_W
    nk = 3 * NA_ROWS
    nh = rpb.shape[0]
    edge = w - NA_KW
    ext = jnp.concatenate([jnp.repeat(rpb[..., :1], edge, axis=-1), rpb,
                           jnp.repeat(rpb[..., -1:], edge + 1, axis=-1)], axis=-1)
    flat = jnp.tile(ext, (1, 1, w + 1))[..., :w * (2 * w + 1)]
    hankel = flat.reshape(nh, -1, w, 2 * w + 1)[..., :w]
    toep = jnp.flip(hankel, axis=-2)
    cq = np.arange(w)[:, None]
    cx = np.arange(w)[None, :]
    col_start = np.clip(cq - NA_KW // 2, 0, w - NA_KW)
    col_ok = (cx >= col_start) & (cx < col_start + NA_KW)
    toep = jnp.where(col_ok, toep * LOG2E, NEG_INF)
    masked = jnp.full((nh, w, w), NEG_INF, F32)
    tabs = []
    for r0, k0 in ((0, 0), (NA_ROWS, 0), (rows - NA_ROWS, rows - nk)):
        row_blocks = []
        for a in range(NA_ROWS):
            r = r0 + a
            rs = min(max(r - NA_KH // 2, 0), rows - NA_KH)
            blocks = []
            for e in range(nk):
                ry = k0 + e
                blocks.append(toep[:, ry - r + NA_KH - 1] if rs <= ry < rs + NA_KH else masked)
            row_blocks.append(jnp.concatenate(blocks, axis=-1))
        tabs.append(jnp.concatenate(row_blocks, axis=-2))
    return jnp.swapaxes(jnp.stack(tabs), -1, -2).astype(F32)


def _win_scratch(nheads, nchunks, nkeys_item, tq, kc, with_extra):
    shapes = [pltpu.VMEM((nheads, nchunks, HEAD_DIM + BF16_ROWS, kc), BF16),
              pltpu.VMEM((2, nkeys_item, tq), F32), pltpu.VMEM((2, nkeys_item, tq), BF16),
              pltpu.VMEM((2, F32_ROWS, tq), F32)]
    if with_extra:
        shapes.append(pltpu.VMEM((2, F32_ROWS, tq), F32))
    shapes.append(pltpu.VMEM((HEAD_DIM, tq), F32))
    return shapes


def _na_attention(qkv, qkvc, rpb, batch, seq, nctx, need_ctx):
    d = qkv.shape[1] // 3
    nb = d // LANES
    rows = seq // GRID_W
    tq = NA_ROWS * GRID_W
    kc = tq
    assert seq // tq >= 3 and nctx % kc == 0
    bias = _na_bias_tables(rpb.astype(F32), rows)
    y = pl.pallas_call(
        functools.partial(_na_kernel, tq=tq, kc=kc),
        out_shape=jax.ShapeDtypeStruct((batch * seq, d), BF16),
        grid=(nb, batch),
        in_specs=[
            pl.BlockSpec((3, 2, 3 * tq, tq), lambda hb, b: (0, hb, 0, 0), pipeline_mode=pl.Buffered(1)),
            pl.BlockSpec((seq, LANES), lambda hb, b: (b, hb)),
            pl.BlockSpec((seq, LANES), lambda hb, b: (b, nb + hb)),
            pl.BlockSpec((seq, LANES), lambda hb, b: (b, 2 * nb + hb)),
            pl.BlockSpec((nctx, LANES), lambda hb, b: (b, nb + hb)),
            pl.BlockSpec((nctx, LANES), lambda hb, b: (b, 2 * nb + hb)),
        ],
        out_specs=pl.BlockSpec((seq, LANES), lambda hb, b: (b, hb)),
        scratch_shapes=_win_scratch(2, (seq + nctx) // kc, 3 * tq + nctx, tq, kc, False),
        compiler_params=_cparams("parallel", "arbitrary"),
        name="na_attn",
    )(bias, qkv, qkv, qkv, qkvc, qkvc)
    yc = None
    if need_ctx:
        yc = pl.pallas_call(
            _plain_ctx_kernel,
            out_shape=jax.ShapeDtypeStruct((batch * nctx, d), BF16),
            grid=(batch,),
            in_specs=[
                pl.BlockSpec((nctx, d), lambda b: (b, 0)),
                pl.BlockSpec((nctx, d), lambda b: (b, 1)),
                pl.BlockSpec((nctx, d), lambda b: (b, 2)),
            ],
            out_specs=pl.BlockSpec((nctx, d), lambda b: (b, 0)),
            compiler_params=_cparams("parallel"),
            name="na_attn_ctx",
        )(qkvc, qkvc, qkvc)
    return y, yc


def _swa_heads(q_ref, sinks_ref, g, score_fn, values, o_ref):
    tq = q_ref.shape[0]
    lo = _half_masks((tq, LANES))
    for jb in range(2):
        qs = _split_heads(q_ref[:, jb * LANES:(jb + 1) * LANES], lo)
        outs = []
        for hh in range(2):
            sink = jnp.full((1, 1), sinks_ref[g * 4 + jb * 2 + hh] * LOG2E, F32)
            outs.append(_softmax_pv(score_fn(qs[hh]), values, extra_logit=sink))
        o_ref[:, jb * LANES:(jb + 1) * LANES] = jnp.where(lo, outs[0], outs[1]).astype(o_ref.dtype)


def _swa_kernel(sinks_ref, band_ref, q_ref, k_ref, v_ref, kc_ref, vc_ref, o_ref,
                vt, s_scr, p_scr, m_scr, e_scr, o_scr, *, tq, kc, span):
    head0 = (pl.program_id(1) * 2 + pl.program_id(2)) * 2
    seq, nctx = q_ref.shape[0], kc_ref.shape[0]
    ntile, nlat, nwin = seq // tq, seq // kc, span // kc
    rows = lambda c: slice(c * kc, (c + 1) * kc)
    for c in range(nlat):
        _fill_vt(vt, 0, c, v_ref[rows(c), :], 1)
    for c in range(nctx // kc):
        _fill_vt(vt, 0, nlat + c, vc_ref[rows(c), :], 1)

    def win_start(t):
        return pl.multiple_of(jnp.clip(t * tq - SWA_WINDOW, 0, seq - span), SWA_WINDOW)

    def scores(qm, t, j, c):
        if c >= nwin:
            return _qk(kc_ref[rows(c - nwin), :], qm)
        k = k_ref[pl.ds(win_start(t) + c * kc, kc), :]
        return _qk(k, qm) + band_ref[_tile_class(t, ntile), rows(c), :]

    def values(t, j, c):
        return vt[0, nlat + c - nwin] if c >= nwin else vt[0, win_start(t) // kc + c]

    def sink(j):
        return jnp.full((1, 1), sinks_ref[head0 + j] * LOG2E, F32)

    _window_pipeline(q_ref, o_ref, s_scr, p_scr, m_scr, e_scr, o_scr, tq=tq, kc=kc,
                     nchunk=nwin + nctx // kc, scores=scores, values=values, extra_logit=sink)


def _swa_band_tables(seq, tq, span):
    ntile = seq // tq
    tabs = []
    for t in (0, 1, ntile - 1):
        start = min(max(t * tq - SWA_WINDOW, 0), seq - span)
        kpos = start + np.arange(span)[:, None]
        qpos = t * tq + np.arange(tq)[None, :]
        tabs.append(np.where(np.abs(qpos - kpos) <= SWA_WINDOW, 0.0, NEG_INF))
    return jnp.asarray(np.stack(tabs), F32)


def _swa_ctx_kernel(sinks_ref, q_ref, kc_ref, vc_ref, o_ref):
    qb = 2 * LANES
    for g in range(kc_ref.shape[1] // LANES):
        kc = kc_ref[:, g * LANES:(g + 1) * LANES]
        vc = vc_ref[:, g * LANES:(g + 1) * LANES]
        cols = slice(g * qb, (g + 1) * qb)
        _swa_heads(q_ref.at[:, cols], sinks_ref, g, lambda qm, kc=kc: [_qk(qm, kc)], [vc], o_ref.at[:, cols])


def _swa_attention(qkv, qkvc, sinks, batch, seq, nctx, need_ctx, d):
    nkv = (qkv.shape[1] - d) // (2 * LANES)
    qb = d // nkv
    kcol = d // LANES
    tq = SWA_TQ
    kc = SWA_WINDOW
    span = tq + 2 * SWA_WINDOW
    assert qb == 2 * LANES and seq // tq >= 3 and nctx % kc == 0
    smem = pl.BlockSpec(memory_space=pltpu.SMEM)
    sinks = sinks.astype(F32)
    y = pl.pallas_call(
        functools.partial(_swa_kernel, tq=tq, kc=kc, span=span),
        out_shape=jax.ShapeDtypeStruct((batch * seq, d), BF16),
        grid=(batch, nkv, 2),
        in_specs=[
            smem,
            pl.BlockSpec((3, span, tq), lambda b, g, jb: (0, 0, 0), pipeline_mode=pl.Buffered(1)),
            pl.BlockSpec((seq, LANES), lambda b, g, jb: (b, 2 * g + jb)),
            pl.BlockSpec((seq, LANES), lambda b, g, jb: (b, kcol + g)),
            pl.BlockSpec((seq, LANES), lambda b, g, jb: (b, kcol + nkv + g)),
            pl.BlockSpec((nctx, LANES), lambda b, g, jb: (b, kcol + g)),
            pl.BlockSpec((nctx, LANES), lambda b, g, jb: (b, kcol + nkv + g)),
        ],
        out_specs=pl.BlockSpec((seq, LANES), lambda b, g, jb: (b, 2 * g + jb)),
        scratch_shapes=_win_scratch(1, (seq + nctx) // kc, span + nctx, tq, kc, True),
        compiler_params=_cparams("parallel", "parallel", "arbitrary"),
        name="swa_attn",
    )(sinks, _swa_band_tables(seq, tq, span), qkv, qkv, qkv, qkvc, qkvc)
    yc = None
    if need_ctx:
        yc = pl.pallas_call(
            _swa_ctx_kernel,
            out_shape=jax.ShapeDtypeStruct((batch * nctx, d), BF16),
            grid=(batch,),
            in_specs=[
                smem,
                pl.BlockSpec((nctx, d), lambda b: (b, 0)),
                pl.BlockSpec((nctx, nkv * LANES), lambda b: (b, d // (nkv * LANES))),
                pl.BlockSpec((nctx, nkv * LANES), lambda b: (b, d // (nkv * LANES) + 1)),
            ],
            out_specs=pl.BlockSpec((nctx, d), lambda b: (b, 0)),
            compiler_params=_cparams("parallel"),
            name="swa_attn_ctx",
        )(sinks, qkvc, qkvc, qkvc)
    return y, yc


def _oproj_kernel(o_ref, wo_ref, x_ref, g1_ref, nf_ref, sc2_ref, sh2_ref, xo_ref, h2_ref):
    y = jnp.dot(o_ref[...], wo_ref[...], preferred_element_type=F32)
    xn = x_ref[...] + g1_ref[...] * y
    xo_ref[...] = xn
    h2 = (_rms(xn) * nf_ref[...]) * (1.0 + sc2_ref[...]) + sh2_ref[...]
    h2_ref[...] = h2.astype(BF16)


def _oproj_call(o, wo, x, gamma, mod, *, tiles_per_seq, mod_row, name):
    n, d = x.shape
    t = T_PROJ

    def mrow(i):
        return mod_row if mod_row is not None else i // tiles_per_seq

    def mspec(which):
        return pl.BlockSpec((None, None, 1, d), lambda i: (mrow(i), which, 0, 0))

    return pl.pallas_call(
        _oproj_kernel,
        out_shape=(jax.ShapeDtypeStruct((n, d), F32), jax.ShapeDtypeStruct((n, d), BF16)),
        grid=(n // t,),
        in_specs=[
            pl.BlockSpec((t, d), lambda i: (i, 0)),
            pl.BlockSpec((d, d), lambda i: (0, 0)),
            pl.BlockSpec((t, d), lambda i: (i, 0)),
            mspec(2),
            pl.BlockSpec((1, d), lambda i: (0, 0)),
            mspec(4),
            mspec(3),
        ],
        out_specs=(pl.BlockSpec((t, d), lambda i: (i, 0)), pl.BlockSpec((t, d), lambda i: (i, 0))),
        compiler_params=_cparams("parallel"),
        name=name,
    )(o, wo, x, mod, gamma.reshape(1, d), mod, mod)


def _ffn_kernel(h_ref, hp_ref, hn_ref, wup_ref, cw_ref, cb_ref, wd_ref, x_ref, g2_ref, no_ref, o_ref,
                lhs, act, ubuf, *, nseq, final, f):
    i = pl.program_id(0)
    t = h_ref.shape[0]
    halo = hp_ref.shape[0]
    dff = wd_ref.shape[0]
    rows = t + 2 * halo

    keep_prev = ((i % nseq) != 0).astype(F32)
    keep_next = ((i % nseq) != nseq - 1).astype(F32)
    lhs[0:halo, :] = (hp_ref[...].astype(F32) * keep_prev).astype(BF16)
    lhs[halo:halo + t, :] = h_ref[...]
    lhs[halo + t:, :] = (hn_ref[...].astype(F32) * keep_next).astype(BF16)

    def conv(col0, slot):
        u = jnp.dot(lhs[...], wup_ref[:, col0:col0 + f], preferred_element_type=F32)
        outs = []
        for s in range(f // LANES):
            sl = slice(col0 + s * LANES, col0 + (s + 1) * LANES)
            buf = ubuf.at[slot, s]
            buf[pl.ds(0, rows, stride=2), :] = u[:, s * LANES:(s + 1) * LANES]
            taps = [buf[pl.ds(2 * (halo - 1 + k), t, stride=2), :] for k in range(CONV_W)]
            outs.append(cb_ref[:, sl] + taps[0] * cw_ref[0:1, sl] + taps[1] * cw_ref[1:2, sl]
                        + taps[2] * cw_ref[2:3, sl])
        return jnp.concatenate(outs, axis=1)

    for c in range(dff // f):
        slot = 2 * (c % 2)
        a = conv(c * f, slot)
        g = conv(dff + c * f, slot + 1)
        act[:, c * f:(c + 1) * f] = ((g * (1.0 / (1.0 + jnp.exp(-g)))) * a).astype(BF16)

    xn = x_ref[...] + g2_ref[...] * jnp.dot(act[...], wd_ref[...], preferred_element_type=F32)
    if final:
        xn = _rms(xn) * no_ref[...]
    o_ref[...] = xn


def _ffn_call(h2, x, w_up, w_conv, b_conv, w_down, mod, norm_out, *, t, tiles_per_seq, mod_row, final, name):
    n, d = x.shape
    dff = w_down.shape[0]
    f = FF_CHUNK
    halo = BF16_ROWS
    hb = t // halo
    last_hblock = n // halo - 1
    resident = pl.Buffered(1)

    def mrow(i):
        return mod_row if mod_row is not None else i // tiles_per_seq

    return pl.pallas_call(
        functools.partial(_ffn_kernel, nseq=tiles_per_seq, final=final, f=f),
        out_shape=jax.ShapeDtypeStruct((n, d), F32),
        grid=(n // t,),
        in_specs=[
            pl.BlockSpec((t, d), lambda i: (i, 0)),
            pl.BlockSpec((halo, d), lambda i: (jnp.maximum(i * hb - 1, 0), 0)),
            pl.BlockSpec((halo, d), lambda i: (jnp.minimum((i + 1) * hb, last_hblock), 0)),
            pl.BlockSpec((d, 2 * dff), lambda i: (0, 0), pipeline_mode=resident),
            pl.BlockSpec((CONV_W, 2 * dff), lambda i: (0, 0), pipeline_mode=resident),
            pl.BlockSpec((1, 2 * dff), lambda i: (0, 0), pipeline_mode=resident),
            pl.BlockSpec((dff, d), lambda i: (0, 0), pipeline_mode=resident),
            pl.BlockSpec((t, d), lambda i: (i, 0)),
            pl.BlockSpec((None, None, 1, d), lambda i: (mrow(i), 5, 0, 0)),
            pl.BlockSpec((1, d), lambda i: (0, 0)),
        ],
        out_specs=pl.BlockSpec((t, d), lambda i: (i, 0)),
        scratch_shapes=[pltpu.VMEM((t + 2 * halo, d), BF16), pltpu.VMEM((t, dff), BF16),
                        pltpu.VMEM((4, f // LANES, 2 * (t + 2 * halo), LANES), F32)],
        compiler_params=_cparams("parallel"),
        name=name,
    )(h2, h2, h2, w_up, w_conv, b_conv.reshape(1, 2 * dff), w_down, x, mod, norm_out.reshape(1, d))


def _rope_tables(seq, extra_rows):
    tpos = jnp.arange(seq, dtype=jnp.int32)
    row = (tpos // GRID_W).astype(F32)
    col = (tpos % GRID_W).astype(F32)
    inv = ROPE_BASE ** (-jnp.arange(ROPE_PAIRS, dtype=F32) / ROPE_PAIRS)
    ar = row[:, None] * inv[None, :]
    ac = col[:, None] * inv[None, :]
    cr, sr, cc, sc = jnp.cos(ar), jnp.sin(ar), jnp.cos(ac), jnp.sin(ac)
    cos = jnp.concatenate([cr, cr, cc, cc] * (LANES // HEAD_DIM), axis=1)
    sin = jnp.concatenate([-sr, sr, -sc, sc] * (LANES // HEAD_DIM), axis=1)
    cos = jnp.concatenate([cos, jnp.ones((extra_rows, LANES), F32)], axis=0)
    sin = jnp.concatenate([sin, jnp.zeros((extra_rows, LANES), F32)], axis=0)
    return cos * Q_SCALE, sin * Q_SCALE, cos, sin


def _dup_kv_columns(w, d, nkv):
    q = w[:, :d]
    k = w[:, d:d + nkv * HEAD_DIM].reshape(-1, nkv, 1, HEAD_DIM)
    v = w[:, d + nkv * HEAD_DIM:].reshape(-1, nkv, 1, HEAD_DIM)
    kd = jnp.broadcast_to(k, (w.shape[0], nkv, 2, HEAD_DIM)).reshape(w.shape[0], -1)
    vd = jnp.broadcast_to(v, (w.shape[0], nkv, 2, HEAD_DIM)).reshape(w.shape[0], -1)
    return jnp.concatenate([q, kd, vd], axis=1)


def kernel(x, c, ctx, c_ctx, ada_w, ada_b, norm_mix, norm_ffn, norm_out, ffn_up, ffn_conv, ffn_conv_b,
           ffn_down, a_wqkv, a_wo, a_lambda, a_subln, b_wqkv, b_wo, b_rpb, c_wqkv, c_wo, c_sinks):
    batch, seq, d = x.shape
    nctx = ctx.shape[1]
    depth = ada_w.shape[0]
    assert seq % T_FFN == 0 and seq % T_PROJ == 0 and (batch * nctx) % T_PROJ == 0

    mod_rows = 16
    cs = jnp.concatenate([c, c_ctx[None, :], jnp.zeros((mod_rows - batch - 1, d), F32)], axis=0)
    mod_all = _ada_call(cs, ada_w, ada_b).reshape(depth, mod_rows, 6, 1, d)
    ctx_row = batch

    tables = _rope_tables(seq, T_PROJ)
    lat_tiles = seq // T_PROJ
    nd = d // LANES

    xl = x.reshape(batch * seq, d)
    xc = ctx.reshape(batch * nctx, d)
    for i in range(depth):
        need_ctx = i < depth - 1
        kind, j = i % N_MIXERS, i // N_MIXERS
        mod = mod_all[i]
        if kind == 0:
            w = a_wqkv[j].astype(BF16)
            wo = a_wo[j].astype(BF16)
            kinds = [ROPEQ] * nd + [ROPEK] * nd + [PLAIN] * nd
            tabs = tables
        elif kind == 1:
            w = b_wqkv[j].astype(BF16)
            wo = b_wo[j].astype(BF16)
            kinds = [SCALEQ] * nd + [PLAIN] * (2 * nd)
            tabs = None
        else:
            nkv = (c_wqkv.shape[2] - d) // (2 * HEAD_DIM)
            w = _dup_kv_columns(c_wqkv[j], d, nkv).astype(BF16)
            wo = c_wo[j].astype(BF16)
            kinds = [ROPEQ] * nd + [ROPEK] * nkv + [PLAIN] * nkv
            tabs = tables
        qkv = _proj_call(xl, norm_mix[i], mod, w, kinds, tabs, tiles_per_seq=lat_tiles, mod_row=None,
                         name=f"proj{i}")
        qkvc = _proj_call(xc, norm_mix[i], mod, w, kinds, tabs, tiles_per_seq=lat_tiles, mod_row=ctx_row,
                          name=f"proj_ctx{i}")
        if kind == 0:
            lam_init = 0.8 - 0.6 * math.exp(-0.3 * i)
            y, yc = _diff_attention(qkv, qkvc, a_lambda[j].astype(F32), a_subln[j], lam_init,
                                    batch, seq, nctx, need_ctx)
        elif kind == 1:
            y, yc = _na_attention(qkv, qkvc, b_rpb[j], batch, seq, nctx, need_ctx)
        else:
            y, yc = _swa_attention(qkv, qkvc, c_sinks[j], batch, seq, nctx, need_ctx, d)

        w_up = ffn_up[i].astype(BF16)
        w_down = ffn_down[i].astype(BF16)
        xl, h2 = _oproj_call(y, wo, xl, norm_ffn[i], mod, tiles_per_seq=lat_tiles, mod_row=None,
                             name=f"oproj{i}")
        xl = _ffn_call(h2, xl, w_up, ffn_conv[i], ffn_conv_b[i], w_down, mod, norm_out,
                       t=T_FFN, tiles_per_seq=seq // T_FFN, mod_row=None, final=not need_ctx,
                       name=f"ffn{i}")
        if need_ctx:
            xc, h2c = _oproj_call(yc, wo, xc, norm_ffn[i], mod, tiles_per_seq=1, mod_row=ctx_row,
                                  name=f"oproj_ctx{i}")
            xc = _ffn_call(h2c, xc, w_up, ffn_conv[i], ffn_conv_b[i], w_down, mod, norm_out,
                           t=nctx, tiles_per_seq=1, mod_row=ctx_row, final=False, name=f"ffn_ctx{i}")
    return xl.reshape(batch, seq, d)
```

```python
import functools
import math

import numpy as np
import jax
import jax.numpy as jnp
from jax import lax
from jax.experimental import pallas as pl
from jax.experimental.pallas import tpu as pltpu

F32 = jnp.float32
BF16 = jnp.bfloat16

GRID_W = 64
HEAD_DIM = 64
ROPE_BASE = 10000.0
ROPE_PAIRS = HEAD_DIM // 4
NORM_EPS = 1e-6
NEG_INF = -1e30
N_MIXERS = 3
NA_KH = 8
NA_KW = 16
SWA_WINDOW = 128
CONV_W = 3
LOG2E = 1.4426950408889634
Q_SCALE = (HEAD_DIM ** -0.5) * LOG2E

LANES = 128
BF16_ROWS = 16
F32_ROWS = 8
VMEM_LIMIT = 52 * 1024 * 1024

T_PROJ = 512
T_FFN = 512
FF_CHUNK = 256
DOWN_PIECES = 4
NA_ROWS = 4
SWA_TQ = 256
DIFF_TQ = 256
DIFF_KC = 256

PLAIN, SCALEQ, ROPEQ, ROPEK = 0, 1, 2, 3


def _cparams(*sem):
    return pltpu.CompilerParams(dimension_semantics=sem, vmem_limit_bytes=VMEM_LIMIT)


def _rms(x):
    return x * lax.rsqrt(jnp.mean(x * x, axis=-1, keepdims=True) + NORM_EPS)


def _ada_kernel(cs_ref, w_ref, b_ref, o_ref):
    cs = cs_ref[...]
    s = cs * (1.0 / (1.0 + jnp.exp(-cs)))
    w = w_ref[...]
    s_hi = s.astype(BF16)
    s_lo = (s - s_hi.astype(F32)).astype(BF16)
    w_hi = w.astype(BF16)
    w_lo = (w - w_hi.astype(F32)).astype(BF16)
    acc = jnp.dot(s_hi, w_hi, preferred_element_type=F32)
    acc = acc + jnp.dot(s_hi, w_lo, preferred_element_type=F32)
    acc = acc + jnp.dot(s_lo, w_hi, preferred_element_type=F32)
    o_ref[...] = acc + b_ref[...]


def _ada_call(cs, ada_w, ada_b):
    depth, d, n = ada_w.shape
    nt = 1536
    rows = cs.shape[0]
    return pl.pallas_call(
        _ada_kernel,
        out_shape=jax.ShapeDtypeStruct((depth, rows, n), F32),
        grid=(depth, n // nt),
        in_specs=[
            pl.BlockSpec((rows, d), lambda l, j: (0, 0)),
            pl.BlockSpec((None, d, nt), lambda l, j: (l, 0, j)),
            pl.BlockSpec((None, 1, nt), lambda l, j: (l, 0, j)),
        ],
        out_specs=pl.BlockSpec((None, rows, nt), lambda l, j: (l, 0, j)),
        compiler_params=_cparams("parallel", "parallel"),
        name="ada_mod",
    )(cs, ada_w, ada_b.reshape(depth, 1, n))


def _rope(r, c, s, first_half):
    partner = jnp.where(first_half, pltpu.roll(r, LANES - 16, 1), pltpu.roll(r, 16, 1))
    return r * c + partner * s


def _proj_kernel(*refs, kinds, nc, has_rope):
    if has_rope:
        x_ref, g_ref, sc_ref, sh_ref, w_ref, cq_ref, sq_ref, ck_ref, sk_ref, o_ref = refs
    else:
        x_ref, g_ref, sc_ref, sh_ref, w_ref, o_ref = refs
    t = x_ref.shape[0]
    nout = o_ref.shape[1]
    h = (_rms(x_ref[...]) * g_ref[...]) * (1.0 + sc_ref[...]) + sh_ref[...]
    hb = h.astype(BF16)
    if has_rope:
        lane = lax.broadcasted_iota(jnp.int32, (t, LANES), 1)
        first_half = (lane & 16) == 0
    for n0 in range(0, nout, nc):
        r = jnp.dot(hb, w_ref[:, n0:n0 + nc], preferred_element_type=F32)
        for j in range(nc // LANES):
            kind = kinds[n0 // LANES + j]
            rj = r[:, j * LANES:(j + 1) * LANES]
            if kind == ROPEQ:
                rj = _rope(rj, cq_ref[...], sq_ref[...], first_half)
            elif kind == ROPEK:
                rj = _rope(rj, ck_ref[...], sk_ref[...], first_half)
            elif kind == SCALEQ:
                rj = rj * Q_SCALE
            o_ref[:, n0 + j * LANES:n0 + (j + 1) * LANES] = rj.astype(BF16)


def _proj_call(x, gamma, mod, w, kinds, tables, *, tiles_per_seq, mod_row, name):
    n, d = x.shape
    nout = w.shape[1]
    t = T_PROJ
    has_rope = tables is not None

    def mrow(i):
        return mod_row if mod_row is not None else i // tiles_per_seq

    def tblock(i):
        return i % tiles_per_seq if mod_row is None else tiles_per_seq

    in_specs = [
        pl.BlockSpec((t, d), lambda i: (i, 0)),
        pl.BlockSpec((1, d), lambda i: (0, 0)),
        pl.BlockSpec((None, None, 1, d), lambda i: (mrow(i), 1, 0, 0)),
        pl.BlockSpec((None, None, 1, d), lambda i: (mrow(i), 0, 0, 0)),
        pl.BlockSpec((d, nout), lambda i: (0, 0)),
    ]
    args = [x, gamma.reshape(1, d), mod, mod, w]
    if has_rope:
        for tab in tables:
            in_specs.append(pl.BlockSpec((t, LANES), lambda i: (tblock(i), 0)))
            args.append(tab)
    return pl.pallas_call(
        functools.partial(_proj_kernel, kinds=tuple(kinds), nc=512, has_rope=has_rope),
        out_shape=jax.ShapeDtypeStruct((n, nout), BF16),
        grid=(n // t,),
        in_specs=in_specs,
        out_specs=pl.BlockSpec((t, nout), lambda i: (i, 0)),
        compiler_params=_cparams("parallel"),
        name=name,
    )(*args)


def _qk(q, k):
    return lax.dot_general(q, k, (((1,), (1,)), ((), ())), preferred_element_type=F32)


def _half_masks(shape):
    lane = lax.broadcasted_iota(jnp.int32, shape, 1)
    return lane < HEAD_DIM


def _split_heads(q, lo):
    qf = q.astype(F32)
    zero = jnp.zeros_like(qf)
    return jnp.where(lo, qf, zero).astype(q.dtype), jnp.where(lo, zero, qf).astype(q.dtype)


def _diff_lambda(lam_ref, lam_init):
    lv = lam_ref[...]
    d1 = jnp.sum(lv[0:1] * lv[1:2], axis=1, keepdims=True)
    d2 = jnp.sum(lv[2:3] * lv[3:4], axis=1, keepdims=True)
    return jnp.exp(d1) - jnp.exp(d2) + lam_init


def _diff_finish(o1, o2, lam, subln_ref, lam_init, o_ref):
    o = _rms(o1 - lam * o2) * subln_ref[...] * (1.0 - lam_init)
    o_ref[...] = o.astype(o_ref.dtype)


def _diff_kernel(lam_ref, subln_ref, q_ref, kl_ref, vl_ref, kc_ref, vc_ref, o_ref,
                 vt, s_scr, p_scr, m_scr, o_scr, *, lam_init, tq, kc):
    seq, nctx = q_ref.shape[0], kc_ref.shape[0]
    nkeys = vt.shape[1]
    nlat = seq // kc
    kcc = min(kc, nctx)
    nchunk = nlat + nctx // kcc
    ntile = seq // tq
    sub = m_scr.shape[1]
    lam = _diff_lambda(lam_ref, lam_init)
    lo = _half_masks((tq, LANES))

    def rows(c):
        if c < nlat:
            return slice(c * kc, (c + 1) * kc)
        return slice(seq + (c - nlat) * kcc, seq + (c - nlat + 1) * kcc)

    def chunk_of(lat_ref, ctx_ref, c):
        if c < nlat:
            return lat_ref[rows(c), :]
        return ctx_ref[(c - nlat) * kcc:(c - nlat + 1) * kcc, :]

    eye = (lax.broadcasted_iota(jnp.int32, (LANES, LANES), 0)
           == lax.broadcasted_iota(jnp.int32, (LANES, LANES), 1)).astype(F32).astype(BF16)
    for c in range(nchunk):
        vt[0:LANES, rows(c)] = _qk(eye, chunk_of(vl_ref, vc_ref, c)).astype(BF16)
    vt[LANES:, :] = jnp.ones((vt.shape[0] - LANES, nkeys), BF16)

    def half(m, tile, do_qk, do_exp, do_pv):
        if do_qk:
            q = q_ref[pl.ds(pl.multiple_of(tile * tq, tq), tq), :].astype(F32)
            zero = jnp.zeros_like(q)
            qm = (jnp.where(lo, q, zero) if m == 0 else jnp.where(lo, zero, q)).astype(BF16)
            mrun = jnp.full((sub, tq), NEG_INF, F32)
        if do_exp:
            mb = m_scr[1 - m, 0:1, :]
        if do_pv:
            acc = jnp.zeros((vt.shape[0], tq), F32)
        for c in range(nchunk):
            if do_qk:
                s = _qk(chunk_of(kl_ref, kc_ref, c), qm)
                s_scr[m, rows(c), :] = s
                for u in range(s.shape[0] // sub):
                    mrun = jnp.maximum(mrun, s[u * sub:(u + 1) * sub, :])
            if do_pv:
                acc = acc + jnp.dot(vt[:, rows(c)], p_scr[1 - m, rows(c), :], preferred_element_type=F32)
            if do_exp:
                p_scr[1 - m, rows(c), :] = jnp.exp2(s_scr[1 - m, rows(c), :] - mb).astype(BF16)
        if do_qk:
            m_scr[m] = jnp.broadcast_to(jnp.max(mrun, axis=0, keepdims=True), (sub, tq))
        if do_pv:
            return acc[:LANES] * (1.0 / acc[LANES:LANES + 1])
        return None

    def emit(tile, o2):
        o = _rms((o_scr[...] - lam * o2).T) * subln_ref[...] * (1.0 - lam_init)
        o_ref[pl.ds(pl.multiple_of(tile * tq, tq), tq), :] = o.astype(o_ref.dtype)

    half(0, 0, True, False, False)
    half(1, 0, True, True, False)
    half(0, 1, True, True, False)
    o_scr[...] = half(1, 1, True, True, True)

    def body(t, carry):
        emit(t - 2, half(0, t, True, True, True))
        o_scr[...] = half(1, t, True, True, True)
        return carry

    lax.fori_loop(2, ntile, body, 0)
    emit(ntile - 2, half(0, ntile, False, True, True))
    o_scr[...] = half(1, ntile, False, False, True)
    emit(ntile - 1, half(0, ntile + 1, False, False, True))


def _diff_ctx_kernel(lam_ref, subln_ref, q_ref, kc_ref, vc_ref, o_ref, *, lam_init):
    tq = q_ref.shape[0]
    lam = _diff_lambda(lam_ref, lam_init)
    lo = _half_masks((tq, LANES))
    for h in range(q_ref.shape[1] // LANES):
        cols = slice(h * LANES, (h + 1) * LANES)
        qs = _split_heads(q_ref[:, cols], lo)
        o1, o2 = [_softmax_pv([_qk(qm, kc_ref[:, cols])], [vc_ref[:, cols]]) for qm in qs]
        _diff_finish(o1, o2, lam, subln_ref, lam_init, o_ref.at[:, cols])


def _diff_attention(qkv, qkvc, lam_vecs, subln, lam_init, batch, seq, nctx, need_ctx):
    d = qkv.shape[1] // 3
    nh = d // LANES
    tq = DIFF_TQ
    kc = DIFF_KC
    assert seq % tq == 0 and seq % kc == 0 and nctx % min(kc, nctx) == 0 and seq // tq >= 2
    nkeys = seq + nctx
    subln2 = subln.reshape(1, LANES)
    y = pl.pallas_call(
        functools.partial(_diff_kernel, lam_init=lam_init, tq=tq, kc=kc),
        out_shape=jax.ShapeDtypeStruct((batch * seq, d), BF16),
        grid=(batch, nh),
        in_specs=[
            pl.BlockSpec((4, HEAD_DIM), lambda b, h: (0, 0)),
            pl.BlockSpec((1, LANES), lambda b, h: (0, 0)),
            pl.BlockSpec((seq, LANES), lambda b, h: (b, h)),
            pl.BlockSpec((seq, LANES), lambda b, h: (b, nh + h)),
            pl.BlockSpec((seq, LANES), lambda b, h: (b, 2 * nh + h)),
            pl.BlockSpec((nctx, LANES), lambda b, h: (b, nh + h)),
            pl.BlockSpec((nctx, LANES), lambda b, h: (b, 2 * nh + h)),
        ],
        out_specs=pl.BlockSpec((seq, LANES), lambda b, h: (b, h)),
        scratch_shapes=[pltpu.VMEM((LANES + BF16_ROWS, nkeys), BF16), pltpu.VMEM((2, nkeys, tq), F32),
                        pltpu.VMEM((2, nkeys, tq), BF16), pltpu.VMEM((2, F32_ROWS, tq), F32),
                        pltpu.VMEM((LANES, tq), F32)],
        compiler_params=_cparams("parallel", "parallel"),
        name="diff_attn",
    )(lam_vecs, subln2, qkv, qkv, qkv, qkvc, qkvc)
    yc = None
    if need_ctx:
        yc = pl.pallas_call(
            functools.partial(_diff_ctx_kernel, lam_init=lam_init),
            out_shape=jax.ShapeDtypeStruct((batch * nctx, d), BF16),
            grid=(batch,),
            in_specs=[
                pl.BlockSpec((4, HEAD_DIM), lambda b: (0, 0)),
                pl.BlockSpec((1, LANES), lambda b: (0, 0)),
                pl.BlockSpec((nctx, d), lambda b: (b, 0)),
                pl.BlockSpec((nctx, d), lambda b: (b, 1)),
                pl.BlockSpec((nctx, d), lambda b: (b, 2)),
            ],
            out_specs=pl.BlockSpec((nctx, d), lambda b: (b, 0)),
            compiler_params=_cparams("parallel"),
            name="diff_attn_ctx",
        )(lam_vecs, subln2, qkvc, qkvc, qkvc)
    return y, yc


def _softmax_pv(scores, values, extra_logit=None):
    m = jnp.max(scores[0], axis=1, keepdims=True)
    for s in scores[1:]:
        m = jnp.maximum(m, jnp.max(s, axis=1, keepdims=True))
    if extra_logit is not None:
        m = jnp.maximum(m, extra_logit)
    l = None
    acc = None
    for s, v in zip(scores, values):
        p = jnp.exp2(s - m)
        ps = jnp.sum(p, axis=1, keepdims=True)
        pv = jnp.dot(p.astype(BF16), v, preferred_element_type=F32)
        l = ps if l is None else l + ps
        acc = pv if acc is None else acc + pv
    if extra_logit is not None:
        l = l + jnp.exp2(extra_logit - m)
    return acc * (1.0 / l)


def _identity(n):
    return (lax.broadcasted_iota(jnp.int32, (n, n), 0)
            == lax.broadcasted_iota(jnp.int32, (n, n), 1)).astype(F32).astype(BF16)


def _fill_vt(vt, head0, chunk, v_block, n_heads):
    tr = _qk(_identity(LANES), v_block).astype(BF16)
    for hh in range(n_heads):
        vt[head0 + hh, chunk, 0:HEAD_DIM, :] = tr[hh * HEAD_DIM:(hh + 1) * HEAD_DIM]
        vt[head0 + hh, chunk, HEAD_DIM:, :] = jnp.ones((vt.shape[2] - HEAD_DIM, vt.shape[3]), BF16)


def _window_pipeline(q_ref, o_ref, s_scr, p_scr, m_scr, e_scr, o_scr, *, tq, kc, nchunk,
                     scores, values, extra_logit):
    ntile = q_ref.shape[0] // tq
    sub = m_scr.shape[1]
    lo = _half_masks((tq, LANES))
    rows = lambda c: slice(c * kc, (c + 1) * kc)
    tile_rows = lambda t: pl.ds(pl.multiple_of(t * tq, tq), tq)

    def half(j, t, do_qk, do_exp, do_pv):
        jo = 1 - j
        tv = t - 2 + j
        if do_qk:
            q = q_ref[tile_rows(t), :].astype(F32)
            zero = jnp.zeros_like(q)
            qm = (jnp.where(lo, q, zero) if j == 0 else jnp.where(lo, zero, q)).astype(BF16)
            mrun = jnp.full((sub, tq), NEG_INF, F32)
        if do_exp:
            mb = m_scr[jo, 0:1, :]
        if do_pv:
            acc = jnp.zeros((o_scr.shape[0] + BF16_ROWS, tq), F32)
        for c in range(nchunk):
            if do_qk:
                s = scores(qm, t, j, c)
                s_scr[j, rows(c), :] = s
                for u in range(kc // sub):
                    mrun = jnp.maximum(mrun, s[u * sub:(u + 1) * sub, :])
            if do_pv:
                acc = acc + jnp.dot(values(tv, jo, c), p_scr[jo, rows(c), :], preferred_element_type=F32)
            if do_exp:
                p_scr[jo, rows(c), :] = jnp.exp2(s_scr[jo, rows(c), :] - mb).astype(BF16)
        if do_qk:
            m = jnp.max(mrun, axis=0, keepdims=True)
            if extra_logit is not None:
                m = jnp.maximum(m, extra_logit(j))
            m_scr[j] = jnp.broadcast_to(m, (sub, tq))
        o = None
        if do_pv:
            l = acc[HEAD_DIM:HEAD_DIM + 1]
            if extra_logit is not None:
                l = l + e_scr[jo, 0:1, :]
            o = acc[:HEAD_DIM] * (1.0 / l)
        if do_exp and extra_logit is not None:
            e_scr[jo] = jnp.exp2(extra_logit(jo) - m_scr[jo])
        return o

    def emit(t, o1):
        o = jnp.concatenate([o_scr[...], o1], axis=0).T
        o_ref[tile_rows(t), :] = o.astype(o_ref.dtype)

    half(0, 0, True, False, False)
    half(1, 0, True, True, False)
    half(0, 1, True, True, False)
    o_scr[...] = half(1, 1, True, True, True)

    for t in range(2, ntile):
        emit(t - 2, half(0, t, True, True, True))
        o_scr[...] = half(1, t, True, True, True)
    emit(ntile - 2, half(0, ntile, False, True, True))
    o_scr[...] = half(1, ntile, False, False, True)
    emit(ntile - 1, half(0, ntile + 1, False, False, True))


def _tile_class(t, ntile):
    return jnp.where(t == 0, 0, jnp.where(t == ntile - 1, 2, 1))


def _na_kernel(bias_ref, q_ref, k_ref, v_ref, kc_ref, vc_ref, o_ref, vt, s_scr, p_scr, m_scr, o_scr,
               *, tq, kc):
    seq, nctx = q_ref.shape[0], kc_ref.shape[0]
    ntile, nlat, nwin = seq // tq, seq // kc, 3 * tq // kc
    rows = lambda c: slice(c * kc, (c + 1) * kc)
    for c in range(nlat):
        _fill_vt(vt, 0, c, v_ref[rows(c), :], 2)
    for c in range(nctx // kc):
        _fill_vt(vt, 0, nlat + c, vc_ref[rows(c), :], 2)

    def win_start(t):
        return pl.multiple_of(jnp.clip((t - 1) * tq, 0, seq - 3 * tq), tq)

    def scores(qm, t, j, c):
        if c >= nwin:
            return _qk(kc_ref[rows(c - nwin), :], qm)
        k = k_ref[pl.ds(win_start(t) + c * kc, kc), :]
        return _qk(k, qm) + bias_ref[_tile_class(t, ntile), j, rows(c), :]

    def values(t, j, c):
        return vt[j, nlat + c - nwin] if c >= nwin else vt[j, win_start(t) // kc + c]

    _window_pipeline(q_ref, o_ref, s_scr, p_scr, m_scr, None, o_scr, tq=tq, kc=kc,
                     nchunk=nwin + nctx // kc, scores=scores, values=values, extra_logit=None)


def _plain_ctx_kernel(q_ref, kc_ref, vc_ref, o_ref):
    tq = q_ref.shape[0]
    lo = _half_masks((tq, LANES))
    for hb in range(q_ref.shape[1] // LANES):
        cols = slice(hb * LANES, (hb + 1) * LANES)
        qs = _split_heads(q_ref[:, cols], lo)
        outs = [_softmax_pv([_qk(qs[hh], kc_ref[:, cols])], [vc_ref[:, cols]]) for hh in range(2)]
        o_ref[:, cols] = jnp.where(lo, outs[0], outs[1]).astype(o_ref.dtype)


def _na_bias_tables(rpb, rows):
    w = GRID_W
    nk = 3 * NA_ROWS
    nh = rpb.shape[0]
    edge = w - NA_KW
    ext = jnp.concatenate([jnp.repeat(rpb[..., :1], edge, axis=-1), rpb,
                           jnp.repeat(rpb[..., -1:], edge + 1, axis=-1)], axis=-1)
    flat = jnp.tile(ext, (1, 1, w + 1))[..., :w * (2 * w + 1)]
    hankel = flat.reshape(nh, -1, w, 2 * w + 1)[..., :w]
    toep = jnp.flip(hankel, axis=-2)
    cq = np.arange(w)[:, None]
    cx = np.arange(w)[None, :]
    col_start = np.clip(cq - NA_KW // 2, 0, w - NA_KW)
    col_ok = (cx >= col_start) & (cx < col_start + NA_KW)
    toep = jnp.where(col_ok, toep * LOG2E, NEG_INF)
    masked = jnp.full((nh, w, w), NEG_INF, F32)
    tabs = []
    for r0, k0 in ((0, 0), (NA_ROWS, 0), (rows - NA_ROWS, rows - nk)):
        row_blocks = []
        for a in range(NA_ROWS):
            r = r0 + a
            rs = min(max(r - NA_KH // 2, 0), rows - NA_KH)
            blocks = []
            for e in range(nk):
                ry = k0 + e
                blocks.append(toep[:, ry - r + NA_KH - 1] if rs <= ry < rs + NA_KH else masked)
            row_blocks.append(jnp.concatenate(blocks, axis=-1))
        tabs.append(jnp.concatenate(row_blocks, axis=-2))
    return jnp.swapaxes(jnp.stack(tabs), -1, -2).astype(F32)


def _win_scratch(nheads, nchunks, nkeys_item, tq, kc, with_extra):
    shapes = [pltpu.VMEM((nheads, nchunks, HEAD_DIM + BF16_ROWS, kc), BF16),
              pltpu.VMEM((2, nkeys_item, tq), F32), pltpu.VMEM((2, nkeys_item, tq), BF16),
              pltpu.VMEM((2, F32_ROWS, tq), F32)]
    if with_extra:
        shapes.append(pltpu.VMEM((2, F32_ROWS, tq), F32))
    shapes.append(pltpu.VMEM((HEAD_DIM, tq), F32))
    return shapes


def _na_attention(qkv, qkvc, rpb, batch, seq, nctx, need_ctx):
    d = qkv.shape[1] // 3
    nb = d // LANES
    rows = seq // GRID_W
    tq = NA_ROWS * GRID_W
    kc = tq
    assert seq // tq >= 3 and nctx % kc == 0
    bias = _na_bias_tables(rpb.astype(F32), rows)
    y = pl.pallas_call(
        functools.partial(_na_kernel, tq=tq, kc=kc),
        out_shape=jax.ShapeDtypeStruct((batch * seq, d), BF16),
        grid=(nb, batch),
        in_specs=[
            pl.BlockSpec((3, 2, 3 * tq, tq), lambda hb, b: (0, hb, 0, 0), pipeline_mode=pl.Buffered(1)),
            pl.BlockSpec((seq, LANES), lambda hb, b: (b, hb)),
            pl.BlockSpec((seq, LANES), lambda hb, b: (b, nb + hb)),
            pl.BlockSpec((seq, LANES), lambda hb, b: (b, 2 * nb + hb)),
            pl.BlockSpec((nctx, LANES), lambda hb, b: (b, nb + hb)),
            pl.BlockSpec((nctx, LANES), lambda hb, b: (b, 2 * nb + hb)),
        ],
        out_specs=pl.BlockSpec((seq, LANES), lambda hb, b: (b, hb)),
        scratch_shapes=_win_scratch(2, (seq + nctx) // kc, 3 * tq + nctx, tq, kc, False),
        compiler_params=_cparams("parallel", "arbitrary"),
        name="na_attn",
    )(bias, qkv, qkv, qkv, qkvc, qkvc)
    yc = None
    if need_ctx:
        yc = pl.pallas_call(
            _plain_ctx_kernel,
            out_shape=jax.ShapeDtypeStruct((batch * nctx, d), BF16),
            grid=(batch,),
            in_specs=[
                pl.BlockSpec((nctx, d), lambda b: (b, 0)),
                pl.BlockSpec((nctx, d), lambda b: (b, 1)),
                pl.BlockSpec((nctx, d), lambda b: (b, 2)),
            ],
            out_specs=pl.BlockSpec((nctx, d), lambda b: (b, 0)),
            compiler_params=_cparams("parallel"),
            name="na_attn_ctx",
        )(qkvc, qkvc, qkvc)
    return y, yc


def _swa_heads(q_ref, sinks_ref, g, score_fn, values, o_ref):
    tq = q_ref.shape[0]
    lo = _half_masks((tq, LANES))
    for jb in range(2):
        qs = _split_heads(q_ref[:, jb * LANES:(jb + 1) * LANES], lo)
        outs = []
        for hh in range(2):
            sink = jnp.full((1, 1), sinks_ref[g * 4 + jb * 2 + hh] * LOG2E, F32)
            outs.append(_softmax_pv(score_fn(qs[hh]), values, extra_logit=sink))
        o_ref[:, jb * LANES:(jb + 1) * LANES] = jnp.where(lo, outs[0], outs[1]).astype(o_ref.dtype)


def _swa_kernel(sinks_ref, band_ref, q_ref, k_ref, v_ref, kc_ref, vc_ref, o_ref,
                vt, s_scr, p_scr, m_scr, e_scr, o_scr, *, tq, kc, span):
    head0 = (pl.program_id(1) * 2 + pl.program_id(2)) * 2
    seq, nctx = q_ref.shape[0], kc_ref.shape[0]
    ntile, nlat, nwin = seq // tq, seq // kc, span // kc
    rows = lambda c: slice(c * kc, (c + 1) * kc)
    for c in range(nlat):
        _fill_vt(vt, 0, c, v_ref[rows(c), :], 1)
    for c in range(nctx // kc):
        _fill_vt(vt, 0, nlat + c, vc_ref[rows(c), :], 1)

    def win_start(t):
        return pl.multiple_of(jnp.clip(t * tq - SWA_WINDOW, 0, seq - span), SWA_WINDOW)

    def scores(qm, t, j, c):
        if c >= nwin:
            return _qk(kc_ref[rows(c - nwin), :], qm)
        k = k_ref[pl.ds(win_start(t) + c * kc, kc), :]
        return _qk(k, qm) + band_ref[_tile_class(t, ntile), rows(c), :]

    def values(t, j, c):
        return vt[0, nlat + c - nwin] if c >= nwin else vt[0, win_start(t) // kc + c]

    def sink(j):
        return jnp.full((1, 1), sinks_ref[head0 + j] * LOG2E, F32)

    _window_pipeline(q_ref, o_ref, s_scr, p_scr, m_scr, e_scr, o_scr, tq=tq, kc=kc,
                     nchunk=nwin + nctx // kc, scores=scores, values=values, extra_logit=sink)


def _swa_band_tables(seq, tq, span):
    ntile = seq // tq
    tabs = []
    for t in (0, 1, ntile - 1):
        start = min(max(t * tq - SWA_WINDOW, 0), seq - span)
        kpos = start + np.arange(span)[:, None]
        qpos = t * tq + np.arange(tq)[None, :]
        tabs.append(np.where(np.abs(qpos - kpos) <= SWA_WINDOW, 0.0, NEG_INF))
    return jnp.asarray(np.stack(tabs), F32)


def _swa_ctx_kernel(sinks_ref, q_ref, kc_ref, vc_ref, o_ref):
    qb = 2 * LANES
    for g in range(kc_ref.shape[1] // LANES):
        kc = kc_ref[:, g * LANES:(g + 1) * LANES]
        vc = vc_ref[:, g * LANES:(g + 1) * LANES]
        cols = slice(g * qb, (g + 1) * qb)
        _swa_heads(q_ref.at[:, cols], sinks_ref, g, lambda qm, kc=kc: [_qk(qm, kc)], [vc], o_ref.at[:, cols])


def _swa_attention(qkv, qkvc, sinks, batch, seq, nctx, need_ctx, d):
    nkv = (qkv.shape[1] - d) // (2 * LANES)
    qb = d // nkv
    kcol = d // LANES
    tq = SWA_TQ
    kc = SWA_WINDOW
    span = tq + 2 * SWA_WINDOW
    assert qb == 2 * LANES and seq // tq >= 3 and nctx % kc == 0
    smem = pl.BlockSpec(memory_space=pltpu.SMEM)
    sinks = sinks.astype(F32)
    y = pl.pallas_call(
        functools.partial(_swa_kernel, tq=tq, kc=kc, span=span),
        out_shape=jax.ShapeDtypeStruct((batch * seq, d), BF16),
        grid=(batch, nkv, 2),
        in_specs=[
            smem,
            pl.BlockSpec((3, span, tq), lambda b, g, jb: (0, 0, 0), pipeline_mode=pl.Buffered(1)),
            pl.BlockSpec((seq, LANES), lambda b, g, jb: (b, 2 * g + jb)),
            pl.BlockSpec((seq, LANES), lambda b, g, jb: (b, kcol + g)),
            pl.BlockSpec((seq, LANES), lambda b, g, jb: (b, kcol + nkv + g)),
            pl.BlockSpec((nctx, LANES), lambda b, g, jb: (b, kcol + g)),
            pl.BlockSpec((nctx, LANES), lambda b, g, jb: (b, kcol + nkv + g)),
        ],
        out_specs=pl.BlockSpec((seq, LANES), lambda b, g, jb: (b, 2 * g + jb)),
        scratch_shapes=_win_scratch(1, (seq + nctx) // kc, span + nctx, tq, kc, True),
        compiler_params=_cparams("parallel", "parallel", "arbitrary"),
        name="swa_attn",
    )(sinks, _swa_band_tables(seq, tq, span), qkv, qkv, qkv, qkvc, qkvc)
    yc = None
    if need_ctx:
        yc = pl.pallas_call(
            _swa_ctx_kernel,
            out_shape=jax.ShapeDtypeStruct((batch * nctx, d), BF16),
            grid=(batch,),
            in_specs=[
                smem,
                pl.BlockSpec((nctx, d), lambda b: (b, 0)),
                pl.BlockSpec((nctx, nkv * LANES), lambda b: (b, d // (nkv * LANES))),
                pl.BlockSpec((nctx, nkv * LANES), lambda b: (b, d // (nkv * LANES) + 1)),
            ],
            out_specs=pl.BlockSpec((nctx, d), lambda b: (b, 0)),
            compiler_params=_cparams("parallel"),
            name="swa_attn_ctx",
        )(sinks, qkvc, qkvc, qkvc)
    return y, yc


def _oproj_kernel(o_ref, wo_ref, x_ref, g1_ref, nf_ref, sc2_ref, sh2_ref, xo_ref, h2_ref):
    y = jnp.dot(o_ref[...], wo_ref[...], preferred_element_type=F32)
    xn = x_ref[...] + g1_ref[...] * y
    xo_ref[...] = xn
    h2 = (_rms(xn) * nf_ref[...]) * (1.0 + sc2_ref[...]) + sh2_ref[...]
    h2_ref[...] = h2.astype(BF16)


def _oproj_call(o, wo, x, gamma, mod, *, tiles_per_seq, mod_row, name):
    n, d = x.shape
    t = T_PROJ

    def mrow(i):
        return mod_row if mod_row is not None else i // tiles_per_seq

    def mspec(which):
        return pl.BlockSpec((None, None, 1, d), lambda i: (mrow(i), which, 0, 0))

    return pl.pallas_call(
        _oproj_kernel,
        out_shape=(jax.ShapeDtypeStruct((n, d), F32), jax.ShapeDtypeStruct((n, d), BF16)),
        grid=(n // t,),
        in_specs=[
            pl.BlockSpec((t, d), lambda i: (i, 0)),
            pl.BlockSpec((d, d), lambda i: (0, 0)),
            pl.BlockSpec((t, d), lambda i: (i, 0)),
            mspec(2),
            pl.BlockSpec((1, d), lambda i: (0, 0)),
            mspec(4),
            mspec(3),
        ],
        out_specs=(pl.BlockSpec((t, d), lambda i: (i, 0)), pl.BlockSpec((t, d), lambda i: (i, 0))),
        compiler_params=_cparams("parallel"),
        name=name,
    )(o, wo, x, mod, gamma.reshape(1, d), mod, mod)


def _ffn_kernel(h_ref, hp_ref, hn_ref, wup_ref, cw_ref, cb_ref, wd_ref, x_ref, g2_ref, no_ref, o_ref,
                lhs, act, ubuf, *, nseq, final, f):
    i = pl.program_id(0)
    t = h_ref.shape[0]
    halo = hp_ref.shape[0]
    dff = wd_ref.shape[0]
    rows = t + 2 * halo

    keep_prev = ((i % nseq) != 0).astype(F32)
    keep_next = ((i % nseq) != nseq - 1).astype(F32)
    lhs[0:halo, :] = (hp_ref[...].astype(F32) * keep_prev).astype(BF16)
    lhs[halo:halo + t, :] = h_ref[...]
    lhs[halo + t:, :] = (hn_ref[...].astype(F32) * keep_next).astype(BF16)

    def conv(col0, slot):
        u = jnp.dot(lhs[...], wup_ref[:, col0:col0 + f], preferred_element_type=F32)
        outs = []
        for s in range(f // LANES):
            sl = slice(col0 + s * LANES, col0 + (s + 1) * LANES)
            buf = ubuf.at[slot, s]
            buf[pl.ds(0, rows, stride=2), :] = u[:, s * LANES:(s + 1) * LANES]
            taps = [buf[pl.ds(2 * (halo - 1 + k), t, stride=2), :] for k in range(CONV_W)]
            outs.append(cb_ref[:, sl] + taps[0] * cw_ref[0:1, sl] + taps[1] * cw_ref[1:2, sl]
                        + taps[2] * cw_ref[2:3, sl])
        return jnp.concatenate(outs, axis=1)

    nch = dff // f
    cuts = [0] + [(nch * k) // DOWN_PIECES for k in range(1, DOWN_PIECES + 1)]
    y = None
    for c in range(nch):
        slot = 2 * (c % 2)
        a = conv(c * f, slot)
        g = conv(dff + c * f, slot + 1)
        act[:, c * f:(c + 1) * f] = ((g * (1.0 / (1.0 + jnp.exp(-g)))) * a).astype(BF16)
        if c + 1 in cuts:
            lo_c = cuts[cuts.index(c + 1) - 1] * f
            part = jnp.dot(act[:, lo_c:(c + 1) * f], wd_ref[lo_c:(c + 1) * f, :], preferred_element_type=F32)
            y = part if y is None else y + part

    xn = x_ref[...] + g2_ref[...] * y
    if final:
        xn = _rms(xn) * no_ref[...]
    o_ref[...] = xn


def _ffn_call(h2, x, w_up, w_conv, b_conv, w_down, mod, norm_out, *, t, tiles_per_seq, mod_row, final, name):
    n, d = x.shape
    dff = w_down.shape[0]
    f = FF_CHUNK
    halo = BF16_ROWS
    hb = t // halo
    last_hblock = n // halo - 1
    resident = pl.Buffered(1)

    def mrow(i):
        return mod_row if mod_row is not None else i // tiles_per_seq

    return pl.pallas_call(
        functools.partial(_ffn_kernel, nseq=tiles_per_seq, final=final, f=f),
        out_shape=jax.ShapeDtypeStruct((n, d), F32),
        grid=(n // t,),
        in_specs=[
            pl.BlockSpec((t, d), lambda i: (i, 0)),
            pl.BlockSpec((halo, d), lambda i: (jnp.maximum(i * hb - 1, 0), 0)),
            pl.BlockSpec((halo, d), lambda i: (jnp.minimum((i + 1) * hb, last_hblock), 0)),
            pl.BlockSpec((d, 2 * dff), lambda i: (0, 0), pipeline_mode=resident),
            pl.BlockSpec((CONV_W, 2 * dff), lambda i: (0, 0), pipeline_mode=resident),
            pl.BlockSpec((1, 2 * dff), lambda i: (0, 0), pipeline_mode=resident),
            pl.BlockSpec((dff, d), lambda i: (0, 0), pipeline_mode=resident),
            pl.BlockSpec((t, d), lambda i: (i, 0)),
            pl.BlockSpec((None, None, 1, d), lambda i: (mrow(i), 5, 0, 0)),
            pl.BlockSpec((1, d), lambda i: (0, 0)),
        ],
        out_specs=pl.BlockSpec((t, d), lambda i: (i, 0)),
        scratch_shapes=[pltpu.VMEM((t + 2 * halo, d), BF16), pltpu.VMEM((t, dff), BF16),
                        pltpu.VMEM((4, f // LANES, 2 * (t + 2 * halo), LANES), F32)],
        compiler_params=_cparams("parallel"),
        name=name,
    )(h2, h2, h2, w_up, w_conv, b_conv.reshape(1, 2 * dff), w_down, x, mod, norm_out.reshape(1, d))


def _rope_tables(seq, extra_rows):
    tpos = jnp.arange(seq, dtype=jnp.int32)
    row = (tpos // GRID_W).astype(F32)
    col = (tpos % GRID_W).astype(F32)
    inv = ROPE_BASE ** (-jnp.arange(ROPE_PAIRS, dtype=F32) / ROPE_PAIRS)
    ar = row[:, None] * inv[None, :]
    ac = col[:, None] * inv[None, :]
    cr, sr, cc, sc = jnp.cos(ar), jnp.sin(ar), jnp.cos(ac), jnp.sin(ac)
    cos = jnp.concatenate([cr, cr, cc, cc] * (LANES // HEAD_DIM), axis=1)
    sin = jnp.concatenate([-sr, sr, -sc, sc] * (LANES // HEAD_DIM), axis=1)
    cos = jnp.concatenate([cos, jnp.ones((extra_rows, LANES), F32)], axis=0)
    sin = jnp.concatenate([sin, jnp.zeros((extra_rows, LANES), F32)], axis=0)
    return cos * Q_SCALE, sin * Q_SCALE, cos, sin


def _dup_kv_columns(w, d, nkv):
    q = w[:, :d]
    k = w[:, d:d + nkv * HEAD_DIM].reshape(-1, nkv, 1, HEAD_DIM)
    v = w[:, d + nkv * HEAD_DIM:].reshape(-1, nkv, 1, HEAD_DIM)
    kd = jnp.broadcast_to(k, (w.shape[0], nkv, 2, HEAD_DIM)).reshape(w.shape[0], -1)
    vd = jnp.broadcast_to(v, (w.shape[0], nkv, 2, HEAD_DIM)).reshape(w.shape[0], -1)
    return jnp.concatenate([q, kd, vd], axis=1)


def kernel(x, c, ctx, c_ctx, ada_w, ada_b, norm_mix, norm_ffn, norm_out, ffn_up, ffn_conv, ffn_conv_b,
           ffn_down, a_wqkv, a_wo, a_lambda, a_subln, b_wqkv, b_wo, b_rpb, c_wqkv, c_wo, c_sinks):
    batch, seq, d = x.shape
    nctx = ctx.shape[1]
    depth = ada_w.shape[0]
    assert seq % T_FFN == 0 and seq % T_PROJ == 0 and (batch * nctx) % T_PROJ == 0

    mod_rows = 16
    cs = jnp.concatenate([c, c_ctx[None, :], jnp.zeros((mod_rows - batch - 1, d), F32)], axis=0)
    mod_all = _ada_call(cs, ada_w, ada_b).reshape(depth, mod_rows, 6, 1, d)
    ctx_row = batch

    tables = _rope_tables(seq, T_PROJ)
    lat_tiles = seq // T_PROJ
    nd = d // LANES

    xl = x.reshape(batch * seq, d)
    xc = ctx.reshape(batch * nctx, d)
    for i in range(depth):
        need_ctx = i < depth - 1
        kind, j = i % N_MIXERS, i // N_MIXERS
        mod = mod_all[i]
        if kind == 0:
            w = a_wqkv[j].astype(BF16)
            wo = a_wo[j].astype(BF16)
            kinds = [ROPEQ] * nd + [ROPEK] * nd + [PLAIN] * nd
            tabs = tables
        elif kind == 1:
            w = b_wqkv[j].astype(BF16)
            wo = b_wo[j].astype(BF16)
            kinds = [SCALEQ] * nd + [PLAIN] * (2 * nd)
            tabs = None
        else:
            nkv = (c_wqkv.shape[2] - d) // (2 * HEAD_DIM)
            w = _dup_kv_columns(c_wqkv[j], d, nkv).astype(BF16)
            wo = c_wo[j].astype(BF16)
            kinds = [ROPEQ] * nd + [ROPEK] * nkv + [PLAIN] * nkv
            tabs = tables
        qkv = _proj_call(xl, norm_mix[i], mod, w, kinds, tabs, tiles_per_seq=lat_tiles, mod_row=None,
                         name=f"proj{i}")
        qkvc = _proj_call(xc, norm_mix[i], mod, w, kinds, tabs, tiles_per_seq=lat_tiles, mod_row=ctx_row,
                          name=f"proj_ctx{i}")
        if kind == 0:
            lam_init = 0.8 - 0.6 * math.exp(-0.3 * i)
            y, yc = _diff_attention(qkv, qkvc, a_lambda[j].astype(F32), a_subln[j], lam_init,
                                    batch, seq, nctx, need_ctx)
        elif kind == 1:
            y, yc = _na_attention(qkv, qkvc, b_rpb[j], batch, seq, nctx, need_ctx)
        else:
            y, yc = _swa_attention(qkv, qkvc, c_sinks[j], batch, seq, nctx, need_ctx, d)

        w_up = ffn_up[i].astype(BF16)
        w_down = ffn_down[i].astype(BF16)
        xl, h2 = _oproj_call(y, wo, xl, norm_ffn[i], mod, tiles_per_seq=lat_tiles, mod_row=None,
                             name=f"oproj{i}")
        xl = _ffn_call(h2, xl, w_up, ffn_conv[i], ffn_conv_b[i], w_down, mod, norm_out,
                       t=T_FFN, tiles_per_seq=seq // T_FFN, mod_row=None, final=not need_ctx,
                       name=f"ffn{i}")
        if need_ctx:
            xc, h2c = _oproj_call(yc, wo, xc, norm_ffn[i], mod, tiles_per_seq=1, mod_row=ctx_row,
                                  name=f"oproj_ctx{i}")
            xc = _ffn_call(h2c, xc, w_up, ffn_conv[i], ffn_conv_b[i], w_down, mod, norm_out,
                           t=nctx, tiles_per_seq=1, mod_row=ctx_row, final=False, name=f"ffn_ctx{i}")
    return xl.reshape(batch, seq, d)
```

```python
import functools
import math

import numpy as np
import jax
import jax.numpy as jnp
from jax import lax
from jax.experimental import pallas as pl
from jax.experimental.pallas import tpu as pltpu

F32 = jnp.float32
BF16 = jnp.bfloat16

GRID_W = 64
HEAD_DIM = 64
ROPE_BASE = 10000.0
ROPE_PAIRS = HEAD_DIM // 4
NORM_EPS = 1e-6
NEG_INF = -1e30
N_MIXERS = 3
NA_KH = 8
NA_KW = 16
SWA_WINDOW = 128
CONV_W = 3
LOG2E = 1.4426950408889634
Q_SCALE = (HEAD_DIM ** -0.5) * LOG2E

LANES = 128
BF16_ROWS = 16
F32_ROWS = 8
VMEM_LIMIT = 52 * 1024 * 1024

T_PROJ = 512
T_FFN = 512
FF_CHUNK = 256
DOWN_PIECES = 6
NA_ROWS = 4
SWA_TQ = 256
DIFF_TQ = 256
DIFF_KC = 256

PLAIN, SCALEQ, ROPEQ, ROPEK = 0, 1, 2, 3


def _cparams(*sem):
    return pltpu.CompilerParams(dimension_semantics=sem, vmem_limit_bytes=VMEM_LIMIT)


def _rms(x):
    return x * lax.rsqrt(jnp.mean(x * x, axis=-1, keepdims=True) + NORM_EPS)


def _ada_kernel(cs_ref, w_ref, b_ref, o_ref):
    cs = cs_ref[...]
    s = cs * (1.0 / (1.0 + jnp.exp(-cs)))
    w = w_ref[...]
    s_hi = s.astype(BF16)
    s_lo = (s - s_hi.astype(F32)).astype(BF16)
    w_hi = w.astype(BF16)
    w_lo = (w - w_hi.astype(F32)).astype(BF16)
    acc = jnp.dot(s_hi, w_hi, preferred_element_type=F32)
    acc = acc + jnp.dot(s_hi, w_lo, preferred_element_type=F32)
    acc = acc + jnp.dot(s_lo, w_hi, preferred_element_type=F32)
    o_ref[...] = acc + b_ref[...]


def _ada_call(cs, ada_w, ada_b):
    depth, d, n = ada_w.shape
    nt = 1536
    rows = cs.shape[0]
    return pl.pallas_call(
        _ada_kernel,
        out_shape=jax.ShapeDtypeStruct((depth, rows, n), F32),
        grid=(depth, n // nt),
        in_specs=[
            pl.BlockSpec((rows, d), lambda l, j: (0, 0)),
            pl.BlockSpec((None, d, nt), lambda l, j: (l, 0, j)),
            pl.BlockSpec((None, 1, nt), lambda l, j: (l, 0, j)),
        ],
        out_specs=pl.BlockSpec((None, rows, nt), lambda l, j: (l, 0, j)),
        compiler_params=_cparams("parallel", "parallel"),
        name="ada_mod",
    )(cs, ada_w, ada_b.reshape(depth, 1, n))


def _rope(r, c, s, first_half):
    partner = jnp.where(first_half, pltpu.roll(r, LANES - 16, 1), pltpu.roll(r, 16, 1))
    return r * c + partner * s


def _proj_kernel(*refs, kinds, nc, has_rope):
    if has_rope:
        x_ref, g_ref, sc_ref, sh_ref, w_ref, cq_ref, sq_ref, ck_ref, sk_ref, o_ref = refs
    else:
        x_ref, g_ref, sc_ref, sh_ref, w_ref, o_ref = refs
    t = x_ref.shape[0]
    nout = o_ref.shape[1]
    h = (_rms(x_ref[...]) * g_ref[...]) * (1.0 + sc_ref[...]) + sh_ref[...]
    hb = h.astype(BF16)
    if has_rope:
        lane = lax.broadcasted_iota(jnp.int32, (t, LANES), 1)
        first_half = (lane & 16) == 0
    for n0 in range(0, nout, nc):
        r = jnp.dot(hb, w_ref[:, n0:n0 + nc], preferred_element_type=F32)
        for j in range(nc // LANES):
            kind = kinds[n0 // LANES + j]
            rj = r[:, j * LANES:(j + 1) * LANES]
            if kind == ROPEQ:
                rj = _rope(rj, cq_ref[...], sq_ref[...], first_half)
            elif kind == ROPEK:
                rj = _rope(rj, ck_ref[...], sk_ref[...], first_half)
            elif kind == SCALEQ:
                rj = rj * Q_SCALE
            o_ref[:, n0 + j * LANES:n0 + (j + 1) * LANES] = rj.astype(BF16)


def _proj_call(x, gamma, mod, w, kinds, tables, *, tiles_per_seq, mod_row, name):
    n, d = x.shape
    nout = w.shape[1]
    t = T_PROJ
    has_rope = tables is not None

    def mrow(i):
        return mod_row if mod_row is not None else i // tiles_per_seq

    def tblock(i):
        return i % tiles_per_seq if mod_row is None else tiles_per_seq

    in_specs = [
        pl.BlockSpec((t, d), lambda i: (i, 0)),
        pl.BlockSpec((1, d), lambda i: (0, 0)),
        pl.BlockSpec((None, None, 1, d), lambda i: (mrow(i), 1, 0, 0)),
        pl.BlockSpec((None, None, 1, d), lambda i: (mrow(i), 0, 0, 0)),
        pl.BlockSpec((d, nout), lambda i: (0, 0)),
    ]
    args = [x, gamma.reshape(1, d), mod, mod, w]
    if has_rope:
        for tab in tables:
            in_specs.append(pl.BlockSpec((t, LANES), lambda i: (tblock(i), 0)))
            args.append(tab)
    return pl.pallas_call(
        functools.partial(_proj_kernel, kinds=tuple(kinds), nc=512, has_rope=has_rope),
        out_shape=jax.ShapeDtypeStruct((n, nout), BF16),
        grid=(n // t,),
        in_specs=in_specs,
        out_specs=pl.BlockSpec((t, nout), lambda i: (i, 0)),
        compiler_params=_cparams("parallel"),
        name=name,
    )(*args)


def _qk(q, k):
    return lax.dot_general(q, k, (((1,), (1,)), ((), ())), preferred_element_type=F32)


def _half_masks(shape):
    lane = lax.broadcasted_iota(jnp.int32, shape, 1)
    return lane < HEAD_DIM


def _split_heads(q, lo):
    qf = q.astype(F32)
    zero = jnp.zeros_like(qf)
    return jnp.where(lo, qf, zero).astype(q.dtype), jnp.where(lo, zero, qf).astype(q.dtype)


def _diff_lambda(lam_ref, lam_init):
    lv = lam_ref[...]
    d1 = jnp.sum(lv[0:1] * lv[1:2], axis=1, keepdims=True)
    d2 = jnp.sum(lv[2:3] * lv[3:4], axis=1, keepdims=True)
    return jnp.exp(d1) - jnp.exp(d2) + lam_init


def _diff_finish(o1, o2, lam, subln_ref, lam_init, o_ref):
    o = _rms(o1 - lam * o2) * subln_ref[...] * (1.0 - lam_init)
    o_ref[...] = o.astype(o_ref.dtype)


def _diff_kernel(lam_ref, subln_ref, q_ref, kl_ref, vl_ref, kc_ref, vc_ref, o_ref,
                 vt, s_scr, p_scr, m_scr, o_scr, *, lam_init, tq, kc):
    seq, nctx = q_ref.shape[0], kc_ref.shape[0]
    nkeys = vt.shape[1]
    nlat = seq // kc
    kcc = min(kc, nctx)
    nchunk = nlat + nctx // kcc
    ntile = seq // tq
    sub = m_scr.shape[1]
    lam = _diff_lambda(lam_ref, lam_init)
    lo = _half_masks((tq, LANES))

    def rows(c):
        if c < nlat:
            return slice(c * kc, (c + 1) * kc)
        return slice(seq + (c - nlat) * kcc, seq + (c - nlat + 1) * kcc)

    def chunk_of(lat_ref, ctx_ref, c):
        if c < nlat:
            return lat_ref[rows(c), :]
        return ctx_ref[(c - nlat) * kcc:(c - nlat + 1) * kcc, :]

    eye = (lax.broadcasted_iota(jnp.int32, (LANES, LANES), 0)
           == lax.broadcasted_iota(jnp.int32, (LANES, LANES), 1)).astype(F32).astype(BF16)
    for c in range(nchunk):
        vt[0:LANES, rows(c)] = _qk(eye, chunk_of(vl_ref, vc_ref, c)).astype(BF16)
    vt[LANES:, :] = jnp.ones((vt.shape[0] - LANES, nkeys), BF16)

    def half(m, tile, do_qk, do_exp, do_pv):
        if do_qk:
            q = q_ref[pl.ds(pl.multiple_of(tile * tq, tq), tq), :].astype(F32)
            zero = jnp.zeros_like(q)
            qm = (jnp.where(lo, q, zero) if m == 0 else jnp.where(lo, zero, q)).astype(BF16)
            mrun = jnp.full((sub, tq), NEG_INF, F32)
        if do_exp:
            mb = m_scr[1 - m, 0:1, :]
        if do_pv:
            acc = jnp.zeros((vt.shape[0], tq), F32)
        for c in range(nchunk):
            if do_qk:
                s = _qk(chunk_of(kl_ref, kc_ref, c), qm)
                s_scr[m, rows(c), :] = s
                for u in range(s.shape[0] // sub):
                    mrun = jnp.maximum(mrun, s[u * sub:(u + 1) * sub, :])
            if do_pv:
                acc = acc + jnp.dot(vt[:, rows(c)], p_scr[1 - m, rows(c), :], preferred_element_type=F32)
            if do_exp:
                p_scr[1 - m, rows(c), :] = jnp.exp2(s_scr[1 - m, rows(c), :] - mb).astype(BF16)
        if do_qk:
            m_scr[m] = jnp.broadcast_to(jnp.max(mrun, axis=0, keepdims=True), (sub, tq))
        if do_pv:
            return acc[:LANES] * (1.0 / acc[LANES:LANES + 1])
        return None

    def emit(tile, o2):
        o = _rms((o_scr[...] - lam * o2).T) * subln_ref[...] * (1.0 - lam_init)
        o_ref[pl.ds(pl.multiple_of(tile * tq, tq), tq), :] = o.astype(o_ref.dtype)

    half(0, 0, True, False, False)
    half(1, 0, True, True, False)
    half(0, 1, True, True, False)
    o_scr[...] = half(1, 1, True, True, True)

    def body(t, carry):
        emit(t - 2, half(0, t, True, True, True))
        o_scr[...] = half(1, t, True, True, True)
        return carry

    lax.fori_loop(2, ntile, body, 0)
    emit(ntile - 2, half(0, ntile, False, True, True))
    o_scr[...] = half(1, ntile, False, False, True)
    emit(ntile - 1, half(0, ntile + 1, False, False, True))


def _diff_ctx_kernel(lam_ref, subln_ref, q_ref, kc_ref, vc_ref, o_ref, *, lam_init):
    tq = q_ref.shape[0]
    lam = _diff_lambda(lam_ref, lam_init)
    lo = _half_masks((tq, LANES))
    for h in range(q_ref.shape[1] // LANES):
        cols = slice(h * LANES, (h + 1) * LANES)
        qs = _split_heads(q_ref[:, cols], lo)
        o1, o2 = [_softmax_pv([_qk(qm, kc_ref[:, cols])], [vc_ref[:, cols]]) for qm in qs]
        _diff_finish(o1, o2, lam, subln_ref, lam_init, o_ref.at[:, cols])


def _diff_attention(qkv, qkvc, lam_vecs, subln, lam_init, batch, seq, nctx, need_ctx):
    d = qkv.shape[1] // 3
    nh = d // LANES
    tq = DIFF_TQ
    kc = DIFF_KC
    assert seq % tq == 0 and seq % kc == 0 and nctx % min(kc, nctx) == 0 and seq // tq >= 2
    nkeys = seq + nctx
    subln2 = subln.reshape(1, LANES)
    y = pl.pallas_call(
        functools.partial(_diff_kernel, lam_init=lam_init, tq=tq, kc=kc),
        out_shape=jax.ShapeDtypeStruct((batch * seq, d), BF16),
        grid=(batch, nh),
        in_specs=[
            pl.BlockSpec((4, HEAD_DIM), lambda b, h: (0, 0)),
            pl.BlockSpec((1, LANES), lambda b, h: (0, 0)),
            pl.BlockSpec((seq, LANES), lambda b, h: (b, h)),
            pl.BlockSpec((seq, LANES), lambda b, h: (b, nh + h)),
            pl.BlockSpec((seq, LANES), lambda b, h: (b, 2 * nh + h)),
            pl.BlockSpec((nctx, LANES), lambda b, h: (b, nh + h)),
            pl.BlockSpec((nctx, LANES), lambda b, h: (b, 2 * nh + h)),
        ],
        out_specs=pl.BlockSpec((seq, LANES), lambda b, h: (b, h)),
        scratch_shapes=[pltpu.VMEM((LANES + BF16_ROWS, nkeys), BF16), pltpu.VMEM((2, nkeys, tq), F32),
                        pltpu.VMEM((2, nkeys, tq), BF16), pltpu.VMEM((2, F32_ROWS, tq), F32),
                        pltpu.VMEM((LANES, tq), F32)],
        compiler_params=_cparams("parallel", "parallel"),
        name="diff_attn",
    )(lam_vecs, subln2, qkv, qkv, qkv, qkvc, qkvc)
    yc = None
    if need_ctx:
        yc = pl.pallas_call(
            functools.partial(_diff_ctx_kernel, lam_init=lam_init),
            out_shape=jax.ShapeDtypeStruct((batch * nctx, d), BF16),
            grid=(batch,),
            in_specs=[
                pl.BlockSpec((4, HEAD_DIM), lambda b: (0, 0)),
                pl.BlockSpec((1, LANES), lambda b: (0, 0)),
                pl.BlockSpec((nctx, d), lambda b: (b, 0)),
                pl.BlockSpec((nctx, d), lambda b: (b, 1)),
                pl.BlockSpec((nctx, d), lambda b: (b, 2)),
            ],
            out_specs=pl.BlockSpec((nctx, d), lambda b: (b, 0)),
            compiler_params=_cparams("parallel"),
            name="diff_attn_ctx",
        )(lam_vecs, subln2, qkvc, qkvc, qkvc)
    return y, yc


def _softmax_pv(scores, values, extra_logit=None):
    m = jnp.max(scores[0], axis=1, keepdims=True)
    for s in scores[1:]:
        m = jnp.maximum(m, jnp.max(s, axis=1, keepdims=True))
    if extra_logit is not None:
        m = jnp.maximum(m, extra_logit)
    l = None
    acc = None
    for s, v in zip(scores, values):
        p = jnp.exp2(s - m)
        ps = jnp.sum(p, axis=1, keepdims=True)
        pv = jnp.dot(p.astype(BF16), v, preferred_element_type=F32)
        l = ps if l is None else l + ps
        acc = pv if acc is None else acc + pv
    if extra_logit is not None:
        l = l + jnp.exp2(extra_logit - m)
    return acc * (1.0 / l)


def _identity(n):
    return (lax.broadcasted_iota(jnp.int32, (n, n), 0)
            == lax.broadcasted_iota(jnp.int32, (n, n), 1)).astype(F32).astype(BF16)


def _fill_vt(vt, head0, chunk, v_block, n_heads):
    tr = _qk(_identity(LANES), v_block).astype(BF16)
    for hh in range(n_heads):
        vt[head0 + hh, chunk, 0:HEAD_DIM, :] = tr[hh * HEAD_DIM:(hh + 1) * HEAD_DIM]
        vt[head0 + hh, chunk, HEAD_DIM:, :] = jnp.ones((vt.shape[2] - HEAD_DIM, vt.shape[3]), BF16)


def _window_pipeline(q_ref, o_ref, s_scr, p_scr, m_scr, e_scr, o_scr, *, tq, kc, nchunk,
                     scores, values, extra_logit):
    ntile = q_ref.shape[0] // tq
    sub = m_scr.shape[1]
    lo = _half_masks((tq, LANES))
    rows = lambda c: slice(c * kc, (c + 1) * kc)
    tile_rows = lambda t: pl.ds(pl.multiple_of(t * tq, tq), tq)

    def half(j, t, do_qk, do_exp, do_pv):
        jo = 1 - j
        tv = t - 2 + j
        if do_qk:
            q = q_ref[tile_rows(t), :].astype(F32)
            zero = jnp.zeros_like(q)
            qm = (jnp.where(lo, q, zero) if j == 0 else jnp.where(lo, zero, q)).astype(BF16)
            mrun = jnp.full((sub, tq), NEG_INF, F32)
        if do_exp:
            mb = m_scr[jo, 0:1, :]
        if do_pv:
            acc = jnp.zeros((o_scr.shape[0] + BF16_ROWS, tq), F32)
        for c in range(nchunk):
            if do_qk:
                s = scores(qm, t, j, c)
                s_scr[j, rows(c), :] = s
                for u in range(kc // sub):
                    mrun = jnp.maximum(mrun, s[u * sub:(u + 1) * sub, :])
            if do_pv:
                acc = acc + jnp.dot(values(tv, jo, c), p_scr[jo, rows(c), :], preferred_element_type=F32)
            if do_exp:
                p_scr[jo, rows(c), :] = jnp.exp2(s_scr[jo, rows(c), :] - mb).astype(BF16)
        if do_qk:
            m = jnp.max(mrun, axis=0, keepdims=True)
            if extra_logit is not None:
                m = jnp.maximum(m, extra_logit(j))
            m_scr[j] = jnp.broadcast_to(m, (sub, tq))
        o = None
        if do_pv:
            l = acc[HEAD_DIM:HEAD_DIM + 1]
            if extra_logit is not None:
                l = l + e_scr[jo, 0:1, :]
            o = acc[:HEAD_DIM] * (1.0 / l)
        if do_exp and extra_logit is not None:
            e_scr[jo] = jnp.exp2(extra_logit(jo) - m_scr[jo])
        return o

    def emit(t, o1):
        o = jnp.concatenate([o_scr[...], o1], axis=0).T
        o_ref[tile_rows(t), :] = o.astype(o_ref.dtype)

    half(0, 0, True, False, False)
    half(1, 0, True, True, False)
    half(0, 1, True, True, False)
    o_scr[...] = half(1, 1, True, True, True)

    for t in range(2, ntile):
        emit(t - 2, half(0, t, True, True, True))
        o_scr[...] = half(1, t, True, True, True)
    emit(ntile - 2, half(0, ntile, False, True, True))
    o_scr[...] = half(1, ntile, False, False, True)
    emit(ntile - 1, half(0, ntile + 1, False, False, True))


def _tile_class(t, ntile):
    return jnp.where(t == 0, 0, jnp.where(t == ntile - 1, 2, 1))


def _na_kernel(bias_ref, q_ref, k_ref, v_ref, kc_ref, vc_ref, o_ref, vt, s_scr, p_scr, m_scr, o_scr,
               *, tq, kc):
    seq, nctx = q_ref.shape[0], kc_ref.shape[0]
    ntile, nlat, nwin = seq // tq, seq // kc, 3 * tq // kc
    rows = lambda c: slice(c * kc, (c + 1) * kc)
    for c in range(nlat):
        _fill_vt(vt, 0, c, v_ref[rows(c), :], 2)
    for c in range(nctx // kc):
        _fill_vt(vt, 0, nlat + c, vc_ref[rows(c), :], 2)

    def win_start(t):
        return pl.multiple_of(jnp.clip((t - 1) * tq, 0, seq - 3 * tq), tq)

    def scores(qm, t, j, c):
        if c >= nwin:
            return _qk(kc_ref[rows(c - nwin), :], qm)
        k = k_ref[pl.ds(win_start(t) + c * kc, kc), :]
        return _qk(k, qm) + bias_ref[_tile_class(t, ntile), j, rows(c), :]

    def values(t, j, c):
        return vt[j, nlat + c - nwin] if c >= nwin else vt[j, win_start(t) // kc + c]

    _window_pipeline(q_ref, o_ref, s_scr, p_scr, m_scr, None, o_scr, tq=tq, kc=kc,
                     nchunk=nwin + nctx // kc, scores=scores, values=values, extra_logit=None)


def _plain_ctx_kernel(q_ref, kc_ref, vc_ref, o_ref):
    tq = q_ref.shape[0]
    lo = _half_masks((tq, LANES))
    for hb in range(q_ref.shape[1] // LANES):
        cols = slice(hb * LANES, (hb + 1) * LANES)
        qs = _split_heads(q_ref[:, cols], lo)
        outs = [_softmax_pv([_qk(qs[hh], kc_ref[:, cols])], [vc_ref[:, cols]]) for hh in range(2)]
        o_ref[:, cols] = jnp.where(lo, outs[0], outs[1]).astype(o_ref.dtype)


def _na_bias_tables(rpb, rows):
    w = GRID_W
    nk = 3 * NA_ROWS
    nh = rpb.shape[0]
    edge = w - NA_KW
    ext = jnp.concatenate([jnp.repeat(rpb[..., :1], edge, axis=-1), rpb,
                           jnp.repeat(rpb[..., -1:], edge + 1, axis=-1)], axis=-1)
    flat = jnp.tile(ext, (1, 1, w + 1))[..., :w * (2 * w + 1)]
    hankel = flat.reshape(nh, -1, w, 2 * w + 1)[..., :w]
    toep = jnp.flip(hankel, axis=-2)
    cq = np.arange(w)[:, None]
    cx = np.arange(w)[None, :]
    col_start = np.clip(cq - NA_KW // 2, 0, w - NA_KW)
    col_ok = (cx >= col_start) & (cx < col_start + NA_KW)
    toep = jnp.where(col_ok, toep * LOG2E, NEG_INF)
    masked = jnp.full((nh, w, w), NEG_INF, F32)
    tabs = []
    for r0, k0 in ((0, 0), (NA_ROWS, 0), (rows - NA_ROWS, rows - nk)):
        row_blocks = []
        for a in range(NA_ROWS):
            r = r0 + a
            rs = min(max(r - NA_KH // 2, 0), rows - NA_KH)
            blocks = []
            for e in range(nk):
                ry = k0 + e
                blocks.append(toep[:, ry - r + NA_KH - 1] if rs <= ry < rs + NA_KH else masked)
            row_blocks.append(jnp.concatenate(blocks, axis=-1))
        tabs.append(jnp.concatenate(row_blocks, axis=-2))
    return jnp.swapaxes(jnp.stack(tabs), -1, -2).astype(F32)


def _win_scratch(nheads, nchunks, nkeys_item, tq, kc, with_extra):
    shapes = [pltpu.VMEM((nheads, nchunks, HEAD_DIM + BF16_ROWS, kc), BF16),
              pltpu.VMEM((2, nkeys_item, tq), F32), pltpu.VMEM((2, nkeys_item, tq), BF16),
              pltpu.VMEM((2, F32_ROWS, tq), F32)]
    if with_extra:
        shapes.append(pltpu.VMEM((2, F32_ROWS, tq), F32))
    shapes.append(pltpu.VMEM((HEAD_DIM, tq), F32))
    return shapes


def _na_attention(qkv, qkvc, rpb, batch, seq, nctx, need_ctx):
    d = qkv.shape[1] // 3
    nb = d // LANES
    rows = seq // GRID_W
    tq = NA_ROWS * GRID_W
    kc = tq
    assert seq // tq >= 3 and nctx % kc == 0
    bias = _na_bias_tables(rpb.astype(F32), rows)
    y = pl.pallas_call(
        functools.partial(_na_kernel, tq=tq, kc=kc),
        out_shape=jax.ShapeDtypeStruct((batch * seq, d), BF16),
        grid=(nb, batch),
        in_specs=[
            pl.BlockSpec((3, 2, 3 * tq, tq), lambda hb, b: (0, hb, 0, 0), pipeline_mode=pl.Buffered(1)),
            pl.BlockSpec((seq, LANES), lambda hb, b: (b, hb)),
            pl.BlockSpec((seq, LANES), lambda hb, b: (b, nb + hb)),
            pl.BlockSpec((seq, LANES), lambda hb, b: (b, 2 * nb + hb)),
            pl.BlockSpec((nctx, LANES), lambda hb, b: (b, nb + hb)),
            pl.BlockSpec((nctx, LANES), lambda hb, b: (b, 2 * nb + hb)),
        ],
        out_specs=pl.BlockSpec((seq, LANES), lambda hb, b: (b, hb)),
        scratch_shapes=_win_scratch(2, (seq + nctx) // kc, 3 * tq + nctx, tq, kc, False),
        compiler_params=_cparams("parallel", "arbitrary"),
        name="na_attn",
    )(bias, qkv, qkv, qkv, qkvc, qkvc)
    yc = None
    if need_ctx:
        yc = pl.pallas_call(
            _plain_ctx_kernel,
            out_shape=jax.ShapeDtypeStruct((batch * nctx, d), BF16),
            grid=(batch,),
            in_specs=[
                pl.BlockSpec((nctx, d), lambda b: (b, 0)),
                pl.BlockSpec((nctx, d), lambda b: (b, 1)),
                pl.BlockSpec((nctx, d), lambda b: (b, 2)),
            ],
            out_specs=pl.BlockSpec((nctx, d), lambda b: (b, 0)),
            compiler_params=_cparams("parallel"),
            name="na_attn_ctx",
        )(qkvc, qkvc, qkvc)
    return y, yc


def _swa_heads(q_ref, sinks_ref, g, score_fn, values, o_ref):
    tq = q_ref.shape[0]
    lo = _half_masks((tq, LANES))
    for jb in range(2):
        qs = _split_heads(q_ref[:, jb * LANES:(jb + 1) * LANES], lo)
        outs = []
        for hh in range(2):
            sink = jnp.full((1, 1), sinks_ref[g * 4 + jb * 2 + hh] * LOG2E, F32)
            outs.append(_softmax_pv(score_fn(qs[hh]), values, extra_logit=sink))
        o_ref[:, jb * LANES:(jb + 1) * LANES] = jnp.where(lo, outs[0], outs[1]).astype(o_ref.dtype)


def _swa_kernel(sinks_ref, band_ref, q_ref, k_ref, v_ref, kc_ref, vc_ref, o_ref,
                vt, s_scr, p_scr, m_scr, e_scr, o_scr, *, tq, kc, span):
    head0 = (pl.program_id(1) * 2 + pl.program_id(2)) * 2
    seq, nctx = q_ref.shape[0], kc_ref.shape[0]
    ntile, nlat, nwin = seq // tq, seq // kc, span // kc
    rows = lambda c: slice(c * kc, (c + 1) * kc)
    for c in range(nlat):
        _fill_vt(vt, 0, c, v_ref[rows(c), :], 1)
    for c in range(nctx // kc):
        _fill_vt(vt, 0, nlat + c, vc_ref[rows(c), :], 1)

    def win_start(t):
        return pl.multiple_of(jnp.clip(t * tq - SWA_WINDOW, 0, seq - span), SWA_WINDOW)

    def scores(qm, t, j, c):
        if c >= nwin:
            return _qk(kc_ref[rows(c - nwin), :], qm)
        k = k_ref[pl.ds(win_start(t) + c * kc, kc), :]
        return _qk(k, qm) + band_ref[_tile_class(t, ntile), rows(c), :]

    def values(t, j, c):
        return vt[0, nlat + c - nwin] if c >= nwin else vt[0, win_start(t) // kc + c]

    def sink(j):
        return jnp.full((1, 1), sinks_ref[head0 + j] * LOG2E, F32)

    _window_pipeline(q_ref, o_ref, s_scr, p_scr, m_scr, e_scr, o_scr, tq=tq, kc=kc,
                     nchunk=nwin + nctx // kc, scores=scores, values=values, extra_logit=sink)


def _swa_band_tables(seq, tq, span):
    ntile = seq // tq
    tabs = []
    for t in (0, 1, ntile - 1):
        start = min(max(t * tq - SWA_WINDOW, 0), seq - span)
        kpos = start + np.arange(span)[:, None]
        qpos = t * tq + np.arange(tq)[None, :]
        tabs.append(np.where(np.abs(qpos - kpos) <= SWA_WINDOW, 0.0, NEG_INF))
    return jnp.asarray(np.stack(tabs), F32)


def _swa_ctx_kernel(sinks_ref, q_ref, kc_ref, vc_ref, o_ref):
    qb = 2 * LANES
    for g in range(kc_ref.shape[1] // LANES):
        kc = kc_ref[:, g * LANES:(g + 1) * LANES]
        vc = vc_ref[:, g * LANES:(g + 1) * LANES]
        cols = slice(g * qb, (g + 1) * qb)
        _swa_heads(q_ref.at[:, cols], sinks_ref, g, lambda qm, kc=kc: [_qk(qm, kc)], [vc], o_ref.at[:, cols])


def _swa_attention(qkv, qkvc, sinks, batch, seq, nctx, need_ctx, d):
    nkv = (qkv.shape[1] - d) // (2 * LANES)
    qb = d // nkv
    kcol = d // LANES
    tq = SWA_TQ
    kc = SWA_WINDOW
    span = tq + 2 * SWA_WINDOW
    assert qb == 2 * LANES and seq // tq >= 3 and nctx % kc == 0
    smem = pl.BlockSpec(memory_space=pltpu.SMEM)
    sinks = sinks.astype(F32)
    y = pl.pallas_call(
        functools.partial(_swa_kernel, tq=tq, kc=kc, span=span),
        out_shape=jax.ShapeDtypeStruct((batch * seq, d), BF16),
        grid=(batch, nkv, 2),
        in_specs=[
            smem,
            pl.BlockSpec((3, span, tq), lambda b, g, jb: (0, 0, 0), pipeline_mode=pl.Buffered(1)),
            pl.BlockSpec((seq, LANES), lambda b, g, jb: (b, 2 * g + jb)),
            pl.BlockSpec((seq, LANES), lambda b, g, jb: (b, kcol + g)),
            pl.BlockSpec((seq, LANES), lambda b, g, jb: (b, kcol + nkv + g)),
            pl.BlockSpec((nctx, LANES), lambda b, g, jb: (b, kcol + g)),
            pl.BlockSpec((nctx, LANES), lambda b, g, jb: (b, kcol + nkv + g)),
        ],
        out_specs=pl.BlockSpec((seq, LANES), lambda b, g, jb: (b, 2 * g + jb)),
        scratch_shapes=_win_scratch(1, (seq + nctx) // kc, span + nctx, tq, kc, True),
        compiler_params=_cparams("parallel", "parallel", "arbitrary"),
        name="swa_attn",
    )(sinks, _swa_band_tables(seq, tq, span), qkv, qkv, qkv, qkvc, qkvc)
    yc = None
    if need_ctx:
        yc = pl.pallas_call(
            _swa_ctx_kernel,
            out_shape=jax.ShapeDtypeStruct((batch * nctx, d), BF16),
            grid=(batch,),
            in_specs=[
                smem,
                pl.BlockSpec((nctx, d), lambda b: (b, 0)),
                pl.BlockSpec((nctx, nkv * LANES), lambda b: (b, d // (nkv * LANES))),
                pl.BlockSpec((nctx, nkv * LANES), lambda b: (b, d // (nkv * LANES) + 1)),
            ],
            out_specs=pl.BlockSpec((nctx, d), lambda b: (b, 0)),
            compiler_params=_cparams("parallel"),
            name="swa_attn_ctx",
        )(sinks, qkvc, qkvc, qkvc)
    return y, yc


def _oproj_kernel(o_ref, wo_ref, x_ref, g1_ref, nf_ref, sc2_ref, sh2_ref, xo_ref, h2_ref):
    y = jnp.dot(o_ref[...], wo_ref[...], preferred_element_type=F32)
    xn = x_ref[...] + g1_ref[...] * y
    xo_ref[...] = xn
    h2 = (_rms(xn) * nf_ref[...]) * (1.0 + sc2_ref[...]) + sh2_ref[...]
    h2_ref[...] = h2.astype(BF16)


def _oproj_call(o, wo, x, gamma, mod, *, tiles_per_seq, mod_row, name):
    n, d = x.shape
    t = T_PROJ

    def mrow(i):
        return mod_row if mod_row is not None else i // tiles_per_seq

    def mspec(which):
        return pl.BlockSpec((None, None, 1, d), lambda i: (mrow(i), which, 0, 0))

    return pl.pallas_call(
        _oproj_kernel,
        out_shape=(jax.ShapeDtypeStruct((n, d), F32), jax.ShapeDtypeStruct((n, d), BF16)),
        grid=(n // t,),
        in_specs=[
            pl.BlockSpec((t, d), lambda i: (i, 0)),
            pl.BlockSpec((d, d), lambda i: (0, 0)),
            pl.BlockSpec((t, d), lambda i: (i, 0)),
            mspec(2),
            pl.BlockSpec((1, d), lambda i: (0, 0)),
            mspec(4),
            mspec(3),
        ],
        out_specs=(pl.BlockSpec((t, d), lambda i: (i, 0)), pl.BlockSpec((t, d), lambda i: (i, 0))),
        compiler_params=_cparams("parallel"),
        name=name,
    )(o, wo, x, mod, gamma.reshape(1, d), mod, mod)


def _ffn_kernel(h_ref, hp_ref, hn_ref, wup_ref, cw_ref, cb_ref, wd_ref, x_ref, g2_ref, no_ref, o_ref,
                lhs, act, ubuf, *, nseq, final, f):
    i = pl.program_id(0)
    t = h_ref.shape[0]
    halo = hp_ref.shape[0]
    dff = wd_ref.shape[0]
    rows = t + 2 * halo

    keep_prev = ((i % nseq) != 0).astype(F32)
    keep_next = ((i % nseq) != nseq - 1).astype(F32)
    lhs[0:halo, :] = (hp_ref[...].astype(F32) * keep_prev).astype(BF16)
    lhs[halo:halo + t, :] = h_ref[...]
    lhs[halo + t:, :] = (hn_ref[...].astype(F32) * keep_next).astype(BF16)

    def conv(col0, slot):
        u = jnp.dot(lhs[...], wup_ref[:, col0:col0 + f], preferred_element_type=F32)
        outs = []
        for s in range(f // LANES):
            sl = slice(col0 + s * LANES, col0 + (s + 1) * LANES)
            buf = ubuf.at[slot, s]
            buf[pl.ds(0, rows, stride=2), :] = u[:, s * LANES:(s + 1) * LANES]
            taps = [buf[pl.ds(2 * (halo - 1 + k), t, stride=2), :] for k in range(CONV_W)]
            outs.append(cb_ref[:, sl] + taps[0] * cw_ref[0:1, sl] + taps[1] * cw_ref[1:2, sl]
                        + taps[2] * cw_ref[2:3, sl])
        return jnp.concatenate(outs, axis=1)

    nch = dff // f
    cuts = [0] + [(nch * k) // DOWN_PIECES for k in range(1, DOWN_PIECES + 1)]
    y = None
    for c in range(nch):
        slot = 2 * (c % 2)
        a = conv(c * f, slot)
        g = conv(dff + c * f, slot + 1)
        act[:, c * f:(c + 1) * f] = ((g * (1.0 / (1.0 + jnp.exp(-g)))) * a).astype(BF16)
        if c + 1 in cuts:
            lo_c = cuts[cuts.index(c + 1) - 1] * f
            part = jnp.dot(act[:, lo_c:(c + 1) * f], wd_ref[lo_c:(c + 1) * f, :], preferred_element_type=F32)
            y = part if y is None else y + part

    xn = x_ref[...] + g2_ref[...] * y
    if final:
        xn = _rms(xn) * no_ref[...]
    o_ref[...] = xn


def _ffn_call(h2, x, w_up, w_conv, b_conv, w_down, mod, norm_out, *, t, tiles_per_seq, mod_row, final, name):
    n, d = x.shape
    dff = w_down.shape[0]
    f = FF_CHUNK
    halo = BF16_ROWS
    hb = t // halo
    last_hblock = n // halo - 1
    resident = pl.Buffered(1)

    def mrow(i):
        return mod_row if mod_row is not None else i // tiles_per_seq

    return pl.pallas_call(
        functools.partial(_ffn_kernel, nseq=tiles_per_seq, final=final, f=f),
        out_shape=jax.ShapeDtypeStruct((n, d), F32),
        grid=(n // t,),
        in_specs=[
            pl.BlockSpec((t, d), lambda i: (i, 0)),
            pl.BlockSpec((halo, d), lambda i: (jnp.maximum(i * hb - 1, 0), 0)),
            pl.BlockSpec((halo, d), lambda i: (jnp.minimum((i + 1) * hb, last_hblock), 0)),
            pl.BlockSpec((d, 2 * dff), lambda i: (0, 0), pipeline_mode=resident),
            pl.BlockSpec((CONV_W, 2 * dff), lambda i: (0, 0), pipeline_mode=resident),
            pl.BlockSpec((1, 2 * dff), lambda i: (0, 0), pipeline_mode=resident),
            pl.BlockSpec((dff, d), lambda i: (0, 0), pipeline_mode=resident),
            pl.BlockSpec((t, d), lambda i: (i, 0)),
            pl.BlockSpec((None, None, 1, d), lambda i: (mrow(i), 5, 0, 0)),
            pl.BlockSpec((1, d), lambda i: (0, 0)),
        ],
        out_specs=pl.BlockSpec((t, d), lambda i: (i, 0)),
        scratch_shapes=[pltpu.VMEM((t + 2 * halo, d), BF16), pltpu.VMEM((t, dff), BF16),
                        pltpu.VMEM((4, f // LANES, 2 * (t + 2 * halo), LANES), F32)],
        compiler_params=_cparams("parallel"),
        name=name,
    )(h2, h2, h2, w_up, w_conv, b_conv.reshape(1, 2 * dff), w_down, x, mod, norm_out.reshape(1, d))


def _rope_tables(seq, extra_rows):
    tpos = jnp.arange(seq, dtype=jnp.int32)
    row = (tpos // GRID_W).astype(F32)
    col = (tpos % GRID_W).astype(F32)
    inv = ROPE_BASE ** (-jnp.arange(ROPE_PAIRS, dtype=F32) / ROPE_PAIRS)
    ar = row[:, None] * inv[None, :]
    ac = col[:, None] * inv[None, :]
    cr, sr, cc, sc = jnp.cos(ar), jnp.sin(ar), jnp.cos(ac), jnp.sin(ac)
    cos = jnp.concatenate([cr, cr, cc, cc] * (LANES // HEAD_DIM), axis=1)
    sin = jnp.concatenate([-sr, sr, -sc, sc] * (LANES // HEAD_DIM), axis=1)
    cos = jnp.concatenate([cos, jnp.ones((extra_rows, LANES), F32)], axis=0)
    sin = jnp.concatenate([sin, jnp.zeros((extra_rows, LANES), F32)], axis=0)
    return cos * Q_SCALE, sin * Q_SCALE, cos, sin


def _dup_kv_columns(w, d, nkv):
    q = w[:, :d]
    k = w[:, d:d + nkv * HEAD_DIM].reshape(-1, nkv, 1, HEAD_DIM)
    v = w[:, d + nkv * HEAD_DIM:].reshape(-1, nkv, 1, HEAD_DIM)
    kd = jnp.broadcast_to(k, (w.shape[0], nkv, 2, HEAD_DIM)).reshape(w.shape[0], -1)
    vd = jnp.broadcast_to(v, (w.shape[0], nkv, 2, HEAD_DIM)).reshape(w.shape[0], -1)
    return jnp.concatenate([q, kd, vd], axis=1)


def kernel(x, c, ctx, c_ctx, ada_w, ada_b, norm_mix, norm_ffn, norm_out, ffn_up, ffn_conv, ffn_conv_b,
           ffn_down, a_wqkv, a_wo, a_lambda, a_subln, b_wqkv, b_wo, b_rpb, c_wqkv, c_wo, c_sinks):
    batch, seq, d = x.shape
    nctx = ctx.shape[1]
    depth = ada_w.shape[0]
    assert seq % T_FFN == 0 and seq % T_PROJ == 0 and (batch * nctx) % T_PROJ == 0

    mod_rows = 16
    cs = jnp.concatenate([c, c_ctx[None, :], jnp.zeros((mod_rows - batch - 1, d), F32)], axis=0)
    mod_all = _ada_call(cs, ada_w, ada_b).reshape(depth, mod_rows, 6, 1, d)
    ctx_row = batch

    tables = _rope_tables(seq, T_PROJ)
    lat_tiles = seq // T_PROJ
    nd = d // LANES

    xl = x.reshape(batch * seq, d)
    xc = ctx.reshape(batch * nctx, d)
    for i in range(depth):
        need_ctx = i < depth - 1
        kind, j = i % N_MIXERS, i // N_MIXERS
        mod = mod_all[i]
        if kind == 0:
            w = a_wqkv[j].astype(BF16)
            wo = a_wo[j].astype(BF16)
            kinds = [ROPEQ] * nd + [ROPEK] * nd + [PLAIN] * nd
            tabs = tables
        elif kind == 1:
            w = b_wqkv[j].astype(BF16)
            wo = b_wo[j].astype(BF16)
            kinds = [SCALEQ] * nd + [PLAIN] * (2 * nd)
            tabs = None
        else:
            nkv = (c_wqkv.shape[2] - d) // (2 * HEAD_DIM)
            w = _dup_kv_columns(c_wqkv[j], d, nkv).astype(BF16)
            wo = c_wo[j].astype(BF16)
            kinds = [ROPEQ] * nd + [ROPEK] * nkv + [PLAIN] * nkv
            tabs = tables
        qkv = _proj_call(xl, norm_mix[i], mod, w, kinds, tabs, tiles_per_seq=lat_tiles, mod_row=None,
                         name=f"proj{i}")
        qkvc = _proj_call(xc, norm_mix[i], mod, w, kinds, tabs, tiles_per_seq=lat_tiles, mod_row=ctx_row,
                          name=f"proj_ctx{i}")
        if kind == 0:
            lam_init = 0.8 - 0.6 * math.exp(-0.3 * i)
            y, yc = _diff_attention(qkv, qkvc, a_lambda[j].astype(F32), a_subln[j], lam_init,
                                    batch, seq, nctx, need_ctx)
        elif kind == 1:
            y, yc = _na_attention(qkv, qkvc, b_rpb[j], batch, seq, nctx, need_ctx)
        else:
            y, yc = _swa_attention(qkv, qkvc, c_sinks[j], batch, seq, nctx, need_ctx, d)

        w_up = ffn_up[i].astype(BF16)
        w_down = ffn_down[i].astype(BF16)
        xl, h2 = _oproj_call(y, wo, xl, norm_ffn[i], mod, tiles_per_seq=lat_tiles, mod_row=None,
                             name=f"oproj{i}")
        xl = _ffn_call(h2, xl, w_up, ffn_conv[i], ffn_conv_b[i], w_down, mod, norm_out,
                       t=T_FFN, tiles_per_seq=seq // T_FFN, mod_row=None, final=not need_ctx,
                       name=f"ffn{i}")
        if need_ctx:
            xc, h2c = _oproj_call(yc, wo, xc, norm_ffn[i], mod, tiles_per_seq=1, mod_row=ctx_row,
                                  name=f"oproj_ctx{i}")
            xc = _ffn_call(h2c, xc, w_up, ffn_conv[i], ffn_conv_b[i], w_down, mod, norm_out,
                           t=nctx, tiles_per_seq=1, mod_row=ctx_row, final=False, name=f"ffn_ctx{i}")
    return xl.reshape(batch, seq, d)
```

```python
import functools
import math

import numpy as np
import jax
import jax.numpy as jnp
from jax import lax
from jax.experimental import pallas as pl
from jax.experimental.pallas import tpu as pltpu

F32 = jnp.float32
BF16 = jnp.bfloat16

GRID_W = 64
HEAD_DIM = 64
ROPE_BASE = 10000.0
ROPE_PAIRS = HEAD_DIM // 4
NORM_EPS = 1e-6
NEG_INF = -1e30
N_MIXERS = 3
NA_KH = 8
NA_KW = 16
SWA_WINDOW = 128
CONV_W = 3
LOG2E = 1.4426950408889634
Q_SCALE = (HEAD_DIM ** -0.5) * LOG2E

LANES = 128
BF16_ROWS = 16
F32_ROWS = 8
VMEM_LIMIT = 52 * 1024 * 1024

T_PROJ = 512
T_FFN = 512
FF_CHUNK = 256
DOWN_PIECES = 11
NA_ROWS = 4
SWA_TQ = 256
DIFF_TQ = 256
DIFF_KC = 256

PLAIN, SCALEQ, ROPEQ, ROPEK = 0, 1, 2, 3


def _cparams(*sem):
    return pltpu.CompilerParams(dimension_semantics=sem, vmem_limit_bytes=VMEM_LIMIT)


def _rms(x):
    return x * lax.rsqrt(jnp.mean(x * x, axis=-1, keepdims=True) + NORM_EPS)


def _ada_kernel(cs_ref, w_ref, b_ref, o_ref):
    cs = cs_ref[...]
    s = cs * (1.0 / (1.0 + jnp.exp(-cs)))
    w = w_ref[...]
    s_hi = s.astype(BF16)
    s_lo = (s - s_hi.astype(F32)).astype(BF16)
    w_hi = w.astype(BF16)
    w_lo = (w - w_hi.astype(F32)).astype(BF16)
    acc = jnp.dot(s_hi, w_hi, preferred_element_type=F32)
    acc = acc + jnp.dot(s_hi, w_lo, preferred_element_type=F32)
    acc = acc + jnp.dot(s_lo, w_hi, preferred_element_type=F32)
    o_ref[...] = acc + b_ref[...]


def _ada_call(cs, ada_w, ada_b):
    depth, d, n = ada_w.shape
    nt = 1536
    rows = cs.shape[0]
    return pl.pallas_call(
        _ada_kernel,
        out_shape=jax.ShapeDtypeStruct((depth, rows, n), F32),
        grid=(depth, n // nt),
        in_specs=[
            pl.BlockSpec((rows, d), lambda l, j: (0, 0)),
            pl.BlockSpec((None, d, nt), lambda l, j: (l, 0, j)),
            pl.BlockSpec((None, 1, nt), lambda l, j: (l, 0, j)),
        ],
        out_specs=pl.BlockSpec((None, rows, nt), lambda l, j: (l, 0, j)),
        compiler_params=_cparams("parallel", "parallel"),
        name="ada_mod",
    )(cs, ada_w, ada_b.reshape(depth, 1, n))


def _rope(r, c, s, first_half):
    partner = jnp.where(first_half, pltpu.roll(r, LANES - 16, 1), pltpu.roll(r, 16, 1))
    return r * c + partner * s


def _proj_kernel(*refs, kinds, nc, has_rope):
    if has_rope:
        x_ref, g_ref, sc_ref, sh_ref, w_ref, cq_ref, sq_ref, ck_ref, sk_ref, o_ref = refs
    else:
        x_ref, g_ref, sc_ref, sh_ref, w_ref, o_ref = refs
    t = x_ref.shape[0]
    nout = o_ref.shape[1]
    h = (_rms(x_ref[...]) * g_ref[...]) * (1.0 + sc_ref[...]) + sh_ref[...]
    hb = h.astype(BF16)
    if has_rope:
        lane = lax.broadcasted_iota(jnp.int32, (t, LANES), 1)
        first_half = (lane & 16) == 0
    for n0 in range(0, nout, nc):
        r = jnp.dot(hb, w_ref[:, n0:n0 + nc], preferred_element_type=F32)
        for j in range(nc // LANES):
            kind = kinds[n0 // LANES + j]
            rj = r[:, j * LANES:(j + 1) * LANES]
            if kind == ROPEQ:
                rj = _rope(rj, cq_ref[...], sq_ref[...], first_half)
            elif kind == ROPEK:
                rj = _rope(rj, ck_ref[...], sk_ref[...], first_half)
            elif kind == SCALEQ:
                rj = rj * Q_SCALE
            o_ref[:, n0 + j * LANES:n0 + (j + 1) * LANES] = rj.astype(BF16)


def _proj_call(x, gamma, mod, w, kinds, tables, *, tiles_per_seq, mod_row, name):
    n, d = x.shape
    nout = w.shape[1]
    t = T_PROJ
    has_rope = tables is not None

    def mrow(i):
        return mod_row if mod_row is not None else i // tiles_per_seq

    def tblock(i):
        return i % tiles_per_seq if mod_row is None else tiles_per_seq

    in_specs = [
        pl.BlockSpec((t, d), lambda i: (i, 0)),
        pl.BlockSpec((1, d), lambda i: (0, 0)),
        pl.BlockSpec((None, None, 1, d), lambda i: (mrow(i), 1, 0, 0)),
        pl.BlockSpec((None, None, 1, d), lambda i: (mrow(i), 0, 0, 0)),
        pl.BlockSpec((d, nout), lambda i: (0, 0)),
    ]
    args = [x, gamma.reshape(1, d), mod, mod, w]
    if has_rope:
        for tab in tables:
            in_specs.append(pl.BlockSpec((t, LANES), lambda i: (tblock(i), 0)))
            args.append(tab)
    return pl.pallas_call(
        functools.partial(_proj_kernel, kinds=tuple(kinds), nc=512, has_rope=has_rope),
        out_shape=jax.ShapeDtypeStruct((n, nout), BF16),
        grid=(n // t,),
        in_specs=in_specs,
        out_specs=pl.BlockSpec((t, nout), lambda i: (i, 0)),
        compiler_params=_cparams("parallel"),
        name=name,
    )(*args)


def _qk(q, k):
    return lax.dot_general(q, k, (((1,), (1,)), ((), ())), preferred_element_type=F32)


def _half_masks(shape):
    lane = lax.broadcasted_iota(jnp.int32, shape, 1)
    return lane < HEAD_DIM


def _split_heads(q, lo):
    qf = q.astype(F32)
    zero = jnp.zeros_like(qf)
    return jnp.where(lo, qf, zero).astype(q.dtype), jnp.where(lo, zero, qf).astype(q.dtype)


def _diff_lambda(lam_ref, lam_init):
    lv = lam_ref[...]
    d1 = jnp.sum(lv[0:1] * lv[1:2], axis=1, keepdims=True)
    d2 = jnp.sum(lv[2:3] * lv[3:4], axis=1, keepdims=True)
    return jnp.exp(d1) - jnp.exp(d2) + lam_init


def _diff_finish(o1, o2, lam, subln_ref, lam_init, o_ref):
    o = _rms(o1 - lam * o2) * subln_ref[...] * (1.0 - lam_init)
    o_ref[...] = o.astype(o_ref.dtype)


def _diff_kernel(lam_ref, subln_ref, q_ref, kl_ref, vl_ref, kc_ref, vc_ref, o_ref,
                 vt, s_scr, p_scr, m_scr, o_scr, *, lam_init, tq, kc):
    seq, nctx = q_ref.shape[0], kc_ref.shape[0]
    nkeys = vt.shape[1]
    nlat = seq // kc
    kcc = min(kc, nctx)
    nchunk = nlat + nctx // kcc
    ntile = seq // tq
    sub = m_scr.shape[1]
    lam = _diff_lambda(lam_ref, lam_init)
    lo = _half_masks((tq, LANES))

    def rows(c):
        if c < nlat:
            return slice(c * kc, (c + 1) * kc)
        return slice(seq + (c - nlat) * kcc, seq + (c - nlat + 1) * kcc)

    def chunk_of(lat_ref, ctx_ref, c):
        if c < nlat:
            return lat_ref[rows(c), :]
        return ctx_ref[(c - nlat) * kcc:(c - nlat + 1) * kcc, :]

    eye = (lax.broadcasted_iota(jnp.int32, (LANES, LANES), 0)
           == lax.broadcasted_iota(jnp.int32, (LANES, LANES), 1)).astype(F32).astype(BF16)
    for c in range(nchunk):
        vt[0:LANES, rows(c)] = _qk(eye, chunk_of(vl_ref, vc_ref, c)).astype(BF16)
    vt[LANES:, :] = jnp.ones((vt.shape[0] - LANES, nkeys), BF16)

    def half(m, tile, do_qk, do_exp, do_pv):
        if do_qk:
            q = q_ref[pl.ds(pl.multiple_of(tile * tq, tq), tq), :].astype(F32)
            zero = jnp.zeros_like(q)
            qm = (jnp.where(lo, q, zero) if m == 0 else jnp.where(lo, zero, q)).astype(BF16)
            mrun = jnp.full((sub, tq), NEG_INF, F32)
        if do_exp:
            mb = m_scr[1 - m, 0:1, :]
        if do_pv:
            acc = jnp.zeros((vt.shape[0], tq), F32)
        for c in range(nchunk):
            if do_qk:
                s = _qk(chunk_of(kl_ref, kc_ref, c), qm)
                s_scr[m, rows(c), :] = s
                for u in range(s.shape[0] // sub):
                    mrun = jnp.maximum(mrun, s[u * sub:(u + 1) * sub, :])
            if do_pv:
                acc = acc + jnp.dot(vt[:, rows(c)], p_scr[1 - m, rows(c), :], preferred_element_type=F32)
            if do_exp:
                p_scr[1 - m, rows(c), :] = jnp.exp2(s_scr[1 - m, rows(c), :] - mb).astype(BF16)
        if do_qk:
            m_scr[m] = jnp.broadcast_to(jnp.max(mrun, axis=0, keepdims=True), (sub, tq))
        if do_pv:
            return acc[:LANES] * (1.0 / acc[LANES:LANES + 1])
        return None

    def emit(tile, o2):
        o = _rms((o_scr[...] - lam * o2).T) * subln_ref[...] * (1.0 - lam_init)
        o_ref[pl.ds(pl.multiple_of(tile * tq, tq), tq), :] = o.astype(o_ref.dtype)

    half(0, 0, True, False, False)
    half(1, 0, True, True, False)
    half(0, 1, True, True, False)
    o_scr[...] = half(1, 1, True, True, True)

    def body(t, carry):
        emit(t - 2, half(0, t, True, True, True))
        o_scr[...] = half(1, t, True, True, True)
        return carry

    lax.fori_loop(2, ntile, body, 0)
    emit(ntile - 2, half(0, ntile, False, True, True))
    o_scr[...] = half(1, ntile, False, False, True)
    emit(ntile - 1, half(0, ntile + 1, False, False, True))


def _diff_ctx_kernel(lam_ref, subln_ref, q_ref, kc_ref, vc_ref, o_ref, *, lam_init):
    tq = q_ref.shape[0]
    lam = _diff_lambda(lam_ref, lam_init)
    lo = _half_masks((tq, LANES))
    for h in range(q_ref.shape[1] // LANES):
        cols = slice(h * LANES, (h + 1) * LANES)
        qs = _split_heads(q_ref[:, cols], lo)
        o1, o2 = [_softmax_pv([_qk(qm, kc_ref[:, cols])], [vc_ref[:, cols]]) for qm in qs]
        _diff_finish(o1, o2, lam, subln_ref, lam_init, o_ref.at[:, cols])


def _diff_attention(qkv, qkvc, lam_vecs, subln, lam_init, batch, seq, nctx, need_ctx):
    d = qkv.shape[1] // 3
    nh = d // LANES
    tq = DIFF_TQ
    kc = DIFF_KC
    assert seq % tq == 0 and seq % kc == 0 and nctx % min(kc, nctx) == 0 and seq // tq >= 2
    nkeys = seq + nctx
    subln2 = subln.reshape(1, LANES)
    y = pl.pallas_call(
        functools.partial(_diff_kernel, lam_init=lam_init, tq=tq, kc=kc),
        out_shape=jax.ShapeDtypeStruct((batch * seq, d), BF16),
        grid=(batch, nh),
        in_specs=[
            pl.BlockSpec((4, HEAD_DIM), lambda b, h: (0, 0)),
            pl.BlockSpec((1, LANES), lambda b, h: (0, 0)),
            pl.BlockSpec((seq, LANES), lambda b, h: (b, h)),
            pl.BlockSpec((seq, LANES), lambda b, h: (b, nh + h)),
            pl.BlockSpec((seq, LANES), lambda b, h: (b, 2 * nh + h)),
            pl.BlockSpec((nctx, LANES), lambda b, h: (b, nh + h)),
            pl.BlockSpec((nctx, LANES), lambda b, h: (b, 2 * nh + h)),
        ],
        out_specs=pl.BlockSpec((seq, LANES), lambda b, h: (b, h)),
        scratch_shapes=[pltpu.VMEM((LANES + BF16_ROWS, nkeys), BF16), pltpu.VMEM((2, nkeys, tq), F32),
                        pltpu.VMEM((2, nkeys, tq), BF16), pltpu.VMEM((2, F32_ROWS, tq), F32),
                        pltpu.VMEM((LANES, tq), F32)],
        compiler_params=_cparams("parallel", "parallel"),
        name="diff_attn",
    )(lam_vecs, subln2, qkv, qkv, qkv, qkvc, qkvc)
    yc = None
    if need_ctx:
        yc = pl.pallas_call(
            functools.partial(_diff_ctx_kernel, lam_init=lam_init),
            out_shape=jax.ShapeDtypeStruct((batch * nctx, d), BF16),
            grid=(batch,),
            in_specs=[
                pl.BlockSpec((4, HEAD_DIM), lambda b: (0, 0)),
                pl.BlockSpec((1, LANES), lambda b: (0, 0)),
                pl.BlockSpec((nctx, d), lambda b: (b, 0)),
                pl.BlockSpec((nctx, d), lambda b: (b, 1)),
                pl.BlockSpec((nctx, d), lambda b: (b, 2)),
            ],
            out_specs=pl.BlockSpec((nctx, d), lambda b: (b, 0)),
            compiler_params=_cparams("parallel"),
            name="diff_attn_ctx",
        )(lam_vecs, subln2, qkvc, qkvc, qkvc)
    return y, yc


def _softmax_pv(scores, values, extra_logit=None):
    m = jnp.max(scores[0], axis=1, keepdims=True)
    for s in scores[1:]:
        m = jnp.maximum(m, jnp.max(s, axis=1, keepdims=True))
    if extra_logit is not None:
        m = jnp.maximum(m, extra_logit)
    l = None
    acc = None
    for s, v in zip(scores, values):
        p = jnp.exp2(s - m)
        ps = jnp.sum(p, axis=1, keepdims=True)
        pv = jnp.dot(p.astype(BF16), v, preferred_element_type=F32)
        l = ps if l is None else l + ps
        acc = pv if acc is None else acc + pv
    if extra_logit is not None:
        l = l + jnp.exp2(extra_logit - m)
    return acc * (1.0 / l)


def _identity(n):
    return (lax.broadcasted_iota(jnp.int32, (n, n), 0)
            == lax.broadcasted_iota(jnp.int32, (n, n), 1)).astype(F32).astype(BF16)


def _fill_vt(vt, head0, chunk, v_block, n_heads):
    tr = _qk(_identity(LANES), v_block).astype(BF16)
    for hh in range(n_heads):
        vt[head0 + hh, chunk, 0:HEAD_DIM, :] = tr[hh * HEAD_DIM:(hh + 1) * HEAD_DIM]
        vt[head0 + hh, chunk, HEAD_DIM:, :] = jnp.ones((vt.shape[2] - HEAD_DIM, vt.shape[3]), BF16)


def _window_pipeline(q_ref, o_ref, s_scr, p_scr, m_scr, e_scr, o_scr, *, tq, kc, nchunk,
                     scores, values, extra_logit):
    ntile = q_ref.shape[0] // tq
    sub = m_scr.shape[1]
    lo = _half_masks((tq, LANES))
    rows = lambda c: slice(c * kc, (c + 1) * kc)
    tile_rows = lambda t: pl.ds(pl.multiple_of(t * tq, tq), tq)

    def half(j, t, do_qk, do_exp, do_pv):
        jo = 1 - j
        tv = t - 2 + j
        if do_qk:
            q = q_ref[tile_rows(t), :].astype(F32)
            zero = jnp.zeros_like(q)
            qm = (jnp.where(lo, q, zero) if j == 0 else jnp.where(lo, zero, q)).astype(BF16)
            mrun = jnp.full((sub, tq), NEG_INF, F32)
        if do_exp:
            mb = m_scr[jo, 0:1, :]
        if do_pv:
            acc = jnp.zeros((o_scr.shape[0] + BF16_ROWS, tq), F32)
        for c in range(nchunk):
            if do_qk:
                s = scores(qm, t, j, c)
                s_scr[j, rows(c), :] = s
                for u in range(kc // sub):
                    mrun = jnp.maximum(mrun, s[u * sub:(u + 1) * sub, :])
            if do_pv:
                acc = acc + jnp.dot(values(tv, jo, c), p_scr[jo, rows(c), :], preferred_element_type=F32)
            if do_exp:
                p_scr[jo, rows(c), :] = jnp.exp2(s_scr[jo, rows(c), :] - mb).astype(BF16)
        if do_qk:
            m = jnp.max(mrun, axis=0, keepdims=True)
            if extra_logit is not None:
                m = jnp.maximum(m, extra_logit(j))
            m_scr[j] = jnp.broadcast_to(m, (sub, tq))
        o = None
        if do_pv:
            l = acc[HEAD_DIM:HEAD_DIM + 1]
            if extra_logit is not None:
                l = l + e_scr[jo, 0:1, :]
            o = acc[:HEAD_DIM] * (1.0 / l)
        if do_exp and extra_logit is not None:
            e_scr[jo] = jnp.exp2(extra_logit(jo) - m_scr[jo])
        return o

    def emit(t, o1):
        o = jnp.concatenate([o_scr[...], o1], axis=0).T
        o_ref[tile_rows(t), :] = o.astype(o_ref.dtype)

    half(0, 0, True, False, False)
    half(1, 0, True, True, False)
    half(0, 1, True, True, False)
    o_scr[...] = half(1, 1, True, True, True)

    for t in range(2, ntile):
        emit(t - 2, half(0, t, True, True, True))
        o_scr[...] = half(1, t, True, True, True)
    emit(ntile - 2, half(0, ntile, False, True, True))
    o_scr[...] = half(1, ntile, False, False, True)
    emit(ntile - 1, half(0, ntile + 1, False, False, True))


def _tile_class(t, ntile):
    return jnp.where(t == 0, 0, jnp.where(t == ntile - 1, 2, 1))


def _na_kernel(bias_ref, q_ref, k_ref, v_ref, kc_ref, vc_ref, o_ref, vt, s_scr, p_scr, m_scr, o_scr,
               *, tq, kc):
    seq, nctx = q_ref.shape[0], kc_ref.shape[0]
    ntile, nlat, nwin = seq // tq, seq // kc, 3 * tq // kc
    rows = lambda c: slice(c * kc, (c + 1) * kc)
    for c in range(nlat):
        _fill_vt(vt, 0, c, v_ref[rows(c), :], 2)
    for c in range(nctx // kc):
        _fill_vt(vt, 0, nlat + c, vc_ref[rows(c), :], 2)

    def win_start(t):
        return pl.multiple_of(jnp.clip((t - 1) * tq, 0, seq - 3 * tq), tq)

    def scores(qm, t, j, c):
        if c >= nwin:
            return _qk(kc_ref[rows(c - nwin), :], qm)
        k = k_ref[pl.ds(win_start(t) + c * kc, kc), :]
        return _qk(k, qm) + bias_ref[_tile_class(t, ntile), j, rows(c), :]

    def values(t, j, c):
        return vt[j, nlat + c - nwin] if c >= nwin else vt[j, win_start(t) // kc + c]

    _window_pipeline(q_ref, o_ref, s_scr, p_scr, m_scr, None, o_scr, tq=tq, kc=kc,
                     nchunk=nwin + nctx // kc, scores=scores, values=values, extra_logit=None)


def _plain_ctx_kernel(q_ref, kc_ref, vc_ref, o_ref):
    tq = q_ref.shape[0]
    lo = _half_masks((tq, LANES))
    for hb in range(q_ref.shape[1] // LANES):
        cols = slice(hb * LANES, (hb + 1) * LANES)
        qs = _split_heads(q_ref[:, cols], lo)
        outs = [_softmax_pv([_qk(qs[hh], kc_ref[:, cols])], [vc_ref[:, cols]]) for hh in range(2)]
        o_ref[:, cols] = jnp.where(lo, outs[0], outs[1]).astype(o_ref.dtype)


def _na_bias_tables(rpb, rows):
    w = GRID_W
    nk = 3 * NA_ROWS
    nh = rpb.shape[0]
    edge = w - NA_KW
    ext = jnp.concatenate([jnp.repeat(rpb[..., :1], edge, axis=-1), rpb,
                           jnp.repeat(rpb[..., -1:], edge + 1, axis=-1)], axis=-1)
    flat = jnp.tile(ext, (1, 1, w + 1))[..., :w * (2 * w + 1)]
    hankel = flat.reshape(nh, -1, w, 2 * w + 1)[..., :w]
    toep = jnp.flip(hankel, axis=-2)
    cq = np.arange(w)[:, None]
    cx = np.arange(w)[None, :]
    col_start = np.clip(cq - NA_KW // 2, 0, w - NA_KW)
    col_ok = (cx >= col_start) & (cx < col_start + NA_KW)
    toep = jnp.where(col_ok, toep * LOG2E, NEG_INF)
    masked = jnp.full((nh, w, w), NEG_INF, F32)
    tabs = []
    for r0, k0 in ((0, 0), (NA_ROWS, 0), (rows - NA_ROWS, rows - nk)):
        row_blocks = []
        for a in range(NA_ROWS):
            r = r0 + a
            rs = min(max(r - NA_KH // 2, 0), rows - NA_KH)
            blocks = []
            for e in range(nk):
                ry = k0 + e
                blocks.append(toep[:, ry - r + NA_KH - 1] if rs <= ry < rs + NA_KH else masked)
            row_blocks.append(jnp.concatenate(blocks, axis=-1))
        tabs.append(jnp.concatenate(row_blocks, axis=-2))
    return jnp.swapaxes(jnp.stack(tabs), -1, -2).astype(F32)


def _win_scratch(nheads, nchunks, nkeys_item, tq, kc, with_extra):
    shapes = [pltpu.VMEM((nheads, nchunks, HEAD_DIM + BF16_ROWS, kc), BF16),
              pltpu.VMEM((2, nkeys_item, tq), F32), pltpu.VMEM((2, nkeys_item, tq), BF16),
              pltpu.VMEM((2, F32_ROWS, tq), F32)]
    if with_extra:
        shapes.append(pltpu.VMEM((2, F32_ROWS, tq), F32))
    shapes.append(pltpu.VMEM((HEAD_DIM, tq), F32))
    return shapes


def _na_attention(qkv, qkvc, rpb, batch, seq, nctx, need_ctx):
    d = qkv.shape[1] // 3
    nb = d // LANES
    rows = seq // GRID_W
    tq = NA_ROWS * GRID_W
    kc = tq
    assert seq // tq >= 3 and nctx % kc == 0
    bias = _na_bias_tables(rpb.astype(F32), rows)
    y = pl.pallas_call(
        functools.partial(_na_kernel, tq=tq, kc=kc),
        out_shape=jax.ShapeDtypeStruct((batch * seq, d), BF16),
        grid=(nb, batch),
        in_specs=[
            pl.BlockSpec((3, 2, 3 * tq, tq), lambda hb, b: (0, hb, 0, 0), pipeline_mode=pl.Buffered(1)),
            pl.BlockSpec((seq, LANES), lambda hb, b: (b, hb)),
            pl.BlockSpec((seq, LANES), lambda hb, b: (b, nb + hb)),
            pl.BlockSpec((seq, LANES), lambda hb, b: (b, 2 * nb + hb)),
            pl.BlockSpec((nctx, LANES), lambda hb, b: (b, nb + hb)),
            pl.BlockSpec((nctx, LANES), lambda hb, b: (b, 2 * nb + hb)),
        ],
        out_specs=pl.BlockSpec((seq, LANES), lambda hb, b: (b, hb)),
        scratch_shapes=_win_scratch(2, (seq + nctx) // kc, 3 * tq + nctx, tq, kc, False),
        compiler_params=_cparams("parallel", "arbitrary"),
        name="na_attn",
    )(bias, qkv, qkv, qkv, qkvc, qkvc)
    yc = None
    if need_ctx:
        yc = pl.pallas_call(
            _plain_ctx_kernel,
            out_shape=jax.ShapeDtypeStruct((batch * nctx, d), BF16),
            grid=(batch,),
            in_specs=[
                pl.BlockSpec((nctx, d), lambda b: (b, 0)),
                pl.BlockSpec((nctx, d), lambda b: (b, 1)),
                pl.BlockSpec((nctx, d), lambda b: (b, 2)),
            ],
            out_specs=pl.BlockSpec((nctx, d), lambda b: (b, 0)),
            compiler_params=_cparams("parallel"),
            name="na_attn_ctx",
        )(qkvc, qkvc, qkvc)
    return y, yc


def _swa_heads(q_ref, sinks_ref, g, score_fn, values, o_ref):
    tq = q_ref.shape[0]
    lo = _half_masks((tq, LANES))
    for jb in range(2):
        qs = _split_heads(q_ref[:, jb * LANES:(jb + 1) * LANES], lo)
        outs = []
        for hh in range(2):
            sink = jnp.full((1, 1), sinks_ref[g * 4 + jb * 2 + hh] * LOG2E, F32)
            outs.append(_softmax_pv(score_fn(qs[hh]), values, extra_logit=sink))
        o_ref[:, jb * LANES:(jb + 1) * LANES] = jnp.where(lo, outs[0], outs[1]).astype(o_ref.dtype)


def _swa_kernel(sinks_ref, band_ref, q_ref, k_ref, v_ref, kc_ref, vc_ref, o_ref,
                vt, s_scr, p_scr, m_scr, e_scr, o_scr, *, tq, kc, span):
    head0 = (pl.program_id(1) * 2 + pl.program_id(2)) * 2
    seq, nctx = q_ref.shape[0], kc_ref.shape[0]
    ntile, nlat, nwin = seq // tq, seq // kc, span // kc
    rows = lambda c: slice(c * kc, (c + 1) * kc)
    for c in range(nlat):
        _fill_vt(vt, 0, c, v_ref[rows(c), :], 1)
    for c in range(nctx // kc):
        _fill_vt(vt, 0, nlat + c, vc_ref[rows(c), :], 1)

    def win_start(t):
        return pl.multiple_of(jnp.clip(t * tq - SWA_WINDOW, 0, seq - span), SWA_WINDOW)

    def scores(qm, t, j, c):
        if c >= nwin:
            return _qk(kc_ref[rows(c - nwin), :], qm)
        k = k_ref[pl.ds(win_start(t) + c * kc, kc), :]
        return _qk(k, qm) + band_ref[_tile_class(t, ntile), rows(c), :]

    def values(t, j, c):
        return vt[0, nlat + c - nwin] if c >= nwin else vt[0, win_start(t) // kc + c]

    def sink(j):
        return jnp.full((1, 1), sinks_ref[head0 + j] * LOG2E, F32)

    _window_pipeline(q_ref, o_ref, s_scr, p_scr, m_scr, e_scr, o_scr, tq=tq, kc=kc,
                     nchunk=nwin + nctx // kc, scores=scores, values=values, extra_logit=sink)


def _swa_band_tables(seq, tq, span):
    ntile = seq // tq
    tabs = []
    for t in (0, 1, ntile - 1):
        start = min(max(t * tq - SWA_WINDOW, 0), seq - span)
        kpos = start + np.arange(span)[:, None]
        qpos = t * tq + np.arange(tq)[None, :]
        tabs.append(np.where(np.abs(qpos - kpos) <= SWA_WINDOW, 0.0, NEG_INF))
    return jnp.asarray(np.stack(tabs), F32)


def _swa_ctx_kernel(sinks_ref, q_ref, kc_ref, vc_ref, o_ref):
    qb = 2 * LANES
    for g in range(kc_ref.shape[1] // LANES):
        kc = kc_ref[:, g * LANES:(g + 1) * LANES]
        vc = vc_ref[:, g * LANES:(g + 1) * LANES]
        cols = slice(g * qb, (g + 1) * qb)
        _swa_heads(q_ref.at[:, cols], sinks_ref, g, lambda qm, kc=kc: [_qk(qm, kc)], [vc], o_ref.at[:, cols])


def _swa_attention(qkv, qkvc, sinks, batch, seq, nctx, need_ctx, d):
    nkv = (qkv.shape[1] - d) // (2 * LANES)
    qb = d // nkv
    kcol = d // LANES
    tq = SWA_TQ
    kc = SWA_WINDOW
    span = tq + 2 * SWA_WINDOW
    assert qb == 2 * LANES and seq // tq >= 3 and nctx % kc == 0
    smem = pl.BlockSpec(memory_space=pltpu.SMEM)
    sinks = sinks.astype(F32)
    y = pl.pallas_call(
        functools.partial(_swa_kernel, tq=tq, kc=kc, span=span),
        out_shape=jax.ShapeDtypeStruct((batch * seq, d), BF16),
        grid=(batch, nkv, 2),
        in_specs=[
            smem,
            pl.BlockSpec((3, span, tq), lambda b, g, jb: (0, 0, 0), pipeline_mode=pl.Buffered(1)),
            pl.BlockSpec((seq, LANES), lambda b, g, jb: (b, 2 * g + jb)),
            pl.BlockSpec((seq, LANES), lambda b, g, jb: (b, kcol + g)),
            pl.BlockSpec((seq, LANES), lambda b, g, jb: (b, kcol + nkv + g)),
            pl.BlockSpec((nctx, LANES), lambda b, g, jb: (b, kcol + g)),
            pl.BlockSpec((nctx, LANES), lambda b, g, jb: (b, kcol + nkv + g)),
        ],
        out_specs=pl.BlockSpec((seq, LANES), lambda b, g, jb: (b, 2 * g + jb)),
        scratch_shapes=_win_scratch(1, (seq + nctx) // kc, span + nctx, tq, kc, True),
        compiler_params=_cparams("parallel", "parallel", "arbitrary"),
        name="swa_attn",
    )(sinks, _swa_band_tables(seq, tq, span), qkv, qkv, qkv, qkvc, qkvc)
    yc = None
    if need_ctx:
        yc = pl.pallas_call(
            _swa_ctx_kernel,
            out_shape=jax.ShapeDtypeStruct((batch * nctx, d), BF16),
            grid=(batch,),
            in_specs=[
                smem,
                pl.BlockSpec((nctx, d), lambda b: (b, 0)),
                pl.BlockSpec((nctx, nkv * LANES), lambda b: (b, d // (nkv * LANES))),
                pl.BlockSpec((nctx, nkv * LANES), lambda b: (b, d // (nkv * LANES) + 1)),
            ],
            out_specs=pl.BlockSpec((nctx, d), lambda b: (b, 0)),
            compiler_params=_cparams("parallel"),
            name="swa_attn_ctx",
        )(sinks, qkvc, qkvc, qkvc)
    return y, yc


def _oproj_kernel(o_ref, wo_ref, x_ref, g1_ref, nf_ref, sc2_ref, sh2_ref, xo_ref, h2_ref):
    y = jnp.dot(o_ref[...], wo_ref[...], preferred_element_type=F32)
    xn = x_ref[...] + g1_ref[...] * y
    xo_ref[...] = xn
    h2 = (_rms(xn) * nf_ref[...]) * (1.0 + sc2_ref[...]) + sh2_ref[...]
    h2_ref[...] = h2.astype(BF16)


def _oproj_call(o, wo, x, gamma, mod, *, tiles_per_seq, mod_row, name):
    n, d = x.shape
    t = T_PROJ

    def mrow(i):
        return mod_row if mod_row is not None else i // tiles_per_seq

    def mspec(which):
        return pl.BlockSpec((None, None, 1, d), lambda i: (mrow(i), which, 0, 0))

    return pl.pallas_call(
        _oproj_kernel,
        out_shape=(jax.ShapeDtypeStruct((n, d), F32), jax.ShapeDtypeStruct((n, d), BF16)),
        grid=(n // t,),
        in_specs=[
            pl.BlockSpec((t, d), lambda i: (i, 0)),
            pl.BlockSpec((d, d), lambda i: (0, 0)),
            pl.BlockSpec((t, d), lambda i: (i, 0)),
            mspec(2),
            pl.BlockSpec((1, d), lambda i: (0, 0)),
            mspec(4),
            mspec(3),
        ],
        out_specs=(pl.BlockSpec((t, d), lambda i: (i, 0)), pl.BlockSpec((t, d), lambda i: (i, 0))),
        compiler_params=_cparams("parallel"),
        name=name,
    )(o, wo, x, mod, gamma.reshape(1, d), mod, mod)


def _ffn_kernel(h_ref, hp_ref, hn_ref, wup_ref, cw_ref, cb_ref, wd_ref, x_ref, g2_ref, no_ref, o_ref,
                lhs, act, ubuf, *, nseq, final, f):
    i = pl.program_id(0)
    t = h_ref.shape[0]
    halo = hp_ref.shape[0]
    dff = wd_ref.shape[0]
    rows = t + 2 * halo

    keep_prev = ((i % nseq) != 0).astype(F32)
    keep_next = ((i % nseq) != nseq - 1).astype(F32)
    lhs[0:halo, :] = (hp_ref[...].astype(F32) * keep_prev).astype(BF16)
    lhs[halo:halo + t, :] = h_ref[...]
    lhs[halo + t:, :] = (hn_ref[...].astype(F32) * keep_next).astype(BF16)

    def conv(col0, slot):
        u = jnp.dot(lhs[...], wup_ref[:, col0:col0 + f], preferred_element_type=F32)
        outs = []
        for s in range(f // LANES):
            sl = slice(col0 + s * LANES, col0 + (s + 1) * LANES)
            buf = ubuf.at[slot, s]
            buf[pl.ds(0, rows, stride=2), :] = u[:, s * LANES:(s + 1) * LANES]
            taps = [buf[pl.ds(2 * (halo - 1 + k), t, stride=2), :] for k in range(CONV_W)]
            outs.append(cb_ref[:, sl] + taps[0] * cw_ref[0:1, sl] + taps[1] * cw_ref[1:2, sl]
                        + taps[2] * cw_ref[2:3, sl])
        return jnp.concatenate(outs, axis=1)

    nch = dff // f
    cuts = [0] + [(nch * k) // DOWN_PIECES for k in range(1, DOWN_PIECES + 1)]
    y = None
    for c in range(nch):
        slot = 2 * (c % 2)
        a = conv(c * f, slot)
        g = conv(dff + c * f, slot + 1)
        act[:, c * f:(c + 1) * f] = ((g * (1.0 / (1.0 + jnp.exp(-g)))) * a).astype(BF16)
        if c + 1 in cuts:
            lo_c = cuts[cuts.index(c + 1) - 1] * f
            part = jnp.dot(act[:, lo_c:(c + 1) * f], wd_ref[lo_c:(c + 1) * f, :], preferred_element_type=F32)
            y = part if y is None else y + part

    xn = x_ref[...] + g2_ref[...] * y
    if final:
        xn = _rms(xn) * no_ref[...]
    o_ref[...] = xn


def _ffn_call(h2, x, w_up, w_conv, b_conv, w_down, mod, norm_out, *, t, tiles_per_seq, mod_row, final, name):
    n, d = x.shape
    dff = w_down.shape[0]
    f = FF_CHUNK
    halo = BF16_ROWS
    hb = t // halo
    last_hblock = n // halo - 1
    resident = pl.Buffered(1)

    def mrow(i):
        return mod_row if mod_row is not None else i // tiles_per_seq

    return pl.pallas_call(
        functools.partial(_ffn_kernel, nseq=tiles_per_seq, final=final, f=f),
        out_shape=jax.ShapeDtypeStruct((n, d), F32),
        grid=(n // t,),
        in_specs=[
            pl.BlockSpec((t, d), lambda i: (i, 0)),
            pl.BlockSpec((halo, d), lambda i: (jnp.maximum(i * hb - 1, 0), 0)),
            pl.BlockSpec((halo, d), lambda i: (jnp.minimum((i + 1) * hb, last_hblock), 0)),
            pl.BlockSpec((d, 2 * dff), lambda i: (0, 0), pipeline_mode=resident),
            pl.BlockSpec((CONV_W, 2 * dff), lambda i: (0, 0), pipeline_mode=resident),
            pl.BlockSpec((1, 2 * dff), lambda i: (0, 0), pipeline_mode=resident),
            pl.BlockSpec((dff, d), lambda i: (0, 0), pipeline_mode=resident),
            pl.BlockSpec((t, d), lambda i: (i, 0)),
            pl.BlockSpec((None, None, 1, d), lambda i: (mrow(i), 5, 0, 0)),
            pl.BlockSpec((1, d), lambda i: (0, 0)),
        ],
        out_specs=pl.BlockSpec((t, d), lambda i: (i, 0)),
        scratch_shapes=[pltpu.VMEM((t + 2 * halo, d), BF16), pltpu.VMEM((t, dff), BF16),
                        pltpu.VMEM((4, f // LANES, 2 * (t + 2 * halo), LANES), F32)],
        compiler_params=_cparams("parallel"),
        name=name,
    )(h2, h2, h2, w_up, w_conv, b_conv.reshape(1, 2 * dff), w_down, x, mod, norm_out.reshape(1, d))


def _rope_tables(seq, extra_rows):
    tpos = jnp.arange(seq, dtype=jnp.int32)
    row = (tpos // GRID_W).astype(F32)
    col = (tpos % GRID_W).astype(F32)
    inv = ROPE_BASE ** (-jnp.arange(ROPE_PAIRS, dtype=F32) / ROPE_PAIRS)
    ar = row[:, None] * inv[None, :]
    ac = col[:, None] * inv[None, :]
    cr, sr, cc, sc = jnp.cos(ar), jnp.sin(ar), jnp.cos(ac), jnp.sin(ac)
    cos = jnp.concatenate([cr, cr, cc, cc] * (LANES // HEAD_DIM), axis=1)
    sin = jnp.concatenate([-sr, sr, -sc, sc] * (LANES // HEAD_DIM), axis=1)
    cos = jnp.concatenate([cos, jnp.ones((extra_rows, LANES), F32)], axis=0)
    sin = jnp.concatenate([sin, jnp.zeros((extra_rows, LANES), F32)], axis=0)
    return cos * Q_SCALE, sin * Q_SCALE, cos, sin


def _dup_kv_columns(w, d, nkv):
    q = w[:, :d]
    k = w[:, d:d + nkv * HEAD_DIM].reshape(-1, nkv, 1, HEAD_DIM)
    v = w[:, d + nkv * HEAD_DIM:].reshape(-1, nkv, 1, HEAD_DIM)
    kd = jnp.broadcast_to(k, (w.shape[0], nkv, 2, HEAD_DIM)).reshape(w.shape[0], -1)
    vd = jnp.broadcast_to(v, (w.shape[0], nkv, 2, HEAD_DIM)).reshape(w.shape[0], -1)
    return jnp.concatenate([q, kd, vd], axis=1)


def kernel(x, c, ctx, c_ctx, ada_w, ada_b, norm_mix, norm_ffn, norm_out, ffn_up, ffn_conv, ffn_conv_b,
           ffn_down, a_wqkv, a_wo, a_lambda, a_subln, b_wqkv, b_wo, b_rpb, c_wqkv, c_wo, c_sinks):
    batch, seq, d = x.shape
    nctx = ctx.shape[1]
    depth = ada_w.shape[0]
    assert seq % T_FFN == 0 and seq % T_PROJ == 0 and (batch * nctx) % T_PROJ == 0

    mod_rows = 16
    cs = jnp.concatenate([c, c_ctx[None, :], jnp.zeros((mod_rows - batch - 1, d), F32)], axis=0)
    mod_all = _ada_call(cs, ada_w, ada_b).reshape(depth, mod_rows, 6, 1, d)
    ctx_row = batch

    tables = _rope_tables(seq, T_PROJ)
    lat_tiles = seq // T_PROJ
    nd = d // LANES

    xl = x.reshape(batch * seq, d)
    xc = ctx.reshape(batch * nctx, d)
    for i in range(depth):
        need_ctx = i < depth - 1
        kind, j = i % N_MIXERS, i // N_MIXERS
        mod = mod_all[i]
        if kind == 0:
            w = a_wqkv[j].astype(BF16)
            wo = a_wo[j].astype(BF16)
            kinds = [ROPEQ] * nd + [ROPEK] * nd + [PLAIN] * nd
            tabs = tables
        elif kind == 1:
            w = b_wqkv[j].astype(BF16)
            wo = b_wo[j].astype(BF16)
            kinds = [SCALEQ] * nd + [PLAIN] * (2 * nd)
            tabs = None
        else:
            nkv = (c_wqkv.shape[2] - d) // (2 * HEAD_DIM)
            w = _dup_kv_columns(c_wqkv[j], d, nkv).astype(BF16)
            wo = c_wo[j].astype(BF16)
            kinds = [ROPEQ] * nd + [ROPEK] * nkv + [PLAIN] * nkv
            tabs = tables
        qkv = _proj_call(xl, norm_mix[i], mod, w, kinds, tabs, tiles_per_seq=lat_tiles, mod_row=None,
                         name=f"proj{i}")
        qkvc = _proj_call(xc, norm_mix[i], mod, w, kinds, tabs, tiles_per_seq=lat_tiles, mod_row=ctx_row,
                          name=f"proj_ctx{i}")
        if kind == 0:
            lam_init = 0.8 - 0.6 * math.exp(-0.3 * i)
            y, yc = _diff_attention(qkv, qkvc, a_lambda[j].astype(F32), a_subln[j], lam_init,
                                    batch, seq, nctx, need_ctx)
        elif kind == 1:
            y, yc = _na_attention(qkv, qkvc, b_rpb[j], batch, seq, nctx, need_ctx)
        else:
            y, yc = _swa_attention(qkv, qkvc, c_sinks[j], batch, seq, nctx, need_ctx, d)

        w_up = ffn_up[i].astype(BF16)
        w_down = ffn_down[i].astype(BF16)
        xl, h2 = _oproj_call(y, wo, xl, norm_ffn[i], mod, tiles_per_seq=lat_tiles, mod_row=None,
                             name=f"oproj{i}")
        xl = _ffn_call(h2, xl, w_up, ffn_conv[i], ffn_conv_b[i], w_down, mod, norm_out,
                       t=T_FFN, tiles_per_seq=seq // T_FFN, mod_row=None, final=not need_ctx,
                       name=f"ffn{i}")
        if need_ctx:
            xc, h2c = _oproj_call(yc, wo, xc, norm_ffn[i], mod, tiles_per_seq=1, mod_row=ctx_row,
                                  name=f"oproj_ctx{i}")
            xc = _ffn_call(h2c, xc, w_up, ffn_conv[i], ffn_conv_b[i], w_down, mod, norm_out,
                           t=nctx, tiles_per_seq=1, mod_row=ctx_row, final=False, name=f"ffn_ctx{i}")
    return xl.reshape(batch, seq, d)
```

```python
import functools
import math

import numpy as np
import jax
import jax.numpy as jnp
from jax import lax
from jax.experimental import pallas as pl
from jax.experimental.pallas import tpu as pltpu

F32 = jnp.float32
BF16 = jnp.bfloat16

GRID_W = 64
HEAD_DIM = 64
ROPE_BASE = 10000.0
ROPE_PAIRS = HEAD_DIM // 4
NORM_EPS = 1e-6
NEG_INF = -1e30
N_MIXERS = 3
NA_KH = 8
NA_KW = 16
SWA_WINDOW = 128
CONV_W = 3
LOG2E = 1.4426950408889634
Q_SCALE = (HEAD_DIM ** -0.5) * LOG2E

LANES = 128
BF16_ROWS = 16
F32_ROWS = 8
VMEM_LIMIT = 52 * 1024 * 1024

T_PROJ = 1024
T_FFN = 512
FF_CHUNK = 256
DOWN_PIECES = 6
NA_ROWS = 4
SWA_TQ = 256
DIFF_TQ = 256
DIFF_KC = 256

PLAIN, SCALEQ, ROPEQ, ROPEK = 0, 1, 2, 3


def _cparams(*sem):
    return pltpu.CompilerParams(dimension_semantics=sem, vmem_limit_bytes=VMEM_LIMIT)


def _rms(x):
    return x * lax.rsqrt(jnp.mean(x * x, axis=-1, keepdims=True) + NORM_EPS)


def _ada_kernel(cs_ref, w_ref, b_ref, o_ref):
    cs = cs_ref[...]
    s = cs * (1.0 / (1.0 + jnp.exp(-cs)))
    w = w_ref[...]
    s_hi = s.astype(BF16)
    s_lo = (s - s_hi.astype(F32)).astype(BF16)
    w_hi = w.astype(BF16)
    w_lo = (w - w_hi.astype(F32)).astype(BF16)
    acc = jnp.dot(s_hi, w_hi, preferred_element_type=F32)
    acc = acc + jnp.dot(s_hi, w_lo, preferred_element_type=F32)
    acc = acc + jnp.dot(s_lo, w_hi, preferred_element_type=F32)
    o_ref[...] = acc + b_ref[...]


def _ada_call(cs, ada_w, ada_b):
    depth, d, n = ada_w.shape
    nt = 1536
    rows = cs.shape[0]
    return pl.pallas_call(
        _ada_kernel,
        out_shape=jax.ShapeDtypeStruct((depth, rows, n), F32),
        grid=(depth, n // nt),
        in_specs=[
            pl.BlockSpec((rows, d), lambda l, j: (0, 0)),
            pl.BlockSpec((None, d, nt), lambda l, j: (l, 0, j)),
            pl.BlockSpec((None, 1, nt), lambda l, j: (l, 0, j)),
        ],
        out_specs=pl.BlockSpec((None, rows, nt), lambda l, j: (l, 0, j)),
        compiler_params=_cparams("parallel", "parallel"),
        name="ada_mod",
    )(cs, ada_w, ada_b.reshape(depth, 1, n))


def _rope(r, c, s, first_half):
    partner = jnp.where(first_half, pltpu.roll(r, LANES - 16, 1), pltpu.roll(r, 16, 1))
    return r * c + partner * s


def _proj_kernel(*refs, kinds, nc, has_rope):
    if has_rope:
        x_ref, g_ref, sc_ref, sh_ref, w_ref, cq_ref, sq_ref, ck_ref, sk_ref, o_ref = refs
    else:
        x_ref, g_ref, sc_ref, sh_ref, w_ref, o_ref = refs
    t = x_ref.shape[0]
    nout = o_ref.shape[1]
    h = (_rms(x_ref[...]) * g_ref[...]) * (1.0 + sc_ref[...]) + sh_ref[...]
    hb = h.astype(BF16)
    if has_rope:
        lane = lax.broadcasted_iota(jnp.int32, (t, LANES), 1)
        first_half = (lane & 16) == 0
    for n0 in range(0, nout, nc):
        r = jnp.dot(hb, w_ref[:, n0:n0 + nc], preferred_element_type=F32)
        for j in range(nc // LANES):
            kind = kinds[n0 // LANES + j]
            rj = r[:, j * LANES:(j + 1) * LANES]
            if kind == ROPEQ:
                rj = _rope(rj, cq_ref[...], sq_ref[...], first_half)
            elif kind == ROPEK:
                rj = _rope(rj, ck_ref[...], sk_ref[...], first_half)
            elif kind == SCALEQ:
                rj = rj * Q_SCALE
            o_ref[:, n0 + j * LANES:n0 + (j + 1) * LANES] = rj.astype(BF16)


def _proj_call(x, gamma, mod, w, kinds, tables, *, tiles_per_seq, mod_row, name):
    n, d = x.shape
    nout = w.shape[1]
    t = T_PROJ
    has_rope = tables is not None

    def mrow(i):
        return mod_row if mod_row is not None else i // tiles_per_seq

    def tblock(i):
        return i % tiles_per_seq if mod_row is None else tiles_per_seq

    in_specs = [
        pl.BlockSpec((t, d), lambda i: (i, 0)),
        pl.BlockSpec((1, d), lambda i: (0, 0)),
        pl.BlockSpec((None, None, 1, d), lambda i: (mrow(i), 1, 0, 0)),
        pl.BlockSpec((None, None, 1, d), lambda i: (mrow(i), 0, 0, 0)),
        pl.BlockSpec((d, nout), lambda i: (0, 0)),
    ]
    args = [x, gamma.reshape(1, d), mod, mod, w]
    if has_rope:
        for tab in tables:
            in_specs.append(pl.BlockSpec((t, LANES), lambda i: (tblock(i), 0)))
            args.append(tab)
    return pl.pallas_call(
        functools.partial(_proj_kernel, kinds=tuple(kinds), nc=512, has_rope=has_rope),
        out_shape=jax.ShapeDtypeStruct((n, nout), BF16),
        grid=(n // t,),
        in_specs=in_specs,
        out_specs=pl.BlockSpec((t, nout), lambda i: (i, 0)),
        compiler_params=_cparams("parallel"),
        name=name,
    )(*args)


def _qk(q, k):
    return lax.dot_general(q, k, (((1,), (1,)), ((), ())), preferred_element_type=F32)


def _half_masks(shape):
    lane = lax.broadcasted_iota(jnp.int32, shape, 1)
    return lane < HEAD_DIM


def _split_heads(q, lo):
    qf = q.astype(F32)
    zero = jnp.zeros_like(qf)
    return jnp.where(lo, qf, zero).astype(q.dtype), jnp.where(lo, zero, qf).astype(q.dtype)


def _diff_lambda(lam_ref, lam_init):
    lv = lam_ref[...]
    d1 = jnp.sum(lv[0:1] * lv[1:2], axis=1, keepdims=True)
    d2 = jnp.sum(lv[2:3] * lv[3:4], axis=1, keepdims=True)
    return jnp.exp(d1) - jnp.exp(d2) + lam_init


def _diff_finish(o1, o2, lam, subln_ref, lam_init, o_ref):
    o = _rms(o1 - lam * o2) * subln_ref[...] * (1.0 - lam_init)
    o_ref[...] = o.astype(o_ref.dtype)


def _diff_kernel(lam_ref, subln_ref, q_ref, kl_ref, vl_ref, kc_ref, vc_ref, o_ref,
                 vt, s_scr, p_scr, m_scr, o_scr, *, lam_init, tq, kc):
    seq, nctx = q_ref.shape[0], kc_ref.shape[0]
    nkeys = vt.shape[1]
    nlat = seq // kc
    kcc = min(kc, nctx)
    nchunk = nlat + nctx // kcc
    ntile = seq // tq
    sub = m_scr.shape[1]
    lam = _diff_lambda(lam_ref, lam_init)
    lo = _half_masks((tq, LANES))

    def rows(c):
        if c < nlat:
            return slice(c * kc, (c + 1) * kc)
        return slice(seq + (c - nlat) * kcc, seq + (c - nlat + 1) * kcc)

    def chunk_of(lat_ref, ctx_ref, c):
        if c < nlat:
            return lat_ref[rows(c), :]
        return ctx_ref[(c - nlat) * kcc:(c - nlat + 1) * kcc, :]

    eye = (lax.broadcasted_iota(jnp.int32, (LANES, LANES), 0)
           == lax.broadcasted_iota(jnp.int32, (LANES, LANES), 1)).astype(F32).astype(BF16)
    for c in range(nchunk):
        vt[0:LANES, rows(c)] = _qk(eye, chunk_of(vl_ref, vc_ref, c)).astype(BF16)
    vt[LANES:, :] = jnp.ones((vt.shape[0] - LANES, nkeys), BF16)

    def half(m, tile, do_qk, do_exp, do_pv):
        if do_qk:
            q = q_ref[pl.ds(pl.multiple_of(tile * tq, tq), tq), :].astype(F32)
            zero = jnp.zeros_like(q)
            qm = (jnp.where(lo, q, zero) if m == 0 else jnp.where(lo, zero, q)).astype(BF16)
            mrun = jnp.full((sub, tq), NEG_INF, F32)
        if do_exp:
            mb = m_scr[1 - m, 0:1, :]
        if do_pv:
            acc = jnp.zeros((vt.shape[0], tq), F32)
        for c in range(nchunk):
            if do_qk:
                s = _qk(chunk_of(kl_ref, kc_ref, c), qm)
                s_scr[m, rows(c), :] = s
                for u in range(s.shape[0] // sub):
                    mrun = jnp.maximum(mrun, s[u * sub:(u + 1) * sub, :])
            if do_pv:
                acc = acc + jnp.dot(vt[:, rows(c)], p_scr[1 - m, rows(c), :], preferred_element_type=F32)
            if do_exp:
                p_scr[1 - m, rows(c), :] = jnp.exp2(s_scr[1 - m, rows(c), :] - mb).astype(BF16)
        if do_qk:
            m_scr[m] = jnp.broadcast_to(jnp.max(mrun, axis=0, keepdims=True), (sub, tq))
        if do_pv:
            return acc[:LANES] * (1.0 / acc[LANES:LANES + 1])
        return None

    def emit(tile, o2):
        o = _rms((o_scr[...] - lam * o2).T) * subln_ref[...] * (1.0 - lam_init)
        o_ref[pl.ds(pl.multiple_of(tile * tq, tq), tq), :] = o.astype(o_ref.dtype)

    half(0, 0, True, False, False)
    half(1, 0, True, True, False)
    half(0, 1, True, True, False)
    o_scr[...] = half(1, 1, True, True, True)

    def body(t, carry):
        emit(t - 2, half(0, t, True, True, True))
        o_scr[...] = half(1, t, True, True, True)
        return carry

    lax.fori_loop(2, ntile, body, 0)
    emit(ntile - 2, half(0, ntile, False, True, True))
    o_scr[...] = half(1, ntile, False, False, True)
    emit(ntile - 1, half(0, ntile + 1, False, False, True))


def _diff_ctx_kernel(lam_ref, subln_ref, q_ref, kc_ref, vc_ref, o_ref, *, lam_init):
    tq = q_ref.shape[0]
    lam = _diff_lambda(lam_ref, lam_init)
    lo = _half_masks((tq, LANES))
    for h in range(q_ref.shape[1] // LANES):
        cols = slice(h * LANES, (h + 1) * LANES)
        qs = _split_heads(q_ref[:, cols], lo)
        o1, o2 = [_softmax_pv([_qk(qm, kc_ref[:, cols])], [vc_ref[:, cols]]) for qm in qs]
        _diff_finish(o1, o2, lam, subln_ref, lam_init, o_ref.at[:, cols])


def _diff_attention(qkv, qkvc, lam_vecs, subln, lam_init, batch, seq, nctx, need_ctx):
    d = qkv.shape[1] // 3
    nh = d // LANES
    tq = DIFF_TQ
    kc = DIFF_KC
    assert seq % tq == 0 and seq % kc == 0 and nctx % min(kc, nctx) == 0 and seq // tq >= 2
    nkeys = seq + nctx
    subln2 = subln.reshape(1, LANES)
    y = pl.pallas_call(
        functools.partial(_diff_kernel, lam_init=lam_init, tq=tq, kc=kc),
        out_shape=jax.ShapeDtypeStruct((batch * seq, d), BF16),
        grid=(batch, nh),
        in_specs=[
            pl.BlockSpec((4, HEAD_DIM), lambda b, h: (0, 0)),
            pl.BlockSpec((1, LANES), lambda b, h: (0, 0)),
            pl.BlockSpec((seq, LANES), lambda b, h: (b, h)),
            pl.BlockSpec((seq, LANES), lambda b, h: (b, nh + h)),
            pl.BlockSpec((seq, LANES), lambda b, h: (b, 2 * nh + h)),
            pl.BlockSpec((nctx, LANES), lambda b, h: (b, nh + h)),
            pl.BlockSpec((nctx, LANES), lambda b, h: (b, 2 * nh + h)),
        ],
        out_specs=pl.BlockSpec((seq, LANES), lambda b, h: (b, h)),
        scratch_shapes=[pltpu.VMEM((LANES + BF16_ROWS, nkeys), BF16), pltpu.VMEM((2, nkeys, tq), F32),
                        pltpu.VMEM((2, nkeys, tq), BF16), pltpu.VMEM((2, F32_ROWS, tq), F32),
                        pltpu.VMEM((LANES, tq), F32)],
        compiler_params=_cparams("parallel", "parallel"),
        name="diff_attn",
    )(lam_vecs, subln2, qkv, qkv, qkv, qkvc, qkvc)
    yc = None
    if need_ctx:
        yc = pl.pallas_call(
            functools.partial(_diff_ctx_kernel, lam_init=lam_init),
            out_shape=jax.ShapeDtypeStruct((batch * nctx, d), BF16),
            grid=(batch,),
            in_specs=[
                pl.BlockSpec((4, HEAD_DIM), lambda b: (0, 0)),
                pl.BlockSpec((1, LANES), lambda b: (0, 0)),
                pl.BlockSpec((nctx, d), lambda b: (b, 0)),
                pl.BlockSpec((nctx, d), lambda b: (b, 1)),
                pl.BlockSpec((nctx, d), lambda b: (b, 2)),
            ],
            out_specs=pl.BlockSpec((nctx, d), lambda b: (b, 0)),
            compiler_params=_cparams("parallel"),
            name="diff_attn_ctx",
        )(lam_vecs, subln2, qkvc, qkvc, qkvc)
    return y, yc


def _softmax_pv(scores, values, extra_logit=None):
    m = jnp.max(scores[0], axis=1, keepdims=True)
    for s in scores[1:]:
        m = jnp.maximum(m, jnp.max(s, axis=1, keepdims=True))
    if extra_logit is not None:
        m = jnp.maximum(m, extra_logit)
    l = None
    acc = None
    for s, v in zip(scores, values):
        p = jnp.exp2(s - m)
        ps = jnp.sum(p, axis=1, keepdims=True)
        pv = jnp.dot(p.astype(BF16), v, preferred_element_type=F32)
        l = ps if l is None else l + ps
        acc = pv if acc is None else acc + pv
    if extra_logit is not None:
        l = l + jnp.exp2(extra_logit - m)
    return acc * (1.0 / l)


def _identity(n):
    return (lax.broadcasted_iota(jnp.int32, (n, n), 0)
            == lax.broadcasted_iota(jnp.int32, (n, n), 1)).astype(F32).astype(BF16)


def _fill_vt(vt, head0, chunk, v_block, n_heads):
    tr = _qk(_identity(LANES), v_block).astype(BF16)
    for hh in range(n_heads):
        vt[head0 + hh, chunk, 0:HEAD_DIM, :] = tr[hh * HEAD_DIM:(hh + 1) * HEAD_DIM]
        vt[head0 + hh, chunk, HEAD_DIM:, :] = jnp.ones((vt.shape[2] - HEAD_DIM, vt.shape[3]), BF16)


def _window_pipeline(q_ref, o_ref, s_scr, p_scr, m_scr, e_scr, o_scr, *, tq, kc, nchunk,
                     scores, values, extra_logit):
    ntile = q_ref.shape[0] // tq
    sub = m_scr.shape[1]
    lo = _half_masks((tq, LANES))
    rows = lambda c: slice(c * kc, (c + 1) * kc)
    tile_rows = lambda t: pl.ds(pl.multiple_of(t * tq, tq), tq)

    def half(j, t, do_qk, do_exp, do_pv):
        jo = 1 - j
        tv = t - 2 + j
        if do_qk:
            q = q_ref[tile_rows(t), :].astype(F32)
            zero = jnp.zeros_like(q)
            qm = (jnp.where(lo, q, zero) if j == 0 else jnp.where(lo, zero, q)).astype(BF16)
            mrun = jnp.full((sub, tq), NEG_INF, F32)
        if do_exp:
            mb = m_scr[jo, 0:1, :]
        if do_pv:
            acc = jnp.zeros((o_scr.shape[0] + BF16_ROWS, tq), F32)
        for c in range(nchunk):
            if do_qk:
                s = scores(qm, t, j, c)
                s_scr[j, rows(c), :] = s
                for u in range(kc // sub):
                    mrun = jnp.maximum(mrun, s[u * sub:(u + 1) * sub, :])
            if do_pv:
                acc = acc + jnp.dot(values(tv, jo, c), p_scr[jo, rows(c), :], preferred_element_type=F32)
            if do_exp:
                p_scr[jo, rows(c), :] = jnp.exp2(s_scr[jo, rows(c), :] - mb).astype(BF16)
        if do_qk:
            m = jnp.max(mrun, axis=0, keepdims=True)
            if extra_logit is not None:
                m = jnp.maximum(m, extra_logit(j))
            m_scr[j] = jnp.broadcast_to(m, (sub, tq))
        o = None
        if do_pv:
            l = acc[HEAD_DIM:HEAD_DIM + 1]
            if extra_logit is not None:
                l = l + e_scr[jo, 0:1, :]
            o = acc[:HEAD_DIM] * (1.0 / l)
        if do_exp and extra_logit is not None:
            e_scr[jo] = jnp.exp2(extra_logit(jo) - m_scr[jo])
        return o

    def emit(t, o1):
        o = jnp.concatenate([o_scr[...], o1], axis=0).T
        o_ref[tile_rows(t), :] = o.astype(o_ref.dtype)

    half(0, 0, True, False, False)
    half(1, 0, True, True, False)
    half(0, 1, True, True, False)
    o_scr[...] = half(1, 1, True, True, True)

    for t in range(2, ntile):
        emit(t - 2, half(0, t, True, True, True))
        o_scr[...] = half(1, t, True, True, True)
    emit(ntile - 2, half(0, ntile, False, True, True))
    o_scr[...] = half(1, ntile, False, False, True)
    emit(ntile - 1, half(0, ntile + 1, False, False, True))


def _tile_class(t, ntile):
    return jnp.where(t == 0, 0, jnp.where(t == ntile - 1, 2, 1))


def _na_kernel(bias_ref, q_ref, k_ref, v_ref, kc_ref, vc_ref, o_ref, vt, s_scr, p_scr, m_scr, o_scr,
               *, tq, kc):
    seq, nctx = q_ref.shape[0], kc_ref.shape[0]
    ntile, nlat, nwin = seq // tq, seq // kc, 3 * tq // kc
    rows = lambda c: slice(c * kc, (c + 1) * kc)
    for c in range(nlat):
        _fill_vt(vt, 0, c, v_ref[rows(c), :], 2)
    for c in range(nctx // kc):
        _fill_vt(vt, 0, nlat + c, vc_ref[rows(c), :], 2)

    def win_start(t):
        return pl.multiple_of(jnp.clip((t - 1) * tq, 0, seq - 3 * tq), tq)

    def scores(qm, t, j, c):
        if c >= nwin:
            return _qk(kc_ref[rows(c - nwin), :], qm)
        k = k_ref[pl.ds(win_start(t) + c * kc, kc), :]
        return _qk(k, qm) + bias_ref[_tile_class(t, ntile), j, rows(c), :]

    def values(t, j, c):
        return vt[j, nlat + c - nwin] if c >= nwin else vt[j, win_start(t) // kc + c]

    _window_pipeline(q_ref, o_ref, s_scr, p_scr, m_scr, None, o_scr, tq=tq, kc=kc,
                     nchunk=nwin + nctx // kc, scores=scores, values=values, extra_logit=None)


def _plain_ctx_kernel(q_ref, kc_ref, vc_ref, o_ref):
    tq = q_ref.shape[0]
    lo = _half_masks((tq, LANES))
    for hb in range(q_ref.shape[1] // LANES):
        cols = slice(hb * LANES, (hb + 1) * LANES)
        qs = _split_heads(q_ref[:, cols], lo)
        outs = [_softmax_pv([_qk(qs[hh], kc_ref[:, cols])], [vc_ref[:, cols]]) for hh in range(2)]
        o_ref[:, cols] = jnp.where(lo, outs[0], outs[1]).astype(o_ref.dtype)


def _na_bias_tables(rpb, rows):
    w = GRID_W
    nk = 3 * NA_ROWS
    nh = rpb.shape[0]
    edge = w - NA_KW
    ext = jnp.concatenate([jnp.repeat(rpb[..., :1], edge, axis=-1), rpb,
                           jnp.repeat(rpb[..., -1:], edge + 1, axis=-1)], axis=-1)
    flat = jnp.tile(ext, (1, 1, w + 1))[..., :w * (2 * w + 1)]
    hankel = flat.reshape(nh, -1, w, 2 * w + 1)[..., :w]
    toep = jnp.flip(hankel, axis=-2)
    cq = np.arange(w)[:, None]
    cx = np.arange(w)[None, :]
    col_start = np.clip(cq - NA_KW // 2, 0, w - NA_KW)
    col_ok = (cx >= col_start) & (cx < col_start + NA_KW)
    toep = jnp.where(col_ok, toep * LOG2E, NEG_INF)
    masked = jnp.full((nh, w, w), NEG_INF, F32)
    tabs = []
    for r0, k0 in ((0, 0), (NA_ROWS, 0), (rows - NA_ROWS, rows - nk)):
        row_blocks = []
        for a in range(NA_ROWS):
            r = r0 + a
            rs = min(max(r - NA_KH // 2, 0), rows - NA_KH)
            blocks = []
            for e in range(nk):
                ry = k0 + e
                blocks.append(toep[:, ry - r + NA_KH - 1] if rs <= ry < rs + NA_KH else masked)
            row_blocks.append(jnp.concatenate(blocks, axis=-1))
        tabs.append(jnp.concatenate(row_blocks, axis=-2))
    return jnp.swapaxes(jnp.stack(tabs), -1, -2).astype(F32)


def _win_scratch(nheads, nchunks, nkeys_item, tq, kc, with_extra):
    shapes = [pltpu.VMEM((nheads, nchunks, HEAD_DIM + BF16_ROWS, kc), BF16),
              pltpu.VMEM((2, nkeys_item, tq), F32), pltpu.VMEM((2, nkeys_item, tq), BF16),
              pltpu.VMEM((2, F32_ROWS, tq), F32)]
    if with_extra:
        shapes.append(pltpu.VMEM((2, F32_ROWS, tq), F32))
    shapes.append(pltpu.VMEM((HEAD_DIM, tq), F32))
    return shapes


def _na_attention(qkv, qkvc, rpb, batch, seq, nctx, need_ctx):
    d = qkv.shape[1] // 3
    nb = d // LANES
    rows = seq // GRID_W
    tq = NA_ROWS * GRID_W
    kc = tq
    assert seq // tq >= 3 and nctx % kc == 0
    bias = _na_bias_tables(rpb.astype(F32), rows)
    y = pl.pallas_call(
        functools.partial(_na_kernel, tq=tq, kc=kc),
        out_shape=jax.ShapeDtypeStruct((batch * seq, d), BF16),
        grid=(nb, batch),
        in_specs=[
            pl.BlockSpec((3, 2, 3 * tq, tq), lambda hb, b: (0, hb, 0, 0), pipeline_mode=pl.Buffered(1)),
            pl.BlockSpec((seq, LANES), lambda hb, b: (b, hb)),
            pl.BlockSpec((seq, LANES), lambda hb, b: (b, nb + hb)),
            pl.BlockSpec((seq, LANES), lambda hb, b: (b, 2 * nb + hb)),
            pl.BlockSpec((nctx, LANES), lambda hb, b: (b, nb + hb)),
            pl.BlockSpec((nctx, LANES), lambda hb, b: (b, 2 * nb + hb)),
        ],
        out_specs=pl.BlockSpec((seq, LANES), lambda hb, b: (b, hb)),
        scratch_shapes=_win_scratch(2, (seq + nctx) // kc, 3 * tq + nctx, tq, kc, False),
        compiler_params=_cparams("parallel", "arbitrary"),
        name="na_attn",
    )(bias, qkv, qkv, qkv, qkvc, qkvc)
    yc = None
    if need_ctx:
        yc = pl.pallas_call(
            _plain_ctx_kernel,
            out_shape=jax.ShapeDtypeStruct((batch * nctx, d), BF16),
            grid=(batch,),
            in_specs=[
                pl.BlockSpec((nctx, d), lambda b: (b, 0)),
                pl.BlockSpec((nctx, d), lambda b: (b, 1)),
                pl.BlockSpec((nctx, d), lambda b: (b, 2)),
            ],
            out_specs=pl.BlockSpec((nctx, d), lambda b: (b, 0)),
            compiler_params=_cparams("parallel"),
            name="na_attn_ctx",
        )(qkvc, qkvc, qkvc)
    return y, yc


def _swa_heads(q_ref, sinks_ref, g, score_fn, values, o_ref):
    tq = q_ref.shape[0]
    lo = _half_masks((tq, LANES))
    for jb in range(2):
        qs = _split_heads(q_ref[:, jb * LANES:(jb + 1) * LANES], lo)
        outs = []
        for hh in range(2):
            sink = jnp.full((1, 1), sinks_ref[g * 4 + jb * 2 + hh] * LOG2E, F32)
            outs.append(_softmax_pv(score_fn(qs[hh]), values, extra_logit=sink))
        o_ref[:, jb * LANES:(jb + 1) * LANES] = jnp.where(lo, outs[0], outs[1]).astype(o_ref.dtype)


def _swa_kernel(sinks_ref, band_ref, q_ref, k_ref, v_ref, kc_ref, vc_ref, o_ref,
                vt, s_scr, p_scr, m_scr, e_scr, o_scr, *, tq, kc, span):
    head0 = (pl.program_id(1) * 2 + pl.program_id(2)) * 2
    seq, nctx = q_ref.shape[0], kc_ref.shape[0]
    ntile, nlat, nwin = seq // tq, seq // kc, span // kc
    rows = lambda c: slice(c * kc, (c + 1) * kc)
    for c in range(nlat):
        _fill_vt(vt, 0, c, v_ref[rows(c), :], 1)
    for c in range(nctx // kc):
        _fill_vt(vt, 0, nlat + c, vc_ref[rows(c), :], 1)

    def win_start(t):
        return pl.multiple_of(jnp.clip(t * tq - SWA_WINDOW, 0, seq - span), SWA_WINDOW)

    def scores(qm, t, j, c):
        if c >= nwin:
            return _qk(kc_ref[rows(c - nwin), :], qm)
        k = k_ref[pl.ds(win_start(t) + c * kc, kc), :]
        return _qk(k, qm) + band_ref[_tile_class(t, ntile), rows(c), :]

    def values(t, j, c):
        return vt[0, nlat + c - nwin] if c >= nwin else vt[0, win_start(t) // kc + c]

    def sink(j):
        return jnp.full((1, 1), sinks_ref[head0 + j] * LOG2E, F32)

    _window_pipeline(q_ref, o_ref, s_scr, p_scr, m_scr, e_scr, o_scr, tq=tq, kc=kc,
                     nchunk=nwin + nctx // kc, scores=scores, values=values, extra_logit=sink)


def _swa_band_tables(seq, tq, span):
    ntile = seq // tq
    tabs = []
    for t in (0, 1, ntile - 1):
        start = min(max(t * tq - SWA_WINDOW, 0), seq - span)
        kpos = start + np.arange(span)[:, None]
        qpos = t * tq + np.arange(tq)[None, :]
        tabs.append(np.where(np.abs(qpos - kpos) <= SWA_WINDOW, 0.0, NEG_INF))
    return jnp.asarray(np.stack(tabs), F32)


def _swa_ctx_kernel(sinks_ref, q_ref, kc_ref, vc_ref, o_ref):
    qb = 2 * LANES
    for g in range(kc_ref.shape[1] // LANES):
        kc = kc_ref[:, g * LANES:(g + 1) * LANES]
        vc = vc_ref[:, g * LANES:(g + 1) * LANES]
        cols = slice(g * qb, (g + 1) * qb)
        _swa_heads(q_ref.at[:, cols], sinks_ref, g, lambda qm, kc=kc: [_qk(qm, kc)], [vc], o_ref.at[:, cols])


def _swa_attention(qkv, qkvc, sinks, batch, seq, nctx, need_ctx, d):
    nkv = (qkv.shape[1] - d) // (2 * LANES)
    qb = d // nkv
    kcol = d // LANES
    tq = SWA_TQ
    kc = SWA_WINDOW
    span = tq + 2 * SWA_WINDOW
    assert qb == 2 * LANES and seq // tq >= 3 and nctx % kc == 0
    smem = pl.BlockSpec(memory_space=pltpu.SMEM)
    sinks = sinks.astype(F32)
    y = pl.pallas_call(
        functools.partial(_swa_kernel, tq=tq, kc=kc, span=span),
        out_shape=jax.ShapeDtypeStruct((batch * seq, d), BF16),
        grid=(batch, nkv, 2),
        in_specs=[
            smem,
            pl.BlockSpec((3, span, tq), lambda b, g, jb: (0, 0, 0), pipeline_mode=pl.Buffered(1)),
            pl.BlockSpec((seq, LANES), lambda b, g, jb: (b, 2 * g + jb)),
            pl.BlockSpec((seq, LANES), lambda b, g, jb: (b, kcol + g)),
            pl.BlockSpec((seq, LANES), lambda b, g, jb: (b, kcol + nkv + g)),
            pl.BlockSpec((nctx, LANES), lambda b, g, jb: (b, kcol + g)),
            pl.BlockSpec((nctx, LANES), lambda b, g, jb: (b, kcol + nkv + g)),
        ],
        out_specs=pl.BlockSpec((seq, LANES), lambda b, g, jb: (b, 2 * g + jb)),
        scratch_shapes=_win_scratch(1, (seq + nctx) // kc, span + nctx, tq, kc, True),
        compiler_params=_cparams("parallel", "parallel", "arbitrary"),
        name="swa_attn",
    )(sinks, _swa_band_tables(seq, tq, span), qkv, qkv, qkv, qkvc, qkvc)
    yc = None
    if need_ctx:
        yc = pl.pallas_call(
            _swa_ctx_kernel,
            out_shape=jax.ShapeDtypeStruct((batch * nctx, d), BF16),
            grid=(batch,),
            in_specs=[
                smem,
                pl.BlockSpec((nctx, d), lambda b: (b, 0)),
                pl.BlockSpec((nctx, nkv * LANES), lambda b: (b, d // (nkv * LANES))),
                pl.BlockSpec((nctx, nkv * LANES), lambda b: (b, d // (nkv * LANES) + 1)),
            ],
            out_specs=pl.BlockSpec((nctx, d), lambda b: (b, 0)),
            compiler_params=_cparams("parallel"),
            name="swa_attn_ctx",
        )(sinks, qkvc, qkvc, qkvc)
    return y, yc


def _oproj_kernel(o_ref, wo_ref, x_ref, g1_ref, nf_ref, sc2_ref, sh2_ref, xo_ref, h2_ref):
    y = jnp.dot(o_ref[...], wo_ref[...], preferred_element_type=F32)
    xn = x_ref[...] + g1_ref[...] * y
    xo_ref[...] = xn
    h2 = (_rms(xn) * nf_ref[...]) * (1.0 + sc2_ref[...]) + sh2_ref[...]
    h2_ref[...] = h2.astype(BF16)


def _oproj_call(o, wo, x, gamma, mod, *, tiles_per_seq, mod_row, name):
    n, d = x.shape
    t = T_PROJ

    def mrow(i):
        return mod_row if mod_row is not None else i // tiles_per_seq

    def mspec(which):
        return pl.BlockSpec((None, None, 1, d), lambda i: (mrow(i), which, 0, 0))

    return pl.pallas_call(
        _oproj_kernel,
        out_shape=(jax.ShapeDtypeStruct((n, d), F32), jax.ShapeDtypeStruct((n, d), BF16)),
        grid=(n // t,),
        in_specs=[
            pl.BlockSpec((t, d), lambda i: (i, 0)),
            pl.BlockSpec((d, d), lambda i: (0, 0)),
            pl.BlockSpec((t, d), lambda i: (i, 0)),
            mspec(2),
            pl.BlockSpec((1, d), lambda i: (0, 0)),
            mspec(4),
            mspec(3),
        ],
        out_specs=(pl.BlockSpec((t, d), lambda i: (i, 0)), pl.BlockSpec((t, d), lambda i: (i, 0))),
        compiler_params=_cparams("parallel"),
        name=name,
    )(o, wo, x, mod, gamma.reshape(1, d), mod, mod)


def _ffn_kernel(h_ref, hp_ref, hn_ref, wup_ref, cw_ref, cb_ref, wd_ref, x_ref, g2_ref, no_ref, o_ref,
                lhs, act, ubuf, *, nseq, final, f):
    i = pl.program_id(0)
    t = h_ref.shape[0]
    halo = hp_ref.shape[0]
    dff = wd_ref.shape[0]
    rows = t + 2 * halo

    keep_prev = ((i % nseq) != 0).astype(F32)
    keep_next = ((i % nseq) != nseq - 1).astype(F32)
    lhs[0:halo, :] = (hp_ref[...].astype(F32) * keep_prev).astype(BF16)
    lhs[halo:halo + t, :] = h_ref[...]
    lhs[halo + t:, :] = (hn_ref[...].astype(F32) * keep_next).astype(BF16)

    def conv(col0, slot):
        u = jnp.dot(lhs[...], wup_ref[:, col0:col0 + f], preferred_element_type=F32)
        outs = []
        for s in range(f // LANES):
            sl = slice(col0 + s * LANES, col0 + (s + 1) * LANES)
            buf = ubuf.at[slot, s]
            buf[pl.ds(0, rows, stride=2), :] = u[:, s * LANES:(s + 1) * LANES]
            taps = [buf[pl.ds(2 * (halo - 1 + k), t, stride=2), :] for k in range(CONV_W)]
            outs.append(cb_ref[:, sl] + taps[0] * cw_ref[0:1, sl] + taps[1] * cw_ref[1:2, sl]
                        + taps[2] * cw_ref[2:3, sl])
        return jnp.concatenate(outs, axis=1)

    nch = dff // f
    cuts = [0] + [(nch * k) // DOWN_PIECES for k in range(1, DOWN_PIECES + 1)]
    y = None
    for c in range(nch):
        slot = 2 * (c % 2)
        a = conv(c * f, slot)
        g = conv(dff + c * f, slot + 1)
        act[:, c * f:(c + 1) * f] = ((g * (1.0 / (1.0 + jnp.exp(-g)))) * a).astype(BF16)
        if c + 1 in cuts:
            lo_c = cuts[cuts.index(c + 1) - 1] * f
            part = jnp.dot(act[:, lo_c:(c + 1) * f], wd_ref[lo_c:(c + 1) * f, :], preferred_element_type=F32)
            y = part if y is None else y + part

    xn = x_ref[...] + g2_ref[...] * y
    if final:
        xn = _rms(xn) * no_ref[...]
    o_ref[...] = xn


def _ffn_call(h2, x, w_up, w_conv, b_conv, w_down, mod, norm_out, *, t, tiles_per_seq, mod_row, final, name):
    n, d = x.shape
    dff = w_down.shape[0]
    f = FF_CHUNK
    halo = BF16_ROWS
    hb = t // halo
    last_hblock = n // halo - 1
    resident = pl.Buffered(1)

    def mrow(i):
        return mod_row if mod_row is not None else i // tiles_per_seq

    return pl.pallas_call(
        functools.partial(_ffn_kernel, nseq=tiles_per_seq, final=final, f=f),
        out_shape=jax.ShapeDtypeStruct((n, d), F32),
        grid=(n // t,),
        in_specs=[
            pl.BlockSpec((t, d), lambda i: (i, 0)),
            pl.BlockSpec((halo, d), lambda i: (jnp.maximum(i * hb - 1, 0), 0)),
            pl.BlockSpec((halo, d), lambda i: (jnp.minimum((i + 1) * hb, last_hblock), 0)),
            pl.BlockSpec((d, 2 * dff), lambda i: (0, 0), pipeline_mode=resident),
            pl.BlockSpec((CONV_W, 2 * dff), lambda i: (0, 0), pipeline_mode=resident),
            pl.BlockSpec((1, 2 * dff), lambda i: (0, 0), pipeline_mode=resident),
            pl.BlockSpec((dff, d), lambda i: (0, 0), pipeline_mode=resident),
            pl.BlockSpec((t, d), lambda i: (i, 0)),
            pl.BlockSpec((None, None, 1, d), lambda i: (mrow(i), 5, 0, 0)),
            pl.BlockSpec((1, d), lambda i: (0, 0)),
        ],
        out_specs=pl.BlockSpec((t, d), lambda i: (i, 0)),
        scratch_shapes=[pltpu.VMEM((t + 2 * halo, d), BF16), pltpu.VMEM((t, dff), BF16),
                        pltpu.VMEM((4, f // LANES, 2 * (t + 2 * halo), LANES), F32)],
        compiler_params=_cparams("parallel"),
        name=name,
    )(h2, h2, h2, w_up, w_conv, b_conv.reshape(1, 2 * dff), w_down, x, mod, norm_out.reshape(1, d))


def _rope_tables(seq, extra_rows):
    tpos = jnp.arange(seq, dtype=jnp.int32)
    row = (tpos // GRID_W).astype(F32)
    col = (tpos % GRID_W).astype(F32)
    inv = ROPE_BASE ** (-jnp.arange(ROPE_PAIRS, dtype=F32) / ROPE_PAIRS)
    ar = row[:, None] * inv[None, :]
    ac = col[:, None] * inv[None, :]
    cr, sr, cc, sc = jnp.cos(ar), jnp.sin(ar), jnp.cos(ac), jnp.sin(ac)
    cos = jnp.concatenate([cr, cr, cc, cc] * (LANES // HEAD_DIM), axis=1)
    sin = jnp.concatenate([-sr, sr, -sc, sc] * (LANES // HEAD_DIM), axis=1)
    cos = jnp.concatenate([cos, jnp.ones((extra_rows, LANES), F32)], axis=0)
    sin = jnp.concatenate([sin, jnp.zeros((extra_rows, LANES), F32)], axis=0)
    return cos * Q_SCALE, sin * Q_SCALE, cos, sin


def _dup_kv_columns(w, d, nkv):
    q = w[:, :d]
    k = w[:, d:d + nkv * HEAD_DIM].reshape(-1, nkv, 1, HEAD_DIM)
    v = w[:, d + nkv * HEAD_DIM:].reshape(-1, nkv, 1, HEAD_DIM)
    kd = jnp.broadcast_to(k, (w.shape[0], nkv, 2, HEAD_DIM)).reshape(w.shape[0], -1)
    vd = jnp.broadcast_to(v, (w.shape[0], nkv, 2, HEAD_DIM)).reshape(w.shape[0], -1)
    return jnp.concatenate([q, kd, vd], axis=1)


def kernel(x, c, ctx, c_ctx, ada_w, ada_b, norm_mix, norm_ffn, norm_out, ffn_up, ffn_conv, ffn_conv_b,
           ffn_down, a_wqkv, a_wo, a_lambda, a_subln, b_wqkv, b_wo, b_rpb, c_wqkv, c_wo, c_sinks):
    batch, seq, d = x.shape
    nctx = ctx.shape[1]
    depth = ada_w.shape[0]
    assert seq % T_FFN == 0 and seq % T_PROJ == 0 and (batch * nctx) % T_PROJ == 0

    mod_rows = 16
    cs = jnp.concatenate([c, c_ctx[None, :], jnp.zeros((mod_rows - batch - 1, d), F32)], axis=0)
    mod_all = _ada_call(cs, ada_w, ada_b).reshape(depth, mod_rows, 6, 1, d)
    ctx_row = batch

    tables = _rope_tables(seq, T_PROJ)
    lat_tiles = seq // T_PROJ
    nd = d // LANES

    xl = x.reshape(batch * seq, d)
    xc = ctx.reshape(batch * nctx, d)
    for i in range(depth):
        need_ctx = i < depth - 1
        kind, j = i % N_MIXERS, i // N_MIXERS
        mod = mod_all[i]
        if kind == 0:
            w = a_wqkv[j].astype(BF16)
            wo = a_wo[j].astype(BF16)
            kinds = [ROPEQ] * nd + [ROPEK] * nd + [PLAIN] * nd
            tabs = tables
        elif kind == 1:
            w = b_wqkv[j].astype(BF16)
            wo = b_wo[j].astype(BF16)
            kinds = [SCALEQ] * nd + [PLAIN] * (2 * nd)
            tabs = None
        else:
            nkv = (c_wqkv.shape[2] - d) // (2 * HEAD_DIM)
            w = _dup_kv_columns(c_wqkv[j], d, nkv).astype(BF16)
            wo = c_wo[j].astype(BF16)
            kinds = [ROPEQ] * nd + [ROPEK] * nkv + [PLAIN] * nkv
            tabs = tables
        qkv = _proj_call(xl, norm_mix[i], mod, w, kinds, tabs, tiles_per_seq=lat_tiles, mod_row=None,
                         name=f"proj{i}")
        qkvc = _proj_call(xc, norm_mix[i], mod, w, kinds, tabs, tiles_per_seq=lat_tiles, mod_row=ctx_row,
                          name=f"proj_ctx{i}")
        if kind == 0:
            lam_init = 0.8 - 0.6 * math.exp(-0.3 * i)
            y, yc = _diff_attention(qkv, qkvc, a_lambda[j].astype(F32), a_subln[j], lam_init,
                                    batch, seq, nctx, need_ctx)
        elif kind == 1:
            y, yc = _na_attention(qkv, qkvc, b_rpb[j], batch, seq, nctx, need_ctx)
        else:
            y, yc = _swa_attention(qkv, qkvc, c_sinks[j], batch, seq, nctx, need_ctx, d)

        w_up = ffn_up[i].astype(BF16)
        w_down = ffn_down[i].astype(BF16)
        xl, h2 = _oproj_call(y, wo, xl, norm_ffn[i], mod, tiles_per_seq=lat_tiles, mod_row=None,
                             name=f"oproj{i}")
        xl = _ffn_call(h2, xl, w_up, ffn_conv[i], ffn_conv_b[i], w_down, mod, norm_out,
                       t=T_FFN, tiles_per_seq=seq // T_FFN, mod_row=None, final=not need_ctx,
                       name=f"ffn{i}")
        if need_ctx:
            xc, h2c = _oproj_call(yc, wo, xc, norm_ffn[i], mod, tiles_per_seq=1, mod_row=ctx_row,
                                  name=f"oproj_ctx{i}")
            xc = _ffn_call(h2c, xc, w_up, ffn_conv[i], ffn_conv_b[i], w_down, mod, norm_out,
                           t=nctx, tiles_per_seq=1, mod_row=ctx_row, final=False, name=f"ffn_ctx{i}")
    return xl.reshape(batch, seq, d)
```

```python
import functools
import math

import numpy as np
import jax
import jax.numpy as jnp
from jax import lax
from jax.experimental import pallas as pl
from jax.experimental.pallas import tpu as pltpu

F32 = jnp.float32
BF16 = jnp.bfloat16

GRID_W = 64
HEAD_DIM = 64
ROPE_BASE = 10000.0
ROPE_PAIRS = HEAD_DIM // 4
NORM_EPS = 1e-6
NEG_INF = -1e30
N_MIXERS = 3
NA_KH = 8
NA_KW = 16
SWA_WINDOW = 128
CONV_W = 3
LOG2E = 1.4426950408889634
Q_SCALE = (HEAD_DIM ** -0.5) * LOG2E

LANES = 128
BF16_ROWS = 16
F32_ROWS = 8
VMEM_LIMIT = 52 * 1024 * 1024

T_PROJ = 1024
T_FFN = 512
FF_CHUNK = 256
DOWN_PIECES = 6
NA_ROWS = 4
SWA_TQ = 256
DIFF_TQ = 256
DIFF_KC = 256

PLAIN, SCALEQ, ROPEQ, ROPEK = 0, 1, 2, 3


def _cparams(*sem):
    return pltpu.CompilerParams(dimension_semantics=sem, vmem_limit_bytes=VMEM_LIMIT)


def _rms(x):
    return x * lax.rsqrt(jnp.mean(x * x, axis=-1, keepdims=True) + NORM_EPS)


def _ada_kernel(cs_ref, w_ref, b_ref, o_ref):
    cs = cs_ref[...]
    s = cs * (1.0 / (1.0 + jnp.exp(-cs)))
    w = w_ref[...]
    s_hi = s.astype(BF16)
    s_lo = (s - s_hi.astype(F32)).astype(BF16)
    w_hi = w.astype(BF16)
    w_lo = (w - w_hi.astype(F32)).astype(BF16)
    acc = jnp.dot(s_hi, w_hi, preferred_element_type=F32)
    acc = acc + jnp.dot(s_hi, w_lo, preferred_element_type=F32)
    acc = acc + jnp.dot(s_lo, w_hi, preferred_element_type=F32)
    o_ref[...] = acc + b_ref[...]


def _ada_call(cs, ada_w, ada_b):
    depth, d, n = ada_w.shape
    nt = 1536
    rows = cs.shape[0]
    return pl.pallas_call(
        _ada_kernel,
        out_shape=jax.ShapeDtypeStruct((depth, rows, n), F32),
        grid=(depth, n // nt),
        in_specs=[
            pl.BlockSpec((rows, d), lambda l, j: (0, 0)),
            pl.BlockSpec((None, d, nt), lambda l, j: (l, 0, j)),
            pl.BlockSpec((None, 1, nt), lambda l, j: (l, 0, j)),
        ],
        out_specs=pl.BlockSpec((None, rows, nt), lambda l, j: (l, 0, j)),
        compiler_params=_cparams("parallel", "parallel"),
        name="ada_mod",
    )(cs, ada_w, ada_b.reshape(depth, 1, n))


def _rope(r, c, s, first_half):
    partner = jnp.where(first_half, pltpu.roll(r, LANES - 16, 1), pltpu.roll(r, 16, 1))
    return r * c + partner * s


def _proj_kernel(*refs, kinds, nc, has_rope):
    if has_rope:
        x_ref, g_ref, sc_ref, sh_ref, w_ref, cq_ref, sq_ref, ck_ref, sk_ref, o_ref = refs
    else:
        x_ref, g_ref, sc_ref, sh_ref, w_ref, o_ref = refs
    t = x_ref.shape[0]
    nout = o_ref.shape[1]
    h = (_rms(x_ref[...]) * g_ref[...]) * (1.0 + sc_ref[...]) + sh_ref[...]
    hb = h.astype(BF16)
    if has_rope:
        lane = lax.broadcasted_iota(jnp.int32, (t, LANES), 1)
        first_half = (lane & 16) == 0
    for n0 in range(0, nout, nc):
        r = jnp.dot(hb, w_ref[:, n0:n0 + nc], preferred_element_type=F32)
        for j in range(nc // LANES):
            kind = kinds[n0 // LANES + j]
            rj = r[:, j * LANES:(j + 1) * LANES]
            if kind == ROPEQ:
                rj = _rope(rj, cq_ref[...], sq_ref[...], first_half)
            elif kind == ROPEK:
                rj = _rope(rj, ck_ref[...], sk_ref[...], first_half)
            elif kind == SCALEQ:
                rj = rj * Q_SCALE
            o_ref[:, n0 + j * LANES:n0 + (j + 1) * LANES] = rj.astype(BF16)


def _proj_call(x, gamma, mod, w, kinds, tables, *, tiles_per_seq, mod_row, name):
    n, d = x.shape
    nout = w.shape[1]
    t = T_PROJ
    has_rope = tables is not None

    def mrow(i):
        return mod_row if mod_row is not None else i // tiles_per_seq

    def tblock(i):
        return i % tiles_per_seq if mod_row is None else tiles_per_seq

    in_specs = [
        pl.BlockSpec((t, d), lambda i: (i, 0)),
        pl.BlockSpec((1, d), lambda i: (0, 0)),
        pl.BlockSpec((None, None, 1, d), lambda i: (mrow(i), 1, 0, 0)),
        pl.BlockSpec((None, None, 1, d), lambda i: (mrow(i), 0, 0, 0)),
        pl.BlockSpec((d, nout), lambda i: (0, 0)),
    ]
    args = [x, gamma.reshape(1, d), mod, mod, w]
    if has_rope:
        for tab in tables:
            in_specs.append(pl.BlockSpec((t, LANES), lambda i: (tblock(i), 0)))
            args.append(tab)
    return pl.pallas_call(
        functools.partial(_proj_kernel, kinds=tuple(kinds), nc=512, has_rope=has_rope),
        out_shape=jax.ShapeDtypeStruct((n, nout), BF16),
        grid=(n // t,),
        in_specs=in_specs,
        out_specs=pl.BlockSpec((t, nout), lambda i: (i, 0)),
        compiler_params=_cparams("parallel"),
        name=name,
    )(*args)


def _qk(q, k):
    return lax.dot_general(q, k, (((1,), (1,)), ((), ())), preferred_element_type=F32)


def _half_masks(shape):
    lane = lax.broadcasted_iota(jnp.int32, shape, 1)
    return lane < HEAD_DIM


def _split_heads(q, lo):
    qf = q.astype(F32)
    zero = jnp.zeros_like(qf)
    return jnp.where(lo, qf, zero).astype(q.dtype), jnp.where(lo, zero, qf).astype(q.dtype)


def _diff_lambda(lam_ref, lam_init):
    lv = lam_ref[...]
    d1 = jnp.sum(lv[0:1] * lv[1:2], axis=1, keepdims=True)
    d2 = jnp.sum(lv[2:3] * lv[3:4], axis=1, keepdims=True)
    return jnp.exp(d1) - jnp.exp(d2) + lam_init


def _diff_finish(o1, o2, lam, subln_ref, lam_init, o_ref):
    o = _rms(o1 - lam * o2) * subln_ref[...] * (1.0 - lam_init)
    o_ref[...] = o.astype(o_ref.dtype)


def _diff_kernel(lam_ref, subln_ref, q_ref, kl_ref, vl_ref, kc_ref, vc_ref, o_ref,
                 vt, s_scr, p_scr, m_scr, o_scr, *, lam_init, tq, kc):
    seq, nctx = q_ref.shape[0], kc_ref.shape[0]
    nkeys = vt.shape[1]
    nlat = seq // kc
    kcc = min(kc, nctx)
    nchunk = nlat + nctx // kcc
    ntile = seq // tq
    sub = m_scr.shape[1]
    lam = _diff_lambda(lam_ref, lam_init)
    lo = _half_masks((tq, LANES))

    def rows(c):
        if c < nlat:
            return slice(c * kc, (c + 1) * kc)
        return slice(seq + (c - nlat) * kcc, seq + (c - nlat + 1) * kcc)

    def chunk_of(lat_ref, ctx_ref, c):
        if c < nlat:
            return lat_ref[rows(c), :]
        return ctx_ref[(c - nlat) * kcc:(c - nlat + 1) * kcc, :]

    eye = (lax.broadcasted_iota(jnp.int32, (LANES, LANES), 0)
           == lax.broadcasted_iota(jnp.int32, (LANES, LANES), 1)).astype(F32).astype(BF16)
    for c in range(nchunk):
        vt[0:LANES, rows(c)] = _qk(eye, chunk_of(vl_ref, vc_ref, c)).astype(BF16)
    vt[LANES:, :] = jnp.ones((vt.shape[0] - LANES, nkeys), BF16)

    def half(m, tile, do_qk, do_exp, do_pv):
        if do_qk:
            q = q_ref[pl.ds(pl.multiple_of(tile * tq, tq), tq), :].astype(F32)
            zero = jnp.zeros_like(q)
            qm = (jnp.where(lo, q, zero) if m == 0 else jnp.where(lo, zero, q)).astype(BF16)
            mrun = jnp.full((sub, tq), NEG_INF, F32)
        if do_exp:
            mb = m_scr[1 - m, 0:1, :]
        if do_pv:
            acc = jnp.zeros((vt.shape[0], tq), F32)
        for c in range(nchunk):
            if do_qk:
                s = _qk(chunk_of(kl_ref, kc_ref, c), qm)
                s_scr[m, rows(c), :] = s
                for u in range(s.shape[0] // sub):
                    mrun = jnp.maximum(mrun, s[u * sub:(u + 1) * sub, :])
            if do_pv:
                acc = acc + jnp.dot(vt[:, rows(c)], p_scr[1 - m, rows(c), :], preferred_element_type=F32)
            if do_exp:
                p_scr[1 - m, rows(c), :] = jnp.exp2(s_scr[1 - m, rows(c), :] - mb).astype(BF16)
        if do_qk:
            m_scr[m] = jnp.broadcast_to(jnp.max(mrun, axis=0, keepdims=True), (sub, tq))
        if do_pv:
            return acc[:LANES] * (1.0 / acc[LANES:LANES + 1])
        return None

    def emit(tile, o2):
        o = _rms((o_scr[...] - lam * o2).T) * subln_ref[...] * (1.0 - lam_init)
        o_ref[pl.ds(pl.multiple_of(tile * tq, tq), tq), :] = o.astype(o_ref.dtype)

    half(0, 0, True, False, False)
    half(1, 0, True, True, False)
    half(0, 1, True, True, False)
    o_scr[...] = half(1, 1, True, True, True)

    def body(t, carry):
        emit(t - 2, half(0, t, True, True, True))
        o_scr[...] = half(1, t, True, True, True)
        return carry

    lax.fori_loop(2, ntile, body, 0)
    emit(ntile - 2, half(0, ntile, False, True, True))
    o_scr[...] = half(1, ntile, False, False, True)
    emit(ntile - 1, half(0, ntile + 1, False, False, True))


def _diff_ctx_kernel(lam_ref, subln_ref, q_ref, kc_ref, vc_ref, o_ref, *, lam_init):
    tq = q_ref.shape[0]
    lam = _diff_lambda(lam_ref, lam_init)
    lo = _half_masks((tq, LANES))
    for h in range(q_ref.shape[1] // LANES):
        cols = slice(h * LANES, (h + 1) * LANES)
        qs = _split_heads(q_ref[:, cols], lo)
        o1, o2 = [_softmax_pv([_qk(qm, kc_ref[:, cols])], [vc_ref[:, cols]]) for qm in qs]
        _diff_finish(o1, o2, lam, subln_ref, lam_init, o_ref.at[:, cols])


def _diff_attention(qkv, qkvc, lam_vecs, subln, lam_init, batch, seq, nctx, need_ctx):
    d = qkv.shape[1] // 3
    nh = d // LANES
    tq = DIFF_TQ
    kc = DIFF_KC
    assert seq % tq == 0 and seq % kc == 0 and nctx % min(kc, nctx) == 0 and seq // tq >= 2
    nkeys = seq + nctx
    subln2 = subln.reshape(1, LANES)
    y = pl.pallas_call(
        functools.partial(_diff_kernel, lam_init=lam_init, tq=tq, kc=kc),
        out_shape=jax.ShapeDtypeStruct((batch * seq, d), BF16),
        grid=(batch, nh),
        in_specs=[
            pl.BlockSpec((4, HEAD_DIM), lambda b, h: (0, 0)),
            pl.BlockSpec((1, LANES), lambda b, h: (0, 0)),
            pl.BlockSpec((seq, LANES), lambda b, h: (b, h)),
            pl.BlockSpec((seq, LANES), lambda b, h: (b, nh + h)),
            pl.BlockSpec((seq, LANES), lambda b, h: (b, 2 * nh + h)),
            pl.BlockSpec((nctx, LANES), lambda b, h: (b, nh + h)),
            pl.BlockSpec((nctx, LANES), lambda b, h: (b, 2 * nh + h)),
        ],
        out_specs=pl.BlockSpec((seq, LANES), lambda b, h: (b, h)),
        scratch_shapes=[pltpu.VMEM((LANES + BF16_ROWS, nkeys), BF16), pltpu.VMEM((2, nkeys, tq), F32),
                        pltpu.VMEM((2, nkeys, tq), BF16), pltpu.VMEM((2, F32_ROWS, tq), F32),
                        pltpu.VMEM((LANES, tq), F32)],
        compiler_params=_cparams("parallel", "parallel"),
        name="diff_attn",
    )(lam_vecs, subln2, qkv, qkv, qkv, qkvc, qkvc)
    yc = None
    if need_ctx:
        yc = pl.pallas_call(
            functools.partial(_diff_ctx_kernel, lam_init=lam_init),
            out_shape=jax.ShapeDtypeStruct((batch * nctx, d), BF16),
            grid=(batch,),
            in_specs=[
                pl.BlockSpec((4, HEAD_DIM), lambda b: (0, 0)),
                pl.BlockSpec((1, LANES), lambda b: (0, 0)),
                pl.BlockSpec((nctx, d), lambda b: (b, 0)),
                pl.BlockSpec((nctx, d), lambda b: (b, 1)),
                pl.BlockSpec((nctx, d), lambda b: (b, 2)),
            ],
            out_specs=pl.BlockSpec((nctx, d), lambda b: (b, 0)),
            compiler_params=_cparams("parallel"),
            name="diff_attn_ctx",
        )(lam_vecs, subln2, qkvc, qkvc, qkvc)
    return y, yc


def _softmax_pv(scores, values, extra_logit=None):
    m = jnp.max(scores[0], axis=1, keepdims=True)
    for s in scores[1:]:
        m = jnp.maximum(m, jnp.max(s, axis=1, keepdims=True))
    if extra_logit is not None:
        m = jnp.maximum(m, extra_logit)
    l = None
    acc = None
    for s, v in zip(scores, values):
        p = jnp.exp2(s - m)
        ps = jnp.sum(p, axis=1, keepdims=True)
        pv = jnp.dot(p.astype(BF16), v, preferred_element_type=F32)
        l = ps if l is None else l + ps
        acc = pv if acc is None else acc + pv
    if extra_logit is not None:
        l = l + jnp.exp2(extra_logit - m)
    return acc * (1.0 / l)


def _identity(n):
    return (lax.broadcasted_iota(jnp.int32, (n, n), 0)
            == lax.broadcasted_iota(jnp.int32, (n, n), 1)).astype(F32).astype(BF16)


def _fill_vt(vt, head0, chunk, v_block, n_heads):
    tr = _qk(_identity(LANES), v_block).astype(BF16)
    for hh in range(n_heads):
        vt[head0 + hh, chunk, 0:HEAD_DIM, :] = tr[hh * HEAD_DIM:(hh + 1) * HEAD_DIM]
        vt[head0 + hh, chunk, HEAD_DIM:, :] = jnp.ones((vt.shape[2] - HEAD_DIM, vt.shape[3]), BF16)


def _window_pipeline(q_ref, o_ref, s_scr, p_scr, m_scr, e_scr, o_scr, *, tq, kc, nchunk,
                     scores, values, extra_logit):
    ntile = q_ref.shape[0] // tq
    sub = m_scr.shape[1]
    lo = _half_masks((tq, LANES))
    rows = lambda c: slice(c * kc, (c + 1) * kc)
    tile_rows = lambda t: pl.ds(pl.multiple_of(t * tq, tq), tq)

    def half(j, t, do_qk, do_exp, do_pv):
        jo = 1 - j
        tv = t - 2 + j
        if do_qk:
            q = q_ref[tile_rows(t), :].astype(F32)
            zero = jnp.zeros_like(q)
            qm = (jnp.where(lo, q, zero) if j == 0 else jnp.where(lo, zero, q)).astype(BF16)
            mrun = jnp.full((sub, tq), NEG_INF, F32)
        if do_exp:
            mb = m_scr[jo, 0:1, :]
        if do_pv:
            acc = jnp.zeros((o_scr.shape[0] + BF16_ROWS, tq), F32)
        for c in range(nchunk):
            if do_qk:
                s = scores(qm, t, j, c)
                s_scr[j, rows(c), :] = s
                for u in range(kc // sub):
                    mrun = jnp.maximum(mrun, s[u * sub:(u + 1) * sub, :])
            if do_pv:
                acc = acc + jnp.dot(values(tv, jo, c), p_scr[jo, rows(c), :], preferred_element_type=F32)
            if do_exp:
                p_scr[jo, rows(c), :] = jnp.exp2(s_scr[jo, rows(c), :] - mb).astype(BF16)
        if do_qk:
            m = jnp.max(mrun, axis=0, keepdims=True)
            if extra_logit is not None:
                m = jnp.maximum(m, extra_logit(j))
            m_scr[j] = jnp.broadcast_to(m, (sub, tq))
        o = None
        if do_pv:
            l = acc[HEAD_DIM:HEAD_DIM + 1]
            if extra_logit is not None:
                l = l + e_scr[jo, 0:1, :]
            o = acc[:HEAD_DIM] * (1.0 / l)
        if do_exp and extra_logit is not None:
            e_scr[jo] = jnp.exp2(extra_logit(jo) - m_scr[jo])
        return o

    def emit(t, o1):
        o = jnp.concatenate([o_scr[...], o1], axis=0).T
        o_ref[tile_rows(t), :] = o.astype(o_ref.dtype)

    half(0, 0, True, False, False)
    half(1, 0, True, True, False)
    half(0, 1, True, True, False)
    o_scr[...] = half(1, 1, True, True, True)

    for t in range(2, ntile):
        emit(t - 2, half(0, t, True, True, True))
        o_scr[...] = half(1, t, True, True, True)
    emit(ntile - 2, half(0, ntile, False, True, True))
    o_scr[...] = half(1, ntile, False, False, True)
    emit(ntile - 1, half(0, ntile + 1, False, False, True))


def _tile_class(t, ntile):
    return jnp.where(t == 0, 0, jnp.where(t == ntile - 1, 2, 1))


def _na_kernel(bias_ref, q_ref, k_ref, v_ref, kc_ref, vc_ref, o_ref, vt, s_scr, p_scr, m_scr, o_scr,
               *, tq, kc):
    seq, nctx = q_ref.shape[0], kc_ref.shape[0]
    ntile, nlat, nwin = seq // tq, seq // kc, 3 * tq // kc
    rows = lambda c: slice(c * kc, (c + 1) * kc)
    for c in range(nlat):
        _fill_vt(vt, 0, c, v_ref[rows(c), :], 2)
    for c in range(nctx // kc):
        _fill_vt(vt, 0, nlat + c, vc_ref[rows(c), :], 2)

    def win_start(t):
        return pl.multiple_of(jnp.clip((t - 1) * tq, 0, seq - 3 * tq), tq)

    def scores(qm, t, j, c):
        if c >= nwin:
            return _qk(kc_ref[rows(c - nwin), :], qm)
        k = k_ref[pl.ds(win_start(t) + c * kc, kc), :]
        return _qk(k, qm) + bias_ref[_tile_class(t, ntile), j, rows(c), :]

    def values(t, j, c):
        return vt[j, nlat + c - nwin] if c >= nwin else vt[j, win_start(t) // kc + c]

    _window_pipeline(q_ref, o_ref, s_scr, p_scr, m_scr, None, o_scr, tq=tq, kc=kc,
                     nchunk=nwin + nctx // kc, scores=scores, values=values, extra_logit=None)


def _plain_ctx_kernel(q_ref, kc_ref, vc_ref, o_ref):
    tq = q_ref.shape[0]
    lo = _half_masks((tq, LANES))
    for hb in range(q_ref.shape[1] // LANES):
        cols = slice(hb * LANES, (hb + 1) * LANES)
        qs = _split_heads(q_ref[:, cols], lo)
        outs = [_softmax_pv([_qk(qs[hh], kc_ref[:, cols])], [vc_ref[:, cols]]) for hh in range(2)]
        o_ref[:, cols] = jnp.where(lo, outs[0], outs[1]).astype(o_ref.dtype)


def _na_bias_tables(rpb, rows):
    w = GRID_W
    nk = 3 * NA_ROWS
    nh = rpb.shape[0]
    edge = w - NA_KW
    ext = jnp.concatenate([jnp.repeat(rpb[..., :1], edge, axis=-1), rpb,
                           jnp.repeat(rpb[..., -1:], edge + 1, axis=-1)], axis=-1)
    flat = jnp.tile(ext, (1, 1, w + 1))[..., :w * (2 * w + 1)]
    hankel = flat.reshape(nh, -1, w, 2 * w + 1)[..., :w]
    toep = jnp.flip(hankel, axis=-1)
    cx = np.arange(w)[:, None]
    cq = np.arange(w)[None, :]
    col_start = np.clip(cq - NA_KW // 2, 0, w - NA_KW)
    col_ok = (cx >= col_start) & (cx < col_start + NA_KW)
    toep = jnp.where(col_ok, toep * LOG2E, NEG_INF)
    masked = jnp.full((nh, w, w), NEG_INF, F32)
    tabs = []
    for r0, k0 in ((0, 0), (NA_ROWS, 0), (rows - NA_ROWS, rows - nk)):
        key_blocks = []
        for e in range(nk):
            ry = k0 + e
            blocks = []
            for a in range(NA_ROWS):
                r = r0 + a
                rs = min(max(r - NA_KH // 2, 0), rows - NA_KH)
                blocks.append(toep[:, ry - r + NA_KH - 1] if rs <= ry < rs + NA_KH else masked)
            key_blocks.append(jnp.concatenate(blocks, axis=-1))
        tabs.append(jnp.concatenate(key_blocks, axis=-2))
    return jnp.stack(tabs).astype(F32)


def _win_scratch(nheads, nchunks, nkeys_item, tq, kc, with_extra):
    shapes = [pltpu.VMEM((nheads, nchunks, HEAD_DIM + BF16_ROWS, kc), BF16),
              pltpu.VMEM((2, nkeys_item, tq), F32), pltpu.VMEM((2, nkeys_item, tq), BF16),
              pltpu.VMEM((2, F32_ROWS, tq), F32)]
    if with_extra:
        shapes.append(pltpu.VMEM((2, F32_ROWS, tq), F32))
    shapes.append(pltpu.VMEM((HEAD_DIM, tq), F32))
    return shapes


def _na_attention(qkv, qkvc, rpb, batch, seq, nctx, need_ctx):
    d = qkv.shape[1] // 3
    nb = d // LANES
    rows = seq // GRID_W
    tq = NA_ROWS * GRID_W
    kc = tq
    assert seq // tq >= 3 and nctx % kc == 0
    bias = _na_bias_tables(rpb.astype(F32), rows)
    y = pl.pallas_call(
        functools.partial(_na_kernel, tq=tq, kc=kc),
        out_shape=jax.ShapeDtypeStruct((batch * seq, d), BF16),
        grid=(nb, batch),
        in_specs=[
            pl.BlockSpec((3, 2, 3 * tq, tq), lambda hb, b: (0, hb, 0, 0), pipeline_mode=pl.Buffered(1)),
            pl.BlockSpec((seq, LANES), lambda hb, b: (b, hb)),
            pl.BlockSpec((seq, LANES), lambda hb, b: (b, nb + hb)),
            pl.BlockSpec((seq, LANES), lambda hb, b: (b, 2 * nb + hb)),
            pl.BlockSpec((nctx, LANES), lambda hb, b: (b, nb + hb)),
            pl.BlockSpec((nctx, LANES), lambda hb, b: (b, 2 * nb + hb)),
        ],
        out_specs=pl.BlockSpec((seq, LANES), lambda hb, b: (b, hb)),
        scratch_shapes=_win_scratch(2, (seq + nctx) // kc, 3 * tq + nctx, tq, kc, False),
        compiler_params=_cparams("parallel", "arbitrary"),
        name="na_attn",
    )(bias, qkv, qkv, qkv, qkvc, qkvc)
    yc = None
    if need_ctx:
        yc = pl.pallas_call(
            _plain_ctx_kernel,
            out_shape=jax.ShapeDtypeStruct((batch * nctx, d), BF16),
            grid=(batch,),
            in_specs=[
                pl.BlockSpec((nctx, d), lambda b: (b, 0)),
                pl.BlockSpec((nctx, d), lambda b: (b, 1)),
                pl.BlockSpec((nctx, d), lambda b: (b, 2)),
            ],
            out_specs=pl.BlockSpec((nctx, d), lambda b: (b, 0)),
            compiler_params=_cparams("parallel"),
            name="na_attn_ctx",
        )(qkvc, qkvc, qkvc)
    return y, yc


def _swa_heads(q_ref, sinks_ref, g, score_fn, values, o_ref):
    tq = q_ref.shape[0]
    lo = _half_masks((tq, LANES))
    for jb in range(2):
        qs = _split_heads(q_ref[:, jb * LANES:(jb + 1) * LANES], lo)
        outs = []
        for hh in range(2):
            sink = jnp.full((1, 1), sinks_ref[g * 4 + jb * 2 + hh] * LOG2E, F32)
            outs.append(_softmax_pv(score_fn(qs[hh]), values, extra_logit=sink))
        o_ref[:, jb * LANES:(jb + 1) * LANES] = jnp.where(lo, outs[0], outs[1]).astype(o_ref.dtype)


def _swa_kernel(sinks_ref, band_ref, q_ref, k_ref, v_ref, kc_ref, vc_ref, o_ref,
                vt, s_scr, p_scr, m_scr, e_scr, o_scr, *, tq, kc, span):
    head0 = (pl.program_id(1) * 2 + pl.program_id(2)) * 2
    seq, nctx = q_ref.shape[0], kc_ref.shape[0]
    ntile, nlat, nwin = seq // tq, seq // kc, span // kc
    rows = lambda c: slice(c * kc, (c + 1) * kc)
    for c in range(nlat):
        _fill_vt(vt, 0, c, v_ref[rows(c), :], 1)
    for c in range(nctx // kc):
        _fill_vt(vt, 0, nlat + c, vc_ref[rows(c), :], 1)

    def win_start(t):
        return pl.multiple_of(jnp.clip(t * tq - SWA_WINDOW, 0, seq - span), SWA_WINDOW)

    def scores(qm, t, j, c):
        if c >= nwin:
            return _qk(kc_ref[rows(c - nwin), :], qm)
        k = k_ref[pl.ds(win_start(t) + c * kc, kc), :]
        return _qk(k, qm) + band_ref[_tile_class(t, ntile), rows(c), :]

    def values(t, j, c):
        return vt[0, nlat + c - nwin] if c >= nwin else vt[0, win_start(t) // kc + c]

    def sink(j):
        return jnp.full((1, 1), sinks_ref[head0 + j] * LOG2E, F32)

    _window_pipeline(q_ref, o_ref, s_scr, p_scr, m_scr, e_scr, o_scr, tq=tq, kc=kc,
                     nchunk=nwin + nctx // kc, scores=scores, values=values, extra_logit=sink)


def _swa_band_tables(seq, tq, span):
    ntile = seq // tq
    tabs = []
    for t in (0, 1, ntile - 1):
        start = min(max(t * tq - SWA_WINDOW, 0), seq - span)
        kpos = start + np.arange(span)[:, None]
        qpos = t * tq + np.arange(tq)[None, :]
        tabs.append(np.where(np.abs(qpos - kpos) <= SWA_WINDOW, 0.0, NEG_INF))
    return jnp.asarray(np.stack(tabs), F32)


def _swa_ctx_kernel(sinks_ref, q_ref, kc_ref, vc_ref, o_ref):
    qb = 2 * LANES
    for g in range(kc_ref.shape[1] // LANES):
        kc = kc_ref[:, g * LANES:(g + 1) * LANES]
        vc = vc_ref[:, g * LANES:(g + 1) * LANES]
        cols = slice(g * qb, (g + 1) * qb)
        _swa_heads(q_ref.at[:, cols], sinks_ref, g, lambda qm, kc=kc: [_qk(qm, kc)], [vc], o_ref.at[:, cols])


def _swa_attention(qkv, qkvc, sinks, batch, seq, nctx, need_ctx, d):
    nkv = (qkv.shape[1] - d) // (2 * LANES)
    qb = d // nkv
    kcol = d // LANES
    tq = SWA_TQ
    kc = SWA_WINDOW
    span = tq + 2 * SWA_WINDOW
    assert qb == 2 * LANES and seq // tq >= 3 and nctx % kc == 0
    smem = pl.BlockSpec(memory_space=pltpu.SMEM)
    sinks = sinks.astype(F32)
    y = pl.pallas_call(
        functools.partial(_swa_kernel, tq=tq, kc=kc, span=span),
        out_shape=jax.ShapeDtypeStruct((batch * seq, d), BF16),
        grid=(batch, nkv, 2),
        in_specs=[
            smem,
            pl.BlockSpec((3, span, tq), lambda b, g, jb: (0, 0, 0), pipeline_mode=pl.Buffered(1)),
            pl.BlockSpec((seq, LANES), lambda b, g, jb: (b, 2 * g + jb)),
            pl.BlockSpec((seq, LANES), lambda b, g, jb: (b, kcol + g)),
            pl.BlockSpec((seq, LANES), lambda b, g, jb: (b, kcol + nkv + g)),
            pl.BlockSpec((nctx, LANES), lambda b, g, jb: (b, kcol + g)),
            pl.BlockSpec((nctx, LANES), lambda b, g, jb: (b, kcol + nkv + g)),
        ],
        out_specs=pl.BlockSpec((seq, LANES), lambda b, g, jb: (b, 2 * g + jb)),
        scratch_shapes=_win_scratch(1, (seq + nctx) // kc, span + nctx, tq, kc, True),
        compiler_params=_cparams("parallel", "parallel", "arbitrary"),
        name="swa_attn",
    )(sinks, _swa_band_tables(seq, tq, span), qkv, qkv, qkv, qkvc, qkvc)
    yc = None
    if need_ctx:
        yc = pl.pallas_call(
            _swa_ctx_kernel,
            out_shape=jax.ShapeDtypeStruct((batch * nctx, d), BF16),
            grid=(batch,),
            in_specs=[
                smem,
                pl.BlockSpec((nctx, d), lambda b: (b, 0)),
                pl.BlockSpec((nctx, nkv * LANES), lambda b: (b, d // (nkv * LANES))),
                pl.BlockSpec((nctx, nkv * LANES), lambda b: (b, d // (nkv * LANES) + 1)),
            ],
            out_specs=pl.BlockSpec((nctx, d), lambda b: (b, 0)),
            compiler_params=_cparams("parallel"),
            name="swa_attn_ctx",
        )(sinks, qkvc, qkvc, qkvc)
    return y, yc


def _oproj_kernel(o_ref, wo_ref, x_ref, g1_ref, nf_ref, sc2_ref, sh2_ref, xo_ref, h2_ref):
    y = jnp.dot(o_ref[...], wo_ref[...], preferred_element_type=F32)
    xn = x_ref[...] + g1_ref[...] * y
    xo_ref[...] = xn
    h2 = (_rms(xn) * nf_ref[...]) * (1.0 + sc2_ref[...]) + sh2_ref[...]
    h2_ref[...] = h2.astype(BF16)


def _oproj_call(o, wo, x, gamma, mod, *, tiles_per_seq, mod_row, name):
    n, d = x.shape
    t = T_PROJ

    def mrow(i):
        return mod_row if mod_row is not None else i // tiles_per_seq

    def mspec(which):
        return pl.BlockSpec((None, None, 1, d), lambda i: (mrow(i), which, 0, 0))

    return pl.pallas_call(
        _oproj_kernel,
        out_shape=(jax.ShapeDtypeStruct((n, d), F32), jax.ShapeDtypeStruct((n, d), BF16)),
        grid=(n // t,),
        in_specs=[
            pl.BlockSpec((t, d), lambda i: (i, 0)),
            pl.BlockSpec((d, d), lambda i: (0, 0)),
            pl.BlockSpec((t, d), lambda i: (i, 0)),
            mspec(2),
            pl.BlockSpec((1, d), lambda i: (0, 0)),
            mspec(4),
            mspec(3),
        ],
        out_specs=(pl.BlockSpec((t, d), lambda i: (i, 0)), pl.BlockSpec((t, d), lambda i: (i, 0))),
        compiler_params=_cparams("parallel"),
        name=name,
    )(o, wo, x, mod, gamma.reshape(1, d), mod, mod)


def _ffn_kernel(h_ref, hp_ref, hn_ref, wup_ref, cw_ref, cb_ref, wd_ref, x_ref, g2_ref, no_ref, o_ref,
                lhs, act, ubuf, *, nseq, final, f):
    i = pl.program_id(0)
    t = h_ref.shape[0]
    halo = hp_ref.shape[0]
    dff = wd_ref.shape[0]
    rows = t + 2 * halo

    keep_prev = ((i % nseq) != 0).astype(F32)
    keep_next = ((i % nseq) != nseq - 1).astype(F32)
    lhs[0:halo, :] = (hp_ref[...].astype(F32) * keep_prev).astype(BF16)
    lhs[halo:halo + t, :] = h_ref[...]
    lhs[halo + t:, :] = (hn_ref[...].astype(F32) * keep_next).astype(BF16)

    def conv(col0, slot):
        u = jnp.dot(lhs[...], wup_ref[:, col0:col0 + f], preferred_element_type=F32)
        outs = []
        for s in range(f // LANES):
            sl = slice(col0 + s * LANES, col0 + (s + 1) * LANES)
            buf = ubuf.at[slot, s]
            buf[pl.ds(0, rows, stride=2), :] = u[:, s * LANES:(s + 1) * LANES]
            taps = [buf[pl.ds(2 * (halo - 1 + k), t, stride=2), :] for k in range(CONV_W)]
            outs.append(cb_ref[:, sl] + taps[0] * cw_ref[0:1, sl] + taps[1] * cw_ref[1:2, sl]
                        + taps[2] * cw_ref[2:3, sl])
        return jnp.concatenate(outs, axis=1)

    nch = dff // f
    cuts = [0] + [(nch * k) // DOWN_PIECES for k in range(1, DOWN_PIECES + 1)]
    y = None
    for c in range(nch):
        slot = 2 * (c % 2)
        a = conv(c * f, slot)
        g = conv(dff + c * f, slot + 1)
        act[:, c * f:(c + 1) * f] = ((g * (1.0 / (1.0 + jnp.exp(-g)))) * a).astype(BF16)
        if c + 1 in cuts:
            lo_c = cuts[cuts.index(c + 1) - 1] * f
            part = jnp.dot(act[:, lo_c:(c + 1) * f], wd_ref[lo_c:(c + 1) * f, :], preferred_element_type=F32)
            y = part if y is None else y + part

    xn = x_ref[...] + g2_ref[...] * y
    if final:
        xn = _rms(xn) * no_ref[...]
    o_ref[...] = xn


def _ffn_call(h2, x, w_up, w_conv, b_conv, w_down, mod, norm_out, *, t, tiles_per_seq, mod_row, final, name):
    n, d = x.shape
    dff = w_down.shape[0]
    f = FF_CHUNK
    halo = BF16_ROWS
    hb = t // halo
    last_hblock = n // halo - 1
    resident = pl.Buffered(1)

    def mrow(i):
        return mod_row if mod_row is not None else i // tiles_per_seq

    return pl.pallas_call(
        functools.partial(_ffn_kernel, nseq=tiles_per_seq, final=final, f=f),
        out_shape=jax.ShapeDtypeStruct((n, d), F32),
        grid=(n // t,),
        in_specs=[
            pl.BlockSpec((t, d), lambda i: (i, 0)),
            pl.BlockSpec((halo, d), lambda i: (jnp.maximum(i * hb - 1, 0), 0)),
            pl.BlockSpec((halo, d), lambda i: (jnp.minimum((i + 1) * hb, last_hblock), 0)),
            pl.BlockSpec((d, 2 * dff), lambda i: (0, 0), pipeline_mode=resident),
            pl.BlockSpec((CONV_W, 2 * dff), lambda i: (0, 0), pipeline_mode=resident),
            pl.BlockSpec((1, 2 * dff), lambda i: (0, 0), pipeline_mode=resident),
            pl.BlockSpec((dff, d), lambda i: (0, 0), pipeline_mode=resident),
            pl.BlockSpec((t, d), lambda i: (i, 0)),
            pl.BlockSpec((None, None, 1, d), lambda i: (mrow(i), 5, 0, 0)),
            pl.BlockSpec((1, d), lambda i: (0, 0)),
        ],
        out_specs=pl.BlockSpec((t, d), lambda i: (i, 0)),
        scratch_shapes=[pltpu.VMEM((t + 2 * halo, d), BF16), pltpu.VMEM((t, dff), BF16),
                        pltpu.VMEM((4, f // LANES, 2 * (t + 2 * halo), LANES), F32)],
        compiler_params=_cparams("parallel"),
        name=name,
    )(h2, h2, h2, w_up, w_conv, b_conv.reshape(1, 2 * dff), w_down, x, mod, norm_out.reshape(1, d))


def _rope_tables(seq, extra_rows):
    tpos = jnp.arange(seq, dtype=jnp.int32)
    row = (tpos // GRID_W).astype(F32)
    col = (tpos % GRID_W).astype(F32)
    inv = ROPE_BASE ** (-jnp.arange(ROPE_PAIRS, dtype=F32) / ROPE_PAIRS)
    ar = row[:, None] * inv[None, :]
    ac = col[:, None] * inv[None, :]
    cr, sr, cc, sc = jnp.cos(ar), jnp.sin(ar), jnp.cos(ac), jnp.sin(ac)
    cos = jnp.concatenate([cr, cr, cc, cc] * (LANES // HEAD_DIM), axis=1)
    sin = jnp.concatenate([-sr, sr, -sc, sc] * (LANES // HEAD_DIM), axis=1)
    cos = jnp.concatenate([cos, jnp.ones((extra_rows, LANES), F32)], axis=0)
    sin = jnp.concatenate([sin, jnp.zeros((extra_rows, LANES), F32)], axis=0)
    return cos * Q_SCALE, sin * Q_SCALE, cos, sin


def _dup_kv_columns(w, d, nkv):
    q = w[:, :d]
    k = w[:, d:d + nkv * HEAD_DIM].reshape(-1, nkv, 1, HEAD_DIM)
    v = w[:, d + nkv * HEAD_DIM:].reshape(-1, nkv, 1, HEAD_DIM)
    kd = jnp.broadcast_to(k, (w.shape[0], nkv, 2, HEAD_DIM)).reshape(w.shape[0], -1)
    vd = jnp.broadcast_to(v, (w.shape[0], nkv, 2, HEAD_DIM)).reshape(w.shape[0], -1)
    return jnp.concatenate([q, kd, vd], axis=1)


def kernel(x, c, ctx, c_ctx, ada_w, ada_b, norm_mix, norm_ffn, norm_out, ffn_up, ffn_conv, ffn_conv_b,
           ffn_down, a_wqkv, a_wo, a_lambda, a_subln, b_wqkv, b_wo, b_rpb, c_wqkv, c_wo, c_sinks):
    batch, seq, d = x.shape
    nctx = ctx.shape[1]
    depth = ada_w.shape[0]
    assert seq % T_FFN == 0 and seq % T_PROJ == 0 and (batch * nctx) % T_PROJ == 0

    mod_rows = 16
    cs = jnp.concatenate([c, c_ctx[None, :], jnp.zeros((mod_rows - batch - 1, d), F32)], axis=0)
    mod_all = _ada_call(cs, ada_w, ada_b).reshape(depth, mod_rows, 6, 1, d)
    ctx_row = batch

    tables = _rope_tables(seq, T_PROJ)
    lat_tiles = seq // T_PROJ
    nd = d // LANES

    xl = x.reshape(batch * seq, d)
    xc = ctx.reshape(batch * nctx, d)
    for i in range(depth):
        need_ctx = i < depth - 1
        kind, j = i % N_MIXERS, i // N_MIXERS
        mod = mod_all[i]
        if kind == 0:
            w = a_wqkv[j].astype(BF16)
            wo = a_wo[j].astype(BF16)
            kinds = [ROPEQ] * nd + [ROPEK] * nd + [PLAIN] * nd
            tabs = tables
        elif kind == 1:
            w = b_wqkv[j].astype(BF16)
            wo = b_wo[j].astype(BF16)
            kinds = [SCALEQ] * nd + [PLAIN] * (2 * nd)
            tabs = None
        else:
            nkv = (c_wqkv.shape[2] - d) // (2 * HEAD_DIM)
            w = _dup_kv_columns(c_wqkv[j], d, nkv).astype(BF16)
            wo = c_wo[j].astype(BF16)
            kinds = [ROPEQ] * nd + [ROPEK] * nkv + [PLAIN] * nkv
            tabs = tables
        qkv = _proj_call(xl, norm_mix[i], mod, w, kinds, tabs, tiles_per_seq=lat_tiles, mod_row=None,
                         name=f"proj{i}")
        qkvc = _proj_call(xc, norm_mix[i], mod, w, kinds, tabs, tiles_per_seq=lat_tiles, mod_row=ctx_row,
                          name=f"proj_ctx{i}")
        if kind == 0:
            lam_init = 0.8 - 0.6 * math.exp(-0.3 * i)
            y, yc = _diff_attention(qkv, qkvc, a_lambda[j].astype(F32), a_subln[j], lam_init,
                                    batch, seq, nctx, need_ctx)
        elif kind == 1:
            y, yc = _na_attention(qkv, qkvc, b_rpb[j], batch, seq, nctx, need_ctx)
        else:
            y, yc = _swa_attention(qkv, qkvc, c_sinks[j], batch, seq, nctx, need_ctx, d)

        w_up = ffn_up[i].astype(BF16)
        w_down = ffn_down[i].astype(BF16)
        xl, h2 = _oproj_call(y, wo, xl, norm_ffn[i], mod, tiles_per_seq=lat_tiles, mod_row=None,
                             name=f"oproj{i}")
        xl = _ffn_call(h2, xl, w_up, ffn_conv[i], ffn_conv_b[i], w_down, mod, norm_out,
                       t=T_FFN, tiles_per_seq=seq // T_FFN, mod_row=None, final=not need_ctx,
                       name=f"ffn{i}")
        if need_ctx:
            xc, h2c = _oproj_call(yc, wo, xc, norm_ffn[i], mod, tiles_per_seq=1, mod_row=ctx_row,
                                  name=f"oproj_ctx{i}")
            xc = _ffn_call(h2c, xc, w_up, ffn_conv[i], ffn_conv_b[i], w_down, mod, norm_out,
                           t=nctx, tiles_per_seq=1, mod_row=ctx_row, final=False, name=f"ffn_ctx{i}")
    return xl.reshape(batch, seq, d)
```

```python
import functools
import math

import numpy as np
import jax
import jax.numpy as jnp
from jax import lax
from jax.experimental import pallas as pl
from jax.experimental.pallas import tpu as pltpu

F32 = jnp.float32
BF16 = jnp.bfloat16

GRID_W = 64
HEAD_DIM = 64
ROPE_BASE = 10000.0
ROPE_PAIRS = HEAD_DIM // 4
NORM_EPS = 1e-6
NEG_INF = -1e30
N_MIXERS = 3
NA_KH = 8
NA_KW = 16
SWA_WINDOW = 128
CONV_W = 3
LOG2E = 1.4426950408889634
Q_SCALE = (HEAD_DIM ** -0.5) * LOG2E

LANES = 128
BF16_ROWS = 16
F32_ROWS = 8
VMEM_LIMIT = 52 * 1024 * 1024

T_PROJ = 1024
T_FFN = 512
FF_CHUNK = 256
DOWN_PIECES = 6
NA_ROWS = 4
SWA_TQ = 256
DIFF_TQ = 256
DIFF_KC = 256

PLAIN, SCALEQ, ROPEQ, ROPEK = 0, 1, 2, 3


def _cparams(*sem):
    return pltpu.CompilerParams(dimension_semantics=sem, vmem_limit_bytes=VMEM_LIMIT)


def _rms(x):
    return x * lax.rsqrt(jnp.mean(x * x, axis=-1, keepdims=True) + NORM_EPS)


def _ada_kernel(cs_ref, w_ref, b_ref, o_ref):
    cs = cs_ref[...]
    s = cs * (1.0 / (1.0 + jnp.exp(-cs)))
    w = w_ref[...]
    s_hi = s.astype(BF16)
    s_lo = (s - s_hi.astype(F32)).astype(BF16)
    w_hi = w.astype(BF16)
    w_lo = (w - w_hi.astype(F32)).astype(BF16)
    acc = jnp.dot(s_hi, w_hi, preferred_element_type=F32)
    acc = acc + jnp.dot(s_hi, w_lo, preferred_element_type=F32)
    acc = acc + jnp.dot(s_lo, w_hi, preferred_element_type=F32)
    o_ref[...] = acc + b_ref[...]


def _ada_call(cs, ada_w, ada_b):
    depth, d, n = ada_w.shape
    nt = 1536
    rows = cs.shape[0]
    return pl.pallas_call(
        _ada_kernel,
        out_shape=jax.ShapeDtypeStruct((depth, rows, n), F32),
        grid=(depth, n // nt),
        in_specs=[
            pl.BlockSpec((rows, d), lambda l, j: (0, 0)),
            pl.BlockSpec((None, d, nt), lambda l, j: (l, 0, j)),
            pl.BlockSpec((None, 1, nt), lambda l, j: (l, 0, j)),
        ],
        out_specs=pl.BlockSpec((None, rows, nt), lambda l, j: (l, 0, j)),
        compiler_params=_cparams("parallel", "parallel"),
        name="ada_mod",
    )(cs, ada_w, ada_b.reshape(depth, 1, n))


def _rope(r, c, s, first_half):
    partner = jnp.where(first_half, pltpu.roll(r, LANES - 16, 1), pltpu.roll(r, 16, 1))
    return r * c + partner * s


def _proj_kernel(*refs, kinds, nc, has_rope):
    if has_rope:
        x_ref, g_ref, sc_ref, sh_ref, w_ref, cq_ref, sq_ref, ck_ref, sk_ref, o_ref = refs
    else:
        x_ref, g_ref, sc_ref, sh_ref, w_ref, o_ref = refs
    t = x_ref.shape[0]
    nout = o_ref.shape[1]
    h = (_rms(x_ref[...]) * g_ref[...]) * (1.0 + sc_ref[...]) + sh_ref[...]
    hb = h.astype(BF16)
    if has_rope:
        lane = lax.broadcasted_iota(jnp.int32, (t, LANES), 1)
        first_half = (lane & 16) == 0
    for n0 in range(0, nout, nc):
        r = jnp.dot(hb, w_ref[:, n0:n0 + nc], preferred_element_type=F32)
        for j in range(nc // LANES):
            kind = kinds[n0 // LANES + j]
            rj = r[:, j * LANES:(j + 1) * LANES]
            if kind == ROPEQ:
                rj = _rope(rj, cq_ref[...], sq_ref[...], first_half)
            elif kind == ROPEK:
                rj = _rope(rj, ck_ref[...], sk_ref[...], first_half)
            elif kind == SCALEQ:
                rj = rj * Q_SCALE
            o_ref[:, n0 + j * LANES:n0 + (j + 1) * LANES] = rj.astype(BF16)


def _proj_call(x, gamma, mod, w, kinds, tables, *, tiles_per_seq, mod_row, name):
    n, d = x.shape
    nout = w.shape[1]
    t = T_PROJ
    has_rope = tables is not None

    def mrow(i):
        return mod_row if mod_row is not None else i // tiles_per_seq

    def tblock(i):
        return i % tiles_per_seq if mod_row is None else tiles_per_seq

    in_specs = [
        pl.BlockSpec((t, d), lambda i: (i, 0)),
        pl.BlockSpec((1, d), lambda i: (0, 0)),
        pl.BlockSpec((None, None, 1, d), lambda i: (mrow(i), 1, 0, 0)),
        pl.BlockSpec((None, None, 1, d), lambda i: (mrow(i), 0, 0, 0)),
        pl.BlockSpec((d, nout), lambda i: (0, 0)),
    ]
    args = [x, gamma.reshape(1, d), mod, mod, w]
    if has_rope:
        for tab in tables:
            in_specs.append(pl.BlockSpec((t, LANES), lambda i: (tblock(i), 0)))
            args.append(tab)
    return pl.pallas_call(
        functools.partial(_proj_kernel, kinds=tuple(kinds), nc=512, has_rope=has_rope),
        out_shape=jax.ShapeDtypeStruct((n, nout), BF16),
        grid=(n // t,),
        in_specs=in_specs,
        out_specs=pl.BlockSpec((t, nout), lambda i: (i, 0)),
        compiler_params=_cparams("parallel"),
        name=name,
    )(*args)


def _qk(q, k):
    return lax.dot_general(q, k, (((1,), (1,)), ((), ())), preferred_element_type=F32)


def _half_masks(shape):
    lane = lax.broadcasted_iota(jnp.int32, shape, 1)
    return lane < HEAD_DIM


def _split_heads(q, lo):
    qf = q.astype(F32)
    zero = jnp.zeros_like(qf)
    return jnp.where(lo, qf, zero).astype(q.dtype), jnp.where(lo, zero, qf).astype(q.dtype)


def _diff_lambda(lam_ref, lam_init):
    lv = lam_ref[...]
    d1 = jnp.sum(lv[0:1] * lv[1:2], axis=1, keepdims=True)
    d2 = jnp.sum(lv[2:3] * lv[3:4], axis=1, keepdims=True)
    return jnp.exp(d1) - jnp.exp(d2) + lam_init


def _diff_finish(o1, o2, lam, subln_ref, lam_init, o_ref):
    o = _rms(o1 - lam * o2) * subln_ref[...] * (1.0 - lam_init)
    o_ref[...] = o.astype(o_ref.dtype)


def _diff_kernel(lam_ref, subln_ref, q_ref, kl_ref, vl_ref, kc_ref, vc_ref, o_ref,
                 vt, s_scr, p_scr, m_scr, o_scr, *, lam_init, tq, kc):
    seq, nctx = q_ref.shape[0], kc_ref.shape[0]
    nkeys = vt.shape[1]
    nlat = seq // kc
    kcc = min(kc, nctx)
    nchunk = nlat + nctx // kcc
    ntile = seq // tq
    sub = m_scr.shape[1]
    lam = _diff_lambda(lam_ref, lam_init)
    lo = _half_masks((tq, LANES))

    def rows(c):
        if c < nlat:
            return slice(c * kc, (c + 1) * kc)
        return slice(seq + (c - nlat) * kcc, seq + (c - nlat + 1) * kcc)

    def chunk_of(lat_ref, ctx_ref, c):
        if c < nlat:
            return lat_ref[rows(c), :]
        return ctx_ref[(c - nlat) * kcc:(c - nlat + 1) * kcc, :]

    eye = (lax.broadcasted_iota(jnp.int32, (LANES, LANES), 0)
           == lax.broadcasted_iota(jnp.int32, (LANES, LANES), 1)).astype(F32).astype(BF16)
    for c in range(nchunk):
        vt[0:LANES, rows(c)] = _qk(eye, chunk_of(vl_ref, vc_ref, c)).astype(BF16)
    vt[LANES:, :] = jnp.ones((vt.shape[0] - LANES, nkeys), BF16)

    def half(m, tile, do_qk, do_exp, do_pv):
        if do_qk:
            q = q_ref[pl.ds(pl.multiple_of(tile * tq, tq), tq), :].astype(F32)
            zero = jnp.zeros_like(q)
            qm = (jnp.where(lo, q, zero) if m == 0 else jnp.where(lo, zero, q)).astype(BF16)
            mrun = jnp.full((sub, tq), NEG_INF, F32)
        if do_exp:
            mb = m_scr[1 - m, 0:1, :]
        if do_pv:
            acc = jnp.zeros((vt.shape[0], tq), F32)
        for c in range(nchunk):
            if do_qk:
                s = _qk(chunk_of(kl_ref, kc_ref, c), qm)
                s_scr[m, rows(c), :] = s
                for u in range(s.shape[0] // sub):
                    mrun = jnp.maximum(mrun, s[u * sub:(u + 1) * sub, :])
            if do_pv:
                acc = acc + jnp.dot(vt[:, rows(c)], p_scr[1 - m, rows(c), :], preferred_element_type=F32)
            if do_exp:
                p_scr[1 - m, rows(c), :] = jnp.exp2(s_scr[1 - m, rows(c), :] - mb).astype(BF16)
        if do_qk:
            m_scr[m] = jnp.broadcast_to(jnp.max(mrun, axis=0, keepdims=True), (sub, tq))
        if do_pv:
            return acc[:LANES] * (1.0 / acc[LANES:LANES + 1])
        return None

    def emit(tile, o2):
        o = _rms((o_scr[...] - lam * o2).T) * subln_ref[...] * (1.0 - lam_init)
        o_ref[pl.ds(pl.multiple_of(tile * tq, tq), tq), :] = o.astype(o_ref.dtype)

    half(0, 0, True, False, False)
    half(1, 0, True, True, False)
    half(0, 1, True, True, False)
    o_scr[...] = half(1, 1, True, True, True)

    def body(t, carry):
        emit(t - 2, half(0, t, True, True, True))
        o_scr[...] = half(1, t, True, True, True)
        return carry

    lax.fori_loop(2, ntile, body, 0)
    emit(ntile - 2, half(0, ntile, False, True, True))
    o_scr[...] = half(1, ntile, False, False, True)
    emit(ntile - 1, half(0, ntile + 1, False, False, True))


def _diff_ctx_kernel(lam_ref, subln_ref, q_ref, kc_ref, vc_ref, o_ref, *, lam_init):
    tq = q_ref.shape[0]
    lam = _diff_lambda(lam_ref, lam_init)
    lo = _half_masks((tq, LANES))
    for h in range(q_ref.shape[1] // LANES):
        cols = slice(h * LANES, (h + 1) * LANES)
        qs = _split_heads(q_ref[:, cols], lo)
        o1, o2 = [_softmax_pv([_qk(qm, kc_ref[:, cols])], [vc_ref[:, cols]]) for qm in qs]
        _diff_finish(o1, o2, lam, subln_ref, lam_init, o_ref.at[:, cols])


def _diff_attention(qkv, qkvc, lam_vecs, subln, lam_init, batch, seq, nctx, need_ctx):
    d = qkv.shape[1] // 3
    nh = d // LANES
    tq = DIFF_TQ
    kc = DIFF_KC
    assert seq % tq == 0 and seq % kc == 0 and nctx % min(kc, nctx) == 0 and seq // tq >= 2
    nkeys = seq + nctx
    subln2 = subln.reshape(1, LANES)
    y = pl.pallas_call(
        functools.partial(_diff_kernel, lam_init=lam_init, tq=tq, kc=kc),
        out_shape=jax.ShapeDtypeStruct((batch * seq, d), BF16),
        grid=(batch, nh),
        in_specs=[
            pl.BlockSpec((4, HEAD_DIM), lambda b, h: (0, 0)),
            pl.BlockSpec((1, LANES), lambda b, h: (0, 0)),
            pl.BlockSpec((seq, LANES), lambda b, h: (b, h)),
            pl.BlockSpec((seq, LANES), lambda b, h: (b, nh + h)),
            pl.BlockSpec((seq, LANES), lambda b, h: (b, 2 * nh + h)),
            pl.BlockSpec((nctx, LANES), lambda b, h: (b, nh + h)),
            pl.BlockSpec((nctx, LANES), lambda b, h: (b, 2 * nh + h)),
        ],
        out_specs=pl.BlockSpec((seq, LANES), lambda b, h: (b, h)),
        scratch_shapes=[pltpu.VMEM((LANES + BF16_ROWS, nkeys), BF16), pltpu.VMEM((2, nkeys, tq), F32),
                        pltpu.VMEM((2, nkeys, tq), BF16), pltpu.VMEM((2, F32_ROWS, tq), F32),
                        pltpu.VMEM((LANES, tq), F32)],
        compiler_params=_cparams("parallel", "parallel"),
        name="diff_attn",
    )(lam_vecs, subln2, qkv, qkv, qkv, qkvc, qkvc)
    yc = None
    if need_ctx:
        yc = pl.pallas_call(
            functools.partial(_diff_ctx_kernel, lam_init=lam_init),
            out_shape=jax.ShapeDtypeStruct((batch * nctx, d), BF16),
            grid=(batch,),
            in_specs=[
                pl.BlockSpec((4, HEAD_DIM), lambda b: (0, 0)),
                pl.BlockSpec((1, LANES), lambda b: (0, 0)),
                pl.BlockSpec((nctx, d), lambda b: (b, 0)),
                pl.BlockSpec((nctx, d), lambda b: (b, 1)),
                pl.BlockSpec((nctx, d), lambda b: (b, 2)),
            ],
            out_specs=pl.BlockSpec((nctx, d), lambda b: (b, 0)),
            compiler_params=_cparams("parallel"),
            name="diff_attn_ctx",
        )(lam_vecs, subln2, qkvc, qkvc, qkvc)
    return y, yc


def _softmax_pv(scores, values, extra_logit=None):
    m = jnp.max(scores[0], axis=1, keepdims=True)
    for s in scores[1:]:
        m = jnp.maximum(m, jnp.max(s, axis=1, keepdims=True))
    if extra_logit is not None:
        m = jnp.maximum(m, extra_logit)
    l = None
    acc = None
    for s, v in zip(scores, values):
        p = jnp.exp2(s - m)
        ps = jnp.sum(p, axis=1, keepdims=True)
        pv = jnp.dot(p.astype(BF16), v, preferred_element_type=F32)
        l = ps if l is None else l + ps
        acc = pv if acc is None else acc + pv
    if extra_logit is not None:
        l = l + jnp.exp2(extra_logit - m)
    return acc * (1.0 / l)


def _identity(n):
    return (lax.broadcasted_iota(jnp.int32, (n, n), 0)
            == lax.broadcasted_iota(jnp.int32, (n, n), 1)).astype(F32).astype(BF16)


def _fill_vt(vt, head0, chunk, v_block, n_heads):
    tr = _qk(_identity(LANES), v_block).astype(BF16)
    for hh in range(n_heads):
        vt[head0 + hh, chunk, 0:HEAD_DIM, :] = tr[hh * HEAD_DIM:(hh + 1) * HEAD_DIM]
        vt[head0 + hh, chunk, HEAD_DIM:, :] = jnp.ones((vt.shape[2] - HEAD_DIM, vt.shape[3]), BF16)


def _window_pipeline(q_ref, o_ref, s_scr, p_scr, m_scr, e_scr, o_scr, *, tq, kc, nchunk,
                     scores, values, extra_logit):
    ntile = q_ref.shape[0] // tq
    sub = m_scr.shape[1]
    lo = _half_masks((tq, LANES))
    rows = lambda c: slice(c * kc, (c + 1) * kc)
    tile_rows = lambda t: pl.ds(pl.multiple_of(t * tq, tq), tq)

    def half(j, t, do_qk, do_exp, do_pv):
        jo = 1 - j
        tv = t - 2 + j
        if do_qk:
            q = q_ref[tile_rows(t), :].astype(F32)
            zero = jnp.zeros_like(q)
            qm = (jnp.where(lo, q, zero) if j == 0 else jnp.where(lo, zero, q)).astype(BF16)
            mrun = jnp.full((sub, tq), NEG_INF, F32)
        if do_exp:
            mb = m_scr[jo, 0:1, :]
        if do_pv:
            acc = jnp.zeros((o_scr.shape[0] + BF16_ROWS, tq), F32)
        for c in range(nchunk):
            if do_qk:
                s = scores(qm, t, j, c)
                s_scr[j, rows(c), :] = s
                for u in range(kc // sub):
                    mrun = jnp.maximum(mrun, s[u * sub:(u + 1) * sub, :])
            if do_pv:
                acc = acc + jnp.dot(values(tv, jo, c), p_scr[jo, rows(c), :], preferred_element_type=F32)
            if do_exp:
                p_scr[jo, rows(c), :] = jnp.exp2(s_scr[jo, rows(c), :] - mb).astype(BF16)
        if do_qk:
            m = jnp.max(mrun, axis=0, keepdims=True)
            if extra_logit is not None:
                m = jnp.maximum(m, extra_logit(j))
            m_scr[j] = jnp.broadcast_to(m, (sub, tq))
        o = None
        if do_pv:
            l = acc[HEAD_DIM:HEAD_DIM + 1]
            if extra_logit is not None:
                l = l + e_scr[jo, 0:1, :]
            o = acc[:HEAD_DIM] * (1.0 / l)
        if do_exp and extra_logit is not None:
            e_scr[jo] = jnp.exp2(extra_logit(jo) - m_scr[jo])
        return o

    def emit(t, o1):
        o = jnp.concatenate([o_scr[...], o1], axis=0).T
        o_ref[tile_rows(t), :] = o.astype(o_ref.dtype)

    half(0, 0, True, False, False)
    half(1, 0, True, True, False)
    half(0, 1, True, True, False)
    o_scr[...] = half(1, 1, True, True, True)

    for t in range(2, ntile):
        emit(t - 2, half(0, t, True, True, True))
        o_scr[...] = half(1, t, True, True, True)
    emit(ntile - 2, half(0, ntile, False, True, True))
    o_scr[...] = half(1, ntile, False, False, True)
    emit(ntile - 1, half(0, ntile + 1, False, False, True))


def _tile_class(t, ntile):
    return jnp.where(t == 0, 0, jnp.where(t == ntile - 1, 2, 1))


def _na_kernel(bias_ref, q_ref, k_ref, v_ref, kc_ref, vc_ref, o_ref, vt, s_scr, p_scr, m_scr, o_scr,
               *, tq, kc):
    seq, nctx = q_ref.shape[0], kc_ref.shape[0]
    ntile, nlat, nwin = seq // tq, seq // kc, 3 * tq // kc
    rows = lambda c: slice(c * kc, (c + 1) * kc)
    for c in range(nlat):
        _fill_vt(vt, 0, c, v_ref[rows(c), :], 2)
    for c in range(nctx // kc):
        _fill_vt(vt, 0, nlat + c, vc_ref[rows(c), :], 2)

    def win_start(t):
        return pl.multiple_of(jnp.clip((t - 1) * tq, 0, seq - 3 * tq), tq)

    def scores(qm, t, j, c):
        if c >= nwin:
            return _qk(kc_ref[rows(c - nwin), :], qm)
        k = k_ref[pl.ds(win_start(t) + c * kc, kc), :]
        return _qk(k, qm) + bias_ref[_tile_class(t, ntile), j, rows(c), :]

    def values(t, j, c):
        return vt[j, nlat + c - nwin] if c >= nwin else vt[j, win_start(t) // kc + c]

    _window_pipeline(q_ref, o_ref, s_scr, p_scr, m_scr, None, o_scr, tq=tq, kc=kc,
                     nchunk=nwin + nctx // kc, scores=scores, values=values, extra_logit=None)


def _plain_ctx_kernel(q_ref, kc_ref, vc_ref, o_ref):
    tq = q_ref.shape[0]
    lo = _half_masks((tq, LANES))
    for hb in range(q_ref.shape[1] // LANES):
        cols = slice(hb * LANES, (hb + 1) * LANES)
        qs = _split_heads(q_ref[:, cols], lo)
        outs = [_softmax_pv([_qk(qs[hh], kc_ref[:, cols])], [vc_ref[:, cols]]) for hh in range(2)]
        o_ref[:, cols] = jnp.where(lo, outs[0], outs[1]).astype(o_ref.dtype)


def _na_bias_tables(rpb, rows):
    w = GRID_W
    nk = 3 * NA_ROWS
    nh = rpb.shape[0]
    edge = w - NA_KW
    ext = jnp.concatenate([jnp.repeat(rpb[..., :1], edge, axis=-1), rpb,
                           jnp.repeat(rpb[..., -1:], edge + 1, axis=-1)], axis=-1)
    flat = jnp.tile(ext, (1, 1, w + 1))[..., :w * (2 * w + 1)]
    hankel = flat.reshape(nh, -1, w, 2 * w + 1)[..., :w]
    toep = jnp.flip(hankel, axis=-1)
    cx = np.arange(w)[:, None]
    cq = np.arange(w)[None, :]
    col_start = np.clip(cq - NA_KW // 2, 0, w - NA_KW)
    col_ok = (cx >= col_start) & (cx < col_start + NA_KW)
    toep = jnp.where(col_ok, toep * LOG2E, NEG_INF)
    masked = jnp.full((nh, w, w), NEG_INF, F32)
    tabs = []
    for r0, k0 in ((0, 0), (NA_ROWS, 0), (rows - NA_ROWS, rows - nk)):
        key_blocks = []
        for e in range(nk):
            ry = k0 + e
            blocks = []
            for a in range(NA_ROWS):
                r = r0 + a
                rs = min(max(r - NA_KH // 2, 0), rows - NA_KH)
                blocks.append(toep[:, ry - r + NA_KH - 1] if rs <= ry < rs + NA_KH else masked)
            key_blocks.append(jnp.concatenate(blocks, axis=-1))
        tabs.append(jnp.concatenate(key_blocks, axis=-2))
    return jnp.stack(tabs).astype(F32)


def _win_scratch(nheads, nchunks, nkeys_item, tq, kc, with_extra):
    shapes = [pltpu.VMEM((nheads, nchunks, HEAD_DIM + BF16_ROWS, kc), BF16),
              pltpu.VMEM((2, nkeys_item, tq), F32), pltpu.VMEM((2, nkeys_item, tq), BF16),
              pltpu.VMEM((2, F32_ROWS, tq), F32)]
    if with_extra:
        shapes.append(pltpu.VMEM((2, F32_ROWS, tq), F32))
    shapes.append(pltpu.VMEM((HEAD_DIM, tq), F32))
    return shapes


def _na_attention(qkv, qkvc, rpb, batch, seq, nctx, need_ctx):
    d = qkv.shape[1] // 3
    nb = d // LANES
    rows = seq // GRID_W
    tq = NA_ROWS * GRID_W
    kc = tq
    assert seq // tq >= 3 and nctx % kc == 0
    bias = _na_bias_tables(rpb.astype(F32), rows)
    y = pl.pallas_call(
        functools.partial(_na_kernel, tq=tq, kc=kc),
        out_shape=jax.ShapeDtypeStruct((batch * seq, d), BF16),
        grid=(nb, batch),
        in_specs=[
            pl.BlockSpec((3, 2, 3 * tq, tq), lambda hb, b: (0, hb, 0, 0), pipeline_mode=pl.Buffered(1)),
            pl.BlockSpec((seq, LANES), lambda hb, b: (b, hb)),
            pl.BlockSpec((seq, LANES), lambda hb, b: (b, nb + hb)),
            pl.BlockSpec((seq, LANES), lambda hb, b: (b, 2 * nb + hb)),
            pl.BlockSpec((nctx, LANES), lambda hb, b: (b, nb + hb)),
            pl.BlockSpec((nctx, LANES), lambda hb, b: (b, 2 * nb + hb)),
        ],
        out_specs=pl.BlockSpec((seq, LANES), lambda hb, b: (b, hb)),
        scratch_shapes=_win_scratch(2, (seq + nctx) // kc, 3 * tq + nctx, tq, kc, False),
        compiler_params=_cparams("parallel", "arbitrary"),
        name="na_attn",
    )(bias, qkv, qkv, qkv, qkvc, qkvc)
    yc = None
    if need_ctx:
        yc = pl.pallas_call(
            _plain_ctx_kernel,
            out_shape=jax.ShapeDtypeStruct((batch * nctx, d), BF16),
            grid=(batch,),
            in_specs=[
                pl.BlockSpec((nctx, d), lambda b: (b, 0)),
                pl.BlockSpec((nctx, d), lambda b: (b, 1)),
                pl.BlockSpec((nctx, d), lambda b: (b, 2)),
            ],
            out_specs=pl.BlockSpec((nctx, d), lambda b: (b, 0)),
            compiler_params=_cparams("parallel"),
            name="na_attn_ctx",
        )(qkvc, qkvc, qkvc)
    return y, yc


def _swa_heads(q_ref, sinks_ref, g, score_fn, values, o_ref):
    tq = q_ref.shape[0]
    lo = _half_masks((tq, LANES))
    for jb in range(2):
        qs = _split_heads(q_ref[:, jb * LANES:(jb + 1) * LANES], lo)
        outs = []
        for hh in range(2):
            sink = jnp.full((1, 1), sinks_ref[g * 4 + jb * 2 + hh] * LOG2E, F32)
            outs.append(_softmax_pv(score_fn(qs[hh]), values, extra_logit=sink))
        o_ref[:, jb * LANES:(jb + 1) * LANES] = jnp.where(lo, outs[0], outs[1]).astype(o_ref.dtype)


def _swa_kernel(sinks_ref, band_ref, q_ref, k_ref, v_ref, kc_ref, vc_ref, o_ref,
                vt, s_scr, p_scr, m_scr, e_scr, o_scr, *, tq, kc, span):
    head0 = (pl.program_id(1) * 2 + pl.program_id(2)) * 2
    seq, nctx = q_ref.shape[0], kc_ref.shape[0]
    ntile, nlat, nwin = seq // tq, seq // kc, span // kc
    rows = lambda c: slice(c * kc, (c + 1) * kc)
    for c in range(nlat):
        _fill_vt(vt, 0, c, v_ref[rows(c), :], 1)
    for c in range(nctx // kc):
        _fill_vt(vt, 0, nlat + c, vc_ref[rows(c), :], 1)

    def win_start(t):
        return pl.multiple_of(jnp.clip(t * tq - SWA_WINDOW, 0, seq - span), SWA_WINDOW)

    def scores(qm, t, j, c):
        if c >= nwin:
            return _qk(kc_ref[rows(c - nwin), :], qm)
        k = k_ref[pl.ds(win_start(t) + c * kc, kc), :]
        return _qk(k, qm) + band_ref[_tile_class(t, ntile), rows(c), :]

    def values(t, j, c):
        return vt[0, nlat + c - nwin] if c >= nwin else vt[0, win_start(t) // kc + c]

    def sink(j):
        return jnp.full((1, 1), sinks_ref[head0 + j] * LOG2E, F32)

    _window_pipeline(q_ref, o_ref, s_scr, p_scr, m_scr, e_scr, o_scr, tq=tq, kc=kc,
                     nchunk=nwin + nctx // kc, scores=scores, values=values, extra_logit=sink)


def _swa_band_tables(seq, tq, span):
    ntile = seq // tq
    tabs = []
    for t in (0, 1, ntile - 1):
        start = min(max(t * tq - SWA_WINDOW, 0), seq - span)
        kpos = start + np.arange(span)[:, None]
        qpos = t * tq + np.arange(tq)[None, :]
        tabs.append(np.where(np.abs(qpos - kpos) <= SWA_WINDOW, 0.0, NEG_INF))
    return jnp.asarray(np.stack(tabs), F32)


def _swa_ctx_kernel(sinks_ref, q_ref, kc_ref, vc_ref, o_ref):
    qb = 2 * LANES
    for g in range(kc_ref.shape[1] // LANES):
        kc = kc_ref[:, g * LANES:(g + 1) * LANES]
        vc = vc_ref[:, g * LANES:(g + 1) * LANES]
        cols = slice(g * qb, (g + 1) * qb)
        _swa_heads(q_ref.at[:, cols], sinks_ref, g, lambda qm, kc=kc: [_qk(qm, kc)], [vc], o_ref.at[:, cols])


def _swa_attention(qkv, qkvc, sinks, batch, seq, nctx, need_ctx, d):
    nkv = (qkv.shape[1] - d) // (2 * LANES)
    qb = d // nkv
    kcol = d // LANES
    tq = SWA_TQ
    kc = SWA_WINDOW
    span = tq + 2 * SWA_WINDOW
    assert qb == 2 * LANES and seq // tq >= 3 and nctx % kc == 0
    smem = pl.BlockSpec(memory_space=pltpu.SMEM)
    sinks = sinks.astype(F32)
    y = pl.pallas_call(
        functools.partial(_swa_kernel, tq=tq, kc=kc, span=span),
        out_shape=jax.ShapeDtypeStruct((batch * seq, d), BF16),
        grid=(batch, nkv, 2),
        in_specs=[
            smem,
            pl.BlockSpec((3, span, tq), lambda b, g, jb: (0, 0, 0), pipeline_mode=pl.Buffered(1)),
            pl.BlockSpec((seq, LANES), lambda b, g, jb: (b, 2 * g + jb)),
            pl.BlockSpec((seq, LANES), lambda b, g, jb: (b, kcol + g)),
            pl.BlockSpec((seq, LANES), lambda b, g, jb: (b, kcol + nkv + g)),
            pl.BlockSpec((nctx, LANES), lambda b, g, jb: (b, kcol + g)),
            pl.BlockSpec((nctx, LANES), lambda b, g, jb: (b, kcol + nkv + g)),
        ],
        out_specs=pl.BlockSpec((seq, LANES), lambda b, g, jb: (b, 2 * g + jb)),
        scratch_shapes=_win_scratch(1, (seq + nctx) // kc, span + nctx, tq, kc, True),
        compiler_params=_cparams("parallel", "parallel", "arbitrary"),
        name="swa_attn",
    )(sinks, _swa_band_tables(seq, tq, span), qkv, qkv, qkv, qkvc, qkvc)
    yc = None
    if need_ctx:
        yc = pl.pallas_call(
            _swa_ctx_kernel,
            out_shape=jax.ShapeDtypeStruct((batch * nctx, d), BF16),
            grid=(batch,),
            in_specs=[
                smem,
                pl.BlockSpec((nctx, d), lambda b: (b, 0)),
                pl.BlockSpec((nctx, nkv * LANES), lambda b: (b, d // (nkv * LANES))),
                pl.BlockSpec((nctx, nkv * LANES), lambda b: (b, d // (nkv * LANES) + 1)),
            ],
            out_specs=pl.BlockSpec((nctx, d), lambda b: (b, 0)),
            compiler_params=_cparams("parallel"),
            name="swa_attn_ctx",
        )(sinks, qkvc, qkvc, qkvc)
    return y, yc


def _oproj_kernel(o_ref, wo_ref, x_ref, g1_ref, nf_ref, sc2_ref, sh2_ref, xo_ref, h2_ref):
    y = jnp.dot(o_ref[...], wo_ref[...], preferred_element_type=F32)
    xn = x_ref[...] + g1_ref[...] * y
    xo_ref[...] = xn
    h2 = (_rms(xn) * nf_ref[...]) * (1.0 + sc2_ref[...]) + sh2_ref[...]
    h2_ref[...] = h2.astype(BF16)


def _oproj_call(o, wo, x, gamma, mod, *, tiles_per_seq, mod_row, name):
    n, d = x.shape
    t = T_PROJ

    def mrow(i):
        return mod_row if mod_row is not None else i // tiles_per_seq

    def mspec(which):
        return pl.BlockSpec((None, None, 1, d), lambda i: (mrow(i), which, 0, 0))

    return pl.pallas_call(
        _oproj_kernel,
        out_shape=(jax.ShapeDtypeStruct((n, d), F32), jax.ShapeDtypeStruct((n, d), BF16)),
        grid=(n // t,),
        in_specs=[
            pl.BlockSpec((t, d), lambda i: (i, 0)),
            pl.BlockSpec((d, d), lambda i: (0, 0)),
            pl.BlockSpec((t, d), lambda i: (i, 0)),
            mspec(2),
            pl.BlockSpec((1, d), lambda i: (0, 0)),
            mspec(4),
            mspec(3),
        ],
        out_specs=(pl.BlockSpec((t, d), lambda i: (i, 0)), pl.BlockSpec((t, d), lambda i: (i, 0))),
        compiler_params=_cparams("parallel"),
        name=name,
    )(o, wo, x, mod, gamma.reshape(1, d), mod, mod)


def _ffn_kernel(h_ref, hp_ref, hn_ref, wup_ref, cw_ref, cb_ref, wd_ref, x_ref, g2_ref, no_ref, o_ref,
                lhs, act, ubuf, *, nseq, final, f):
    i = pl.program_id(0)
    t = h_ref.shape[0]
    halo = hp_ref.shape[0]
    dff = wd_ref.shape[0]
    rows = t + 2 * halo

    keep_prev = ((i % nseq) != 0).astype(F32)
    keep_next = ((i % nseq) != nseq - 1).astype(F32)
    lhs[0:halo, :] = (hp_ref[...].astype(F32) * keep_prev).astype(BF16)
    lhs[halo:halo + t, :] = h_ref[...]
    lhs[halo + t:, :] = (hn_ref[...].astype(F32) * keep_next).astype(BF16)

    def conv(col0, slot):
        u = jnp.dot(lhs[...], wup_ref[:, col0:col0 + f], preferred_element_type=F32)
        outs = []
        for s in range(f // LANES):
            sl = slice(col0 + s * LANES, col0 + (s + 1) * LANES)
            buf = ubuf.at[slot, s]
            buf[pl.ds(0, rows, stride=2), :] = u[:, s * LANES:(s + 1) * LANES]
            taps = [buf[pl.ds(2 * (halo - 1 + k), t, stride=2), :] for k in range(CONV_W)]
            outs.append(cb_ref[:, sl] + taps[0] * cw_ref[0:1, sl] + taps[1] * cw_ref[1:2, sl]
                        + taps[2] * cw_ref[2:3, sl])
        return jnp.concatenate(outs, axis=1)

    nch = dff // f
    cuts = [0] + [nch - (nch * (DOWN_PIECES - k)) // DOWN_PIECES for k in range(1, DOWN_PIECES + 1)]
    y = None
    for c in range(nch):
        slot = 2 * (c % 2)
        a = conv(c * f, slot)
        g = conv(dff + c * f, slot + 1)
        act[:, c * f:(c + 1) * f] = ((g * (1.0 / (1.0 + jnp.exp(-g)))) * a).astype(BF16)
        if c + 1 in cuts:
            lo_c = cuts[cuts.index(c + 1) - 1] * f
            part = jnp.dot(act[:, lo_c:(c + 1) * f], wd_ref[lo_c:(c + 1) * f, :], preferred_element_type=F32)
            y = part if y is None else y + part

    xn = x_ref[...] + g2_ref[...] * y
    if final:
        xn = _rms(xn) * no_ref[...]
    o_ref[...] = xn


def _ffn_call(h2, x, w_up, w_conv, b_conv, w_down, mod, norm_out, *, t, tiles_per_seq, mod_row, final, name):
    n, d = x.shape
    dff = w_down.shape[0]
    f = FF_CHUNK
    halo = BF16_ROWS
    hb = t // halo
    last_hblock = n // halo - 1
    resident = pl.Buffered(1)

    def mrow(i):
        return mod_row if mod_row is not None else i // tiles_per_seq

    return pl.pallas_call(
        functools.partial(_ffn_kernel, nseq=tiles_per_seq, final=final, f=f),
        out_shape=jax.ShapeDtypeStruct((n, d), F32),
        grid=(n // t,),
        in_specs=[
            pl.BlockSpec((t, d), lambda i: (i, 0)),
            pl.BlockSpec((halo, d), lambda i: (jnp.maximum(i * hb - 1, 0), 0)),
            pl.BlockSpec((halo, d), lambda i: (jnp.minimum((i + 1) * hb, last_hblock), 0)),
            pl.BlockSpec((d, 2 * dff), lambda i: (0, 0), pipeline_mode=resident),
            pl.BlockSpec((CONV_W, 2 * dff), lambda i: (0, 0), pipeline_mode=resident),
            pl.BlockSpec((1, 2 * dff), lambda i: (0, 0), pipeline_mode=resident),
            pl.BlockSpec((dff, d), lambda i: (0, 0), pipeline_mode=resident),
            pl.BlockSpec((t, d), lambda i: (i, 0)),
            pl.BlockSpec((None, None, 1, d), lambda i: (mrow(i), 5, 0, 0)),
            pl.BlockSpec((1, d), lambda i: (0, 0)),
        ],
        out_specs=pl.BlockSpec((t, d), lambda i: (i, 0)),
        scratch_shapes=[pltpu.VMEM((t + 2 * halo, d), BF16), pltpu.VMEM((t, dff), BF16),
                        pltpu.VMEM((4, f // LANES, 2 * (t + 2 * halo), LANES), F32)],
        compiler_params=_cparams("parallel"),
        name=name,
    )(h2, h2, h2, w_up, w_conv, b_conv.reshape(1, 2 * dff), w_down, x, mod, norm_out.reshape(1, d))


def _rope_tables(seq, extra_rows):
    tpos = jnp.arange(seq, dtype=jnp.int32)
    row = (tpos // GRID_W).astype(F32)
    col = (tpos % GRID_W).astype(F32)
    inv = ROPE_BASE ** (-jnp.arange(ROPE_PAIRS, dtype=F32) / ROPE_PAIRS)
    ar = row[:, None] * inv[None, :]
    ac = col[:, None] * inv[None, :]
    cr, sr, cc, sc = jnp.cos(ar), jnp.sin(ar), jnp.cos(ac), jnp.sin(ac)
    cos = jnp.concatenate([cr, cr, cc, cc] * (LANES // HEAD_DIM), axis=1)
    sin = jnp.concatenate([-sr, sr, -sc, sc] * (LANES // HEAD_DIM), axis=1)
    cos = jnp.concatenate([cos, jnp.ones((extra_rows, LANES), F32)], axis=0)
    sin = jnp.concatenate([sin, jnp.zeros((extra_rows, LANES), F32)], axis=0)
    return cos * Q_SCALE, sin * Q_SCALE, cos, sin


def _dup_kv_columns(w, d, nkv):
    q = w[:, :d]
    k = w[:, d:d + nkv * HEAD_DIM].reshape(-1, nkv, 1, HEAD_DIM)
    v = w[:, d + nkv * HEAD_DIM:].reshape(-1, nkv, 1, HEAD_DIM)
    kd = jnp.broadcast_to(k, (w.shape[0], nkv, 2, HEAD_DIM)).reshape(w.shape[0], -1)
    vd = jnp.broadcast_to(v, (w.shape[0], nkv, 2, HEAD_DIM)).reshape(w.shape[0], -1)
    return jnp.concatenate([q, kd, vd], axis=1)


def kernel(x, c, ctx, c_ctx, ada_w, ada_b, norm_mix, norm_ffn, norm_out, ffn_up, ffn_conv, ffn_conv_b,
           ffn_down, a_wqkv, a_wo, a_lambda, a_subln, b_wqkv, b_wo, b_rpb, c_wqkv, c_wo, c_sinks):
    batch, seq, d = x.shape
    nctx = ctx.shape[1]
    depth = ada_w.shape[0]
    assert seq % T_FFN == 0 and seq % T_PROJ == 0 and (batch * nctx) % T_PROJ == 0

    mod_rows = 16
    cs = jnp.concatenate([c, c_ctx[None, :], jnp.zeros((mod_rows - batch - 1, d), F32)], axis=0)
    mod_all = _ada_call(cs, ada_w, ada_b).reshape(depth, mod_rows, 6, 1, d)
    ctx_row = batch

    tables = _rope_tables(seq, T_PROJ)
    lat_tiles = seq // T_PROJ
    nd = d // LANES

    xl = x.reshape(batch * seq, d)
    xc = ctx.reshape(batch * nctx, d)
    for i in range(depth):
        need_ctx = i < depth - 1
        kind, j = i % N_MIXERS, i // N_MIXERS
        mod = mod_all[i]
        if kind == 0:
            w = a_wqkv[j].astype(BF16)
            wo = a_wo[j].astype(BF16)
            kinds = [ROPEQ] * nd + [ROPEK] * nd + [PLAIN] * nd
            tabs = tables
        elif kind == 1:
            w = b_wqkv[j].astype(BF16)
            wo = b_wo[j].astype(BF16)
            kinds = [SCALEQ] * nd + [PLAIN] * (2 * nd)
            tabs = None
        else:
            nkv = (c_wqkv.shape[2] - d) // (2 * HEAD_DIM)
            w = _dup_kv_columns(c_wqkv[j], d, nkv).astype(BF16)
            wo = c_wo[j].astype(BF16)
            kinds = [ROPEQ] * nd + [ROPEK] * nkv + [PLAIN] * nkv
            tabs = tables
        qkv = _proj_call(xl, norm_mix[i], mod, w, kinds, tabs, tiles_per_seq=lat_tiles, mod_row=None,
                         name=f"proj{i}")
        qkvc = _proj_call(xc, norm_mix[i], mod, w, kinds, tabs, tiles_per_seq=lat_tiles, mod_row=ctx_row,
                          name=f"proj_ctx{i}")
        if kind == 0:
            lam_init = 0.8 - 0.6 * math.exp(-0.3 * i)
            y, yc = _diff_attention(qkv, qkvc, a_lambda[j].astype(F32), a_subln[j], lam_init,
                                    batch, seq, nctx, need_ctx)
        elif kind == 1:
            y, yc = _na_attention(qkv, qkvc, b_rpb[j], batch, seq, nctx, need_ctx)
        else:
            y, yc = _swa_attention(qkv, qkvc, c_sinks[j], batch, seq, nctx, need_ctx, d)

        w_up = ffn_up[i].astype(BF16)
        w_down = ffn_down[i].astype(BF16)
        xl, h2 = _oproj_call(y, wo, xl, norm_ffn[i], mod, tiles_per_seq=lat_tiles, mod_row=None,
                             name=f"oproj{i}")
        xl = _ffn_call(h2, xl, w_up, ffn_conv[i], ffn_conv_b[i], w_down, mod, norm_out,
                       t=T_FFN, tiles_per_seq=seq // T_FFN, mod_row=None, final=not need_ctx,
                       name=f"ffn{i}")
        if need_ctx:
            xc, h2c = _oproj_call(yc, wo, xc, norm_ffn[i], mod, tiles_per_seq=1, mod_row=ctx_row,
                                  name=f"oproj_ctx{i}")
            xc = _ffn_call(h2c, xc, w_up, ffn_conv[i], ffn_conv_b[i], w_down, mod, norm_out,
                           t=nctx, tiles_per_seq=1, mod_row=ctx_row, final=False, name=f"ffn_ctx{i}")
    return xl.reshape(batch, seq, d)
```

```python
import functools
import math

import numpy as np
import jax
import jax.numpy as jnp
from jax import lax
from jax.experimental import pallas as pl
from jax.experimental.pallas import tpu as pltpu

F32 = jnp.float32
BF16 = jnp.bfloat16

GRID_W = 64
HEAD_DIM = 64
ROPE_BASE = 10000.0
ROPE_PAIRS = HEAD_DIM // 4
NORM_EPS = 1e-6
NEG_INF = -1e30
N_MIXERS = 3
NA_KH = 8
NA_KW = 16
SWA_WINDOW = 128
CONV_W = 3
LOG2E = 1.4426950408889634
Q_SCALE = (HEAD_DIM ** -0.5) * LOG2E

LANES = 128
BF16_ROWS = 16
F32_ROWS = 8
VMEM_LIMIT = 52 * 1024 * 1024

T_PROJ = 1024
T_FFN = 512
FF_CHUNK = 256
DOWN_PIECES = 8
NA_ROWS = 4
SWA_TQ = 256
DIFF_TQ = 256
DIFF_KC = 256

PLAIN, SCALEQ, ROPEQ, ROPEK = 0, 1, 2, 3


def _cparams(*sem):
    return pltpu.CompilerParams(dimension_semantics=sem, vmem_limit_bytes=VMEM_LIMIT)


def _rms(x):
    return x * lax.rsqrt(jnp.mean(x * x, axis=-1, keepdims=True) + NORM_EPS)


def _ada_kernel(cs_ref, w_ref, b_ref, o_ref):
    cs = cs_ref[...]
    s = cs * (1.0 / (1.0 + jnp.exp(-cs)))
    w = w_ref[...]
    s_hi = s.astype(BF16)
    s_lo = (s - s_hi.astype(F32)).astype(BF16)
    w_hi = w.astype(BF16)
    w_lo = (w - w_hi.astype(F32)).astype(BF16)
    acc = jnp.dot(s_hi, w_hi, preferred_element_type=F32)
    acc = acc + jnp.dot(s_hi, w_lo, preferred_element_type=F32)
    acc = acc + jnp.dot(s_lo, w_hi, preferred_element_type=F32)
    o_ref[...] = acc + b_ref[...]


def _ada_call(cs, ada_w, ada_b):
    depth, d, n = ada_w.shape
    nt = 1536
    rows = cs.shape[0]
    return pl.pallas_call(
        _ada_kernel,
        out_shape=jax.ShapeDtypeStruct((depth, rows, n), F32),
        grid=(depth, n // nt),
        in_specs=[
            pl.BlockSpec((rows, d), lambda l, j: (0, 0)),
            pl.BlockSpec((None, d, nt), lambda l, j: (l, 0, j)),
            pl.BlockSpec((None, 1, nt), lambda l, j: (l, 0, j)),
        ],
        out_specs=pl.BlockSpec((None, rows, nt), lambda l, j: (l, 0, j)),
        compiler_params=_cparams("parallel", "parallel"),
        name="ada_mod",
    )(cs, ada_w, ada_b.reshape(depth, 1, n))


def _rope(r, c, s, first_half):
    partner = jnp.where(first_half, pltpu.roll(r, LANES - 16, 1), pltpu.roll(r, 16, 1))
    return r * c + partner * s


def _proj_kernel(*refs, kinds, nc, has_rope):
    if has_rope:
        x_ref, g_ref, sc_ref, sh_ref, w_ref, cq_ref, sq_ref, ck_ref, sk_ref, o_ref = refs
    else:
        x_ref, g_ref, sc_ref, sh_ref, w_ref, o_ref = refs
    t = x_ref.shape[0]
    nout = o_ref.shape[1]
    h = (_rms(x_ref[...]) * g_ref[...]) * (1.0 + sc_ref[...]) + sh_ref[...]
    hb = h.astype(BF16)
    if has_rope:
        lane = lax.broadcasted_iota(jnp.int32, (t, LANES), 1)
        first_half = (lane & 16) == 0
    for n0 in range(0, nout, nc):
        r = jnp.dot(hb, w_ref[:, n0:n0 + nc], preferred_element_type=F32)
        for j in range(nc // LANES):
            kind = kinds[n0 // LANES + j]
            rj = r[:, j * LANES:(j + 1) * LANES]
            if kind == ROPEQ:
                rj = _rope(rj, cq_ref[...], sq_ref[...], first_half)
            elif kind == ROPEK:
                rj = _rope(rj, ck_ref[...], sk_ref[...], first_half)
            elif kind == SCALEQ:
                rj = rj * Q_SCALE
            o_ref[:, n0 + j * LANES:n0 + (j + 1) * LANES] = rj.astype(BF16)


def _proj_call(x, gamma, mod, w, kinds, tables, *, tiles_per_seq, mod_row, name):
    n, d = x.shape
    nout = w.shape[1]
    t = T_PROJ
    has_rope = tables is not None

    def mrow(i):
        return mod_row if mod_row is not None else i // tiles_per_seq

    def tblock(i):
        return i % tiles_per_seq if mod_row is None else tiles_per_seq

    in_specs = [
        pl.BlockSpec((t, d), lambda i: (i, 0)),
        pl.BlockSpec((1, d), lambda i: (0, 0)),
        pl.BlockSpec((None, None, 1, d), lambda i: (mrow(i), 1, 0, 0)),
        pl.BlockSpec((None, None, 1, d), lambda i: (mrow(i), 0, 0, 0)),
        pl.BlockSpec((d, nout), lambda i: (0, 0)),
    ]
    args = [x, gamma.reshape(1, d), mod, mod, w]
    if has_rope:
        for tab in tables:
            in_specs.append(pl.BlockSpec((t, LANES), lambda i: (tblock(i), 0)))
            args.append(tab)
    return pl.pallas_call(
        functools.partial(_proj_kernel, kinds=tuple(kinds), nc=512, has_rope=has_rope),
        out_shape=jax.ShapeDtypeStruct((n, nout), BF16),
        grid=(n // t,),
        in_specs=in_specs,
        out_specs=pl.BlockSpec((t, nout), lambda i: (i, 0)),
        compiler_params=_cparams("parallel"),
        name=name,
    )(*args)


def _qk(q, k):
    return lax.dot_general(q, k, (((1,), (1,)), ((), ())), preferred_element_type=F32)


def _half_masks(shape):
    lane = lax.broadcasted_iota(jnp.int32, shape, 1)
    return lane < HEAD_DIM


def _split_heads(q, lo):
    qf = q.astype(F32)
    zero = jnp.zeros_like(qf)
    return jnp.where(lo, qf, zero).astype(q.dtype), jnp.where(lo, zero, qf).astype(q.dtype)


def _diff_lambda(lam_ref, lam_init):
    lv = lam_ref[...]
    d1 = jnp.sum(lv[0:1] * lv[1:2], axis=1, keepdims=True)
    d2 = jnp.sum(lv[2:3] * lv[3:4], axis=1, keepdims=True)
    return jnp.exp(d1) - jnp.exp(d2) + lam_init


def _diff_finish(o1, o2, lam, subln_ref, lam_init, o_ref):
    o = _rms(o1 - lam * o2) * subln_ref[...] * (1.0 - lam_init)
    o_ref[...] = o.astype(o_ref.dtype)


def _diff_kernel(lam_ref, subln_ref, q_ref, kl_ref, vl_ref, kc_ref, vc_ref, o_ref,
                 vt, s_scr, p_scr, m_scr, o_scr, *, lam_init, tq, kc):
    seq, nctx = q_ref.shape[0], kc_ref.shape[0]
    nkeys = vt.shape[1]
    nlat = seq // kc
    kcc = min(kc, nctx)
    nchunk = nlat + nctx // kcc
    ntile = seq // tq
    sub = m_scr.shape[1]
    lam = _diff_lambda(lam_ref, lam_init)
    lo = _half_masks((tq, LANES))

    def rows(c):
        if c < nlat:
            return slice(c * kc, (c + 1) * kc)
        return slice(seq + (c - nlat) * kcc, seq + (c - nlat + 1) * kcc)

    def chunk_of(lat_ref, ctx_ref, c):
        if c < nlat:
            return lat_ref[rows(c), :]
        return ctx_ref[(c - nlat) * kcc:(c - nlat + 1) * kcc, :]

    eye = (lax.broadcasted_iota(jnp.int32, (LANES, LANES), 0)
           == lax.broadcasted_iota(jnp.int32, (LANES, LANES), 1)).astype(F32).astype(BF16)
    for c in range(nchunk):
        vt[0:LANES, rows(c)] = _qk(eye, chunk_of(vl_ref, vc_ref, c)).astype(BF16)
    vt[LANES:, :] = jnp.ones((vt.shape[0] - LANES, nkeys), BF16)

    def half(m, tile, do_qk, do_exp, do_pv):
        if do_qk:
            q = q_ref[pl.ds(pl.multiple_of(tile * tq, tq), tq), :].astype(F32)
            zero = jnp.zeros_like(q)
            qm = (jnp.where(lo, q, zero) if m == 0 else jnp.where(lo, zero, q)).astype(BF16)
            mrun = jnp.full((sub, tq), NEG_INF, F32)
        if do_exp:
            mb = m_scr[1 - m, 0:1, :]
        if do_pv:
            acc = jnp.zeros((vt.shape[0], tq), F32)
        for c in range(nchunk):
            if do_qk:
                s = _qk(chunk_of(kl_ref, kc_ref, c), qm)
                s_scr[m, rows(c), :] = s
                for u in range(s.shape[0] // sub):
                    mrun = jnp.maximum(mrun, s[u * sub:(u + 1) * sub, :])
            if do_pv:
                acc = acc + jnp.dot(vt[:, rows(c)], p_scr[1 - m, rows(c), :], preferred_element_type=F32)
            if do_exp:
                p_scr[1 - m, rows(c), :] = jnp.exp2(s_scr[1 - m, rows(c), :] - mb).astype(BF16)
        if do_qk:
            m_scr[m] = jnp.broadcast_to(jnp.max(mrun, axis=0, keepdims=True), (sub, tq))
        if do_pv:
            return acc[:LANES] * (1.0 / acc[LANES:LANES + 1])
        return None

    def emit(tile, o2):
        o = _rms((o_scr[...] - lam * o2).T) * subln_ref[...] * (1.0 - lam_init)
        o_ref[pl.ds(pl.multiple_of(tile * tq, tq), tq), :] = o.astype(o_ref.dtype)

    half(0, 0, True, False, False)
    half(1, 0, True, True, False)
    half(0, 1, True, True, False)
    o_scr[...] = half(1, 1, True, True, True)

    def body(t, carry):
        emit(t - 2, half(0, t, True, True, True))
        o_scr[...] = half(1, t, True, True, True)
        return carry

    lax.fori_loop(2, ntile, body, 0)
    emit(ntile - 2, half(0, ntile, False, True, True))
    o_scr[...] = half(1, ntile, False, False, True)
    emit(ntile - 1, half(0, ntile + 1, False, False, True))


def _diff_ctx_kernel(lam_ref, subln_ref, q_ref, kc_ref, vc_ref, o_ref, *, lam_init):
    tq = q_ref.shape[0]
    lam = _diff_lambda(lam_ref, lam_init)
    lo = _half_masks((tq, LANES))
    for h in range(q_ref.shape[1] // LANES):
        cols = slice(h * LANES, (h + 1) * LANES)
        qs = _split_heads(q_ref[:, cols], lo)
        o1, o2 = [_softmax_pv([_qk(qm, kc_ref[:, cols])], [vc_ref[:, cols]]) for qm in qs]
        _diff_finish(o1, o2, lam, subln_ref, lam_init, o_ref.at[:, cols])


def _diff_attention(qkv, qkvc, lam_vecs, subln, lam_init, batch, seq, nctx, need_ctx):
    d = qkv.shape[1] // 3
    nh = d // LANES
    tq = DIFF_TQ
    kc = DIFF_KC
    assert seq % tq == 0 and seq % kc == 0 and nctx % min(kc, nctx) == 0 and seq // tq >= 2
    nkeys = seq + nctx
    subln2 = subln.reshape(1, LANES)
    y = pl.pallas_call(
        functools.partial(_diff_kernel, lam_init=lam_init, tq=tq, kc=kc),
        out_shape=jax.ShapeDtypeStruct((batch * seq, d), BF16),
        grid=(batch, nh),
        in_specs=[
            pl.BlockSpec((4, HEAD_DIM), lambda b, h: (0, 0)),
            pl.BlockSpec((1, LANES), lambda b, h: (0, 0)),
            pl.BlockSpec((seq, LANES), lambda b, h: (b, h)),
            pl.BlockSpec((seq, LANES), lambda b, h: (b, nh + h)),
            pl.BlockSpec((seq, LANES), lambda b, h: (b, 2 * nh + h)),
            pl.BlockSpec((nctx, LANES), lambda b, h: (b, nh + h)),
            pl.BlockSpec((nctx, LANES), lambda b, h: (b, 2 * nh + h)),
        ],
        out_specs=pl.BlockSpec((seq, LANES), lambda b, h: (b, h)),
        scratch_shapes=[pltpu.VMEM((LANES + BF16_ROWS, nkeys), BF16), pltpu.VMEM((2, nkeys, tq), F32),
                        pltpu.VMEM((2, nkeys, tq), BF16), pltpu.VMEM((2, F32_ROWS, tq), F32),
                        pltpu.VMEM((LANES, tq), F32)],
        compiler_params=_cparams("parallel", "parallel"),
        name="diff_attn",
    )(lam_vecs, subln2, qkv, qkv, qkv, qkvc, qkvc)
    yc = None
    if need_ctx:
        yc = pl.pallas_call(
            functools.partial(_diff_ctx_kernel, lam_init=lam_init),
            out_shape=jax.ShapeDtypeStruct((batch * nctx, d), BF16),
            grid=(batch,),
            in_specs=[
                pl.BlockSpec((4, HEAD_DIM), lambda b: (0, 0)),
                pl.BlockSpec((1, LANES), lambda b: (0, 0)),
                pl.BlockSpec((nctx, d), lambda b: (b, 0)),
                pl.BlockSpec((nctx, d), lambda b: (b, 1)),
                pl.BlockSpec((nctx, d), lambda b: (b, 2)),
            ],
            out_specs=pl.BlockSpec((nctx, d), lambda b: (b, 0)),
            compiler_params=_cparams("parallel"),
            name="diff_attn_ctx",
        )(lam_vecs, subln2, qkvc, qkvc, qkvc)
    return y, yc


def _softmax_pv(scores, values, extra_logit=None):
    m = jnp.max(scores[0], axis=1, keepdims=True)
    for s in scores[1:]:
        m = jnp.maximum(m, jnp.max(s, axis=1, keepdims=True))
    if extra_logit is not None:
        m = jnp.maximum(m, extra_logit)
    l = None
    acc = None
    for s, v in zip(scores, values):
        p = jnp.exp2(s - m)
        ps = jnp.sum(p, axis=1, keepdims=True)
        pv = jnp.dot(p.astype(BF16), v, preferred_element_type=F32)
        l = ps if l is None else l + ps
        acc = pv if acc is None else acc + pv
    if extra_logit is not None:
        l = l + jnp.exp2(extra_logit - m)
    return acc * (1.0 / l)


def _identity(n):
    return (lax.broadcasted_iota(jnp.int32, (n, n), 0)
            == lax.broadcasted_iota(jnp.int32, (n, n), 1)).astype(F32).astype(BF16)


def _fill_vt(vt, head0, chunk, v_block, n_heads):
    tr = _qk(_identity(LANES), v_block).astype(BF16)
    for hh in range(n_heads):
        vt[head0 + hh, chunk, 0:HEAD_DIM, :] = tr[hh * HEAD_DIM:(hh + 1) * HEAD_DIM]
        vt[head0 + hh, chunk, HEAD_DIM:, :] = jnp.ones((vt.shape[2] - HEAD_DIM, vt.shape[3]), BF16)


def _window_pipeline(q_ref, o_ref, s_scr, p_scr, m_scr, e_scr, o_scr, *, tq, kc, nchunk,
                     scores, values, extra_logit):
    ntile = q_ref.shape[0] // tq
    sub = m_scr.shape[1]
    lo = _half_masks((tq, LANES))
    rows = lambda c: slice(c * kc, (c + 1) * kc)
    tile_rows = lambda t: pl.ds(pl.multiple_of(t * tq, tq), tq)

    def half(j, t, do_qk, do_exp, do_pv):
        jo = 1 - j
        tv = t - 2 + j
        if do_qk:
            q = q_ref[tile_rows(t), :].astype(F32)
            zero = jnp.zeros_like(q)
            qm = (jnp.where(lo, q, zero) if j == 0 else jnp.where(lo, zero, q)).astype(BF16)
            mrun = jnp.full((sub, tq), NEG_INF, F32)
        if do_exp:
            mb = m_scr[jo, 0:1, :]
        if do_pv:
            acc = jnp.zeros((o_scr.shape[0] + BF16_ROWS, tq), F32)
        for c in range(nchunk):
            if do_qk:
                s = scores(qm, t, j, c)
                s_scr[j, rows(c), :] = s
                for u in range(kc // sub):
                    mrun = jnp.maximum(mrun, s[u * sub:(u + 1) * sub, :])
            if do_pv:
                acc = acc + jnp.dot(values(tv, jo, c), p_scr[jo, rows(c), :], preferred_element_type=F32)
            if do_exp:
                p_scr[jo, rows(c), :] = jnp.exp2(s_scr[jo, rows(c), :] - mb).astype(BF16)
        if do_qk:
            m = jnp.max(mrun, axis=0, keepdims=True)
            if extra_logit is not None:
                m = jnp.maximum(m, extra_logit(j))
            m_scr[j] = jnp.broadcast_to(m, (sub, tq))
        o = None
        if do_pv:
            l = acc[HEAD_DIM:HEAD_DIM + 1]
            if extra_logit is not None:
                l = l + e_scr[jo, 0:1, :]
            o = acc[:HEAD_DIM] * (1.0 / l)
        if do_exp and extra_logit is not None:
            e_scr[jo] = jnp.exp2(extra_logit(jo) - m_scr[jo])
        return o

    def emit(t, o1):
        o = jnp.concatenate([o_scr[...], o1], axis=0).T
        o_ref[tile_rows(t), :] = o.astype(o_ref.dtype)

    half(0, 0, True, False, False)
    half(1, 0, True, True, False)
    half(0, 1, True, True, False)
    o_scr[...] = half(1, 1, True, True, True)

    for t in range(2, ntile):
        emit(t - 2, half(0, t, True, True, True))
        o_scr[...] = half(1, t, True, True, True)
    emit(ntile - 2, half(0, ntile, False, True, True))
    o_scr[...] = half(1, ntile, False, False, True)
    emit(ntile - 1, half(0, ntile + 1, False, False, True))


def _tile_class(t, ntile):
    return jnp.where(t == 0, 0, jnp.where(t == ntile - 1, 2, 1))


def _na_kernel(bias_ref, q_ref, k_ref, v_ref, kc_ref, vc_ref, o_ref, vt, s_scr, p_scr, m_scr, o_scr,
               *, tq, kc):
    seq, nctx = q_ref.shape[0], kc_ref.shape[0]
    ntile, nlat, nwin = seq // tq, seq // kc, 3 * tq // kc
    rows = lambda c: slice(c * kc, (c + 1) * kc)
    for c in range(nlat):
        _fill_vt(vt, 0, c, v_ref[rows(c), :], 2)
    for c in range(nctx // kc):
        _fill_vt(vt, 0, nlat + c, vc_ref[rows(c), :], 2)

    def win_start(t):
        return pl.multiple_of(jnp.clip((t - 1) * tq, 0, seq - 3 * tq), tq)

    def scores(qm, t, j, c):
        if c >= nwin:
            return _qk(kc_ref[rows(c - nwin), :], qm)
        k = k_ref[pl.ds(win_start(t) + c * kc, kc), :]
        return _qk(k, qm) + bias_ref[_tile_class(t, ntile), j, rows(c), :]

    def values(t, j, c):
        return vt[j, nlat + c - nwin] if c >= nwin else vt[j, win_start(t) // kc + c]

    _window_pipeline(q_ref, o_ref, s_scr, p_scr, m_scr, None, o_scr, tq=tq, kc=kc,
                     nchunk=nwin + nctx // kc, scores=scores, values=values, extra_logit=None)


def _plain_ctx_kernel(q_ref, kc_ref, vc_ref, o_ref):
    tq = q_ref.shape[0]
    lo = _half_masks((tq, LANES))
    for hb in range(q_ref.shape[1] // LANES):
        cols = slice(hb * LANES, (hb + 1) * LANES)
        qs = _split_heads(q_ref[:, cols], lo)
        outs = [_softmax_pv([_qk(qs[hh], kc_ref[:, cols])], [vc_ref[:, cols]]) for hh in range(2)]
        o_ref[:, cols] = jnp.where(lo, outs[0], outs[1]).astype(o_ref.dtype)


def _na_bias_tables(rpb, rows):
    w = GRID_W
    nk = 3 * NA_ROWS
    nh = rpb.shape[0]
    edge = w - NA_KW
    ext = jnp.concatenate([jnp.repeat(rpb[..., :1], edge, axis=-1), rpb,
                           jnp.repeat(rpb[..., -1:], edge + 1, axis=-1)], axis=-1)
    flat = jnp.tile(ext, (1, 1, w + 1))[..., :w * (2 * w + 1)]
    hankel = flat.reshape(nh, -1, w, 2 * w + 1)[..., :w]
    toep = jnp.flip(hankel, axis=-1)
    cx = np.arange(w)[:, None]
    cq = np.arange(w)[None, :]
    col_start = np.clip(cq - NA_KW // 2, 0, w - NA_KW)
    col_ok = (cx >= col_start) & (cx < col_start + NA_KW)
    toep = jnp.where(col_ok, toep * LOG2E, NEG_INF)
    masked = jnp.full((nh, w, w), NEG_INF, F32)
    tabs = []
    for r0, k0 in ((0, 0), (NA_ROWS, 0), (rows - NA_ROWS, rows - nk)):
        key_blocks = []
        for e in range(nk):
            ry = k0 + e
            blocks = []
            for a in range(NA_ROWS):
                r = r0 + a
                rs = min(max(r - NA_KH // 2, 0), rows - NA_KH)
                blocks.append(toep[:, ry - r + NA_KH - 1] if rs <= ry < rs + NA_KH else masked)
            key_blocks.append(jnp.concatenate(blocks, axis=-1))
        tabs.append(jnp.concatenate(key_blocks, axis=-2))
    return jnp.stack(tabs).astype(F32)


def _win_scratch(nheads, nchunks, nkeys_item, tq, kc, with_extra):
    shapes = [pltpu.VMEM((nheads, nchunks, HEAD_DIM + BF16_ROWS, kc), BF16),
              pltpu.VMEM((2, nkeys_item, tq), F32), pltpu.VMEM((2, nkeys_item, tq), BF16),
              pltpu.VMEM((2, F32_ROWS, tq), F32)]
    if with_extra:
        shapes.append(pltpu.VMEM((2, F32_ROWS, tq), F32))
    shapes.append(pltpu.VMEM((HEAD_DIM, tq), F32))
    return shapes


def _na_attention(qkv, qkvc, rpb, batch, seq, nctx, need_ctx):
    d = qkv.shape[1] // 3
    nb = d // LANES
    rows = seq // GRID_W
    tq = NA_ROWS * GRID_W
    kc = tq
    assert seq // tq >= 3 and nctx % kc == 0
    bias = _na_bias_tables(rpb.astype(F32), rows)
    y = pl.pallas_call(
        functools.partial(_na_kernel, tq=tq, kc=kc),
        out_shape=jax.ShapeDtypeStruct((batch * seq, d), BF16),
        grid=(nb, batch),
        in_specs=[
            pl.BlockSpec((3, 2, 3 * tq, tq), lambda hb, b: (0, hb, 0, 0), pipeline_mode=pl.Buffered(1)),
            pl.BlockSpec((seq, LANES), lambda hb, b: (b, hb)),
            pl.BlockSpec((seq, LANES), lambda hb, b: (b, nb + hb)),
            pl.BlockSpec((seq, LANES), lambda hb, b: (b, 2 * nb + hb)),
            pl.BlockSpec((nctx, LANES), lambda hb, b: (b, nb + hb)),
            pl.BlockSpec((nctx, LANES), lambda hb, b: (b, 2 * nb + hb)),
        ],
        out_specs=pl.BlockSpec((seq, LANES), lambda hb, b: (b, hb)),
        scratch_shapes=_win_scratch(2, (seq + nctx) // kc, 3 * tq + nctx, tq, kc, False),
        compiler_params=_cparams("parallel", "arbitrary"),
        name="na_attn",
    )(bias, qkv, qkv, qkv, qkvc, qkvc)
    yc = None
    if need_ctx:
        yc = pl.pallas_call(
            _plain_ctx_kernel,
            out_shape=jax.ShapeDtypeStruct((batch * nctx, d), BF16),
            grid=(batch,),
            in_specs=[
                pl.BlockSpec((nctx, d), lambda b: (b, 0)),
                pl.BlockSpec((nctx, d), lambda b: (b, 1)),
                pl.BlockSpec((nctx, d), lambda b: (b, 2)),
            ],
            out_specs=pl.BlockSpec((nctx, d), lambda b: (b, 0)),
            compiler_params=_cparams("parallel"),
            name="na_attn_ctx",
        )(qkvc, qkvc, qkvc)
    return y, yc


def _swa_heads(q_ref, sinks_ref, g, score_fn, values, o_ref):
    tq = q_ref.shape[0]
    lo = _half_masks((tq, LANES))
    for jb in range(2):
        qs = _split_heads(q_ref[:, jb * LANES:(jb + 1) * LANES], lo)
        outs = []
        for hh in range(2):
            sink = jnp.full((1, 1), sinks_ref[g * 4 + jb * 2 + hh] * LOG2E, F32)
            outs.append(_softmax_pv(score_fn(qs[hh]), values, extra_logit=sink))
        o_ref[:, jb * LANES:(jb + 1) * LANES] = jnp.where(lo, outs[0], outs[1]).astype(o_ref.dtype)


def _swa_kernel(sinks_ref, band_ref, q_ref, k_ref, v_ref, kc_ref, vc_ref, o_ref,
                vt, s_scr, p_scr, m_scr, e_scr, o_scr, *, tq, kc, span):
    head0 = (pl.program_id(1) * 2 + pl.program_id(2)) * 2
    seq, nctx = q_ref.shape[0], kc_ref.shape[0]
    ntile, nlat, nwin = seq // tq, seq // kc, span // kc
    rows = lambda c: slice(c * kc, (c + 1) * kc)
    for c in range(nlat):
        _fill_vt(vt, 0, c, v_ref[rows(c), :], 1)
    for c in range(nctx // kc):
        _fill_vt(vt, 0, nlat + c, vc_ref[rows(c), :], 1)

    def win_start(t):
        return pl.multiple_of(jnp.clip(t * tq - SWA_WINDOW, 0, seq - span), SWA_WINDOW)

    def scores(qm, t, j, c):
        if c >= nwin:
            return _qk(kc_ref[rows(c - nwin), :], qm)
        k = k_ref[pl.ds(win_start(t) + c * kc, kc), :]
        return _qk(k, qm) + band_ref[_tile_class(t, ntile), rows(c), :]

    def values(t, j, c):
        return vt[0, nlat + c - nwin] if c >= nwin else vt[0, win_start(t) // kc + c]

    def sink(j):
        return jnp.full((1, 1), sinks_ref[head0 + j] * LOG2E, F32)

    _window_pipeline(q_ref, o_ref, s_scr, p_scr, m_scr, e_scr, o_scr, tq=tq, kc=kc,
                     nchunk=nwin + nctx // kc, scores=scores, values=values, extra_logit=sink)


def _swa_band_tables(seq, tq, span):
    ntile = seq // tq
    tabs = []
    for t in (0, 1, ntile - 1):
        start = min(max(t * tq - SWA_WINDOW, 0), seq - span)
        kpos = start + np.arange(span)[:, None]
        qpos = t * tq + np.arange(tq)[None, :]
        tabs.append(np.where(np.abs(qpos - kpos) <= SWA_WINDOW, 0.0, NEG_INF))
    return jnp.asarray(np.stack(tabs), F32)


def _swa_ctx_kernel(sinks_ref, q_ref, kc_ref, vc_ref, o_ref):
    qb = 2 * LANES
    for g in range(kc_ref.shape[1] // LANES):
        kc = kc_ref[:, g * LANES:(g + 1) * LANES]
        vc = vc_ref[:, g * LANES:(g + 1) * LANES]
        cols = slice(g * qb, (g + 1) * qb)
        _swa_heads(q_ref.at[:, cols], sinks_ref, g, lambda qm, kc=kc: [_qk(qm, kc)], [vc], o_ref.at[:, cols])


def _swa_attention(qkv, qkvc, sinks, batch, seq, nctx, need_ctx, d):
    nkv = (qkv.shape[1] - d) // (2 * LANES)
    qb = d // nkv
    kcol = d // LANES
    tq = SWA_TQ
    kc = SWA_WINDOW
    span = tq + 2 * SWA_WINDOW
    assert qb == 2 * LANES and seq // tq >= 3 and nctx % kc == 0
    smem = pl.BlockSpec(memory_space=pltpu.SMEM)
    sinks = sinks.astype(F32)
    y = pl.pallas_call(
        functools.partial(_swa_kernel, tq=tq, kc=kc, span=span),
        out_shape=jax.ShapeDtypeStruct((batch * seq, d), BF16),
        grid=(batch, nkv, 2),
        in_specs=[
            smem,
            pl.BlockSpec((3, span, tq), lambda b, g, jb: (0, 0, 0), pipeline_mode=pl.Buffered(1)),
            pl.BlockSpec((seq, LANES), lambda b, g, jb: (b, 2 * g + jb)),
            pl.BlockSpec((seq, LANES), lambda b, g, jb: (b, kcol + g)),
            pl.BlockSpec((seq, LANES), lambda b, g, jb: (b, kcol + nkv + g)),
            pl.BlockSpec((nctx, LANES), lambda b, g, jb: (b, kcol + g)),
            pl.BlockSpec((nctx, LANES), lambda b, g, jb: (b, kcol + nkv + g)),
        ],
        out_specs=pl.BlockSpec((seq, LANES), lambda b, g, jb: (b, 2 * g + jb)),
        scratch_shapes=_win_scratch(1, (seq + nctx) // kc, span + nctx, tq, kc, True),
        compiler_params=_cparams("parallel", "parallel", "arbitrary"),
        name="swa_attn",
    )(sinks, _swa_band_tables(seq, tq, span), qkv, qkv, qkv, qkvc, qkvc)
    yc = None
    if need_ctx:
        yc = pl.pallas_call(
            _swa_ctx_kernel,
            out_shape=jax.ShapeDtypeStruct((batch * nctx, d), BF16),
            grid=(batch,),
            in_specs=[
                smem,
                pl.BlockSpec((nctx, d), lambda b: (b, 0)),
                pl.BlockSpec((nctx, nkv * LANES), lambda b: (b, d // (nkv * LANES))),
                pl.BlockSpec((nctx, nkv * LANES), lambda b: (b, d // (nkv * LANES) + 1)),
            ],
            out_specs=pl.BlockSpec((nctx, d), lambda b: (b, 0)),
            compiler_params=_cparams("parallel"),
            name="swa_attn_ctx",
        )(sinks, qkvc, qkvc, qkvc)
    return y, yc


def _oproj_kernel(o_ref, wo_ref, x_ref, g1_ref, nf_ref, sc2_ref, sh2_ref, xo_ref, h2_ref):
    y = jnp.dot(o_ref[...], wo_ref[...], preferred_element_type=F32)
    xn = x_ref[...] + g1_ref[...] * y
    xo_ref[...] = xn
    h2 = (_rms(xn) * nf_ref[...]) * (1.0 + sc2_ref[...]) + sh2_ref[...]
    h2_ref[...] = h2.astype(BF16)


def _oproj_call(o, wo, x, gamma, mod, *, tiles_per_seq, mod_row, name):
    n, d = x.shape
    t = T_PROJ

    def mrow(i):
        return mod_row if mod_row is not None else i // tiles_per_seq

    def mspec(which):
        return pl.BlockSpec((None, None, 1, d), lambda i: (mrow(i), which, 0, 0))

    return pl.pallas_call(
        _oproj_kernel,
        out_shape=(jax.ShapeDtypeStruct((n, d), F32), jax.ShapeDtypeStruct((n, d), BF16)),
        grid=(n // t,),
        in_specs=[
            pl.BlockSpec((t, d), lambda i: (i, 0)),
            pl.BlockSpec((d, d), lambda i: (0, 0)),
            pl.BlockSpec((t, d), lambda i: (i, 0)),
            mspec(2),
            pl.BlockSpec((1, d), lambda i: (0, 0)),
            mspec(4),
            mspec(3),
        ],
        out_specs=(pl.BlockSpec((t, d), lambda i: (i, 0)), pl.BlockSpec((t, d), lambda i: (i, 0))),
        compiler_params=_cparams("parallel"),
        name=name,
    )(o, wo, x, mod, gamma.reshape(1, d), mod, mod)


def _ffn_kernel(h_ref, hp_ref, hn_ref, wup_ref, cw_ref, cb_ref, wd_ref, x_ref, g2_ref, no_ref, o_ref,
                lhs, act, ubuf, *, nseq, final, f):
    i = pl.program_id(0)
    t = h_ref.shape[0]
    halo = hp_ref.shape[0]
    dff = wd_ref.shape[0]
    rows = t + 2 * halo

    keep_prev = ((i % nseq) != 0).astype(F32)
    keep_next = ((i % nseq) != nseq - 1).astype(F32)
    lhs[0:halo, :] = (hp_ref[...].astype(F32) * keep_prev).astype(BF16)
    lhs[halo:halo + t, :] = h_ref[...]
    lhs[halo + t:, :] = (hn_ref[...].astype(F32) * keep_next).astype(BF16)

    def conv(col0, slot):
        u = jnp.dot(lhs[...], wup_ref[:, col0:col0 + f], preferred_element_type=F32)
        outs = []
        for s in range(f // LANES):
            sl = slice(col0 + s * LANES, col0 + (s + 1) * LANES)
            buf = ubuf.at[slot, s]
            buf[pl.ds(0, rows, stride=2), :] = u[:, s * LANES:(s + 1) * LANES]
            taps = [buf[pl.ds(2 * (halo - 1 + k), t, stride=2), :] for k in range(CONV_W)]
            outs.append(cb_ref[:, sl] + taps[0] * cw_ref[0:1, sl] + taps[1] * cw_ref[1:2, sl]
                        + taps[2] * cw_ref[2:3, sl])
        return jnp.concatenate(outs, axis=1)

    nch = dff // f
    cuts = [0] + [(nch * k) // DOWN_PIECES for k in range(1, DOWN_PIECES + 1)]
    y = None
    for c in range(nch):
        slot = 2 * (c % 2)
        a = conv(c * f, slot)
        g = conv(dff + c * f, slot + 1)
        act[:, c * f:(c + 1) * f] = ((g * (1.0 / (1.0 + jnp.exp(-g)))) * a).astype(BF16)
        if c + 1 in cuts:
            lo_c = cuts[cuts.index(c + 1) - 1] * f
            part = jnp.dot(act[:, lo_c:(c + 1) * f], wd_ref[lo_c:(c + 1) * f, :], preferred_element_type=F32)
            y = part if y is None else y + part

    xn = x_ref[...] + g2_ref[...] * y
    if final:
        xn = _rms(xn) * no_ref[...]
    o_ref[...] = xn


def _ffn_call(h2, x, w_up, w_conv, b_conv, w_down, mod, norm_out, *, t, tiles_per_seq, mod_row, final, name):
    n, d = x.shape
    dff = w_down.shape[0]
    f = FF_CHUNK
    halo = BF16_ROWS
    hb = t // halo
    last_hblock = n // halo - 1
    resident = pl.Buffered(1)

    def mrow(i):
        return mod_row if mod_row is not None else i // tiles_per_seq

    return pl.pallas_call(
        functools.partial(_ffn_kernel, nseq=tiles_per_seq, final=final, f=f),
        out_shape=jax.ShapeDtypeStruct((n, d), F32),
        grid=(n // t,),
        in_specs=[
            pl.BlockSpec((t, d), lambda i: (i, 0)),
            pl.BlockSpec((halo, d), lambda i: (jnp.maximum(i * hb - 1, 0), 0)),
            pl.BlockSpec((halo, d), lambda i: (jnp.minimum((i + 1) * hb, last_hblock), 0)),
            pl.BlockSpec((d, 2 * dff), lambda i: (0, 0), pipeline_mode=resident),
            pl.BlockSpec((CONV_W, 2 * dff), lambda i: (0, 0), pipeline_mode=resident),
            pl.BlockSpec((1, 2 * dff), lambda i: (0, 0), pipeline_mode=resident),
            pl.BlockSpec((dff, d), lambda i: (0, 0), pipeline_mode=resident),
            pl.BlockSpec((t, d), lambda i: (i, 0)),
            pl.BlockSpec((None, None, 1, d), lambda i: (mrow(i), 5, 0, 0)),
            pl.BlockSpec((1, d), lambda i: (0, 0)),
        ],
        out_specs=pl.BlockSpec((t, d), lambda i: (i, 0)),
        scratch_shapes=[pltpu.VMEM((t + 2 * halo, d), BF16), pltpu.VMEM((t, dff), BF16),
                        pltpu.VMEM((4, f // LANES, 2 * (t + 2 * halo), LANES), F32)],
        compiler_params=_cparams("parallel"),
        name=name,
    )(h2, h2, h2, w_up, w_conv, b_conv.reshape(1, 2 * dff), w_down, x, mod, norm_out.reshape(1, d))


def _rope_tables(seq, extra_rows):
    tpos = jnp.arange(seq, dtype=jnp.int32)
    row = (tpos // GRID_W).astype(F32)
    col = (tpos % GRID_W).astype(F32)
    inv = ROPE_BASE ** (-jnp.arange(ROPE_PAIRS, dtype=F32) / ROPE_PAIRS)
    ar = row[:, None] * inv[None, :]
    ac = col[:, None] * inv[None, :]
    cr, sr, cc, sc = jnp.cos(ar), jnp.sin(ar), jnp.cos(ac), jnp.sin(ac)
    cos = jnp.concatenate([cr, cr, cc, cc] * (LANES // HEAD_DIM), axis=1)
    sin = jnp.concatenate([-sr, sr, -sc, sc] * (LANES // HEAD_DIM), axis=1)
    cos = jnp.concatenate([cos, jnp.ones((extra_rows, LANES), F32)], axis=0)
    sin = jnp.concatenate([sin, jnp.zeros((extra_rows, LANES), F32)], axis=0)
    return cos * Q_SCALE, sin * Q_SCALE, cos, sin


def _dup_kv_columns(w, d, nkv):
    q = w[:, :d]
    k = w[:, d:d + nkv * HEAD_DIM].reshape(-1, nkv, 1, HEAD_DIM)
    v = w[:, d + nkv * HEAD_DIM:].reshape(-1, nkv, 1, HEAD_DIM)
    kd = jnp.broadcast_to(k, (w.shape[0], nkv, 2, HEAD_DIM)).reshape(w.shape[0], -1)
    vd = jnp.broadcast_to(v, (w.shape[0], nkv, 2, HEAD_DIM)).reshape(w.shape[0], -1)
    return jnp.concatenate([q, kd, vd], axis=1)


def kernel(x, c, ctx, c_ctx, ada_w, ada_b, norm_mix, norm_ffn, norm_out, ffn_up, ffn_conv, ffn_conv_b,
           ffn_down, a_wqkv, a_wo, a_lambda, a_subln, b_wqkv, b_wo, b_rpb, c_wqkv, c_wo, c_sinks):
    batch, seq, d = x.shape
    nctx = ctx.shape[1]
    depth = ada_w.shape[0]
    assert seq % T_FFN == 0 and seq % T_PROJ == 0 and (batch * nctx) % T_PROJ == 0

    mod_rows = 16
    cs = jnp.concatenate([c, c_ctx[None, :], jnp.zeros((mod_rows - batch - 1, d), F32)], axis=0)
    mod_all = _ada_call(cs, ada_w, ada_b).reshape(depth, mod_rows, 6, 1, d)
    ctx_row = batch

    tables = _rope_tables(seq, T_PROJ)
    lat_tiles = seq // T_PROJ
    nd = d // LANES

    xl = x.reshape(batch * seq, d)
    xc = ctx.reshape(batch * nctx, d)
    for i in range(depth):
        need_ctx = i < depth - 1
        kind, j = i % N_MIXERS, i // N_MIXERS
        mod = mod_all[i]
        if kind == 0:
            w = a_wqkv[j].astype(BF16)
            wo = a_wo[j].astype(BF16)
            kinds = [ROPEQ] * nd + [ROPEK] * nd + [PLAIN] * nd
            tabs = tables
        elif kind == 1:
            w = b_wqkv[j].astype(BF16)
            wo = b_wo[j].astype(BF16)
            kinds = [SCALEQ] * nd + [PLAIN] * (2 * nd)
            tabs = None
        else:
            nkv = (c_wqkv.shape[2] - d) // (2 * HEAD_DIM)
            w = _dup_kv_columns(c_wqkv[j], d, nkv).astype(BF16)
            wo = c_wo[j].astype(BF16)
            kinds = [ROPEQ] * nd + [ROPEK] * nkv + [PLAIN] * nkv
            tabs = tables
        qkv = _proj_call(xl, norm_mix[i], mod, w, kinds, tabs, tiles_per_seq=lat_tiles, mod_row=None,
                         name=f"proj{i}")
        qkvc = _proj_call(xc, norm_mix[i], mod, w, kinds, tabs, tiles_per_seq=lat_tiles, mod_row=ctx_row,
                          name=f"proj_ctx{i}")
        if kind == 0:
            lam_init = 0.8 - 0.6 * math.exp(-0.3 * i)
            y, yc = _diff_attention(qkv, qkvc, a_lambda[j].astype(F32), a_subln[j], lam_init,
                                    batch, seq, nctx, need_ctx)
        elif kind == 1:
            y, yc = _na_attention(qkv, qkvc, b_rpb[j], batch, seq, nctx, need_ctx)
        else:
            y, yc = _swa_attention(qkv, qkvc, c_sinks[j], batch, seq, nctx, need_ctx, d)

        w_up = ffn_up[i].astype(BF16)
        w_down = ffn_down[i].astype(BF16)
        xl, h2 = _oproj_call(y, wo, xl, norm_ffn[i], mod, tiles_per_seq=lat_tiles, mod_row=None,
                             name=f"oproj{i}")
        xl = _ffn_call(h2, xl, w_up, ffn_conv[i], ffn_conv_b[i], w_down, mod, norm_out,
                       t=T_FFN, tiles_per_seq=seq // T_FFN, mod_row=None, final=not need_ctx,
                       name=f"ffn{i}")
        if need_ctx:
            xc, h2c = _oproj_call(yc, wo, xc, norm_ffn[i], mod, tiles_per_seq=1, mod_row=ctx_row,
                                  name=f"oproj_ctx{i}")
            xc = _ffn_call(h2c, xc, w_up, ffn_conv[i], ffn_conv_b[i], w_down, mod, norm_out,
                           t=nctx, tiles_per_seq=1, mod_row=ctx_row, final=False, name=f"ffn_ctx{i}")
    return xl.reshape(batch, seq, d)
```
